```python
import math
import jax, jax.numpy as jnp
from jax import lax
import numpy as np

D_MODEL = 2048
BATCH = 4
SEQ = 2048
DEPTH = 1
DEC_BATCH = 128
DEC_SEQ = 8
PAST_LEN = 16384
PAGE_SIZE = 128

D_A = D_MODEL // 2
H_A = 8
DK = D_A // H_A
DV = D_A // H_A
DELTA_CHUNK = 64
CONV_W = 4
D_B = D_MODEL // 2
NB = 8
BW = D_B // NB
LRU_C = 8.0
N_EXPERTS = 32
TOP_K = 4
D_FF = D_MODEL
SWIGLU_LIMIT = 7.0
SWIGLU_ALPHA = 1.702
MOE_BLOCK = 128
DN_ALPHA = (2.0 * DEPTH) ** 0.25
DN_BETA = (8.0 * DEPTH) ** -0.25
LN_EPS = 1e-5
RMS_EPS = 1e-6
D_IN = 4 * D_A + 2 * H_A + 2 * D_B
SPLITS = [3 * D_A, 4 * D_A, 4 * D_A + H_A, 4 * D_A + 2 * H_A, 4 * D_A + 2 * H_A + D_B]

kernel_name = "hymba_deltanet_rglru_moe_step"


def _layernorm(x, g, b):
    xf = x.astype(jnp.float32)
    mu = xf.mean(-1, keepdims=True)
    var = jnp.square(xf - mu).mean(-1, keepdims=True)
    return ((xf - mu) * lax.rsqrt(var + LN_EPS) * g + b).astype(x.dtype)


def _rmsnorm(xf, g):
    return xf * lax.rsqrt(jnp.mean(jnp.square(xf), -1, keepdims=True) + RMS_EPS) * g


def _l2norm(xf):
    return xf * lax.rsqrt(jnp.sum(jnp.square(xf), -1, keepdims=True) + 1e-6)


def _causal_conv(x, buf, w, b=None):
    t = x.shape[1]
    xp = jnp.concatenate([buf.astype(x.dtype), x], axis=1)
    out = xp[:, 0:t] * w[0]
    for j in range(1, CONV_W):
        out = out + xp[:, j:j + t] * w[j]
    if b is not None:
        out = out + b
    return out, xp[:, t:]


def _gated_delta_rule(q, k, v, g, beta, s0):
    bn, t = q.shape[:2]
    c = min(DELTA_CHUNK, t)
    n = -(-t // c)
    pad = n * c - t

    def blocks(a):
        a = jnp.pad(a, [(0, 0), (0, pad)] + [(0, 0)] * (a.ndim - 2))
        a = a.reshape((bn, n, c) + a.shape[2:])
        return jnp.moveaxis(a, (1, 3), (0, 2))

    q = blocks(q * DK ** -0.5)
    k = blocks(k)
    v = blocks(v)
    g = blocks(g)
    beta = blocks(beta)
    gc = jnp.cumsum(g, axis=-1)
    incl = jnp.tril(jnp.ones((c, c), bool))
    strict = jnp.tril(jnp.ones((c, c), bool), -1)
    diff = gc[..., :, None] - gc[..., None, :]
    decay = jnp.where(incl, jnp.exp(jnp.where(incl, diff, 0.0)), 0.0)
    kb = k * beta[..., None]
    lmat = jnp.where(strict, jnp.einsum('nbhcd,nbhsd->nbhcs', kb, k) * decay, 0.0)
    rhs = jnp.concatenate([v * beta[..., None], kb * jnp.exp(gc)[..., None]], axis=-1)
    sol = lax.linalg.triangular_solve(lmat, rhs, left_side=True, lower=True, unit_diagonal=True)
    u, w = sol[..., :DV], sol[..., DV:]
    a_qk = jnp.einsum('nbhcd,nbhsd->nbhcs', q, k) * decay

    def step(s, blk):
        q_c, k_c, u_c, w_c, gc_c, a_c = blk
        v_new = u_c - jnp.einsum('bhcd,bhde->bhce', w_c, s)
        o = (jnp.einsum('bhcd,bhde->bhce', q_c * jnp.exp(gc_c)[..., None], s)
             + jnp.einsum('bhcs,bhse->bhce', a_c, v_new))
        g_last = gc_c[..., -1]
        k_dec = k_c * jnp.exp(g_last[..., None] - gc_c)[..., None]
        s = s * jnp.exp(g_last)[..., None, None] + jnp.einsum('bhcd,bhce->bhde', k_dec, v_new)
        return s, o

    s_fin, o = lax.scan(step, s0, (q, k, u, w, gc, a_qk))
    o = jnp.moveaxis(o, (0, 2), (1, 3)).reshape(bn, n * c, H_A, DV)[:, :t]
    return o, s_fin


def _rg_lru(xb, h0, wa, ba, wx, bx, lam, reset_first):
    bn, t, _ = xb.shape
    xg = xb.reshape(bn, t, NB, BW)
    gate_r = jax.nn.sigmoid(jnp.einsum('btnc,ncd->btnd', xg, wa).reshape(bn, t, D_B) + ba)
    gate_i = jax.nn.sigmoid(jnp.einsum('btnc,ncd->btnd', xg, wx).reshape(bn, t, D_B) + bx)
    log_a = -LRU_C * gate_r * jax.nn.softplus(-lam)
    a = jnp.exp(log_a)
    mult = jnp.sqrt(-jnp.expm1(2.0 * log_a))
    if reset_first:
        first = (jnp.arange(t) == 0)[None, :, None]
        a = jnp.where(first, 0.0, a)
        mult = jnp.where(first, 1.0, mult)
    b = mult * gate_i * xb
    b = b.at[:, 0].add(a[:, 0] * h0)

    def combine(lft, rgt):
        return (lft[0] * rgt[0], rgt[0] * lft[1] + rgt[1])

    _, h = lax.associative_scan(combine, (a, b), axis=1)
    return h, h[:, -1]


def _moe(x2d, w_router, b_router, w_up, b_up, w_down, b_down):
    m = x2d.shape[0]
    mk = m * TOP_K
    logits = jnp.matmul(x2d, w_router).astype(jnp.float32) + b_router.astype(jnp.float32)
    top_v, top_i = lax.top_k(logits, TOP_K)
    gates = jax.nn.softmax(top_v, axis=-1)
    flat_e = top_i.reshape(mk).astype(jnp.int32)
    order = jnp.argsort(flat_e).astype(jnp.int32)
    sorted_e = flat_e[order]
    counts = jnp.zeros((N_EXPERTS,), jnp.int32).at[flat_e].add(1)
    starts = jnp.cumsum(counts) - counts
    padded = (counts + MOE_BLOCK - 1) // MOE_BLOCK * MOE_BLOCK
    pad_ends = jnp.cumsum(padded)
    pad_starts = pad_ends - padded
    slot_sorted = pad_starts[sorted_e] + jnp.arange(mk, dtype=jnp.int32) - starts[sorted_e]
    slot = jnp.zeros((mk,), jnp.int32).at[order].set(slot_sorted)
    n_blocks = -(-mk // MOE_BLOCK) + N_EXPERTS
    row_tok = jnp.zeros((n_blocks * MOE_BLOCK,), jnp.int32).at[slot].set(
        jnp.arange(mk, dtype=jnp.int32) // TOP_K)
    block_e = jnp.minimum(
        jnp.searchsorted(pad_ends, jnp.arange(n_blocks, dtype=jnp.int32) * MOE_BLOCK, side='right'),
        N_EXPERTS - 1)
    xb = x2d[row_tok].reshape(n_blocks, MOE_BLOCK, D_MODEL)

    def expert_block(args):
        xe, e = args
        h = jnp.matmul(xe, w_up[e]) + b_up[e]
        gate = jnp.minimum(h[:, :D_FF], SWIGLU_LIMIT)
        up = jnp.clip(h[:, D_FF:], -SWIGLU_LIMIT, SWIGLU_LIMIT)
        glu = gate * jax.nn.sigmoid(SWIGLU_ALPHA * gate)
        return jnp.matmul(glu * (up + 1.0), w_down[e]) + b_down[e]

    yb = lax.map(expert_block, (xb, block_e)).reshape(n_blocks * MOE_BLOCK, D_MODEL)
    y_assign = yb[slot].reshape(m, TOP_K, D_MODEL)
    return jnp.einsum('mk,mkd->md', gates.astype(y_assign.dtype), y_assign)


def _layer(x, conv_a0, s0, conv_b0, h0, reset_first,
           w_in, conv_a_w, a_log, dt_bias, norm_a_w, conv_b_w, conv_b_b,
           lru_wa, lru_ba, lru_wx, lru_bx, lru_lambda, norm_b_w, w_out, ln1_g, ln1_b,
           w_router, b_router, w_up, b_up, w_down, b_down, ln2_g, ln2_b):
    f32 = jnp.float32
    bn, t, _ = x.shape
    proj = jnp.matmul(x, w_in)
    qkv, z, b_in, a_in, xb, yb = jnp.split(proj, SPLITS, axis=-1)
    qkv, conv_a1 = _causal_conv(qkv, conv_a0, conv_a_w)
    qkv = jax.nn.silu(qkv.astype(f32)).reshape(bn, t, 3, H_A, DK)
    q = _l2norm(qkv[:, :, 0])
    k = _l2norm(qkv[:, :, 1])
    v = qkv[:, :, 2]
    beta = jax.nn.sigmoid(b_in.astype(f32))
    g = -jnp.exp(a_log.astype(f32)) * jax.nn.softplus(a_in.astype(f32) + dt_bias)
    o_a, s1 = _gated_delta_rule(q, k, v, g, beta, s0.astype(f32))
    zf = z.astype(f32).reshape(bn, t, H_A, DV)
    o_a = (_rmsnorm(o_a, norm_a_w) * jax.nn.silu(zf)).reshape(bn, t, D_A)
    xb, conv_b1 = _causal_conv(xb, conv_b0, conv_b_w, conv_b_b)
    hseq, h1 = _rg_lru(xb.astype(f32), h0.astype(f32), lru_wa, lru_ba, lru_wx, lru_bx,
                       lru_lambda, reset_first)
    o_b = _rmsnorm(hseq * jax.nn.gelu(yb.astype(f32), approximate=True), norm_b_w)
    mix = jnp.matmul(jnp.concatenate([o_a, o_b], axis=-1).astype(x.dtype), w_out)
    x = _layernorm(DN_ALPHA * x + mix, ln1_g, ln1_b)
    ffn = _moe(x.reshape(bn * t, D_MODEL), w_router, b_router, w_up, b_up, w_down, b_down)
    x = _layernorm(DN_ALPHA * x + ffn.reshape(bn, t, D_MODEL).astype(x.dtype), ln2_g, ln2_b)
    return x, conv_a1, s1, conv_b1, h1


def setup_inputs(seed: int = 0) -> dict:
    key = jax.random.key(seed)
    ks = jax.random.split(key, 32)
    f32 = jnp.float32

    def nrm(k, shape, scale):
        return jax.random.normal(k, shape, f32) * scale

    L = DEPTH
    x_prompt = nrm(ks[0], (BATCH, SEQ, D_MODEL), 1.0)
    x_sample = nrm(ks[1], (DEC_BATCH, DEC_SEQ, D_MODEL), 1.0)
    state_conv_a = nrm(ks[2], (L, DEC_BATCH, CONV_W - 1, 3 * D_A), 1.0)
    state_delta = nrm(ks[3], (L, DEC_BATCH, H_A, DK, DV), DK ** -0.5)
    state_conv_b = nrm(ks[4], (L, DEC_BATCH, CONV_W - 1, D_B), 1.0)
    state_lru = nrm(ks[5], (L, DEC_BATCH, D_B), 1.0)
    w_in = nrm(ks[6], (L, D_MODEL, D_IN), D_MODEL ** -0.5)
    conv_a_w = nrm(ks[7], (L, CONV_W, 3 * D_A), CONV_W ** -0.5)
    a_log = jnp.log(jax.random.uniform(ks[8], (L, H_A), f32, 1.0, 16.0))
    dt = jnp.exp(jax.random.uniform(ks[9], (L, H_A), f32, math.log(1e-3), math.log(1e-1)))
    dt_bias = dt + jnp.log(-jnp.expm1(-dt))
    norm_a_w = 1.0 + nrm(ks[10], (L, DV), 0.02)
    conv_b_w = nrm(ks[11], (L, CONV_W, D_B), CONV_W ** -0.5)
    conv_b_b = nrm(ks[12], (L, D_B), 0.01)
    lru_wa = nrm(ks[13], (L, NB, BW, BW), BW ** -0.5)
    lru_ba = nrm(ks[14], (L, D_B), 0.01)
    lru_wx = nrm(ks[15], (L, NB, BW, BW), BW ** -0.5)
    lru_bx = nrm(ks[16], (L, D_B), 0.01)
    a0 = jax.random.uniform(ks[17], (L, D_B), f32, 0.9, 0.999)
    s_root = a0 ** (1.0 / LRU_C)
    lru_lambda = jnp.log(s_root) - jnp.log1p(-s_root)
    norm_b_w = 1.0 + nrm(ks[18], (L, D_B), 0.02)
    w_out = nrm(ks[19], (L, D_A + D_B, D_MODEL), (D_A + D_B) ** -0.5 * DN_BETA)
    ln1_g = 1.0 + nrm(ks[20], (L, D_MODEL), 0.02)
    ln1_b = nrm(ks[21], (L, D_MODEL), 0.01)
    w_router = nrm(ks[22], (L, D_MODEL, N_EXPERTS), D_MODEL ** -0.5)
    b_router = nrm(ks[23], (L, N_EXPERTS), 0.01)
    w_up = nrm(ks[24], (L, N_EXPERTS, D_MODEL, 2 * D_FF), D_MODEL ** -0.5)
    b_up = nrm(ks[25], (L, N_EXPERTS, 2 * D_FF), 0.01)
    w_down = nrm(ks[26], (L, N_EXPERTS, D_FF, D_MODEL), D_FF ** -0.5 * DN_BETA)
    b_down = nrm(ks[27], (L, N_EXPERTS, D_MODEL), 0.01)
    ln2_g = 1.0 + nrm(ks[28], (L, D_MODEL), 0.02)
    ln2_b = nrm(ks[29], (L, D_MODEL), 0.01)
    return {"x_prompt": x_prompt, "x_sample": x_sample,
            "state_conv_a": state_conv_a, "state_delta": state_delta,
            "state_conv_b": state_conv_b, "state_lru": state_lru,
            "w_in": w_in, "conv_a_w": conv_a_w, "a_log": a_log, "dt_bias": dt_bias,
            "norm_a_w": norm_a_w, "conv_b_w": conv_b_w, "conv_b_b": conv_b_b,
            "lru_wa": lru_wa, "lru_ba": lru_ba, "lru_wx": lru_wx, "lru_bx": lru_bx,
            "lru_lambda": lru_lambda, "norm_b_w": norm_b_w, "w_out": w_out,
            "ln1_g": ln1_g, "ln1_b": ln1_b, "w_router": w_router, "b_router": b_router,
            "w_up": w_up, "b_up": b_up, "w_down": w_down, "b_down": b_down,
            "ln2_g": ln2_g, "ln2_b": ln2_b}


def reference(x_prompt, x_sample, state_conv_a, state_delta, state_conv_b, state_lru,
              w_in, conv_a_w, a_log, dt_bias, norm_a_w, conv_b_w, conv_b_b,
              lru_wa, lru_ba, lru_wx, lru_bx, lru_lambda, norm_b_w, w_out, ln1_g, ln1_b,
              w_router, b_router, w_up, b_up, w_down, b_down, ln2_g, ln2_b):
    y_prompt, y_sample = x_prompt, x_sample
    bp = x_prompt.shape[0]
    p_ca, p_sd, p_cb, p_h = [], [], [], []
    s_ca, s_sd, s_cb, s_h = [], [], [], []
    for l in range(DEPTH):
        params = (w_in[l], conv_a_w[l], a_log[l], dt_bias[l], norm_a_w[l], conv_b_w[l], conv_b_b[l],
                  lru_wa[l], lru_ba[l], lru_wx[l], lru_bx[l], lru_lambda[l], norm_b_w[l], w_out[l],
                  ln1_g[l], ln1_b[l], w_router[l], b_router[l], w_up[l], b_up[l], w_down[l],
                  b_down[l], ln2_g[l], ln2_b[l])
        y_prompt, ca, sd, cb, hh = _layer(
            y_prompt,
            jnp.zeros((bp, CONV_W - 1, 3 * D_A), x_prompt.dtype),
            jnp.zeros((bp, H_A, DK, DV), jnp.float32),
            jnp.zeros((bp, CONV_W - 1, D_B), x_prompt.dtype),
            jnp.zeros((bp, D_B), jnp.float32),
            True, *params)
        p_ca.append(ca)
        p_sd.append(sd)
        p_cb.append(cb)
        p_h.append(hh)
        y_sample, ca, sd, cb, hh = _layer(
            y_sample, state_conv_a[l], state_delta[l], state_conv_b[l], state_lru[l],
            False, *params)
        s_ca.append(ca)
        s_sd.append(sd)
        s_cb.append(cb)
        s_h.append(hh)
    return (y_prompt, y_sample,
            jnp.stack(p_ca), jnp.stack(p_sd), jnp.stack(p_cb), jnp.stack(p_h),
            jnp.stack(s_ca), jnp.stack(s_sd), jnp.stack(s_cb), jnp.stack(s_h))
```

```python
import functools
import math

import jax
import jax.numpy as jnp
from jax import lax
from jax.experimental import pallas as pl
from jax.experimental.pallas import tpu as pltpu

F32 = jnp.float32
BF16 = jnp.bfloat16
HIGHEST = lax.Precision.HIGHEST

D_MODEL = 2048
D_A = 1024
H_A = 8
DK = 128
DV = 128
CONV_W = 4
D_B = 1024
NB = 8
BW = 128
LRU_C = 8.0
N_EXPERTS = 32
TOP_K = 4
D_FF = 2048
SWIGLU_LIMIT = 7.0
SWIGLU_ALPHA = 1.702
DEPTH = 1
DN_ALPHA = (2.0 * DEPTH) ** 0.25
LN_EPS = 1e-5
RMS_EPS = 1e-6
L2_EPS = 1e-6

LANES = 128
SUBLANES = 8
VMEM_LIMIT_BYTES = 56 * 1024 * 1024

COL_QKV = 0
COL_Z = 3 * D_A
COL_XB = 4 * D_A
COL_YB = 4 * D_A + D_B
COL_BA = 4 * D_A + 2 * D_B
N_PROJ = COL_BA + 2 * LANES

INPROJ_TM = 512
INPROJ_TN = 1280
CONV_L = 256
DELTA_C = 128
OUT_TM = 256
MOE_RB = 256
MOE_SB = 6
MOE_R = MOE_RB * MOE_SB
MOE_TF = 256
MOE_NT = D_FF // MOE_TF
TOK_TM = 256
ROW_SLAB = D_MODEL // LANES


def _cparams(sem, vmem=VMEM_LIMIT_BYTES):
    return pltpu.CompilerParams(dimension_semantics=sem, vmem_limit_bytes=vmem)


def _sigmoid(x):
    return 1.0 / (1.0 + jnp.exp(-x))


def _softplus(x):
    return jnp.maximum(x, 0.0) + jnp.log(1.0 + jnp.exp(-jnp.abs(x)))


def _inproj_body(xp_ref, xs_ref, w_ref, o_ref, xb_ref, *, n_ptiles):
    i = pl.program_id(0)

    @pl.when(pl.program_id(1) == 0)
    def _():
        @pl.when(i < n_ptiles)
        def _():
            xb_ref[...] = xp_ref[...].astype(BF16)

        @pl.when(i >= n_ptiles)
        def _():
            xb_ref[...] = xs_ref[...].astype(BF16)

    o_ref[...] = jnp.dot(xb_ref[...], w_ref[...], preferred_element_type=F32)


def _inproj(xp2, xs2, wp):
    mp, ms = xp2.shape[0], xs2.shape[0]
    tm = min(INPROJ_TM, ms)
    assert mp % tm == 0 and ms % tm == 0
    n_pt, n_st = mp // tm, ms // tm
    n_nt = N_PROJ // INPROJ_TN
    return pl.pallas_call(
        functools.partial(_inproj_body, n_ptiles=n_pt),
        grid=(n_pt + n_st, n_nt),
        in_specs=[
            pl.BlockSpec((tm, D_MODEL), lambda i, j: (jnp.minimum(i, n_pt - 1), 0)),
            pl.BlockSpec((tm, D_MODEL), lambda i, j: (jnp.maximum(i - n_pt, 0), 0)),
            pl.BlockSpec((D_MODEL, INPROJ_TN), lambda i, j: (0, j)),
        ],
        out_specs=pl.BlockSpec((tm, INPROJ_TN), lambda i, j: (i, j)),
        out_shape=jax.ShapeDtypeStruct((mp + ms, N_PROJ), F32),
        scratch_shapes=[pltpu.VMEM((tm, D_MODEL), BF16)],
        compiler_params=_cparams(("arbitrary", "arbitrary")),
        name="inproj",
    )(xp2, xs2, wp)


def _conv_taps(h_ref, w_ref, c0, width, rows, base):
    acc = h_ref[base:base + rows, c0:c0 + width] * w_ref[CONV_W - 1:CONV_W, c0:c0 + width]
    for j in range(1, CONV_W):
        acc = acc + (h_ref[base - j:base - j + rows, c0:c0 + width]
                     * w_ref[CONV_W - 1 - j:CONV_W - j, c0:c0 + width])
    return acc


def _qkv_activation(acc, blk):
    a = acc * _sigmoid(acc)
    if blk < 2 * H_A:
        a = a * lax.rsqrt(jnp.sum(a * a, axis=-1, keepdims=True) + L2_EPS)
        if blk < H_A:
            a = a * (DK ** -0.5)
    return a


def _beta_g(ba, alog_ref, dtb_ref):
    lane = lax.broadcasted_iota(jnp.int32, ba.shape, 1)
    beta = _sigmoid(ba)
    g = -jnp.exp(alog_ref[...]) * _softplus(ba + dtb_ref[...])
    return jnp.where(lane < H_A, beta, jnp.where(lane < 2 * H_A, g, 0.0))


def _convact_prompt_body(qkv_ref, xb_ref, ba_ref, caw_ref, cbw_ref, cbb_ref, alog_ref, dtb_ref,
                         qkvo_ref, xbo_ref, bgo_ref, hq_ref, hx_ref, *, rows):
    t = pl.program_id(1)
    hdr = SUBLANES

    @pl.when(t == 0)
    def _():
        hq_ref[0:hdr, :] = jnp.zeros((hdr, 3 * D_A), F32)
        hx_ref[0:hdr, :] = jnp.zeros((hdr, D_B), F32)

    @pl.when(t > 0)
    def _():
        hq_ref[0:hdr, :] = hq_ref[rows:rows + hdr, :]
        hx_ref[0:hdr, :] = hx_ref[rows:rows + hdr, :]

    hq_ref[hdr:hdr + rows, :] = qkv_ref[...]
    hx_ref[hdr:hdr + rows, :] = xb_ref[...]
    for blk in range(3 * H_A):
        c0 = blk * DK
        acc = _conv_taps(hq_ref, caw_ref, c0, DK, rows, hdr)
        qkvo_ref[:, c0:c0 + DK] = _qkv_activation(acc, blk)
    for blk in range(NB):
        c0 = blk * BW
        xbo_ref[:, c0:c0 + BW] = _conv_taps(hx_ref, cbw_ref, c0, BW, rows, hdr) + cbb_ref[:, c0:c0 + BW]
    bgo_ref[...] = _beta_g(ba_ref[...], alog_ref, dtb_ref)


def _convact_prompt(proj, bsz, t_len, caw, cbw, cbb, alog_row, dtb_row):
    rows = min(CONV_L, t_len)
    assert t_len % rows == 0
    nt = t_len // rows
    mp = bsz * t_len
    wspec = lambda shape: pl.BlockSpec(shape, lambda b, t: (0, 0))
    return pl.pallas_call(
        functools.partial(_convact_prompt_body, rows=rows),
        grid=(bsz, nt),
        in_specs=[
            pl.BlockSpec((rows, 3 * D_A), lambda b, t: (b * nt + t, COL_QKV // (3 * D_A))),
            pl.BlockSpec((rows, D_B), lambda b, t: (b * nt + t, COL_XB // D_B)),
            pl.BlockSpec((rows, LANES), lambda b, t: (b * nt + t, COL_BA // LANES)),
            wspec((CONV_W, 3 * D_A)), wspec((CONV_W, D_B)), wspec((1, D_B)),
            wspec((1, LANES)), wspec((1, LANES)),
        ],
        out_specs=[
            pl.BlockSpec((rows, 3 * D_A), lambda b, t: (b * nt + t, 0)),
            pl.BlockSpec((rows, D_B), lambda b, t: (b * nt + t, 0)),
            pl.BlockSpec((rows, LANES), lambda b, t: (b * nt + t, 0)),
        ],
        out_shape=[
            jax.ShapeDtypeStruct((mp, 3 * D_A), F32),
            jax.ShapeDtypeStruct((mp, D_B), F32),
            jax.ShapeDtypeStruct((mp, LANES), F32),
        ],
        scratch_shapes=[pltpu.VMEM((SUBLANES + rows + SUBLANES, 3 * D_A), F32),
                        pltpu.VMEM((SUBLANES + rows + SUBLANES, D_B), F32)],
        compiler_params=_cparams(("arbitrary", "arbitrary")),
        name="convact_prompt",
    )(proj, proj, proj, caw, cbw, cbb, alog_row, dtb_row)


def _dot_nt(a, b, precision=None):
    return lax.dot_general(a, b, (((1,), (1,)), ((), ())), precision=precision,
                           preferred_element_type=F32)


def _delta_prep_body(qkv_ref, bg_ref, u_ref, w_ref, qe_ref, kdt_ref, aqk_ref, el_ref, *, c):
    row = lax.broadcasted_iota(jnp.int32, (c, c), 0)
    col = lax.broadcasted_iota(jnp.int32, (c, c), 1)
    incl = row >= col
    strict = row > col
    eye = (row == col).astype(F32)
    tril = incl.astype(F32)
    lane = lax.broadcasted_iota(jnp.int32, (c, LANES), 1)
    eye_l = (lax.broadcasted_iota(jnp.int32, (LANES, LANES), 0)
             == lax.broadcasted_iota(jnp.int32, (LANES, LANES), 1)).astype(F32)
    ones_cl = jnp.ones((c, LANES), F32)

    bg = bg_ref[...]
    gc_all = jnp.dot(tril, bg, precision=HIGHEST, preferred_element_type=F32)
    for h in range(H_A):
        q = qkv_ref[:, h * DK:(h + 1) * DK]
        k = qkv_ref[:, (H_A + h) * DK:(H_A + h + 1) * DK]
        v = qkv_ref[:, (2 * H_A + h) * DK:(2 * H_A + h + 1) * DK]
        beta = bg[:, h:h + 1]
        gc = gc_all[:, H_A + h:H_A + h + 1]
        g_last = gc_all[c - 1:c, H_A + h:H_A + h + 1]
        gc_row = _dot_nt(ones_cl, jnp.where(lane == 0, gc, 0.0), precision=HIGHEST)
        diff = gc - gc_row
        decay = jnp.where(incl, jnp.exp(jnp.where(incl, diff, 0.0)), 0.0)
        kb = k * beta
        egc = jnp.exp(gc)
        kk = _dot_nt(k.astype(BF16), k.astype(BF16))
        lmat = jnp.where(strict, kk * beta * decay, 0.0)
        rhs = jnp.concatenate([v * beta, kb * egc], axis=1)
        x = -lmat
        sol = rhs + jnp.dot(x, rhs, precision=HIGHEST, preferred_element_type=F32)
        p = 2
        while p < c:
            x = jnp.dot(x, x, precision=HIGHEST, preferred_element_type=F32)
            sol = sol + jnp.dot(x, sol, precision=HIGHEST, preferred_element_type=F32)
            p *= 2
        u_ref[:, h * DV:(h + 1) * DV] = sol[:, :DV]
        w_ref[:, h * DK:(h + 1) * DK] = sol[:, DV:]
        qe_ref[:, h * DK:(h + 1) * DK] = q * egc
        aqk_ref[h] = _dot_nt(q.astype(BF16), k.astype(BF16)) * decay
        kdec = k * jnp.exp(g_last - gc)
        kdt_ref[h] = _dot_nt(eye_l, kdec, precision=HIGHEST)
        el_ref[h] = jnp.broadcast_to(jnp.exp(g_last), (SUBLANES, LANES))


def _delta_prep(qkv_act, bg, bsz, t_len):
    c = DELTA_C
    assert t_len % c == 0
    nc = t_len // c
    mp = bsz * t_len
    n_chunks = bsz * nc
    rowspec = lambda w: pl.BlockSpec((c, w), lambda i: (i, 0))
    return pl.pallas_call(
        functools.partial(_delta_prep_body, c=c),
        grid=(n_chunks,),
        in_specs=[rowspec(3 * D_A), rowspec(LANES)],
        out_specs=[
            rowspec(D_A), rowspec(D_A), rowspec(D_A),
            pl.BlockSpec((None, H_A, DK, c), lambda i: (i, 0, 0, 0)),
            pl.BlockSpec((None, H_A, c, c), lambda i: (i, 0, 0, 0)),
            pl.BlockSpec((None, H_A, SUBLANES, LANES), lambda i: (i, 0, 0, 0)),
        ],
        out_shape=[
            jax.ShapeDtypeStruct((mp, D_A), F32),
            jax.ShapeDtypeStruct((mp, D_A), F32),
            jax.ShapeDtypeStruct((mp, D_A), F32),
            jax.ShapeDtypeStruct((n_chunks, H_A, DK, c), F32),
            jax.ShapeDtypeStruct((n_chunks, H_A, c, c), F32),
            jax.ShapeDtypeStruct((n_chunks, H_A, SUBLANES, LANES), F32),
        ],
        compiler_params=_cparams(("arbitrary",)),
        name="delta_prep",
    )(qkv_act, bg)


def _gated_rmsnorm(o, z, nw):
    on = o * lax.rsqrt(jnp.mean(o * o, axis=-1, keepdims=True) + RMS_EPS) * nw
    return on * (z * _sigmoid(z))


def _delta_seq_body(u_ref, w_ref, qe_ref, kdt_ref, aqk_ref, el_ref, z_ref, nw_ref,
                    oa_ref, sfin_ref, s_ref):
    ci = pl.program_id(1)

    @pl.when(ci == 0)
    def _():
        s_ref[...] = jnp.zeros(s_ref.shape, F32)

    nw = nw_ref[...]
    for h in range(H_A):
        s = s_ref[h]
        sb = s.astype(BF16)
        sl = slice(h * DK, (h + 1) * DK)
        v_new = u_ref[:, sl] - jnp.dot(w_ref[:, sl].astype(BF16), sb, preferred_element_type=F32)
        o = (jnp.dot(qe_ref[:, sl].astype(BF16), sb, preferred_element_type=F32)
             + jnp.dot(aqk_ref[h].astype(BF16), v_new.astype(BF16), preferred_element_type=F32))
        s_new = s * el_ref[h][0:1, 0:1] + jnp.dot(kdt_ref[h].astype(BF16), v_new.astype(BF16),
                                                 preferred_element_type=F32)
        s_ref[h] = s_new
        oa_ref[:, sl] = _gated_rmsnorm(o, z_ref[:, sl], nw).astype(BF16)

    @pl.when(ci == pl.num_programs(1) - 1)
    def _():
        sfin_ref[...] = s_ref[...]


def _delta_seq(u, w, qe, kdt, aqk, el, proj, nw_row, bsz, t_len):
    c = DELTA_C
    nc = t_len // c
    mp = bsz * t_len
    rowspec = lambda wd: pl.BlockSpec((c, wd), lambda b, i: (b * nc + i, 0))
    chunkspec = lambda a, bb: pl.BlockSpec((None, H_A, a, bb), lambda b, i: (b * nc + i, 0, 0, 0))
    return pl.pallas_call(
        _delta_seq_body,
        grid=(bsz, nc),
        in_specs=[
            rowspec(D_A), rowspec(D_A), rowspec(D_A),
            chunkspec(DK, c), chunkspec(c, c), chunkspec(SUBLANES, LANES),
            pl.BlockSpec((c, D_A), lambda b, i: (b * nc + i, COL_Z // D_A)),
            pl.BlockSpec((1, DV), lambda b, i: (0, 0)),
        ],
        out_specs=[
            rowspec(D_A),
            pl.BlockSpec((None, H_A, DK, DV), lambda b, i: (b, 0, 0, 0)),
        ],
        out_shape=[
            jax.ShapeDtypeStruct((mp, D_A), BF16),
            jax.ShapeDtypeStruct((bsz, H_A, DK, DV), F32),
        ],
        scratch_shapes=[pltpu.VMEM((H_A, DK, DV), F32)],
        compiler_params=_cparams(("arbitrary", "arbitrary")),
        name="delta_seq",
    )(u, w, qe, kdt, aqk, el, proj, nw_row)


def _sample_a_body(qkv_ref, z_ref, ba_ref, hist_ref, s0_ref, caw_ref, alog_ref, dtb_ref, nw_ref,
                   oa_ref, s1_ref, hq_ref, *, t_len):
    hdr = SUBLANES
    nh = CONV_W - 1
    hq_ref[0:hdr, :] = jnp.zeros((hdr, 3 * D_A), F32)
    hq_ref[hdr - nh:hdr, :] = hist_ref[...]
    hq_ref[hdr:hdr + t_len, :] = qkv_ref[...]
    bg = _beta_g(ba_ref[...], alog_ref, dtb_ref)
    rowi = lax.broadcasted_iota(jnp.int32, bg.shape, 0)
    gc_all = bg
    sft = 1
    while sft < t_len:
        gc_all = gc_all + jnp.where(rowi >= sft, pltpu.roll(gc_all, sft, axis=0), 0.0)
        sft *= 2
    nw = nw_ref[...]
    rowc = lax.broadcasted_iota(jnp.int32, (t_len, 1), 0)
    for h in range(H_A):
        q = _qkv_activation(_conv_taps(hq_ref, caw_ref, h * DK, DK, t_len, hdr), h)
        k = _qkv_activation(_conv_taps(hq_ref, caw_ref, (H_A + h) * DK, DK, t_len, hdr), H_A + h)
        v = _qkv_activation(_conv_taps(hq_ref, caw_ref, (2 * H_A + h) * DK, DK, t_len, hdr), 2 * H_A + h)
        beta = bg[:, h:h + 1]
        gc = gc_all[:, H_A + h:H_A + h + 1]
        g_last = gc[t_len - 1:t_len, :]
        kb = k * beta
        sol = jnp.concatenate([v * beta, kb * jnp.exp(gc)], axis=1)
        acols = []
        lcols = []
        for j in range(t_len):
            kj = k[j:j + 1, :]
            gj = gc[j:j + 1, :]
            dcol = jnp.exp(jnp.where(rowc >= j, gc - gj, 0.0))
            acols.append(jnp.where(rowc >= j, jnp.sum(q * kj, axis=-1, keepdims=True) * dcol, 0.0))
            lcols.append(jnp.where(rowc > j, jnp.sum(kb * kj, axis=-1, keepdims=True) * dcol, 0.0))
        for j in range(t_len - 1):
            sol = sol - lcols[j] * sol[j:j + 1, :]
        u = sol[:, :DV]
        w = sol[:, DV:]
        s = s0_ref[h]
        sb = s.astype(BF16)
        v_new = u - jnp.dot(w.astype(BF16), sb, preferred_element_type=F32)
        o = jnp.dot((q * jnp.exp(gc)).astype(BF16), sb, preferred_element_type=F32)
        for j in range(t_len):
            o = o + acols[j] * v_new[j:j + 1, :]
        kdec = k * jnp.exp(g_last - gc)
        upd = lax.dot_general(kdec.astype(BF16), v_new.astype(BF16), (((0,), (0,)), ((), ())),
                              preferred_element_type=F32)
        s1_ref[h] = s * jnp.exp(g_last) + upd
        sl = slice(h * DV, (h + 1) * DV)
        oa_ref[:, sl] = _gated_rmsnorm(o, z_ref[:, sl], nw).astype(BF16)


def _sample_a(proj, row0, bsz, t_len, hist, s0, caw, alog_row, dtb_row, nw_row):
    assert t_len == SUBLANES and row0 % t_len == 0
    r0 = row0 // t_len
    wspec = lambda shape: pl.BlockSpec(shape, lambda b: (0,) * len(shape))
    return pl.pallas_call(
        functools.partial(_sample_a_body, t_len=t_len),
        grid=(bsz,),
        in_specs=[
            pl.BlockSpec((t_len, 3 * D_A), lambda b: (r0 + b, COL_QKV // (3 * D_A))),
            pl.BlockSpec((t_len, D_A), lambda b: (r0 + b, COL_Z // D_A)),
            pl.BlockSpec((t_len, LANES), lambda b: (r0 + b, COL_BA // LANES)),
            pl.BlockSpec((None, CONV_W - 1, 3 * D_A), lambda b: (b, 0, 0)),
            pl.BlockSpec((None, H_A, DK, DV), lambda b: (b, 0, 0, 0)),
            wspec((CONV_W, 3 * D_A)), wspec((1, LANES)), wspec((1, LANES)), wspec((1, DV)),
        ],
        out_specs=[
            pl.BlockSpec((t_len, D_A), lambda b: (b, 0)),
            pl.BlockSpec((None, H_A, DK, DV), lambda b: (b, 0, 0, 0)),
        ],
        out_shape=[
            jax.ShapeDtypeStruct((bsz * t_len, D_A), BF16),
            jax.ShapeDtypeStruct((bsz, H_A, DK, DV), F32),
        ],
        scratch_shapes=[pltpu.VMEM((2 * SUBLANES, 3 * D_A), F32)],
        compiler_params=_cparams(("arbitrary",)),
        name="sample_a",
    )(proj, proj, proj, hist, s0, caw, alog_row, dtb_row, nw_row)


def _lru_coeffs(xc, wa_ref, wx_ref, lba_ref, lbx_ref, lam_ref):
    parts = []
    for n in range(NB):
        sl = slice(n * BW, (n + 1) * BW)
        xn = xc[:, sl]
        xnb = xn.astype(BF16)
        gr = _sigmoid(jnp.dot(xnb, wa_ref[n].astype(BF16), preferred_element_type=F32) + lba_ref[:, sl])
        gi = _sigmoid(jnp.dot(xnb, wx_ref[n].astype(BF16), preferred_element_type=F32) + lbx_ref[:, sl])
        log_a = -LRU_C * gr * _softplus(-lam_ref[:, sl])
        parts.append((jnp.exp(log_a), jnp.sqrt(1.0 - jnp.exp(2.0 * log_a)), gi * xn))
    return parts


def _group_scan(a, b):
    rowi = lax.broadcasted_iota(jnp.int32, a.shape, 0)
    sft = 1
    while sft < SUBLANES:
        keep = rowi >= sft
        b = b + a * jnp.where(keep, pltpu.roll(b, sft, axis=0), 0.0)
        a = a * jnp.where(keep, pltpu.roll(a, sft, axis=0), 1.0)
        sft *= 2
    return a, b


def _gelu_tanh(x):
    return 0.5 * x * (1.0 + jnp.tanh(math.sqrt(2.0 / math.pi) * (x + 0.044715 * x * x * x)))


def _lru_finish(h, yb, nbw):
    hg = h * _gelu_tanh(yb)
    return (hg * lax.rsqrt(jnp.mean(hg * hg, axis=-1, keepdims=True) + RMS_EPS) * nbw).astype(BF16)


def _lru_prompt_body(xc_ref, yb_ref, wa_ref, wx_ref, lba_ref, lbx_ref, lam_ref, nbw_ref,
                     ob_ref, hfin_ref, a_ref, b_ref, h_ref, carry_ref, *, rows):
    t = pl.program_id(1)
    parts = _lru_coeffs(xc_ref[...], wa_ref, wx_ref, lba_ref, lbx_ref, lam_ref)
    rowi = lax.broadcasted_iota(jnp.int32, (rows, BW), 0)
    first = jnp.logical_and(rowi == 0, t == 0)
    for n in range(NB):
        sl = slice(n * BW, (n + 1) * BW)
        a_n, mult_n, gix_n = parts[n]
        a_ref[:, sl] = jnp.where(first, 0.0, a_n)
        b_ref[:, sl] = jnp.where(first, 1.0, mult_n) * gix_n

    @pl.when(t == 0)
    def _():
        carry_ref[...] = jnp.zeros(carry_ref.shape, F32)

    def group(gidx, carry):
        r0 = pl.multiple_of(gidx * SUBLANES, SUBLANES)
        ag, bg = _group_scan(a_ref[pl.ds(r0, SUBLANES), :], b_ref[pl.ds(r0, SUBLANES), :])
        hg = ag * carry + bg
        h_ref[pl.ds(r0, SUBLANES), :] = hg
        return jnp.broadcast_to(hg[SUBLANES - 1:SUBLANES, :], hg.shape)

    carry = lax.fori_loop(0, rows // SUBLANES, group, carry_ref[...])
    carry_ref[...] = carry
    ob_ref[...] = _lru_finish(h_ref[...], yb_ref[...], nbw_ref[...])

    @pl.when(t == pl.num_programs(1) - 1)
    def _():
        hfin_ref[...] = carry[0:1, :]


def _lru_prompt(xbc, proj, bsz, t_len, wa, wx, lba, lbx, lam, nbw):
    rows = min(CONV_L, t_len)
    nt = t_len // rows
    mp = bsz * t_len
    wspec = lambda shape: pl.BlockSpec(shape, lambda b, t: (0,) * len(shape))
    return pl.pallas_call(
        functools.partial(_lru_prompt_body, rows=rows),
        grid=(bsz, nt),
        in_specs=[
            pl.BlockSpec((rows, D_B), lambda b, t: (b * nt + t, 0)),
            pl.BlockSpec((rows, D_B), lambda b, t: (b * nt + t, COL_YB // D_B)),
            wspec((NB, BW, BW)), wspec((NB, BW, BW)),
            wspec((1, D_B)), wspec((1, D_B)), wspec((1, D_B)), wspec((1, D_B)),
        ],
        out_specs=[
            pl.BlockSpec((rows, D_B), lambda b, t: (b * nt + t, 0)),
            pl.BlockSpec((None, 1, D_B), lambda b, t: (b, 0, 0)),
        ],
        out_shape=[
            jax.ShapeDtypeStruct((mp, D_B), BF16),
            jax.ShapeDtypeStruct((bsz, 1, D_B), F32),
        ],
        scratch_shapes=[pltpu.VMEM((rows, D_B), F32), pltpu.VMEM((rows, D_B), F32),
                        pltpu.VMEM((rows, D_B), F32), pltpu.VMEM((SUBLANES, D_B), F32)],
        compiler_params=_cparams(("arbitrary", "arbitrary")),
        name="lru_prompt",
    )(xbc, proj, wa, wx, lba, lbx, lam, nbw)


def _lru_sample_body(xb_ref, yb_ref, hist_ref, h0_ref, cbw_ref, cbb_ref, wa_ref, wx_ref, lba_ref,
                     lbx_ref, lam_ref, nbw_ref, ob_ref, h1_ref, hx_ref, a_ref, b_ref, h_ref,
                     *, nseq, t_len):
    hdr = SUBLANES
    nh = CONV_W - 1
    hx_ref[0:hdr, :] = jnp.zeros((hdr, D_B), F32)

    def conv_seq(si, _):
        r0 = pl.multiple_of(si * t_len, t_len)
        hx_ref[hdr - nh:hdr, :] = hist_ref[si]
        hx_ref[hdr:hdr + t_len, :] = xb_ref[pl.ds(r0, t_len), :]
        for n in range(NB):
            c0 = n * BW
            h_ref[pl.ds(r0, t_len), c0:c0 + BW] = (_conv_taps(hx_ref, cbw_ref, c0, BW, t_len, hdr)
                                                  + cbb_ref[:, c0:c0 + BW])
        return 0

    lax.fori_loop(0, nseq, conv_seq, 0)
    parts = _lru_coeffs(h_ref[...], wa_ref, wx_ref, lba_ref, lbx_ref, lam_ref)
    for n in range(NB):
        sl = slice(n * BW, (n + 1) * BW)
        a_n, mult_n, gix_n = parts[n]
        a_ref[:, sl] = a_n
        b_ref[:, sl] = mult_n * gix_n

    def seq(si, _):
        r0 = pl.multiple_of(si * t_len, t_len)
        ag, bg = _group_scan(a_ref[pl.ds(r0, t_len), :], b_ref[pl.ds(r0, t_len), :])
        hg = ag * h0_ref[pl.ds(si, 1), :] + bg
        h_ref[pl.ds(r0, t_len), :] = hg
        h1_ref[pl.ds(si, 1), :] = hg[t_len - 1:t_len, :]
        return 0

    lax.fori_loop(0, nseq, seq, 0)
    ob_ref[...] = _lru_finish(h_ref[...], yb_ref[...], nbw_ref[...])


def _lru_sample(proj, row0, bsz, t_len, hist, h0, cbw, cbb, wa, wx, lba, lbx, lam, nbw):
    assert t_len == SUBLANES
    ms = bsz * t_len
    assert row0 % ms == 0
    rblk = row0 // ms
    wspec = lambda shape: pl.BlockSpec(shape, lambda i: (0,) * len(shape))
    return pl.pallas_call(
        functools.partial(_lru_sample_body, nseq=bsz, t_len=t_len),
        grid=(1,),
        in_specs=[
            pl.BlockSpec((ms, D_B), lambda i: (rblk, COL_XB // D_B)),
            pl.BlockSpec((ms, D_B), lambda i: (rblk, COL_YB // D_B)),
            wspec((bsz, CONV_W - 1, D_B)), wspec((bsz, D_B)),
            wspec((CONV_W, D_B)), wspec((1, D_B)),
            wspec((NB, BW, BW)), wspec((NB, BW, BW)),
            wspec((1, D_B)), wspec((1, D_B)), wspec((1, D_B)), wspec((1, D_B)),
        ],
        out_specs=[wspec((ms, D_B)), wspec((bsz, D_B))],
        out_shape=[
            jax.ShapeDtypeStruct((ms, D_B), BF16),
            jax.ShapeDtypeStruct((bsz, D_B), F32),
        ],
        scratch_shapes=[pltpu.VMEM((2 * SUBLANES, D_B), F32), pltpu.VMEM((ms, D_B), F32),
                        pltpu.VMEM((ms, D_B), F32), pltpu.VMEM((ms, D_B), F32)],
        compiler_params=_cparams(("arbitrary",)),
        name="lru_sample",
    )(proj, proj, hist, h0, cbw, cbb, wa, wx, lba, lbx, lam, nbw)


def _layernorm_rows(v, g, b):
    mu = jnp.mean(v, axis=-1, keepdims=True)
    d = v - mu
    var = jnp.mean(d * d, axis=-1, keepdims=True)
    return d * lax.rsqrt(var + LN_EPS) * g + b


def _store_slabs(slab_ref, base, val):
    n = val.shape[0]
    for s in range(ROW_SLAB):
        slab_ref[pl.ds(base + s, n, stride=ROW_SLAB), :] = val[:, s * LANES:(s + 1) * LANES]


def _load_slab_chunk(slab_ref, base, n, s):
    return slab_ref[pl.ds(base + s, n, stride=ROW_SLAB), :]


def _outproj_body(oap_ref, oas_ref, obp_ref, obs_ref, xp_ref, xs_ref, wo_ref, g_ref, b_ref,
                  wr_ref, br_ref, x1_ref, ti_ref, tg_ref, *, n_ptiles):
    i = pl.program_id(0)

    def run(oa_ref, ob_ref, x_ref):
        mix = (jnp.dot(oa_ref[...], wo_ref[0:D_A, :], preferred_element_type=F32)
               + jnp.dot(ob_ref[...], wo_ref[D_A:D_A + D_B, :], preferred_element_type=F32))
        y = _layernorm_rows(DN_ALPHA * x_ref[...] + mix, g_ref[...], b_ref[...])
        _store_slabs(x1_ref, 0, y)
        logits = jnp.dot(y, wr_ref[...], precision=HIGHEST, preferred_element_type=F32) + br_ref[...]
        lane = lax.broadcasted_iota(jnp.int32, logits.shape, 1)
        lane_o = lax.broadcasted_iota(jnp.int32, ti_ref.shape, 1)
        cur = logits
        ti = jnp.zeros(ti_ref.shape, jnp.int32)
        tv = jnp.zeros(tg_ref.shape, F32)
        v0 = None
        den = None
        for kk in range(TOP_K):
            m = jnp.max(cur, axis=-1, keepdims=True)
            idx = jnp.min(jnp.where(cur == m, lane, N_EXPERTS), axis=-1, keepdims=True)
            cur = jnp.where(lane == idx, -jnp.inf, cur)
            if kk == 0:
                v0 = m
            e = jnp.exp(m - v0)
            den = e if den is None else den + e
            ti = jnp.where(lane_o == kk, idx, ti)
            tv = jnp.where(lane_o == kk, e, tv)
        ti_ref[...] = ti
        tg_ref[...] = tv / den

    @pl.when(i < n_ptiles)
    def _():
        run(oap_ref, obp_ref, xp_ref)

    @pl.when(i >= n_ptiles)
    def _():
        run(oas_ref, obs_ref, xs_ref)


def _outproj(oa_p, oa_s, ob_p, ob_s, xp2, xs2, wo, g_row, b_row, wr, br_row):
    mp, ms = xp2.shape[0], xs2.shape[0]
    tm = min(OUT_TM, ms)
    assert mp % tm == 0 and ms % tm == 0
    n_pt, n_st = mp // tm, ms // tm
    pmap = lambda i: (jnp.minimum(i, n_pt - 1), 0)
    smap = lambda i: (jnp.maximum(i - n_pt, 0), 0)
    wspec = lambda shape: pl.BlockSpec(shape, lambda i: (0, 0))
    m = mp + ms
    return pl.pallas_call(
        functools.partial(_outproj_body, n_ptiles=n_pt),
        grid=(n_pt + n_st,),
        in_specs=[
            pl.BlockSpec((tm, D_A), pmap), pl.BlockSpec((tm, D_A), smap),
            pl.BlockSpec((tm, D_B), pmap), pl.BlockSpec((tm, D_B), smap),
            pl.BlockSpec((tm, D_MODEL), pmap), pl.BlockSpec((tm, D_MODEL), smap),
            wspec((D_A + D_B, D_MODEL)), wspec((1, D_MODEL)), wspec((1, D_MODEL)),
            wspec((D_MODEL, N_EXPERTS)), wspec((1, N_EXPERTS)),
        ],
        out_specs=[
            pl.BlockSpec((tm * ROW_SLAB, LANES), lambda i: (i, 0)),
            pl.BlockSpec((tm, LANES), lambda i: (i, 0)),
            pl.BlockSpec((tm, LANES), lambda i: (i, 0)),
        ],
        out_shape=[
            jax.ShapeDtypeStruct((m * ROW_SLAB, LANES), F32),
            jax.ShapeDtypeStruct((m, LANES), jnp.int32),
            jax.ShapeDtypeStruct((m, LANES), F32),
        ],
        compiler_params=_cparams(("arbitrary",)),
        name="outproj_ln_router",
    )(oa_p, oa_s, ob_p, ob_s, xp2, xs2, wo, g_row, b_row, wr, br_row)


def _zero_tail(zbuf_ref, dst_ref, tail_row, n_blocks, sem):
    def tail_copy(t):
        z0 = pl.multiple_of((tail_row + t * MOE_RB) * ROW_SLAB, MOE_RB * ROW_SLAB)
        return pltpu.make_async_copy(zbuf_ref, dst_ref.at[pl.ds(z0, MOE_RB * ROW_SLAB)], sem)

    def start(t, _):
        tail_copy(t).start()
        return 0

    def wait(t, _):
        tail_copy(t).wait()
        return 0

    lax.fori_loop(0, n_blocks, start, 0)
    lax.fori_loop(0, n_blocks, wait, 0)


def _dispatch_body(zrow_ref, slot_ref, x_ref, xs_ref, zbuf_ref, zsem, rsem, *, tm):
    i = pl.program_id(0)

    @pl.when(i == 0)
    def _():
        zbuf_ref[...] = jnp.zeros(zbuf_ref.shape, F32)

        def zero_copy(e):
            z0 = pl.multiple_of(zrow_ref[e] * ROW_SLAB, MOE_RB * ROW_SLAB)
            return pltpu.make_async_copy(zbuf_ref, xs_ref.at[pl.ds(z0, MOE_RB * ROW_SLAB)], zsem)

        for e in range(N_EXPERTS):
            @pl.when(zrow_ref[e] >= 0)
            def _():
                zero_copy(e).start()
        for e in range(N_EXPERTS):
            @pl.when(zrow_ref[e] >= 0)
            def _():
                zero_copy(e).wait()
        _zero_tail(zbuf_ref, xs_ref, zrow_ref[N_EXPERTS], zrow_ref[N_EXPERTS + 1], zsem)

    def row_copy(r, dst):
        src0 = pl.multiple_of(r * ROW_SLAB, ROW_SLAB)
        dst0 = pl.multiple_of(dst * ROW_SLAB, ROW_SLAB)
        return pltpu.make_async_copy(x_ref.at[pl.ds(src0, ROW_SLAB)], xs_ref.at[pl.ds(dst0, ROW_SLAB)], rsem)

    def issue(r, _):
        for kk in range(TOP_K):
            row_copy(r, slot_ref[0, r * TOP_K + kk]).start()
        return 0

    lax.fori_loop(0, tm, issue, 0)

    def drain(r, _):
        for kk in range(TOP_K):
            row_copy(r, 0).wait()
        return 0

    lax.fori_loop(0, tm, drain, 0)


def _dispatch(x1, slot_tiles, zrow, n_rows):
    m = x1.shape[0] // ROW_SLAB
    tm = TOK_TM
    assert m % tm == 0
    return pl.pallas_call(
        functools.partial(_dispatch_body, tm=tm),
        grid_spec=pltpu.PrefetchScalarGridSpec(
            num_scalar_prefetch=1,
            grid=(m // tm,),
            in_specs=[
                pl.BlockSpec((None, 1, tm * TOP_K), lambda i, z: (i, 0, 0), memory_space=pltpu.SMEM),
                pl.BlockSpec((tm * ROW_SLAB, LANES), lambda i, z: (i, 0)),
            ],
            out_specs=pl.BlockSpec(memory_space=pl.ANY),
            scratch_shapes=[pltpu.VMEM((MOE_RB * ROW_SLAB, LANES), F32),
                            pltpu.SemaphoreType.DMA(()), pltpu.SemaphoreType.DMA(())],
        ),
        out_shape=jax.ShapeDtypeStruct((n_rows * ROW_SLAB, LANES), F32),
        compiler_params=_cparams(("arbitrary",)),
        name="moe_dispatch",
    )(zrow, slot_tiles, x1)


def _moe_body(ie_ref, irow_ref, insub_ref, tail_ref, xs_ref, wg_ref, wu_ref, bgate_ref, bup_ref,
              wd_ref, bd_ref, ys_ref, xstage_ref, xb_ref, act_ref, ybuf_ref, wgb_ref, wub_ref, wdb_ref,
              isem, osem):
    wi = pl.program_id(0)
    j = pl.program_id(1)
    nsub = insub_ref[wi]
    row0 = irow_ref[wi]
    nt = MOE_NT

    @pl.when(jnp.logical_and(wi == pl.num_programs(0) - 1, j == 2 * nt - 1))
    def _():
        xstage_ref[...] = jnp.zeros(xstage_ref.shape, F32)
        _zero_tail(xstage_ref, ys_ref, tail_ref[0], tail_ref[1], isem)

    @pl.when(nsub > 0)
    def _():
        @pl.when(j == 0)
        def _():
            def load(s, _):
                r0 = pl.multiple_of(s * MOE_RB, MOE_RB)
                g0 = pl.multiple_of((row0 + r0) * ROW_SLAB, MOE_RB * ROW_SLAB)
                cp = pltpu.make_async_copy(xs_ref.at[pl.ds(g0, MOE_RB * ROW_SLAB)], xstage_ref, isem)
                cp.start()
                cp.wait()
                for c in range(ROW_SLAB):
                    xb_ref[pl.ds(r0, MOE_RB), c * LANES:(c + 1) * LANES] = (
                        _load_slab_chunk(xstage_ref, 0, MOE_RB, c).astype(BF16))
                return 0

            lax.fori_loop(0, nsub, load, 0)

        @pl.when(j < nt)
        def _():
            wgb_ref[...] = wg_ref[...].astype(BF16)
            wub_ref[...] = wu_ref[...].astype(BF16)

            def sub(s, _):
                r0 = pl.multiple_of(s * MOE_RB, MOE_RB)
                x = xb_ref[pl.ds(r0, MOE_RB), :]
                hg = jnp.dot(x, wgb_ref[...], preferred_element_type=F32) + bgate_ref[...]
                hu = jnp.dot(x, wub_ref[...], preferred_element_type=F32) + bup_ref[...]
                gate = jnp.minimum(hg, SWIGLU_LIMIT)
                up = jnp.clip(hu, -SWIGLU_LIMIT, SWIGLU_LIMIT)
                glu = gate * _sigmoid(SWIGLU_ALPHA * gate)
                act_ref[j, pl.ds(r0, MOE_RB), :] = (glu * (up + 1.0)).astype(BF16)
                return 0

            lax.fori_loop(0, nsub, sub, 0)

        @pl.when(j >= nt)
        def _():
            wdb_ref[...] = wd_ref[...].astype(BF16)
            n = j - nt

            def sub(s, _):
                r0 = pl.multiple_of(s * MOE_RB, MOE_RB)
                acc = jnp.dot(act_ref[0, pl.ds(r0, MOE_RB), :], wdb_ref[0:MOE_TF, :],
                              preferred_element_type=F32)
                for jj in range(1, nt):
                    acc = acc + jnp.dot(act_ref[jj, pl.ds(r0, MOE_RB), :],
                                        wdb_ref[jj * MOE_TF:(jj + 1) * MOE_TF, :],
                                        preferred_element_type=F32)
                acc = acc + bd_ref[...]
                cpt = MOE_TF // LANES
                for cc in range(cpt):
                    ybuf_ref[pl.ds(r0 * ROW_SLAB + n * cpt + cc, MOE_RB, stride=ROW_SLAB), :] = (
                        acc[:, cc * LANES:(cc + 1) * LANES])
                return 0

            lax.fori_loop(0, nsub, sub, 0)

        @pl.when(j == 2 * nt - 1)
        def _():
            def out_copy(s):
                b0 = pl.multiple_of(s * MOE_RB * ROW_SLAB, MOE_RB * ROW_SLAB)
                g0 = pl.multiple_of(row0 * ROW_SLAB, MOE_RB * ROW_SLAB) + b0
                return pltpu.make_async_copy(ybuf_ref.at[pl.ds(b0, MOE_RB * ROW_SLAB)],
                                             ys_ref.at[pl.ds(g0, MOE_RB * ROW_SLAB)], osem)

            def store(s, _):
                out_copy(s).start()
                return 0

            def drain(s, _):
                out_copy(s).wait()
                return 0

            lax.fori_loop(0, nsub, store, 0)
            lax.fori_loop(0, nsub, drain, 0)


def _moe(xs, item_e, item_row, item_nsub, tail, w_up, b_up3, w_down, b_down3):
    n_items = item_e.shape[0]
    n_rows = xs.shape[0]
    nt = MOE_NT
    up_off = D_FF // MOE_TF
    p1 = lambda j, n, w: jnp.where(n[w] > 0, jnp.minimum(j, nt - 1), nt - 1)
    p2 = lambda j, n, w: jnp.where(n[w] > 0, jnp.maximum(j - nt, 0), nt - 1)
    return pl.pallas_call(
        _moe_body,
        grid_spec=pltpu.PrefetchScalarGridSpec(
            num_scalar_prefetch=4,
            grid=(n_items, 2 * nt),
            in_specs=[
                pl.BlockSpec(memory_space=pl.ANY),
                pl.BlockSpec((None, D_MODEL, MOE_TF), lambda w, j, e, r, n, t: (e[w], 0, p1(j, n, w))),
                pl.BlockSpec((None, D_MODEL, MOE_TF),
                             lambda w, j, e, r, n, t: (e[w], 0, up_off + p1(j, n, w))),
                pl.BlockSpec((None, 1, MOE_TF), lambda w, j, e, r, n, t: (e[w], 0, p1(j, n, w))),
                pl.BlockSpec((None, 1, MOE_TF), lambda w, j, e, r, n, t: (e[w], 0, up_off + p1(j, n, w))),
                pl.BlockSpec((None, D_FF, MOE_TF), lambda w, j, e, r, n, t: (e[w], 0, p2(j, n, w))),
                pl.BlockSpec((None, 1, MOE_TF), lambda w, j, e, r, n, t: (e[w], 0, p2(j, n, w))),
            ],
            out_specs=pl.BlockSpec(memory_space=pl.ANY),
            scratch_shapes=[
                pltpu.VMEM((MOE_RB * ROW_SLAB, LANES), F32),
                pltpu.VMEM((MOE_R, D_MODEL), BF16),
                pltpu.VMEM((nt, MOE_R, MOE_TF), BF16),
                pltpu.VMEM((MOE_R * ROW_SLAB, LANES), F32),
                pltpu.VMEM((D_MODEL, MOE_TF), BF16),
                pltpu.VMEM((D_MODEL, MOE_TF), BF16),
                pltpu.VMEM((D_FF, MOE_TF), BF16),
                pltpu.SemaphoreType.DMA(()), pltpu.SemaphoreType.DMA(()),
            ],
        ),
        out_shape=jax.ShapeDtypeStruct((n_rows, LANES), F32),
        compiler_params=_cparams(("arbitrary", "arbitrary")),
        name="moe_experts",
    )(item_e, item_row, item_nsub, tail, xs, w_up, w_up, b_up3, b_up3, w_down, b_down3)


def _combine_body(slot_ref, ys_ref, x1_ref, gate_ref, g_ref, b_ref, y_ref, buf_ref, v_ref, sem, *, tm):
    def row_copy(r, kk, src):
        src0 = pl.multiple_of(src * ROW_SLAB, ROW_SLAB)
        dst0 = pl.multiple_of(r * ROW_SLAB, ROW_SLAB)
        return pltpu.make_async_copy(ys_ref.at[pl.ds(src0, ROW_SLAB)],
                                     buf_ref.at[kk, pl.ds(dst0, ROW_SLAB)], sem)

    def issue(r, _):
        for kk in range(TOP_K):
            row_copy(r, kk, slot_ref[0, r * TOP_K + kk]).start()
        return 0

    lax.fori_loop(0, tm, issue, 0)

    def drain(r, _):
        for kk in range(TOP_K):
            row_copy(r, kk, 0).wait()
        return 0

    lax.fori_loop(0, tm, drain, 0)
    gates = gate_ref[...]
    for c in range(ROW_SLAB):
        acc = DN_ALPHA * _load_slab_chunk(x1_ref, 0, tm, c)
        for kk in range(TOP_K):
            acc = acc + gates[:, kk:kk + 1] * _load_slab_chunk(buf_ref.at[kk], 0, tm, c)
        v_ref[:, c * LANES:(c + 1) * LANES] = acc
    y_ref[...] = _layernorm_rows(v_ref[...], g_ref[...], b_ref[...])


def _combine(ys, slot_tiles, x1, gates, g_row, b_row):
    m = x1.shape[0] // ROW_SLAB
    tm = TOK_TM
    wspec = lambda shape: pl.BlockSpec(shape, lambda i: (0, 0))
    return pl.pallas_call(
        functools.partial(_combine_body, tm=tm),
        grid=(m // tm,),
        in_specs=[
            pl.BlockSpec((None, 1, tm * TOP_K), lambda i: (i, 0, 0), memory_space=pltpu.SMEM),
            pl.BlockSpec(memory_space=pl.ANY),
            pl.BlockSpec((tm * ROW_SLAB, LANES), lambda i: (i, 0)),
            pl.BlockSpec((tm, LANES), lambda i: (i, 0)),
            wspec((1, D_MODEL)), wspec((1, D_MODEL)),
        ],
        out_specs=pl.BlockSpec((tm, D_MODEL), lambda i: (i, 0)),
        out_shape=jax.ShapeDtypeStruct((m, D_MODEL), F32),
        scratch_shapes=[pltpu.VMEM((TOP_K, tm * ROW_SLAB, LANES), F32),
                        pltpu.VMEM((tm, D_MODEL), F32), pltpu.SemaphoreType.DMA(())],
        compiler_params=_cparams(("arbitrary",)),
        name="moe_combine_ln",
    )(slot_tiles, ys, x1, gates, g_row, b_row)


def _routing_tables(top_i, m):
    e_ids = jnp.arange(N_EXPERTS, dtype=jnp.int32)
    onehot = (top_i[:, :, None] == e_ids[None, None, :]).astype(jnp.int32)
    mask = jnp.sum(onehot, axis=1)
    incl = jnp.cumsum(mask, axis=0)
    pos = incl - mask
    counts = incl[-1]
    nsub = (counts + MOE_RB - 1) // MOE_RB
    padded = nsub * MOE_RB
    gend = jnp.cumsum(padded)
    gstart = gend - padded
    slot = jnp.sum(onehot * (gstart[None, None, :] + pos[:, None, :]), axis=2)
    n_rows = (m * TOP_K // MOE_RB + N_EXPERTS) * MOE_RB
    tail = jnp.stack([gend[-1], (n_rows - gend[-1]) // MOE_RB]).astype(jnp.int32)
    zrow = jnp.concatenate([jnp.where(counts > 0, gend - MOE_RB, -1).astype(jnp.int32), tail])
    n_items = N_EXPERTS + (m * TOP_K // MOE_RB + N_EXPERTS) // MOE_SB
    ipe = (nsub + MOE_SB - 1) // MOE_SB
    iend = jnp.cumsum(ipe)
    istart = iend - ipe
    total = iend[-1]
    wid = jnp.arange(n_items, dtype=jnp.int32)
    wclamp = jnp.minimum(wid, total - 1)
    ie = jnp.minimum(jnp.searchsorted(iend, wclamp, side="right"), N_EXPERTS - 1).astype(jnp.int32)
    jn = wclamp - istart[ie]
    irow = (gstart[ie] + jn * MOE_R).astype(jnp.int32)
    insub = jnp.where(wid < total, jnp.clip(nsub[ie] - jn * MOE_SB, 0, MOE_SB), 0).astype(jnp.int32)
    return slot.astype(jnp.int32), zrow, tail, ie, irow, insub, n_rows


def kernel(x_prompt, x_sample, state_conv_a, state_delta, state_conv_b, state_lru, w_in, conv_a_w,
           a_log, dt_bias, norm_a_w, conv_b_w, conv_b_b, lru_wa, lru_ba, lru_wx, lru_bx, lru_lambda,
           norm_b_w, w_out, ln1_g, ln1_b, w_router, b_router, w_up, b_up, w_down, b_down, ln2_g, ln2_b):
    bp, tp, _ = x_prompt.shape
    bs, ts, _ = x_sample.shape
    mp, ms = bp * tp, bs * ts
    m = mp + ms
    l = 0
    xp2 = x_prompt.reshape(mp, D_MODEL)
    xs2 = x_sample.reshape(ms, D_MODEL)

    wi = w_in[l]
    c_qkvz = 4 * D_A
    wp = jnp.concatenate(
        [wi[:, :c_qkvz], wi[:, c_qkvz + 2 * H_A:], wi[:, c_qkvz:c_qkvz + 2 * H_A],
         jnp.zeros((D_MODEL, 2 * LANES - 2 * H_A), F32)], axis=1).astype(BF16)
    wo = w_out[l].astype(BF16)
    pad_h = lambda v: jnp.zeros((1, LANES), F32).at[0, H_A:2 * H_A].set(v)
    alog_row = pad_h(a_log[l])
    dtb_row = pad_h(dt_bias[l])
    row = lambda v: v.reshape(1, -1)

    proj = _inproj(xp2, xs2, wp)

    qkv_act, xbc, bg = _convact_prompt(proj, bp, tp, conv_a_w[l], conv_b_w[l], row(conv_b_b[l]),
                                       alog_row, dtb_row)
    u, w, qe, kdt, aqk, el = _delta_prep(qkv_act, bg, bp, tp)
    oa_p, sd_p = _delta_seq(u, w, qe, kdt, aqk, el, proj, row(norm_a_w[l]), bp, tp)
    ob_p, h_p = _lru_prompt(xbc, proj, bp, tp, lru_wa[l], lru_wx[l], row(lru_ba[l]), row(lru_bx[l]),
                            row(lru_lambda[l]), row(norm_b_w[l]))
    oa_s, sd_s = _sample_a(proj, mp, bs, ts, state_conv_a[l], state_delta[l], conv_a_w[l],
                           alog_row, dtb_row, row(norm_a_w[l]))
    ob_s, h_s = _lru_sample(proj, mp, bs, ts, state_conv_b[l], state_lru[l], conv_b_w[l],
                            row(conv_b_b[l]), lru_wa[l], lru_wx[l], row(lru_ba[l]), row(lru_bx[l]),
                            row(lru_lambda[l]), row(norm_b_w[l]))

    x1, ti, tg = _outproj(oa_p, oa_s, ob_p, ob_s, xp2, xs2, wo, row(ln1_g[l]), row(ln1_b[l]),
                          w_router[l], row(b_router[l]))

    slot, zrow, tail, ie, irow, insub, n_rows = _routing_tables(ti[:, :TOP_K], m)
    slot_tiles = slot.reshape(m // TOK_TM, 1, TOK_TM * TOP_K)
    xs_sorted = _dispatch(x1, slot_tiles, zrow, n_rows)
    ys = _moe(xs_sorted, ie, irow, insub, tail, w_up[l], b_up[l].reshape(N_EXPERTS, 1, 2 * D_FF),
              w_down[l], b_down[l].reshape(N_EXPERTS, 1, D_MODEL))
    y = _combine(ys, slot_tiles, x1, tg, row(ln2_g[l]), row(ln2_b[l]))

    y_prompt = y[:mp].reshape(bp, tp, D_MODEL)
    y_sample = y[mp:].reshape(bs, ts, D_MODEL)
    nh = CONV_W - 1
    pp = proj[:mp].reshape(bp, tp, N_PROJ)
    ps = proj[mp:].reshape(bs, ts, N_PROJ)
    ca_p = pp[:, tp - nh:, COL_QKV:COL_QKV + 3 * D_A]
    cb_p = pp[:, tp - nh:, COL_XB:COL_XB + D_B]
    ca_s = ps[:, ts - nh:, COL_QKV:COL_QKV + 3 * D_A]
    cb_s = ps[:, ts - nh:, COL_XB:COL_XB + D_B]
    return (y_prompt, y_sample,
            ca_p[None], sd_p[None], cb_p[None], h_p.reshape(1, bp, D_B),
            ca_s[None], sd_s[None], cb_s[None], h_s[None])
```

```python
import functools
import math

import jax
import jax.numpy as jnp
from jax import lax
from jax.experimental import pallas as pl
from jax.experimental.pallas import tpu as pltpu

F32 = jnp.float32
BF16 = jnp.bfloat16
HIGHEST = lax.Precision.HIGHEST

D_MODEL = 2048
D_A = 1024
H_A = 8
DK = 128
DV = 128
CONV_W = 4
D_B = 1024
NB = 8
BW = 128
LRU_C = 8.0
N_EXPERTS = 32
TOP_K = 4
D_FF = 2048
SWIGLU_LIMIT = 7.0
SWIGLU_ALPHA = 1.702
DEPTH = 1
DN_ALPHA = (2.0 * DEPTH) ** 0.25
LN_EPS = 1e-5
RMS_EPS = 1e-6
L2_EPS = 1e-6

LANES = 128
SUBLANES = 8
VMEM_LIMIT_BYTES = 56 * 1024 * 1024

COL_QKV = 0
COL_Z = 3 * D_A
COL_XB = 4 * D_A
COL_YB = 4 * D_A + D_B
COL_BA = 4 * D_A + 2 * D_B
N_PROJ = COL_BA + 2 * LANES

INPROJ_TM = 512
INPROJ_TN = 1280
CONV_L = 256
DELTA_C = 128
OUT_TM = 256
MOE_RB = 256
MOE_SB = 6
MOE_R = MOE_RB * MOE_SB
MOE_TF = 256
MOE_NT = D_FF // MOE_TF
TOK_TM = 256
ROW_SLAB = D_MODEL // LANES


def _cparams(sem, vmem=VMEM_LIMIT_BYTES):
    return pltpu.CompilerParams(dimension_semantics=sem, vmem_limit_bytes=vmem)


def _sigmoid(x):
    return 1.0 / (1.0 + jnp.exp(-x))


def _softplus(x):
    return jnp.maximum(x, 0.0) + jnp.log(1.0 + jnp.exp(-jnp.abs(x)))


def _inproj_body(xp_ref, xs_ref, w_ref, o_ref, xb_ref, *, n_ptiles):
    i = pl.program_id(0)

    @pl.when(pl.program_id(1) == 0)
    def _():
        @pl.when(i < n_ptiles)
        def _():
            xb_ref[...] = xp_ref[...].astype(BF16)

        @pl.when(i >= n_ptiles)
        def _():
            xb_ref[...] = xs_ref[...].astype(BF16)

    o_ref[...] = jnp.dot(xb_ref[...], w_ref[...], preferred_element_type=F32)


def _inproj(xp2, xs2, wp):
    mp, ms = xp2.shape[0], xs2.shape[0]
    tm = min(INPROJ_TM, ms)
    assert mp % tm == 0 and ms % tm == 0
    n_pt, n_st = mp // tm, ms // tm
    n_nt = N_PROJ // INPROJ_TN
    return pl.pallas_call(
        functools.partial(_inproj_body, n_ptiles=n_pt),
        grid=(n_pt + n_st, n_nt),
        in_specs=[
            pl.BlockSpec((tm, D_MODEL), lambda i, j: (jnp.minimum(i, n_pt - 1), 0)),
            pl.BlockSpec((tm, D_MODEL), lambda i, j: (jnp.maximum(i - n_pt, 0), 0)),
            pl.BlockSpec((D_MODEL, INPROJ_TN), lambda i, j: (0, j)),
        ],
        out_specs=pl.BlockSpec((tm, INPROJ_TN), lambda i, j: (i, j)),
        out_shape=jax.ShapeDtypeStruct((mp + ms, N_PROJ), F32),
        scratch_shapes=[pltpu.VMEM((tm, D_MODEL), BF16)],
        compiler_params=_cparams(("arbitrary", "arbitrary")),
        name="inproj",
    )(xp2, xs2, wp)


def _conv_taps(h_ref, w_ref, c0, width, rows, base):
    acc = h_ref[base:base + rows, c0:c0 + width] * w_ref[CONV_W - 1:CONV_W, c0:c0 + width]
    for j in range(1, CONV_W):
        acc = acc + (h_ref[base - j:base - j + rows, c0:c0 + width]
                     * w_ref[CONV_W - 1 - j:CONV_W - j, c0:c0 + width])
    return acc


def _qkv_activation(acc, blk):
    a = acc * _sigmoid(acc)
    if blk < 2 * H_A:
        a = a * lax.rsqrt(jnp.sum(a * a, axis=-1, keepdims=True) + L2_EPS)
        if blk < H_A:
            a = a * (DK ** -0.5)
    return a


def _beta_g(ba, alog_ref, dtb_ref):
    lane = lax.broadcasted_iota(jnp.int32, ba.shape, 1)
    beta = _sigmoid(ba)
    g = -jnp.exp(alog_ref[...]) * _softplus(ba + dtb_ref[...])
    return jnp.where(lane < H_A, beta, jnp.where(lane < 2 * H_A, g, 0.0))


def _convact_prompt_body(qkv_ref, xb_ref, ba_ref, caw_ref, cbw_ref, cbb_ref, alog_ref, dtb_ref,
                         qkvo_ref, xbo_ref, bgo_ref, hq_ref, hx_ref, *, rows):
    t = pl.program_id(1)
    hdr = SUBLANES

    @pl.when(t == 0)
    def _():
        hq_ref[0:hdr, :] = jnp.zeros((hdr, 3 * D_A), F32)
        hx_ref[0:hdr, :] = jnp.zeros((hdr, D_B), F32)

    @pl.when(t > 0)
    def _():
        hq_ref[0:hdr, :] = hq_ref[rows:rows + hdr, :]
        hx_ref[0:hdr, :] = hx_ref[rows:rows + hdr, :]

    hq_ref[hdr:hdr + rows, :] = qkv_ref[...]
    hx_ref[hdr:hdr + rows, :] = xb_ref[...]
    for blk in range(3 * H_A):
        c0 = blk * DK
        acc = _conv_taps(hq_ref, caw_ref, c0, DK, rows, hdr)
        qkvo_ref[:, c0:c0 + DK] = _qkv_activation(acc, blk)
    for blk in range(NB):
        c0 = blk * BW
        xbo_ref[:, c0:c0 + BW] = _conv_taps(hx_ref, cbw_ref, c0, BW, rows, hdr) + cbb_ref[:, c0:c0 + BW]
    bgo_ref[...] = _beta_g(ba_ref[...], alog_ref, dtb_ref)


def _convact_prompt(proj, bsz, t_len, caw, cbw, cbb, alog_row, dtb_row):
    rows = min(CONV_L, t_len)
    assert t_len % rows == 0
    nt = t_len // rows
    mp = bsz * t_len
    wspec = lambda shape: pl.BlockSpec(shape, lambda b, t: (0, 0))
    return pl.pallas_call(
        functools.partial(_convact_prompt_body, rows=rows),
        grid=(bsz, nt),
        in_specs=[
            pl.BlockSpec((rows, 3 * D_A), lambda b, t: (b * nt + t, COL_QKV // (3 * D_A))),
            pl.BlockSpec((rows, D_B), lambda b, t: (b * nt + t, COL_XB // D_B)),
            pl.BlockSpec((rows, LANES), lambda b, t: (b * nt + t, COL_BA // LANES)),
            wspec((CONV_W, 3 * D_A)), wspec((CONV_W, D_B)), wspec((1, D_B)),
            wspec((1, LANES)), wspec((1, LANES)),
        ],
        out_specs=[
            pl.BlockSpec((rows, 3 * D_A), lambda b, t: (b * nt + t, 0)),
            pl.BlockSpec((rows, D_B), lambda b, t: (b * nt + t, 0)),
            pl.BlockSpec((rows, LANES), lambda b, t: (b * nt + t, 0)),
        ],
        out_shape=[
            jax.ShapeDtypeStruct((mp, 3 * D_A), F32),
            jax.ShapeDtypeStruct((mp, D_B), F32),
            jax.ShapeDtypeStruct((mp, LANES), F32),
        ],
        scratch_shapes=[pltpu.VMEM((SUBLANES + rows + SUBLANES, 3 * D_A), F32),
                        pltpu.VMEM((SUBLANES + rows + SUBLANES, D_B), F32)],
        compiler_params=_cparams(("arbitrary", "arbitrary")),
        name="convact_prompt",
    )(proj, proj, proj, caw, cbw, cbb, alog_row, dtb_row)


def _dot_nt(a, b, precision=None):
    return lax.dot_general(a, b, (((1,), (1,)), ((), ())), precision=precision,
                           preferred_element_type=F32)


def _split_bf16(x):
    hi = x.astype(BF16)
    return hi, (x - hi.astype(F32)).astype(BF16)


def _dot3(a, b):
    ah, al = _split_bf16(a)
    bh, bl = _split_bf16(b)
    return (jnp.dot(ah, bh, preferred_element_type=F32)
            + (jnp.dot(al, bh, preferred_element_type=F32) + jnp.dot(ah, bl, preferred_element_type=F32)))


def _delta_prep_body(qkv_ref, bg_ref, u_ref, w_ref, qe_ref, kdt_ref, aqk_ref, el_ref, *, c):
    row = lax.broadcasted_iota(jnp.int32, (c, c), 0)
    col = lax.broadcasted_iota(jnp.int32, (c, c), 1)
    incl = row >= col
    strict = row > col
    eye = (row == col).astype(F32)
    tril = incl.astype(F32)
    eye_l = (lax.broadcasted_iota(jnp.int32, (LANES, LANES), 0)
             == lax.broadcasted_iota(jnp.int32, (LANES, LANES), 1)).astype(F32)

    bg = bg_ref[...]
    gc_all = jnp.dot(tril, bg, precision=HIGHEST, preferred_element_type=F32)
    gc_t = _dot_nt(eye_l, gc_all, precision=HIGHEST)
    for h in range(H_A):
        q = qkv_ref[:, h * DK:(h + 1) * DK]
        k = qkv_ref[:, (H_A + h) * DK:(H_A + h + 1) * DK]
        v = qkv_ref[:, (2 * H_A + h) * DK:(2 * H_A + h + 1) * DK]
        beta = bg[:, h:h + 1]
        gc = gc_all[:, H_A + h:H_A + h + 1]
        g_last = gc_all[c - 1:c, H_A + h:H_A + h + 1]
        diff = gc - gc_t[H_A + h:H_A + h + 1, :]
        decay = jnp.where(incl, jnp.exp(jnp.where(incl, diff, 0.0)), 0.0)
        kb = k * beta
        egc = jnp.exp(gc)
        kbf = k.astype(BF16)
        kk = _dot_nt(kbf, kbf)
        lmat = jnp.where(strict, kk * beta * decay, 0.0)
        rhs = jnp.concatenate([v * beta, kb * egc], axis=1)
        xb = (-lmat).astype(BF16)
        t0 = eye - lmat
        p = 2
        while p < c:
            xb = jnp.dot(xb, xb, preferred_element_type=F32).astype(BF16)
            t0 = t0 + jnp.dot(t0.astype(BF16), xb, preferred_element_type=F32)
            p *= 2
        t0b = t0.astype(BF16)
        sol0 = jnp.dot(t0b, rhs.astype(BF16), preferred_element_type=F32)
        resid = rhs - sol0 - _dot3(lmat, sol0)
        sol = sol0 + jnp.dot(t0b, resid.astype(BF16), preferred_element_type=F32)
        u_ref[:, h * DV:(h + 1) * DV] = sol[:, :DV]
        w_ref[:, h * DK:(h + 1) * DK] = sol[:, DV:].astype(BF16)
        qe_ref[:, h * DK:(h + 1) * DK] = (q * egc).astype(BF16)
        aqk_ref[h] = (_dot_nt(q.astype(BF16), kbf) * decay).astype(BF16)
        kdec = k * jnp.exp(g_last - gc)
        kdt_ref[h] = _dot_nt(eye_l.astype(BF16), kdec.astype(BF16)).astype(BF16)
        el_ref[h] = jnp.broadcast_to(jnp.exp(g_last), (SUBLANES, LANES))


def _delta_prep(qkv_act, bg, bsz, t_len):
    c = DELTA_C
    assert t_len % c == 0
    nc = t_len // c
    mp = bsz * t_len
    n_chunks = bsz * nc
    rowspec = lambda w: pl.BlockSpec((c, w), lambda i: (i, 0))
    return pl.pallas_call(
        functools.partial(_delta_prep_body, c=c),
        grid=(n_chunks,),
        in_specs=[rowspec(3 * D_A), rowspec(LANES)],
        out_specs=[
            rowspec(D_A), rowspec(D_A), rowspec(D_A),
            pl.BlockSpec((None, H_A, DK, c), lambda i: (i, 0, 0, 0)),
            pl.BlockSpec((None, H_A, c, c), lambda i: (i, 0, 0, 0)),
            pl.BlockSpec((None, H_A, SUBLANES, LANES), lambda i: (i, 0, 0, 0)),
        ],
        out_shape=[
            jax.ShapeDtypeStruct((mp, D_A), F32),
            jax.ShapeDtypeStruct((mp, D_A), BF16),
            jax.ShapeDtypeStruct((mp, D_A), BF16),
            jax.ShapeDtypeStruct((n_chunks, H_A, DK, c), BF16),
            jax.ShapeDtypeStruct((n_chunks, H_A, c, c), BF16),
            jax.ShapeDtypeStruct((n_chunks, H_A, SUBLANES, LANES), F32),
        ],
        compiler_params=_cparams(("arbitrary",)),
        name="delta_prep",
    )(qkv_act, bg)


def _gated_rmsnorm(o, z, nw):
    on = o * lax.rsqrt(jnp.mean(o * o, axis=-1, keepdims=True) + RMS_EPS) * nw
    return on * (z * _sigmoid(z))


def _delta_seq_body(u_ref, w_ref, qe_ref, kdt_ref, aqk_ref, el_ref, z_ref, nw_ref,
                    oa_ref, sfin_ref, s_ref):
    ci = pl.program_id(1)

    @pl.when(ci == 0)
    def _():
        s_ref[...] = jnp.zeros(s_ref.shape, F32)

    nw = nw_ref[...]
    for h in range(H_A):
        s = s_ref[h]
        sb = s.astype(BF16)
        sl = slice(h * DK, (h + 1) * DK)
        v_new = u_ref[:, sl] - jnp.dot(w_ref[:, sl], sb, preferred_element_type=F32)
        vb = v_new.astype(BF16)
        o = (jnp.dot(qe_ref[:, sl], sb, preferred_element_type=F32)
             + jnp.dot(aqk_ref[h], vb, preferred_element_type=F32))
        s_new = s * el_ref[h][0:1, 0:1] + jnp.dot(kdt_ref[h], vb, preferred_element_type=F32)
        s_ref[h] = s_new
        oa_ref[:, sl] = _gated_rmsnorm(o, z_ref[:, sl], nw).astype(BF16)

    @pl.when(ci == pl.num_programs(1) - 1)
    def _():
        sfin_ref[...] = s_ref[...]


def _delta_seq(u, w, qe, kdt, aqk, el, proj, nw_row, bsz, t_len):
    c = DELTA_C
    nc = t_len // c
    mp = bsz * t_len
    rowspec = lambda wd: pl.BlockSpec((c, wd), lambda b, i: (b * nc + i, 0))
    chunkspec = lambda a, bb: pl.BlockSpec((None, H_A, a, bb), lambda b, i: (b * nc + i, 0, 0, 0))
    return pl.pallas_call(
        _delta_seq_body,
        grid=(bsz, nc),
        in_specs=[
            rowspec(D_A), rowspec(D_A), rowspec(D_A),
            chunkspec(DK, c), chunkspec(c, c), chunkspec(SUBLANES, LANES),
            pl.BlockSpec((c, D_A), lambda b, i: (b * nc + i, COL_Z // D_A)),
            pl.BlockSpec((1, DV), lambda b, i: (0, 0)),
        ],
        out_specs=[
            rowspec(D_A),
            pl.BlockSpec((None, H_A, DK, DV), lambda b, i: (b, 0, 0, 0)),
        ],
        out_shape=[
            jax.ShapeDtypeStruct((mp, D_A), BF16),
            jax.ShapeDtypeStruct((bsz, H_A, DK, DV), F32),
        ],
        scratch_shapes=[pltpu.VMEM((H_A, DK, DV), F32)],
        compiler_params=_cparams(("arbitrary", "arbitrary")),
        name="delta_seq",
    )(u, w, qe, kdt, aqk, el, proj, nw_row)


def _sample_a_body(qkv_ref, z_ref, ba_ref, hist_ref, s0_ref, caw_ref, alog_ref, dtb_ref, nw_ref,
                   oa_ref, s1_ref, hq_ref, *, t_len):
    hdr = SUBLANES
    nh = CONV_W - 1
    hq_ref[0:hdr, :] = jnp.zeros((hdr, 3 * D_A), F32)
    hq_ref[hdr - nh:hdr, :] = hist_ref[...]
    hq_ref[hdr:hdr + t_len, :] = qkv_ref[...]
    bg = _beta_g(ba_ref[...], alog_ref, dtb_ref)
    rowi = lax.broadcasted_iota(jnp.int32, bg.shape, 0)
    gc_all = bg
    sft = 1
    while sft < t_len:
        gc_all = gc_all + jnp.where(rowi >= sft, pltpu.roll(gc_all, sft, axis=0), 0.0)
        sft *= 2
    nw = nw_ref[...]
    rowc = lax.broadcasted_iota(jnp.int32, (t_len, 1), 0)
    for h in range(H_A):
        q = _qkv_activation(_conv_taps(hq_ref, caw_ref, h * DK, DK, t_len, hdr), h)
        k = _qkv_activation(_conv_taps(hq_ref, caw_ref, (H_A + h) * DK, DK, t_len, hdr), H_A + h)
        v = _qkv_activation(_conv_taps(hq_ref, caw_ref, (2 * H_A + h) * DK, DK, t_len, hdr), 2 * H_A + h)
        beta = bg[:, h:h + 1]
        gc = gc_all[:, H_A + h:H_A + h + 1]
        g_last = gc[t_len - 1:t_len, :]
        kb = k * beta
        sol = jnp.concatenate([v * beta, kb * jnp.exp(gc)], axis=1)
        acols = []
        lcols = []
        for j in range(t_len):
            kj = k[j:j + 1, :]
            gj = gc[j:j + 1, :]
            dcol = jnp.exp(jnp.where(rowc >= j, gc - gj, 0.0))
            acols.append(jnp.where(rowc >= j, jnp.sum(q * kj, axis=-1, keepdims=True) * dcol, 0.0))
            lcols.append(jnp.where(rowc > j, jnp.sum(kb * kj, axis=-1, keepdims=True) * dcol, 0.0))
        for j in range(t_len - 1):
            sol = sol - lcols[j] * sol[j:j + 1, :]
        u = sol[:, :DV]
        w = sol[:, DV:]
        s = s0_ref[h]
        sb = s.astype(BF16)
        v_new = u - jnp.dot(w.astype(BF16), sb, preferred_element_type=F32)
        o = jnp.dot((q * jnp.exp(gc)).astype(BF16), sb, preferred_element_type=F32)
        for j in range(t_len):
            o = o + acols[j] * v_new[j:j + 1, :]
        kdec = k * jnp.exp(g_last - gc)
        upd = lax.dot_general(kdec.astype(BF16), v_new.astype(BF16), (((0,), (0,)), ((), ())),
                              preferred_element_type=F32)
        s1_ref[h] = s * jnp.exp(g_last) + upd
        sl = slice(h * DV, (h + 1) * DV)
        oa_ref[:, sl] = _gated_rmsnorm(o, z_ref[:, sl], nw).astype(BF16)


def _sample_a(proj, row0, bsz, t_len, hist, s0, caw, alog_row, dtb_row, nw_row):
    assert t_len == SUBLANES and row0 % t_len == 0
    r0 = row0 // t_len
    wspec = lambda shape: pl.BlockSpec(shape, lambda b: (0,) * len(shape))
    return pl.pallas_call(
        functools.partial(_sample_a_body, t_len=t_len),
        grid=(bsz,),
        in_specs=[
            pl.BlockSpec((t_len, 3 * D_A), lambda b: (r0 + b, COL_QKV // (3 * D_A))),
            pl.BlockSpec((t_len, D_A), lambda b: (r0 + b, COL_Z // D_A)),
            pl.BlockSpec((t_len, LANES), lambda b: (r0 + b, COL_BA // LANES)),
            pl.BlockSpec((None, CONV_W - 1, 3 * D_A), lambda b: (b, 0, 0)),
            pl.BlockSpec((None, H_A, DK, DV), lambda b: (b, 0, 0, 0)),
            wspec((CONV_W, 3 * D_A)), wspec((1, LANES)), wspec((1, LANES)), wspec((1, DV)),
        ],
        out_specs=[
            pl.BlockSpec((t_len, D_A), lambda b: (b, 0)),
            pl.BlockSpec((None, H_A, DK, DV), lambda b: (b, 0, 0, 0)),
        ],
        out_shape=[
            jax.ShapeDtypeStruct((bsz * t_len, D_A), BF16),
            jax.ShapeDtypeStruct((bsz, H_A, DK, DV), F32),
        ],
        scratch_shapes=[pltpu.VMEM((2 * SUBLANES, 3 * D_A), F32)],
        compiler_params=_cparams(("arbitrary",)),
        name="sample_a",
    )(proj, proj, proj, hist, s0, caw, alog_row, dtb_row, nw_row)


def _lru_coeffs(xc, wa_ref, wx_ref, lba_ref, lbx_ref, lam_ref):
    parts = []
    for n in range(NB):
        sl = slice(n * BW, (n + 1) * BW)
        xn = xc[:, sl]
        xnb = xn.astype(BF16)
        gr = _sigmoid(jnp.dot(xnb, wa_ref[n].astype(BF16), preferred_element_type=F32) + lba_ref[:, sl])
        gi = _sigmoid(jnp.dot(xnb, wx_ref[n].astype(BF16), preferred_element_type=F32) + lbx_ref[:, sl])
        log_a = -LRU_C * gr * _softplus(-lam_ref[:, sl])
        parts.append((jnp.exp(log_a), jnp.sqrt(1.0 - jnp.exp(2.0 * log_a)), gi * xn))
    return parts


def _group_scan(a, b):
    rowi = lax.broadcasted_iota(jnp.int32, a.shape, 0)
    sft = 1
    while sft < SUBLANES:
        keep = rowi >= sft
        b = b + a * jnp.where(keep, pltpu.roll(b, sft, axis=0), 0.0)
        a = a * jnp.where(keep, pltpu.roll(a, sft, axis=0), 1.0)
        sft *= 2
    return a, b


def _gelu_tanh(x):
    return 0.5 * x * (1.0 + jnp.tanh(math.sqrt(2.0 / math.pi) * (x + 0.044715 * x * x * x)))


def _lru_finish(h, yb, nbw):
    hg = h * _gelu_tanh(yb)
    return (hg * lax.rsqrt(jnp.mean(hg * hg, axis=-1, keepdims=True) + RMS_EPS) * nbw).astype(BF16)


def _lru_prompt_body(xc_ref, yb_ref, wa_ref, wx_ref, lba_ref, lbx_ref, lam_ref, nbw_ref,
                     ob_ref, hfin_ref, a_ref, b_ref, h_ref, carry_ref, *, rows):
    t = pl.program_id(1)
    parts = _lru_coeffs(xc_ref[...], wa_ref, wx_ref, lba_ref, lbx_ref, lam_ref)
    rowi = lax.broadcasted_iota(jnp.int32, (rows, BW), 0)
    first = jnp.logical_and(rowi == 0, t == 0)
    for n in range(NB):
        sl = slice(n * BW, (n + 1) * BW)
        a_n, mult_n, gix_n = parts[n]
        a_ref[:, sl] = jnp.where(first, 0.0, a_n)
        b_ref[:, sl] = jnp.where(first, 1.0, mult_n) * gix_n

    @pl.when(t == 0)
    def _():
        carry_ref[...] = jnp.zeros(carry_ref.shape, F32)

    def group(gidx, carry):
        r0 = pl.multiple_of(gidx * SUBLANES, SUBLANES)
        ag, bg = _group_scan(a_ref[pl.ds(r0, SUBLANES), :], b_ref[pl.ds(r0, SUBLANES), :])
        hg = ag * carry + bg
        h_ref[pl.ds(r0, SUBLANES), :] = hg
        return jnp.broadcast_to(hg[SUBLANES - 1:SUBLANES, :], hg.shape)

    carry = lax.fori_loop(0, rows // SUBLANES, group, carry_ref[...])
    carry_ref[...] = carry
    ob_ref[...] = _lru_finish(h_ref[...], yb_ref[...], nbw_ref[...])

    @pl.when(t == pl.num_programs(1) - 1)
    def _():
        hfin_ref[...] = carry[0:1, :]


def _lru_prompt(xbc, proj, bsz, t_len, wa, wx, lba, lbx, lam, nbw):
    rows = min(CONV_L, t_len)
    nt = t_len // rows
    mp = bsz * t_len
    wspec = lambda shape: pl.BlockSpec(shape, lambda b, t: (0,) * len(shape))
    return pl.pallas_call(
        functools.partial(_lru_prompt_body, rows=rows),
        grid=(bsz, nt),
        in_specs=[
            pl.BlockSpec((rows, D_B), lambda b, t: (b * nt + t, 0)),
            pl.BlockSpec((rows, D_B), lambda b, t: (b * nt + t, COL_YB // D_B)),
            wspec((NB, BW, BW)), wspec((NB, BW, BW)),
            wspec((1, D_B)), wspec((1, D_B)), wspec((1, D_B)), wspec((1, D_B)),
        ],
        out_specs=[
            pl.BlockSpec((rows, D_B), lambda b, t: (b * nt + t, 0)),
            pl.BlockSpec((None, 1, D_B), lambda b, t: (b, 0, 0)),
        ],
        out_shape=[
            jax.ShapeDtypeStruct((mp, D_B), BF16),
            jax.ShapeDtypeStruct((bsz, 1, D_B), F32),
        ],
        scratch_shapes=[pltpu.VMEM((rows, D_B), F32), pltpu.VMEM((rows, D_B), F32),
                        pltpu.VMEM((rows, D_B), F32), pltpu.VMEM((SUBLANES, D_B), F32)],
        compiler_params=_cparams(("arbitrary", "arbitrary")),
        name="lru_prompt",
    )(xbc, proj, wa, wx, lba, lbx, lam, nbw)


def _lru_sample_body(xb_ref, yb_ref, hist_ref, h0_ref, cbw_ref, cbb_ref, wa_ref, wx_ref, lba_ref,
                     lbx_ref, lam_ref, nbw_ref, ob_ref, h1_ref, hx_ref, a_ref, b_ref, h_ref,
                     *, nseq, t_len):
    hdr = SUBLANES
    nh = CONV_W - 1
    hx_ref[0:hdr, :] = jnp.zeros((hdr, D_B), F32)

    def conv_seq(si, _):
        r0 = pl.multiple_of(si * t_len, t_len)
        hx_ref[hdr - nh:hdr, :] = hist_ref[si]
        hx_ref[hdr:hdr + t_len, :] = xb_ref[pl.ds(r0, t_len), :]
        for n in range(NB):
            c0 = n * BW
            h_ref[pl.ds(r0, t_len), c0:c0 + BW] = (_conv_taps(hx_ref, cbw_ref, c0, BW, t_len, hdr)
                                                  + cbb_ref[:, c0:c0 + BW])
        return 0

    lax.fori_loop(0, nseq, conv_seq, 0)
    parts = _lru_coeffs(h_ref[...], wa_ref, wx_ref, lba_ref, lbx_ref, lam_ref)
    for n in range(NB):
        sl = slice(n * BW, (n + 1) * BW)
        a_n, mult_n, gix_n = parts[n]
        a_ref[:, sl] = a_n
        b_ref[:, sl] = mult_n * gix_n

    def seq(si, _):
        r0 = pl.multiple_of(si * t_len, t_len)
        ag, bg = _group_scan(a_ref[pl.ds(r0, t_len), :], b_ref[pl.ds(r0, t_len), :])
        hg = ag * h0_ref[pl.ds(si, 1), :] + bg
        h_ref[pl.ds(r0, t_len), :] = hg
        h1_ref[pl.ds(si, 1), :] = hg[t_len - 1:t_len, :]
        return 0

    lax.fori_loop(0, nseq, seq, 0)
    ob_ref[...] = _lru_finish(h_ref[...], yb_ref[...], nbw_ref[...])


def _lru_sample(proj, row0, bsz, t_len, hist, h0, cbw, cbb, wa, wx, lba, lbx, lam, nbw):
    assert t_len == SUBLANES
    ms = bsz * t_len
    assert row0 % ms == 0
    rblk = row0 // ms
    wspec = lambda shape: pl.BlockSpec(shape, lambda i: (0,) * len(shape))
    return pl.pallas_call(
        functools.partial(_lru_sample_body, nseq=bsz, t_len=t_len),
        grid=(1,),
        in_specs=[
            pl.BlockSpec((ms, D_B), lambda i: (rblk, COL_XB // D_B)),
            pl.BlockSpec((ms, D_B), lambda i: (rblk, COL_YB // D_B)),
            wspec((bsz, CONV_W - 1, D_B)), wspec((bsz, D_B)),
            wspec((CONV_W, D_B)), wspec((1, D_B)),
            wspec((NB, BW, BW)), wspec((NB, BW, BW)),
            wspec((1, D_B)), wspec((1, D_B)), wspec((1, D_B)), wspec((1, D_B)),
        ],
        out_specs=[wspec((ms, D_B)), wspec((bsz, D_B))],
        out_shape=[
            jax.ShapeDtypeStruct((ms, D_B), BF16),
            jax.ShapeDtypeStruct((bsz, D_B), F32),
        ],
        scratch_shapes=[pltpu.VMEM((2 * SUBLANES, D_B), F32), pltpu.VMEM((ms, D_B), F32),
                        pltpu.VMEM((ms, D_B), F32), pltpu.VMEM((ms, D_B), F32)],
        compiler_params=_cparams(("arbitrary",)),
        name="lru_sample",
    )(proj, proj, hist, h0, cbw, cbb, wa, wx, lba, lbx, lam, nbw)


def _layernorm_rows(v, g, b):
    mu = jnp.mean(v, axis=-1, keepdims=True)
    d = v - mu
    var = jnp.mean(d * d, axis=-1, keepdims=True)
    return d * lax.rsqrt(var + LN_EPS) * g + b


def _store_slabs(slab_ref, base, val):
    n = val.shape[0]
    for s in range(ROW_SLAB):
        slab_ref[pl.ds(base + s, n, stride=ROW_SLAB), :] = val[:, s * LANES:(s + 1) * LANES]


def _load_slab_chunk(slab_ref, base, n, s):
    return slab_ref[pl.ds(base + s, n, stride=ROW_SLAB), :]


def _outproj_body(oap_ref, oas_ref, obp_ref, obs_ref, xp_ref, xs_ref, wo_ref, g_ref, b_ref,
                  wr_ref, br_ref, x1_ref, ti_ref, tg_ref, *, n_ptiles):
    i = pl.program_id(0)

    def run(oa_ref, ob_ref, x_ref):
        mix = (jnp.dot(oa_ref[...], wo_ref[0:D_A, :], preferred_element_type=F32)
               + jnp.dot(ob_ref[...], wo_ref[D_A:D_A + D_B, :], preferred_element_type=F32))
        y = _layernorm_rows(DN_ALPHA * x_ref[...] + mix, g_ref[...], b_ref[...])
        _store_slabs(x1_ref, 0, y)
        logits = _dot3(y, wr_ref[...]) + br_ref[...]
        lane = lax.broadcasted_iota(jnp.int32, logits.shape, 1)
        lane_o = lax.broadcasted_iota(jnp.int32, ti_ref.shape, 1)
        cur = logits
        ti = jnp.zeros(ti_ref.shape, jnp.int32)
        tv = jnp.zeros(tg_ref.shape, F32)
        v0 = None
        den = None
        for kk in range(TOP_K):
            m = jnp.max(cur, axis=-1, keepdims=True)
            idx = jnp.min(jnp.where(cur == m, lane, N_EXPERTS), axis=-1, keepdims=True)
            cur = jnp.where(lane == idx, -jnp.inf, cur)
            if kk == 0:
                v0 = m
            e = jnp.exp(m - v0)
            den = e if den is None else den + e
            ti = jnp.where(lane_o == kk, idx, ti)
            tv = jnp.where(lane_o == kk, e, tv)
        ti_ref[...] = ti
        tg_ref[...] = tv / den

    @pl.when(i < n_ptiles)
    def _():
        run(oap_ref, obp_ref, xp_ref)

    @pl.when(i >= n_ptiles)
    def _():
        run(oas_ref, obs_ref, xs_ref)


def _outproj(oa_p, oa_s, ob_p, ob_s, xp2, xs2, wo, g_row, b_row, wr, br_row):
    mp, ms = xp2.shape[0], xs2.shape[0]
    tm = min(OUT_TM, ms)
    assert mp % tm == 0 and ms % tm == 0
    n_pt, n_st = mp // tm, ms // tm
    pmap = lambda i: (jnp.minimum(i, n_pt - 1), 0)
    smap = lambda i: (jnp.maximum(i - n_pt, 0), 0)
    wspec = lambda shape: pl.BlockSpec(shape, lambda i: (0, 0))
    m = mp + ms
    return pl.pallas_call(
        functools.partial(_outproj_body, n_ptiles=n_pt),
        grid=(n_pt + n_st,),
        in_specs=[
            pl.BlockSpec((tm, D_A), pmap), pl.BlockSpec((tm, D_A), smap),
            pl.BlockSpec((tm, D_B), pmap), pl.BlockSpec((tm, D_B), smap),
            pl.BlockSpec((tm, D_MODEL), pmap), pl.BlockSpec((tm, D_MODEL), smap),
            wspec((D_A + D_B, D_MODEL)), wspec((1, D_MODEL)), wspec((1, D_MODEL)),
            wspec((D_MODEL, N_EXPERTS)), wspec((1, N_EXPERTS)),
        ],
        out_specs=[
            pl.BlockSpec((tm * ROW_SLAB, LANES), lambda i: (i, 0)),
            pl.BlockSpec((tm, LANES), lambda i: (i, 0)),
            pl.BlockSpec((tm, LANES), lambda i: (i, 0)),
        ],
        out_shape=[
            jax.ShapeDtypeStruct((m * ROW_SLAB, LANES), F32),
            jax.ShapeDtypeStruct((m, LANES), jnp.int32),
            jax.ShapeDtypeStruct((m, LANES), F32),
        ],
        compiler_params=_cparams(("arbitrary",)),
        name="outproj_ln_router",
    )(oa_p, oa_s, ob_p, ob_s, xp2, xs2, wo, g_row, b_row, wr, br_row)


def _zero_tail(zbuf_ref, dst_ref, tail_row, n_blocks, sem):
    def tail_copy(t):
        z0 = pl.multiple_of((tail_row + t * MOE_RB) * ROW_SLAB, MOE_RB * ROW_SLAB)
        return pltpu.make_async_copy(zbuf_ref, dst_ref.at[pl.ds(z0, MOE_RB * ROW_SLAB)], sem)

    def start(t, _):
        tail_copy(t).start()
        return 0

    def wait(t, _):
        tail_copy(t).wait()
        return 0

    lax.fori_loop(0, n_blocks, start, 0)
    lax.fori_loop(0, n_blocks, wait, 0)


def _dispatch_body(zrow_ref, slot_ref, x_ref, xs_ref, zbuf_ref, zsem, rsem, *, tm):
    i = pl.program_id(0)

    @pl.when(i == 0)
    def _():
        zbuf_ref[...] = jnp.zeros(zbuf_ref.shape, F32)

        def zero_copy(e):
            z0 = pl.multiple_of(zrow_ref[e] * ROW_SLAB, MOE_RB * ROW_SLAB)
            return pltpu.make_async_copy(zbuf_ref, xs_ref.at[pl.ds(z0, MOE_RB * ROW_SLAB)], zsem)

        for e in range(N_EXPERTS):
            @pl.when(zrow_ref[e] >= 0)
            def _():
                zero_copy(e).start()
        for e in range(N_EXPERTS):
            @pl.when(zrow_ref[e] >= 0)
            def _():
                zero_copy(e).wait()
        _zero_tail(zbuf_ref, xs_ref, zrow_ref[N_EXPERTS], zrow_ref[N_EXPERTS + 1], zsem)

    def row_copy(r, dst):
        src0 = pl.multiple_of(r * ROW_SLAB, ROW_SLAB)
        dst0 = pl.multiple_of(dst * ROW_SLAB, ROW_SLAB)
        return pltpu.make_async_copy(x_ref.at[pl.ds(src0, ROW_SLAB)], xs_ref.at[pl.ds(dst0, ROW_SLAB)], rsem)

    def issue(r, _):
        for kk in range(TOP_K):
            row_copy(r, slot_ref[0, r * TOP_K + kk]).start()
        return 0

    lax.fori_loop(0, tm, issue, 0)

    def drain(r, _):
        for kk in range(TOP_K):
            row_copy(r, 0).wait()
        return 0

    lax.fori_loop(0, tm, drain, 0)


def _dispatch(x1, slot_tiles, zrow, n_rows):
    m = x1.shape[0] // ROW_SLAB
    tm = TOK_TM
    assert m % tm == 0
    return pl.pallas_call(
        functools.partial(_dispatch_body, tm=tm),
        grid_spec=pltpu.PrefetchScalarGridSpec(
            num_scalar_prefetch=1,
            grid=(m // tm,),
            in_specs=[
                pl.BlockSpec((None, 1, tm * TOP_K), lambda i, z: (i, 0, 0), memory_space=pltpu.SMEM),
                pl.BlockSpec((tm * ROW_SLAB, LANES), lambda i, z: (i, 0)),
            ],
            out_specs=pl.BlockSpec(memory_space=pl.ANY),
            scratch_shapes=[pltpu.VMEM((MOE_RB * ROW_SLAB, LANES), F32),
                            pltpu.SemaphoreType.DMA(()), pltpu.SemaphoreType.DMA(())],
        ),
        out_shape=jax.ShapeDtypeStruct((n_rows * ROW_SLAB, LANES), F32),
        compiler_params=_cparams(("arbitrary",)),
        name="moe_dispatch",
    )(zrow, slot_tiles, x1)


def _moe_body(ie_ref, irow_ref, insub_ref, tail_ref, xs_ref, wg_ref, wu_ref, bgate_ref, bup_ref,
              wd_ref, bd_ref, ys_ref, xstage_ref, xb_ref, act_ref, ybuf_ref, wgb_ref, wub_ref, wdb_ref,
              isem, osem):
    wi = pl.program_id(0)
    j = pl.program_id(1)
    n_w = pl.num_programs(0)
    nsub = insub_ref[wi]
    row0 = irow_ref[wi]
    nt = MOE_NT
    is_last = wi == n_w - 1
    nxt = jnp.minimum(wi + 1, n_w - 1)
    nsub_next = jnp.where(is_last, 0, insub_ref[nxt])
    row_next = irow_ref[nxt]
    prv = jnp.maximum(wi - 1, 0)
    nsub_prev = jnp.where(wi == 0, 0, insub_ref[prv])
    row_prev = irow_ref[prv]

    def x_copy(row_base, s, slot):
        g0 = pl.multiple_of((row_base + s * MOE_RB) * ROW_SLAB, MOE_RB * ROW_SLAB)
        return pltpu.make_async_copy(xs_ref.at[pl.ds(g0, MOE_RB * ROW_SLAB)], xstage_ref.at[slot],
                                     isem.at[slot])

    def to_matmul_layout(s, slot):
        r0 = pl.multiple_of(s * MOE_RB, MOE_RB)
        for c in range(ROW_SLAB):
            xb_ref[pl.ds(r0, MOE_RB), c * LANES:(c + 1) * LANES] = (
                _load_slab_chunk(xstage_ref.at[slot], 0, MOE_RB, c).astype(BF16))

    def y_copy(row_base, s):
        b0 = pl.multiple_of(s * MOE_RB * ROW_SLAB, MOE_RB * ROW_SLAB)
        g0 = pl.multiple_of(row_base * ROW_SLAB, MOE_RB * ROW_SLAB) + b0
        return pltpu.make_async_copy(ybuf_ref.at[pl.ds(b0, MOE_RB * ROW_SLAB)],
                                     ys_ref.at[pl.ds(g0, MOE_RB * ROW_SLAB)], osem)

    @pl.when(jnp.logical_and(wi == 0, j == 0))
    def _():
        def load(s, _):
            cp = x_copy(row0, s, 0)
            cp.start()
            cp.wait()
            to_matmul_layout(s, 0)
            return 0

        lax.fori_loop(0, nsub, load, 0)

    @pl.when(jnp.logical_and(j == nt - 1, nsub_next > 0))
    def _():
        x_copy(row_next, 0, 0).start()

    @pl.when(jnp.logical_and(j >= nt, j - nt < nsub_next))
    def _():
        s = j - nt
        slot = lax.rem(s, 2)
        x_copy(row_next, s, slot).wait()

        @pl.when(s + 1 < nsub_next)
        def _():
            x_copy(row_next, s + 1, 1 - slot).start()

        to_matmul_layout(s, slot)

    @pl.when(jnp.logical_and(j == nt, nsub_prev > 0))
    def _():
        def drain(s, _):
            y_copy(row_prev, s).wait()
            return 0

        lax.fori_loop(0, nsub_prev, drain, 0)

    @pl.when(nsub > 0)
    def _():
        @pl.when(j < nt)
        def _():
            wgb_ref[...] = wg_ref[...].astype(BF16)
            wub_ref[...] = wu_ref[...].astype(BF16)

            def sub(s, _):
                r0 = pl.multiple_of(s * MOE_RB, MOE_RB)
                x = xb_ref[pl.ds(r0, MOE_RB), :]
                hg = jnp.dot(x, wgb_ref[...], preferred_element_type=F32) + bgate_ref[...]
                hu = jnp.dot(x, wub_ref[...], preferred_element_type=F32) + bup_ref[...]
                gate = jnp.minimum(hg, SWIGLU_LIMIT)
                up = jnp.clip(hu, -SWIGLU_LIMIT, SWIGLU_LIMIT)
                glu = gate * _sigmoid(SWIGLU_ALPHA * gate)
                act_ref[j, pl.ds(r0, MOE_RB), :] = (glu * (up + 1.0)).astype(BF16)
                return 0

            lax.fori_loop(0, nsub, sub, 0)

        @pl.when(j >= nt)
        def _():
            wdb_ref[...] = wd_ref[...].astype(BF16)
            n = j - nt

            def sub(s, _):
                r0 = pl.multiple_of(s * MOE_RB, MOE_RB)
                acc = jnp.dot(act_ref[0, pl.ds(r0, MOE_RB), :], wdb_ref[0:MOE_TF, :],
                              preferred_element_type=F32)
                for jj in range(1, nt):
                    acc = acc + jnp.dot(act_ref[jj, pl.ds(r0, MOE_RB), :],
                                        wdb_ref[jj * MOE_TF:(jj + 1) * MOE_TF, :],
                                        preferred_element_type=F32)
                acc = acc + bd_ref[...]
                cpt = MOE_TF // LANES
                for cc in range(cpt):
                    ybuf_ref[pl.ds(r0 * ROW_SLAB + n * cpt + cc, MOE_RB, stride=ROW_SLAB), :] = (
                        acc[:, cc * LANES:(cc + 1) * LANES])
                return 0

            lax.fori_loop(0, nsub, sub, 0)

        @pl.when(j == 2 * nt - 1)
        def _():
            def store(s, _):
                y_copy(row0, s).start()
                return 0

            lax.fori_loop(0, nsub, store, 0)

    @pl.when(jnp.logical_and(is_last, j == 2 * nt - 1))
    def _():
        def drain(s, _):
            y_copy(row0, s).wait()
            return 0

        lax.fori_loop(0, nsub, drain, 0)
        xstage_ref[0] = jnp.zeros(xstage_ref.shape[1:], F32)
        _zero_tail(xstage_ref.at[0], ys_ref, tail_ref[0], tail_ref[1], isem.at[0])


def _moe(xs, n_items, item_e, item_row, item_nsub, tail, w_up, b_up3, w_down, b_down3):
    n_rows = xs.shape[0]
    nt = MOE_NT
    assert MOE_SB <= nt
    up_off = D_FF // MOE_TF
    p1 = lambda j, n, w: jnp.where(n[w] > 0, jnp.minimum(j, nt - 1), nt - 1)
    p2 = lambda j, n, w: jnp.where(n[w] > 0, jnp.maximum(j - nt, 0), nt - 1)
    return pl.pallas_call(
        _moe_body,
        grid_spec=pltpu.PrefetchScalarGridSpec(
            num_scalar_prefetch=4,
            grid=(n_items, 2 * nt),
            in_specs=[
                pl.BlockSpec(memory_space=pl.ANY),
                pl.BlockSpec((None, D_MODEL, MOE_TF), lambda w, j, e, r, n, t: (e[w], 0, p1(j, n, w))),
                pl.BlockSpec((None, D_MODEL, MOE_TF),
                             lambda w, j, e, r, n, t: (e[w], 0, up_off + p1(j, n, w))),
                pl.BlockSpec((None, 1, MOE_TF), lambda w, j, e, r, n, t: (e[w], 0, p1(j, n, w))),
                pl.BlockSpec((None, 1, MOE_TF), lambda w, j, e, r, n, t: (e[w], 0, up_off + p1(j, n, w))),
                pl.BlockSpec((None, D_FF, MOE_TF), lambda w, j, e, r, n, t: (e[w], 0, p2(j, n, w))),
                pl.BlockSpec((None, 1, MOE_TF), lambda w, j, e, r, n, t: (e[w], 0, p2(j, n, w))),
            ],
            out_specs=pl.BlockSpec(memory_space=pl.ANY),
            scratch_shapes=[
                pltpu.VMEM((2, MOE_RB * ROW_SLAB, LANES), F32),
                pltpu.VMEM((MOE_R, D_MODEL), BF16),
                pltpu.VMEM((nt, MOE_R, MOE_TF), BF16),
                pltpu.VMEM((MOE_R * ROW_SLAB, LANES), F32),
                pltpu.VMEM((D_MODEL, MOE_TF), BF16),
                pltpu.VMEM((D_MODEL, MOE_TF), BF16),
                pltpu.VMEM((D_FF, MOE_TF), BF16),
                pltpu.SemaphoreType.DMA((2,)), pltpu.SemaphoreType.DMA(()),
            ],
        ),
        out_shape=jax.ShapeDtypeStruct((n_rows, LANES), F32),
        compiler_params=_cparams(("arbitrary", "arbitrary")),
        name="moe_experts",
    )(item_e, item_row, item_nsub, tail, xs, w_up, w_up, b_up3, b_up3, w_down, b_down3)


def _combine_body(slot_ref, ys_ref, x1_ref, gate_ref, g_ref, b_ref, yp_ref, ysm_ref, buf_ref, v_ref,
                  sem, *, tm, n_ptiles):
    def row_copy(r, kk, src):
        src0 = pl.multiple_of(src * ROW_SLAB, ROW_SLAB)
        dst0 = pl.multiple_of(r * ROW_SLAB, ROW_SLAB)
        return pltpu.make_async_copy(ys_ref.at[pl.ds(src0, ROW_SLAB)],
                                     buf_ref.at[kk, pl.ds(dst0, ROW_SLAB)], sem)

    def issue(r, _):
        for kk in range(TOP_K):
            row_copy(r, kk, slot_ref[0, r * TOP_K + kk]).start()
        return 0

    lax.fori_loop(0, tm, issue, 0)

    def drain(r, _):
        for kk in range(TOP_K):
            row_copy(r, kk, 0).wait()
        return 0

    lax.fori_loop(0, tm, drain, 0)
    gates = gate_ref[...]
    for c in range(ROW_SLAB):
        acc = DN_ALPHA * _load_slab_chunk(x1_ref, 0, tm, c)
        for kk in range(TOP_K):
            acc = acc + gates[:, kk:kk + 1] * _load_slab_chunk(buf_ref.at[kk], 0, tm, c)
        v_ref[:, c * LANES:(c + 1) * LANES] = acc
    y = _layernorm_rows(v_ref[...], g_ref[...], b_ref[...])

    @pl.when(pl.program_id(0) < n_ptiles)
    def _():
        yp_ref[...] = y

    @pl.when(pl.program_id(0) >= n_ptiles)
    def _():
        ysm_ref[...] = y


def _combine(ys, slot_tiles, x1, gates, g_row, b_row, mp):
    m = x1.shape[0] // ROW_SLAB
    tm = TOK_TM
    assert mp % tm == 0 and (m - mp) % tm == 0
    n_pt = mp // tm
    wspec = lambda shape: pl.BlockSpec(shape, lambda i: (0, 0))
    return pl.pallas_call(
        functools.partial(_combine_body, tm=tm, n_ptiles=n_pt),
        grid=(m // tm,),
        in_specs=[
            pl.BlockSpec((None, 1, tm * TOP_K), lambda i: (i, 0, 0), memory_space=pltpu.SMEM),
            pl.BlockSpec(memory_space=pl.ANY),
            pl.BlockSpec((tm * ROW_SLAB, LANES), lambda i: (i, 0)),
            pl.BlockSpec((tm, LANES), lambda i: (i, 0)),
            wspec((1, D_MODEL)), wspec((1, D_MODEL)),
        ],
        out_specs=[pl.BlockSpec((tm, D_MODEL), lambda i: (jnp.minimum(i, n_pt - 1), 0)),
                   pl.BlockSpec((tm, D_MODEL), lambda i: (jnp.maximum(i - n_pt, 0), 0))],
        out_shape=[jax.ShapeDtypeStruct((mp, D_MODEL), F32),
                   jax.ShapeDtypeStruct((m - mp, D_MODEL), F32)],
        scratch_shapes=[pltpu.VMEM((TOP_K, tm * ROW_SLAB, LANES), F32),
                        pltpu.VMEM((tm, D_MODEL), F32), pltpu.SemaphoreType.DMA(())],
        compiler_params=_cparams(("arbitrary",)),
        name="moe_combine_ln",
    )(slot_tiles, ys, x1, gates, g_row, b_row)


def _routing_tables(top_i, m):
    e_ids = jnp.arange(N_EXPERTS, dtype=jnp.int32)
    onehot = (top_i[:, :, None] == e_ids[None, None, :]).astype(jnp.int32)
    mask = jnp.sum(onehot, axis=1)
    incl = jnp.cumsum(mask, axis=0)
    pos = incl - mask
    counts = incl[-1]
    nsub = (counts + MOE_RB - 1) // MOE_RB
    padded = nsub * MOE_RB
    gend = jnp.cumsum(padded)
    gstart = gend - padded
    slot = jnp.sum(onehot * (gstart[None, None, :] + pos[:, None, :]), axis=2)
    n_rows = (m * TOP_K // MOE_RB + N_EXPERTS) * MOE_RB
    tail = jnp.stack([gend[-1], (n_rows - gend[-1]) // MOE_RB]).astype(jnp.int32)
    zrow = jnp.concatenate([jnp.where(counts > 0, gend - MOE_RB, -1).astype(jnp.int32), tail])
    n_items = N_EXPERTS + (m * TOP_K // MOE_RB + N_EXPERTS) // MOE_SB
    ipe = (nsub + MOE_SB - 1) // MOE_SB
    iend = jnp.cumsum(ipe)
    istart = iend - ipe
    total = iend[-1]
    wid = jnp.arange(n_items, dtype=jnp.int32)
    wclamp = jnp.minimum(wid, total - 1)
    ie = jnp.minimum(jnp.searchsorted(iend, wclamp, side="right"), N_EXPERTS - 1).astype(jnp.int32)
    jn = wclamp - istart[ie]
    irow = (gstart[ie] + jn * MOE_R).astype(jnp.int32)
    insub = jnp.where(wid < total, jnp.clip(nsub[ie] - jn * MOE_SB, 0, MOE_SB), 0).astype(jnp.int32)
    return slot.astype(jnp.int32), zrow, tail, total.astype(jnp.int32), ie, irow, insub, n_rows


def kernel(x_prompt, x_sample, state_conv_a, state_delta, state_conv_b, state_lru, w_in, conv_a_w,
           a_log, dt_bias, norm_a_w, conv_b_w, conv_b_b, lru_wa, lru_ba, lru_wx, lru_bx, lru_lambda,
           norm_b_w, w_out, ln1_g, ln1_b, w_router, b_router, w_up, b_up, w_down, b_down, ln2_g, ln2_b):
    bp, tp, _ = x_prompt.shape
    bs, ts, _ = x_sample.shape
    mp, ms = bp * tp, bs * ts
    m = mp + ms
    l = 0
    xp2 = x_prompt.reshape(mp, D_MODEL)
    xs2 = x_sample.reshape(ms, D_MODEL)

    wi = w_in[l]
    c_qkvz = 4 * D_A
    wp = jnp.concatenate(
        [wi[:, :c_qkvz], wi[:, c_qkvz + 2 * H_A:], wi[:, c_qkvz:c_qkvz + 2 * H_A],
         jnp.zeros((D_MODEL, 2 * LANES - 2 * H_A), F32)], axis=1).astype(BF16)
    wo = w_out[l].astype(BF16)
    pad_h = lambda v: jnp.zeros((1, LANES), F32).at[0, H_A:2 * H_A].set(v)
    alog_row = pad_h(a_log[l])
    dtb_row = pad_h(dt_bias[l])
    row = lambda v: v.reshape(1, -1)

    proj = _inproj(xp2, xs2, wp)

    qkv_act, xbc, bg = _convact_prompt(proj, bp, tp, conv_a_w[l], conv_b_w[l], row(conv_b_b[l]),
                                       alog_row, dtb_row)
    u, w, qe, kdt, aqk, el = _delta_prep(qkv_act, bg, bp, tp)
    oa_p, sd_p = _delta_seq(u, w, qe, kdt, aqk, el, proj, row(norm_a_w[l]), bp, tp)
    ob_p, h_p = _lru_prompt(xbc, proj, bp, tp, lru_wa[l], lru_wx[l], row(lru_ba[l]), row(lru_bx[l]),
                            row(lru_lambda[l]), row(norm_b_w[l]))
    oa_s, sd_s = _sample_a(proj, mp, bs, ts, state_conv_a[l], state_delta[l], conv_a_w[l],
                           alog_row, dtb_row, row(norm_a_w[l]))
    ob_s, h_s = _lru_sample(proj, mp, bs, ts, state_conv_b[l], state_lru[l], conv_b_w[l],
                            row(conv_b_b[l]), lru_wa[l], lru_wx[l], row(lru_ba[l]), row(lru_bx[l]),
                            row(lru_lambda[l]), row(norm_b_w[l]))

    x1, ti, tg = _outproj(oa_p, oa_s, ob_p, ob_s, xp2, xs2, wo, row(ln1_g[l]), row(ln1_b[l]),
                          w_router[l], row(b_router[l]))

    slot, zrow, tail, n_used, ie, irow, insub, n_rows = _routing_tables(ti[:, :TOP_K], m)
    slot_tiles = slot.reshape(m // TOK_TM, 1, TOK_TM * TOP_K)
    xs_sorted = _dispatch(x1, slot_tiles, zrow, n_rows)
    ys = _moe(xs_sorted, n_used, ie, irow, insub, tail, w_up[l], b_up[l].reshape(N_EXPERTS, 1, 2 * D_FF),
              w_down[l], b_down[l].reshape(N_EXPERTS, 1, D_MODEL))
    y_p, y_s = _combine(ys, slot_tiles, x1, tg, row(ln2_g[l]), row(ln2_b[l]), mp)

    y_prompt = y_p.reshape(bp, tp, D_MODEL)
    y_sample = y_s.reshape(bs, ts, D_MODEL)
    nh = CONV_W - 1
    assert tp % SUBLANES == 0 and ts == SUBLANES and nh <= SUBLANES
    pg = proj.reshape(m // SUBLANES, SUBLANES, N_PROJ)
    gp = tp // SUBLANES

    def last_rows(g0, g1, gstep, c0, width):
        return lax.slice(pg, (g0, SUBLANES - nh, c0), (g1, SUBLANES, c0 + width), (gstep, 1, 1))

    ca_p = last_rows(gp - 1, mp // SUBLANES, gp, COL_QKV, 3 * D_A)
    cb_p = last_rows(gp - 1, mp // SUBLANES, gp, COL_XB, D_B)
    ca_s = last_rows(mp // SUBLANES, m // SUBLANES, 1, COL_QKV, 3 * D_A)
    cb_s = last_rows(mp // SUBLANES, m // SUBLANES, 1, COL_XB, D_B)
    return (y_prompt, y_sample,
            ca_p[None], sd_p[None], cb_p[None], h_p.reshape(1, bp, D_B),
            ca_s[None], sd_s[None], cb_s[None], h_s[None])
```

```python
import functools
import math

import jax
import jax.numpy as jnp
from jax import lax
from jax.experimental import pallas as pl
from jax.experimental.pallas import tpu as pltpu

F32 = jnp.float32
BF16 = jnp.bfloat16
HIGHEST = lax.Precision.HIGHEST

D_MODEL = 2048
D_A = 1024
H_A = 8
DK = 128
DV = 128
CONV_W = 4
D_B = 1024
NB = 8
BW = 128
LRU_C = 8.0
N_EXPERTS = 32
TOP_K = 4
D_FF = 2048
SWIGLU_LIMIT = 7.0
SWIGLU_ALPHA = 1.702
DEPTH = 1
DN_ALPHA = (2.0 * DEPTH) ** 0.25
LN_EPS = 1e-5
RMS_EPS = 1e-6
L2_EPS = 1e-6

LANES = 128
SUBLANES = 8
VMEM_LIMIT_BYTES = 56 * 1024 * 1024

COL_QKV = 0
COL_Z = 3 * D_A
COL_XB = 4 * D_A
COL_YB = 4 * D_A + D_B
COL_BA = 4 * D_A + 2 * D_B
N_PROJ = COL_BA + 2 * LANES

INPROJ_TM = 512
INPROJ_TN = 1280
CONV_L = 256
DELTA_C = 128
OUT_TM = 256
MOE_RB = 256
MOE_SB = 6
MOE_R = MOE_RB * MOE_SB
MOE_TF = 256
MOE_NT = D_FF // MOE_TF
TOK_TM = 256
ROW_SLAB = D_MODEL // LANES


def _cparams(sem, vmem=VMEM_LIMIT_BYTES):
    return pltpu.CompilerParams(dimension_semantics=sem, vmem_limit_bytes=vmem)


def _sigmoid(x):
    return 1.0 / (1.0 + jnp.exp(-x))


def _softplus(x):
    return jnp.maximum(x, 0.0) + jnp.log(1.0 + jnp.exp(-jnp.abs(x)))


def _inproj_body(xp_ref, xs_ref, w_ref, o_ref, xb_ref, *, n_ptiles):
    i = pl.program_id(0)

    @pl.when(pl.program_id(1) == 0)
    def _():
        @pl.when(i < n_ptiles)
        def _():
            xb_ref[...] = xp_ref[...].astype(BF16)

        @pl.when(i >= n_ptiles)
        def _():
            xb_ref[...] = xs_ref[...].astype(BF16)

    o_ref[...] = jnp.dot(xb_ref[...], w_ref[...], preferred_element_type=F32)


def _inproj(xp2, xs2, wp):
    mp, ms = xp2.shape[0], xs2.shape[0]
    tm = min(INPROJ_TM, ms)
    assert mp % tm == 0 and ms % tm == 0
    n_pt, n_st = mp // tm, ms // tm
    n_nt = N_PROJ // INPROJ_TN
    return pl.pallas_call(
        functools.partial(_inproj_body, n_ptiles=n_pt),
        grid=(n_pt + n_st, n_nt),
        in_specs=[
            pl.BlockSpec((tm, D_MODEL), lambda i, j: (jnp.minimum(i, n_pt - 1), 0)),
            pl.BlockSpec((tm, D_MODEL), lambda i, j: (jnp.maximum(i - n_pt, 0), 0)),
            pl.BlockSpec((D_MODEL, INPROJ_TN), lambda i, j: (0, j)),
        ],
        out_specs=pl.BlockSpec((tm, INPROJ_TN), lambda i, j: (i, j)),
        out_shape=jax.ShapeDtypeStruct((mp + ms, N_PROJ), F32),
        scratch_shapes=[pltpu.VMEM((tm, D_MODEL), BF16)],
        compiler_params=_cparams(("arbitrary", "arbitrary")),
        name="inproj",
    )(xp2, xs2, wp)


def _conv_taps(h_ref, w_ref, c0, width, rows, base):
    acc = h_ref[base:base + rows, c0:c0 + width] * w_ref[CONV_W - 1:CONV_W, c0:c0 + width]
    for j in range(1, CONV_W):
        acc = acc + (h_ref[base - j:base - j + rows, c0:c0 + width]
                     * w_ref[CONV_W - 1 - j:CONV_W - j, c0:c0 + width])
    return acc


def _qkv_activation(acc, blk):
    a = acc * _sigmoid(acc)
    if blk < 2 * H_A:
        a = a * lax.rsqrt(jnp.sum(a * a, axis=-1, keepdims=True) + L2_EPS)
        if blk < H_A:
            a = a * (DK ** -0.5)
    return a


def _beta_g(ba, alog_ref, dtb_ref):
    lane = lax.broadcasted_iota(jnp.int32, ba.shape, 1)
    beta = _sigmoid(ba)
    g = -jnp.exp(alog_ref[...]) * _softplus(ba + dtb_ref[...])
    return jnp.where(lane < H_A, beta, jnp.where(lane < 2 * H_A, g, 0.0))


def _convact_prompt_body(qkv_ref, xb_ref, ba_ref, caw_ref, cbw_ref, cbb_ref, alog_ref, dtb_ref,
                         qkvo_ref, xbo_ref, bgo_ref, hq_ref, hx_ref, *, rows):
    t = pl.program_id(1)
    hdr = SUBLANES

    @pl.when(t == 0)
    def _():
        hq_ref[0:hdr, :] = jnp.zeros((hdr, 3 * D_A), F32)
        hx_ref[0:hdr, :] = jnp.zeros((hdr, D_B), F32)

    @pl.when(t > 0)
    def _():
        hq_ref[0:hdr, :] = hq_ref[rows:rows + hdr, :]
        hx_ref[0:hdr, :] = hx_ref[rows:rows + hdr, :]

    hq_ref[hdr:hdr + rows, :] = qkv_ref[...]
    hx_ref[hdr:hdr + rows, :] = xb_ref[...]
    for blk in range(3 * H_A):
        c0 = blk * DK
        acc = _conv_taps(hq_ref, caw_ref, c0, DK, rows, hdr)
        qkvo_ref[:, c0:c0 + DK] = _qkv_activation(acc, blk)
    for blk in range(NB):
        c0 = blk * BW
        xbo_ref[:, c0:c0 + BW] = _conv_taps(hx_ref, cbw_ref, c0, BW, rows, hdr) + cbb_ref[:, c0:c0 + BW]
    bgo_ref[...] = _beta_g(ba_ref[...], alog_ref, dtb_ref)


def _convact_prompt(proj, bsz, t_len, caw, cbw, cbb, alog_row, dtb_row):
    rows = min(CONV_L, t_len)
    assert t_len % rows == 0
    nt = t_len // rows
    mp = bsz * t_len
    wspec = lambda shape: pl.BlockSpec(shape, lambda b, t: (0, 0))
    return pl.pallas_call(
        functools.partial(_convact_prompt_body, rows=rows),
        grid=(bsz, nt),
        in_specs=[
            pl.BlockSpec((rows, 3 * D_A), lambda b, t: (b * nt + t, COL_QKV // (3 * D_A))),
            pl.BlockSpec((rows, D_B), lambda b, t: (b * nt + t, COL_XB // D_B)),
            pl.BlockSpec((rows, LANES), lambda b, t: (b * nt + t, COL_BA // LANES)),
            wspec((CONV_W, 3 * D_A)), wspec((CONV_W, D_B)), wspec((1, D_B)),
            wspec((1, LANES)), wspec((1, LANES)),
        ],
        out_specs=[
            pl.BlockSpec((rows, 3 * D_A), lambda b, t: (b * nt + t, 0)),
            pl.BlockSpec((rows, D_B), lambda b, t: (b * nt + t, 0)),
            pl.BlockSpec((rows, LANES), lambda b, t: (b * nt + t, 0)),
        ],
        out_shape=[
            jax.ShapeDtypeStruct((mp, 3 * D_A), F32),
            jax.ShapeDtypeStruct((mp, D_B), F32),
            jax.ShapeDtypeStruct((mp, LANES), F32),
        ],
        scratch_shapes=[pltpu.VMEM((SUBLANES + rows + SUBLANES, 3 * D_A), F32),
                        pltpu.VMEM((SUBLANES + rows + SUBLANES, D_B), F32)],
        compiler_params=_cparams(("arbitrary", "arbitrary")),
        name="convact_prompt",
    )(proj, proj, proj, caw, cbw, cbb, alog_row, dtb_row)


def _dot_nt(a, b, precision=None):
    return lax.dot_general(a, b, (((1,), (1,)), ((), ())), precision=precision,
                           preferred_element_type=F32)


def _split_bf16(x):
    hi = x.astype(BF16)
    return hi, (x - hi.astype(F32)).astype(BF16)


def _dot3(a, b):
    ah, al = _split_bf16(a)
    bh, bl = _split_bf16(b)
    return (jnp.dot(ah, bh, preferred_element_type=F32)
            + (jnp.dot(al, bh, preferred_element_type=F32) + jnp.dot(ah, bl, preferred_element_type=F32)))


def _delta_prep_body(qkv_ref, bg_ref, u_ref, w_ref, qe_ref, kdt_ref, aqk_ref, el_ref, *, c):
    row = lax.broadcasted_iota(jnp.int32, (c, c), 0)
    col = lax.broadcasted_iota(jnp.int32, (c, c), 1)
    incl = row >= col
    strict = row > col
    eye = (row == col).astype(F32)
    tril = incl.astype(F32)
    eye_l = (lax.broadcasted_iota(jnp.int32, (LANES, LANES), 0)
             == lax.broadcasted_iota(jnp.int32, (LANES, LANES), 1)).astype(F32)

    bg = bg_ref[...]
    gc_all = jnp.dot(tril, bg, precision=HIGHEST, preferred_element_type=F32)
    gc_t = _dot_nt(eye_l, gc_all, precision=HIGHEST)
    heads = range(H_A)
    lmats, rhss = [], []
    for h in heads:
        q = qkv_ref[:, h * DK:(h + 1) * DK]
        k = qkv_ref[:, (H_A + h) * DK:(H_A + h + 1) * DK]
        v = qkv_ref[:, (2 * H_A + h) * DK:(2 * H_A + h + 1) * DK]
        beta = bg[:, h:h + 1]
        gc = gc_all[:, H_A + h:H_A + h + 1]
        g_last = gc_all[c - 1:c, H_A + h:H_A + h + 1]
        diff = gc - gc_t[H_A + h:H_A + h + 1, :]
        decay = jnp.where(incl, jnp.exp(jnp.where(incl, diff, 0.0)), 0.0)
        kb = k * beta
        egc = jnp.exp(gc)
        kbf = k.astype(BF16)
        kk = _dot_nt(kbf, kbf)
        lmats.append(jnp.where(strict, kk * beta * decay, 0.0))
        rhss.append(jnp.concatenate([v * beta, kb * egc], axis=1))
        qe_ref[:, h * DK:(h + 1) * DK] = (q * egc).astype(BF16)
        aqk_ref[h] = (_dot_nt(q.astype(BF16), kbf) * decay).astype(BF16)
        kdec = k * jnp.exp(g_last - gc)
        kdt_ref[h] = _dot_nt(eye_l.astype(BF16), kdec.astype(BF16)).astype(BF16)
        el_ref[h] = jnp.broadcast_to(jnp.exp(g_last), (SUBLANES, LANES))
    xbs = [(-lm).astype(BF16) for lm in lmats]
    t0s = [eye - lm for lm in lmats]
    p = 2
    while p < c:
        xbs = [jnp.dot(xb, xb, preferred_element_type=F32).astype(BF16) for xb in xbs]
        t0s = [t0 + jnp.dot(t0.astype(BF16), xb, preferred_element_type=F32) for t0, xb in zip(t0s, xbs)]
        p *= 2
    t0bs = [t0.astype(BF16) for t0 in t0s]
    sol0s = [jnp.dot(t0b, rhs.astype(BF16), preferred_element_type=F32) for t0b, rhs in zip(t0bs, rhss)]
    resids = [rhs - sol0 - _dot3(lm, sol0) for rhs, sol0, lm in zip(rhss, sol0s, lmats)]
    for h in heads:
        sol = sol0s[h] + jnp.dot(t0bs[h], resids[h].astype(BF16), preferred_element_type=F32)
        u_ref[:, h * DV:(h + 1) * DV] = sol[:, :DV]
        w_ref[:, h * DK:(h + 1) * DK] = sol[:, DV:].astype(BF16)


def _delta_prep(qkv_act, bg, bsz, t_len):
    c = DELTA_C
    assert t_len % c == 0
    nc = t_len // c
    mp = bsz * t_len
    n_chunks = bsz * nc
    rowspec = lambda w: pl.BlockSpec((c, w), lambda i: (i, 0))
    return pl.pallas_call(
        functools.partial(_delta_prep_body, c=c),
        grid=(n_chunks,),
        in_specs=[rowspec(3 * D_A), rowspec(LANES)],
        out_specs=[
            rowspec(D_A), rowspec(D_A), rowspec(D_A),
            pl.BlockSpec((None, H_A, DK, c), lambda i: (i, 0, 0, 0)),
            pl.BlockSpec((None, H_A, c, c), lambda i: (i, 0, 0, 0)),
            pl.BlockSpec((None, H_A, SUBLANES, LANES), lambda i: (i, 0, 0, 0)),
        ],
        out_shape=[
            jax.ShapeDtypeStruct((mp, D_A), F32),
            jax.ShapeDtypeStruct((mp, D_A), BF16),
            jax.ShapeDtypeStruct((mp, D_A), BF16),
            jax.ShapeDtypeStruct((n_chunks, H_A, DK, c), BF16),
            jax.ShapeDtypeStruct((n_chunks, H_A, c, c), BF16),
            jax.ShapeDtypeStruct((n_chunks, H_A, SUBLANES, LANES), F32),
        ],
        compiler_params=_cparams(("arbitrary",)),
        name="delta_prep",
    )(qkv_act, bg)


def _gated_rmsnorm(o, z, nw):
    on = o * lax.rsqrt(jnp.mean(o * o, axis=-1, keepdims=True) + RMS_EPS) * nw
    return on * (z * _sigmoid(z))


def _delta_seq_body(u_ref, w_ref, qe_ref, kdt_ref, aqk_ref, el_ref, z_ref, nw_ref,
                    oa_ref, sfin_ref, s_ref):
    ci = pl.program_id(1)

    @pl.when(ci == 0)
    def _():
        s_ref[...] = jnp.zeros(s_ref.shape, F32)

    nw = nw_ref[...]
    heads = range(H_A)
    sls = [slice(h * DK, (h + 1) * DK) for h in heads]
    ss = [s_ref[h] for h in heads]
    sbs = [s.astype(BF16) for s in ss]
    vbs = [(u_ref[:, sl] - jnp.dot(w_ref[:, sl], sb, preferred_element_type=F32)).astype(BF16)
           for sl, sb in zip(sls, sbs)]
    for h in heads:
        s_ref[h] = ss[h] * el_ref[h][0:1, 0:1] + jnp.dot(kdt_ref[h], vbs[h], preferred_element_type=F32)
    for h in heads:
        o = (jnp.dot(qe_ref[:, sls[h]], sbs[h], preferred_element_type=F32)
             + jnp.dot(aqk_ref[h], vbs[h], preferred_element_type=F32))
        oa_ref[:, sls[h]] = _gated_rmsnorm(o, z_ref[:, sls[h]], nw).astype(BF16)

    @pl.when(ci == pl.num_programs(1) - 1)
    def _():
        sfin_ref[...] = s_ref[...]


def _delta_seq(u, w, qe, kdt, aqk, el, proj, nw_row, bsz, t_len):
    c = DELTA_C
    nc = t_len // c
    mp = bsz * t_len
    rowspec = lambda wd: pl.BlockSpec((c, wd), lambda b, i: (b * nc + i, 0))
    chunkspec = lambda a, bb: pl.BlockSpec((None, H_A, a, bb), lambda b, i: (b * nc + i, 0, 0, 0))
    return pl.pallas_call(
        _delta_seq_body,
        grid=(bsz, nc),
        in_specs=[
            rowspec(D_A), rowspec(D_A), rowspec(D_A),
            chunkspec(DK, c), chunkspec(c, c), chunkspec(SUBLANES, LANES),
            pl.BlockSpec((c, D_A), lambda b, i: (b * nc + i, COL_Z // D_A)),
            pl.BlockSpec((1, DV), lambda b, i: (0, 0)),
        ],
        out_specs=[
            rowspec(D_A),
            pl.BlockSpec((None, H_A, DK, DV), lambda b, i: (b, 0, 0, 0)),
        ],
        out_shape=[
            jax.ShapeDtypeStruct((mp, D_A), BF16),
            jax.ShapeDtypeStruct((bsz, H_A, DK, DV), F32),
        ],
        scratch_shapes=[pltpu.VMEM((H_A, DK, DV), F32)],
        compiler_params=_cparams(("arbitrary", "arbitrary")),
        name="delta_seq",
    )(u, w, qe, kdt, aqk, el, proj, nw_row)


def _sample_a_body(qkv_ref, z_ref, ba_ref, hist_ref, s0_ref, caw_ref, alog_ref, dtb_ref, nw_ref,
                   oa_ref, s1_ref, hq_ref, *, t_len):
    hdr = SUBLANES
    nh = CONV_W - 1
    hq_ref[0:hdr, :] = jnp.zeros((hdr, 3 * D_A), F32)
    hq_ref[hdr - nh:hdr, :] = hist_ref[...]
    hq_ref[hdr:hdr + t_len, :] = qkv_ref[...]
    bg = _beta_g(ba_ref[...], alog_ref, dtb_ref)
    rowi = lax.broadcasted_iota(jnp.int32, bg.shape, 0)
    gc_all = bg
    sft = 1
    while sft < t_len:
        gc_all = gc_all + jnp.where(rowi >= sft, pltpu.roll(gc_all, sft, axis=0), 0.0)
        sft *= 2
    nw = nw_ref[...]
    rowc = lax.broadcasted_iota(jnp.int32, (t_len, 1), 0)
    heads = range(H_A)
    act = lambda blk: _qkv_activation(_conv_taps(hq_ref, caw_ref, blk * DK, DK, t_len, hdr), blk)
    qs = [act(h) for h in heads]
    ks = [act(H_A + h) for h in heads]
    vs = [act(2 * H_A + h) for h in heads]
    betas = [bg[:, h:h + 1] for h in heads]
    gcs = [gc_all[:, H_A + h:H_A + h + 1] for h in heads]
    glasts = [gc[t_len - 1:t_len, :] for gc in gcs]
    kbs = [k * b for k, b in zip(ks, betas)]
    egcs = [jnp.exp(gc) for gc in gcs]
    sols = [jnp.concatenate([v * b, kb * e], axis=1) for v, b, kb, e in zip(vs, betas, kbs, egcs)]
    acols = [[] for _ in heads]
    lcols = [[] for _ in heads]
    for j in range(t_len):
        for h in heads:
            kj = ks[h][j:j + 1, :]
            dcol = jnp.exp(jnp.where(rowc >= j, gcs[h] - gcs[h][j:j + 1, :], 0.0))
            acols[h].append(jnp.where(rowc >= j, jnp.sum(qs[h] * kj, axis=-1, keepdims=True) * dcol, 0.0))
            lcols[h].append(jnp.where(rowc > j, jnp.sum(kbs[h] * kj, axis=-1, keepdims=True) * dcol, 0.0))
    for j in range(t_len - 1):
        sols = [sol - lcols[h][j] * sol[j:j + 1, :] for h, sol in enumerate(sols)]
    ss = [s0_ref[h] for h in heads]
    sbs = [s.astype(BF16) for s in ss]
    v_news = [sol[:, :DV] - jnp.dot(sol[:, DV:].astype(BF16), sb, preferred_element_type=F32)
              for sol, sb in zip(sols, sbs)]
    os_ = [jnp.dot((q * e).astype(BF16), sb, preferred_element_type=F32) for q, e, sb in zip(qs, egcs, sbs)]
    for j in range(t_len):
        os_ = [o + acols[h][j] * v_news[h][j:j + 1, :] for h, o in enumerate(os_)]
    upds = [lax.dot_general((k * jnp.exp(gl - gc)).astype(BF16), vn.astype(BF16), (((0,), (0,)), ((), ())),
                            preferred_element_type=F32)
            for k, gl, gc, vn in zip(ks, glasts, gcs, v_news)]
    for h in heads:
        s1_ref[h] = ss[h] * jnp.exp(glasts[h]) + upds[h]
        sl = slice(h * DV, (h + 1) * DV)
        oa_ref[:, sl] = _gated_rmsnorm(os_[h], z_ref[:, sl], nw).astype(BF16)


def _sample_a(proj, row0, bsz, t_len, hist, s0, caw, alog_row, dtb_row, nw_row):
    assert t_len == SUBLANES and row0 % t_len == 0
    r0 = row0 // t_len
    wspec = lambda shape: pl.BlockSpec(shape, lambda b: (0,) * len(shape))
    return pl.pallas_call(
        functools.partial(_sample_a_body, t_len=t_len),
        grid=(bsz,),
        in_specs=[
            pl.BlockSpec((t_len, 3 * D_A), lambda b: (r0 + b, COL_QKV // (3 * D_A))),
            pl.BlockSpec((t_len, D_A), lambda b: (r0 + b, COL_Z // D_A)),
            pl.BlockSpec((t_len, LANES), lambda b: (r0 + b, COL_BA // LANES)),
            pl.BlockSpec((None, CONV_W - 1, 3 * D_A), lambda b: (b, 0, 0)),
            pl.BlockSpec((None, H_A, DK, DV), lambda b: (b, 0, 0, 0)),
            wspec((CONV_W, 3 * D_A)), wspec((1, LANES)), wspec((1, LANES)), wspec((1, DV)),
        ],
        out_specs=[
            pl.BlockSpec((t_len, D_A), lambda b: (b, 0)),
            pl.BlockSpec((None, H_A, DK, DV), lambda b: (b, 0, 0, 0)),
        ],
        out_shape=[
            jax.ShapeDtypeStruct((bsz * t_len, D_A), BF16),
            jax.ShapeDtypeStruct((bsz, H_A, DK, DV), F32),
        ],
        scratch_shapes=[pltpu.VMEM((2 * SUBLANES, 3 * D_A), F32)],
        compiler_params=_cparams(("arbitrary",)),
        name="sample_a",
    )(proj, proj, proj, hist, s0, caw, alog_row, dtb_row, nw_row)


def _lru_coeffs(xc, wa_ref, wx_ref, lba_ref, lbx_ref, lam_ref):
    parts = []
    for n in range(NB):
        sl = slice(n * BW, (n + 1) * BW)
        xn = xc[:, sl]
        xnb = xn.astype(BF16)
        gr = _sigmoid(jnp.dot(xnb, wa_ref[n].astype(BF16), preferred_element_type=F32) + lba_ref[:, sl])
        gi = _sigmoid(jnp.dot(xnb, wx_ref[n].astype(BF16), preferred_element_type=F32) + lbx_ref[:, sl])
        log_a = -LRU_C * gr * _softplus(-lam_ref[:, sl])
        parts.append((jnp.exp(log_a), jnp.sqrt(1.0 - jnp.exp(2.0 * log_a)), gi * xn))
    return parts


def _group_scan(a, b):
    rowi = lax.broadcasted_iota(jnp.int32, a.shape, 0)
    sft = 1
    while sft < SUBLANES:
        keep = rowi >= sft
        b = b + a * jnp.where(keep, pltpu.roll(b, sft, axis=0), 0.0)
        a = a * jnp.where(keep, pltpu.roll(a, sft, axis=0), 1.0)
        sft *= 2
    return a, b


def _gelu_tanh(x):
    return 0.5 * x * (1.0 + jnp.tanh(math.sqrt(2.0 / math.pi) * (x + 0.044715 * x * x * x)))


def _lru_finish(h, yb, nbw):
    hg = h * _gelu_tanh(yb)
    return (hg * lax.rsqrt(jnp.mean(hg * hg, axis=-1, keepdims=True) + RMS_EPS) * nbw).astype(BF16)


def _lru_prompt_body(xc_ref, yb_ref, wa_ref, wx_ref, lba_ref, lbx_ref, lam_ref, nbw_ref,
                     ob_ref, hfin_ref, a_ref, b_ref, h_ref, carry_ref, *, rows):
    t = pl.program_id(1)
    parts = _lru_coeffs(xc_ref[...], wa_ref, wx_ref, lba_ref, lbx_ref, lam_ref)
    rowi = lax.broadcasted_iota(jnp.int32, (rows, BW), 0)
    first = jnp.logical_and(rowi == 0, t == 0)
    for n in range(NB):
        sl = slice(n * BW, (n + 1) * BW)
        a_n, mult_n, gix_n = parts[n]
        a_ref[:, sl] = jnp.where(first, 0.0, a_n)
        b_ref[:, sl] = jnp.where(first, 1.0, mult_n) * gix_n

    @pl.when(t == 0)
    def _():
        carry_ref[...] = jnp.zeros(carry_ref.shape, F32)

    def group(gidx, carry):
        r0 = pl.multiple_of(gidx * SUBLANES, SUBLANES)
        ag, bg = _group_scan(a_ref[pl.ds(r0, SUBLANES), :], b_ref[pl.ds(r0, SUBLANES), :])
        hg = ag * carry + bg
        h_ref[pl.ds(r0, SUBLANES), :] = hg
        return jnp.broadcast_to(hg[SUBLANES - 1:SUBLANES, :], hg.shape)

    carry = lax.fori_loop(0, rows // SUBLANES, group, carry_ref[...])
    carry_ref[...] = carry
    ob_ref[...] = _lru_finish(h_ref[...], yb_ref[...], nbw_ref[...])

    @pl.when(t == pl.num_programs(1) - 1)
    def _():
        hfin_ref[...] = carry[0:1, :]


def _lru_prompt(xbc, proj, bsz, t_len, wa, wx, lba, lbx, lam, nbw):
    rows = min(CONV_L, t_len)
    nt = t_len // rows
    mp = bsz * t_len
    wspec = lambda shape: pl.BlockSpec(shape, lambda b, t: (0,) * len(shape))
    return pl.pallas_call(
        functools.partial(_lru_prompt_body, rows=rows),
        grid=(bsz, nt),
        in_specs=[
            pl.BlockSpec((rows, D_B), lambda b, t: (b * nt + t, 0)),
            pl.BlockSpec((rows, D_B), lambda b, t: (b * nt + t, COL_YB // D_B)),
            wspec((NB, BW, BW)), wspec((NB, BW, BW)),
            wspec((1, D_B)), wspec((1, D_B)), wspec((1, D_B)), wspec((1, D_B)),
        ],
        out_specs=[
            pl.BlockSpec((rows, D_B), lambda b, t: (b * nt + t, 0)),
            pl.BlockSpec((None, 1, D_B), lambda b, t: (b, 0, 0)),
        ],
        out_shape=[
            jax.ShapeDtypeStruct((mp, D_B), BF16),
            jax.ShapeDtypeStruct((bsz, 1, D_B), F32),
        ],
        scratch_shapes=[pltpu.VMEM((rows, D_B), F32), pltpu.VMEM((rows, D_B), F32),
                        pltpu.VMEM((rows, D_B), F32), pltpu.VMEM((SUBLANES, D_B), F32)],
        compiler_params=_cparams(("arbitrary", "arbitrary")),
        name="lru_prompt",
    )(xbc, proj, wa, wx, lba, lbx, lam, nbw)


def _lru_sample_body(xb_ref, yb_ref, hist_ref, h0_ref, cbw_ref, cbb_ref, wa_ref, wx_ref, lba_ref,
                     lbx_ref, lam_ref, nbw_ref, ob_ref, h1_ref, hx_ref, a_ref, b_ref, h_ref,
                     *, nseq, t_len):
    hdr = SUBLANES
    nh = CONV_W - 1
    hx_ref[0:hdr, :] = jnp.zeros((hdr, D_B), F32)

    def conv_seq(si, _):
        r0 = pl.multiple_of(si * t_len, t_len)
        hx_ref[hdr - nh:hdr, :] = hist_ref[si]
        hx_ref[hdr:hdr + t_len, :] = xb_ref[pl.ds(r0, t_len), :]
        for n in range(NB):
            c0 = n * BW
            h_ref[pl.ds(r0, t_len), c0:c0 + BW] = (_conv_taps(hx_ref, cbw_ref, c0, BW, t_len, hdr)
                                                  + cbb_ref[:, c0:c0 + BW])
        return 0

    lax.fori_loop(0, nseq, conv_seq, 0)
    parts = _lru_coeffs(h_ref[...], wa_ref, wx_ref, lba_ref, lbx_ref, lam_ref)
    for n in range(NB):
        sl = slice(n * BW, (n + 1) * BW)
        a_n, mult_n, gix_n = parts[n]
        a_ref[:, sl] = a_n
        b_ref[:, sl] = mult_n * gix_n

    def seq(si, _):
        r0 = pl.multiple_of(si * t_len, t_len)
        ag, bg = _group_scan(a_ref[pl.ds(r0, t_len), :], b_ref[pl.ds(r0, t_len), :])
        hg = ag * h0_ref[pl.ds(si, 1), :] + bg
        h_ref[pl.ds(r0, t_len), :] = hg
        h1_ref[pl.ds(si, 1), :] = hg[t_len - 1:t_len, :]
        return 0

    lax.fori_loop(0, nseq, seq, 0)
    ob_ref[...] = _lru_finish(h_ref[...], yb_ref[...], nbw_ref[...])


def _lru_sample(proj, row0, bsz, t_len, hist, h0, cbw, cbb, wa, wx, lba, lbx, lam, nbw):
    assert t_len == SUBLANES
    ms = bsz * t_len
    assert row0 % ms == 0
    rblk = row0 // ms
    wspec = lambda shape: pl.BlockSpec(shape, lambda i: (0,) * len(shape))
    return pl.pallas_call(
        functools.partial(_lru_sample_body, nseq=bsz, t_len=t_len),
        grid=(1,),
        in_specs=[
            pl.BlockSpec((ms, D_B), lambda i: (rblk, COL_XB // D_B)),
            pl.BlockSpec((ms, D_B), lambda i: (rblk, COL_YB // D_B)),
            wspec((bsz, CONV_W - 1, D_B)), wspec((bsz, D_B)),
            wspec((CONV_W, D_B)), wspec((1, D_B)),
            wspec((NB, BW, BW)), wspec((NB, BW, BW)),
            wspec((1, D_B)), wspec((1, D_B)), wspec((1, D_B)), wspec((1, D_B)),
        ],
        out_specs=[wspec((ms, D_B)), wspec((bsz, D_B))],
        out_shape=[
            jax.ShapeDtypeStruct((ms, D_B), BF16),
            jax.ShapeDtypeStruct((bsz, D_B), F32),
        ],
        scratch_shapes=[pltpu.VMEM((2 * SUBLANES, D_B), F32), pltpu.VMEM((ms, D_B), F32),
                        pltpu.VMEM((ms, D_B), F32), pltpu.VMEM((ms, D_B), F32)],
        compiler_params=_cparams(("arbitrary",)),
        name="lru_sample",
    )(proj, proj, hist, h0, cbw, cbb, wa, wx, lba, lbx, lam, nbw)


def _layernorm_rows(v, g, b):
    mu = jnp.mean(v, axis=-1, keepdims=True)
    d = v - mu
    var = jnp.mean(d * d, axis=-1, keepdims=True)
    return d * lax.rsqrt(var + LN_EPS) * g + b


def _store_slabs(slab_ref, base, val):
    n = val.shape[0]
    for s in range(ROW_SLAB):
        slab_ref[pl.ds(base + s, n, stride=ROW_SLAB), :] = val[:, s * LANES:(s + 1) * LANES]


def _load_slab_chunk(slab_ref, base, n, s):
    return slab_ref[pl.ds(base + s, n, stride=ROW_SLAB), :]


def _outproj_body(oap_ref, oas_ref, obp_ref, obs_ref, xp_ref, xs_ref, wo_ref, g_ref, b_ref,
                  wr_ref, br_ref, x1_ref, ti_ref, tg_ref, *, n_ptiles):
    i = pl.program_id(0)

    def run(oa_ref, ob_ref, x_ref):
        mix = (jnp.dot(oa_ref[...], wo_ref[0:D_A, :], preferred_element_type=F32)
               + jnp.dot(ob_ref[...], wo_ref[D_A:D_A + D_B, :], preferred_element_type=F32))
        y = _layernorm_rows(DN_ALPHA * x_ref[...] + mix, g_ref[...], b_ref[...])
        _store_slabs(x1_ref, 0, y)
        logits = _dot3(y, wr_ref[...]) + br_ref[...]
        lane = lax.broadcasted_iota(jnp.int32, logits.shape, 1)
        lane_o = lax.broadcasted_iota(jnp.int32, ti_ref.shape, 1)
        cur = logits
        ti = jnp.zeros(ti_ref.shape, jnp.int32)
        tv = jnp.zeros(tg_ref.shape, F32)
        v0 = None
        den = None
        for kk in range(TOP_K):
            m = jnp.max(cur, axis=-1, keepdims=True)
            idx = jnp.min(jnp.where(cur == m, lane, N_EXPERTS), axis=-1, keepdims=True)
            cur = jnp.where(lane == idx, -jnp.inf, cur)
            if kk == 0:
                v0 = m
            e = jnp.exp(m - v0)
            den = e if den is None else den + e
            ti = jnp.where(lane_o == kk, idx, ti)
            tv = jnp.where(lane_o == kk, e, tv)
        ti_ref[...] = ti
        tg_ref[...] = tv / den

    @pl.when(i < n_ptiles)
    def _():
        run(oap_ref, obp_ref, xp_ref)

    @pl.when(i >= n_ptiles)
    def _():
        run(oas_ref, obs_ref, xs_ref)


def _outproj(oa_p, oa_s, ob_p, ob_s, xp2, xs2, wo, g_row, b_row, wr, br_row):
    mp, ms = xp2.shape[0], xs2.shape[0]
    tm = min(OUT_TM, ms)
    assert mp % tm == 0 and ms % tm == 0
    n_pt, n_st = mp // tm, ms // tm
    pmap = lambda i: (jnp.minimum(i, n_pt - 1), 0)
    smap = lambda i: (jnp.maximum(i - n_pt, 0), 0)
    wspec = lambda shape: pl.BlockSpec(shape, lambda i: (0, 0))
    m = mp + ms
    return pl.pallas_call(
        functools.partial(_outproj_body, n_ptiles=n_pt),
        grid=(n_pt + n_st,),
        in_specs=[
            pl.BlockSpec((tm, D_A), pmap), pl.BlockSpec((tm, D_A), smap),
            pl.BlockSpec((tm, D_B), pmap), pl.BlockSpec((tm, D_B), smap),
            pl.BlockSpec((tm, D_MODEL), pmap), pl.BlockSpec((tm, D_MODEL), smap),
            wspec((D_A + D_B, D_MODEL)), wspec((1, D_MODEL)), wspec((1, D_MODEL)),
            wspec((D_MODEL, N_EXPERTS)), wspec((1, N_EXPERTS)),
        ],
        out_specs=[
            pl.BlockSpec((tm * ROW_SLAB, LANES), lambda i: (i, 0)),
            pl.BlockSpec((tm, LANES), lambda i: (i, 0)),
            pl.BlockSpec((tm, LANES), lambda i: (i, 0)),
        ],
        out_shape=[
            jax.ShapeDtypeStruct((m * ROW_SLAB, LANES), F32),
            jax.ShapeDtypeStruct((m, LANES), jnp.int32),
            jax.ShapeDtypeStruct((m, LANES), F32),
        ],
        compiler_params=_cparams(("arbitrary",)),
        name="outproj_ln_router",
    )(oa_p, oa_s, ob_p, ob_s, xp2, xs2, wo, g_row, b_row, wr, br_row)


def _zero_tail(zbuf_ref, dst_ref, tail_row, n_blocks, sem):
    scale = ROW_SLAB if len(dst_ref.shape) == 2 else 1

    def tail_copy(t):
        z0 = pl.multiple_of((tail_row + t * MOE_RB) * scale, MOE_RB * scale)
        return pltpu.make_async_copy(zbuf_ref, dst_ref.at[pl.ds(z0, MOE_RB * scale)], sem)

    def start(t, _):
        tail_copy(t).start()
        return 0

    def wait(t, _):
        tail_copy(t).wait()
        return 0

    lax.fori_loop(0, n_blocks, start, 0)
    lax.fori_loop(0, n_blocks, wait, 0)


def _dispatch_body(zrow_ref, slot_ref, x_ref, xs_ref, zbuf_ref, zsem, rsem, *, tm):
    i = pl.program_id(0)

    @pl.when(i == 0)
    def _():
        zbuf_ref[...] = jnp.zeros(zbuf_ref.shape, F32)

        def zero_copy(e):
            z0 = pl.multiple_of(zrow_ref[e], MOE_RB)
            return pltpu.make_async_copy(zbuf_ref, xs_ref.at[pl.ds(z0, MOE_RB)], zsem)

        for e in range(N_EXPERTS):
            @pl.when(zrow_ref[e] >= 0)
            def _():
                zero_copy(e).start()
        for e in range(N_EXPERTS):
            @pl.when(zrow_ref[e] >= 0)
            def _():
                zero_copy(e).wait()
        _zero_tail(zbuf_ref, xs_ref, zrow_ref[N_EXPERTS], zrow_ref[N_EXPERTS + 1], zsem)

    def row_copy(r, dst):
        src0 = pl.multiple_of(r * ROW_SLAB, ROW_SLAB)
        return pltpu.make_async_copy(x_ref.at[pl.ds(src0, ROW_SLAB)], xs_ref.at[dst], rsem)

    def issue(r, _):
        for kk in range(TOP_K):
            row_copy(r, slot_ref[0, r * TOP_K + kk]).start(priority=kk % 2)
        return 0

    lax.fori_loop(0, tm, issue, 0)

    def drain(r, _):
        for kk in range(TOP_K):
            row_copy(r, 0).wait()
        return 0

    lax.fori_loop(0, tm, drain, 0)


def _dispatch(x1, slot_tiles, zrow, n_rows):
    m = x1.shape[0] // ROW_SLAB
    tm = TOK_TM
    assert m % tm == 0
    return pl.pallas_call(
        functools.partial(_dispatch_body, tm=tm),
        grid_spec=pltpu.PrefetchScalarGridSpec(
            num_scalar_prefetch=1,
            grid=(m // tm,),
            in_specs=[
                pl.BlockSpec((None, 1, tm * TOP_K), lambda i, z: (i, 0, 0), memory_space=pltpu.SMEM),
                pl.BlockSpec((tm * ROW_SLAB, LANES), lambda i, z: (i, 0)),
            ],
            out_specs=pl.BlockSpec(memory_space=pl.ANY),
            scratch_shapes=[pltpu.VMEM((MOE_RB, ROW_SLAB, LANES), F32),
                            pltpu.SemaphoreType.DMA(()), pltpu.SemaphoreType.DMA(())],
        ),
        out_shape=jax.ShapeDtypeStruct((n_rows, ROW_SLAB, LANES), F32),
        compiler_params=_cparams(("arbitrary",)),
        name="moe_dispatch",
    )(zrow, slot_tiles, x1)


def _moe_body(ie_ref, irow_ref, insub_ref, tail_ref, xs_ref, wg_ref, wu_ref, bgate_ref, bup_ref,
              wd_ref, bd_ref, ys_ref, xstage_ref, xb_ref, act_ref, ybuf_ref, wgb_ref, wub_ref, wdb_ref,
              isem, osem):
    wi = pl.program_id(0)
    j = pl.program_id(1)
    n_w = pl.num_programs(0)
    nsub = insub_ref[wi]
    row0 = irow_ref[wi]
    nt = MOE_NT
    is_last = wi == n_w - 1
    nxt = jnp.minimum(wi + 1, n_w - 1)
    nsub_next = jnp.where(is_last, 0, insub_ref[nxt])
    row_next = irow_ref[nxt]
    prv = jnp.maximum(wi - 1, 0)
    nsub_prev = jnp.where(wi == 0, 0, insub_ref[prv])
    row_prev = irow_ref[prv]

    class _XCopy:
        def __init__(self, row_base, s, slot):
            g0 = pl.multiple_of(row_base + s * MOE_RB, MOE_RB)
            self.copies = [
                pltpu.make_async_copy(xs_ref.at[pl.ds(g0, MOE_RB), c],
                                      xstage_ref.at[slot, :, pl.ds(c * LANES, LANES)], isem.at[slot])
                for c in range(ROW_SLAB)]

        def start(self):
            for cp in self.copies:
                cp.start()

        def wait(self):
            for cp in self.copies:
                cp.wait()

    x_copy = _XCopy

    def to_matmul_layout(s, slot):
        r0 = pl.multiple_of(s * MOE_RB, MOE_RB)
        xb_ref[pl.ds(r0, MOE_RB), :] = xstage_ref[slot].astype(BF16)

    def y_copy(row_base, s):
        b0 = pl.multiple_of(s * MOE_RB * ROW_SLAB, MOE_RB * ROW_SLAB)
        g0 = pl.multiple_of(row_base * ROW_SLAB, MOE_RB * ROW_SLAB) + b0
        return pltpu.make_async_copy(ybuf_ref.at[pl.ds(b0, MOE_RB * ROW_SLAB)],
                                     ys_ref.at[pl.ds(g0, MOE_RB * ROW_SLAB)], osem)

    @pl.when(jnp.logical_and(wi == 0, j == 0))
    def _():
        def load(s, _):
            cp = x_copy(row0, s, 0)
            cp.start()
            cp.wait()
            to_matmul_layout(s, 0)
            return 0

        lax.fori_loop(0, nsub, load, 0)

    @pl.when(jnp.logical_and(j == nt - 1, nsub_next > 0))
    def _():
        x_copy(row_next, 0, 0).start()

    @pl.when(jnp.logical_and(j >= nt, j - nt < nsub_next))
    def _():
        s = j - nt
        slot = lax.rem(s, 2)
        x_copy(row_next, s, slot).wait()

        @pl.when(s + 1 < nsub_next)
        def _():
            x_copy(row_next, s + 1, 1 - slot).start()

        to_matmul_layout(s, slot)

    @pl.when(jnp.logical_and(j == nt, nsub_prev > 0))
    def _():
        def drain(s, _):
            y_copy(row_prev, s).wait()
            return 0

        lax.fori_loop(0, nsub_prev, drain, 0)

    @pl.when(nsub > 0)
    def _():
        @pl.when(j < nt)
        def _():
            wgb_ref[...] = wg_ref[...].astype(BF16)
            wub_ref[...] = wu_ref[...].astype(BF16)

            def sub(s, _):
                r0 = pl.multiple_of(s * MOE_RB, MOE_RB)
                x = xb_ref[pl.ds(r0, MOE_RB), :]
                hg = jnp.dot(x, wgb_ref[...], preferred_element_type=F32) + bgate_ref[...]
                hu = jnp.dot(x, wub_ref[...], preferred_element_type=F32) + bup_ref[...]
                gate = jnp.minimum(hg, SWIGLU_LIMIT)
                up = jnp.clip(hu, -SWIGLU_LIMIT, SWIGLU_LIMIT)
                glu = gate * _sigmoid(SWIGLU_ALPHA * gate)
                act_ref[j, pl.ds(r0, MOE_RB), :] = (glu * (up + 1.0)).astype(BF16)
                return 0

            lax.fori_loop(0, nsub, sub, 0)

        @pl.when(j >= nt)
        def _():
            wdb_ref[...] = wd_ref[...].astype(BF16)
            n = j - nt

            def sub(s, _):
                r0 = pl.multiple_of(s * MOE_RB, MOE_RB)
                acc = jnp.dot(act_ref[0, pl.ds(r0, MOE_RB), :], wdb_ref[0:MOE_TF, :],
                              preferred_element_type=F32)
                for jj in range(1, nt):
                    acc = acc + jnp.dot(act_ref[jj, pl.ds(r0, MOE_RB), :],
                                        wdb_ref[jj * MOE_TF:(jj + 1) * MOE_TF, :],
                                        preferred_element_type=F32)
                acc = acc + bd_ref[...]
                cpt = MOE_TF // LANES
                for cc in range(cpt):
                    ybuf_ref[pl.ds(r0 * ROW_SLAB + n * cpt + cc, MOE_RB, stride=ROW_SLAB), :] = (
                        acc[:, cc * LANES:(cc + 1) * LANES])
                return 0

            lax.fori_loop(0, nsub, sub, 0)

        @pl.when(j == 2 * nt - 1)
        def _():
            def store(s, _):
                y_copy(row0, s).start()
                return 0

            lax.fori_loop(0, nsub, store, 0)

    @pl.when(jnp.logical_and(is_last, j == 2 * nt - 1))
    def _():
        def drain(s, _):
            y_copy(row0, s).wait()
            return 0

        lax.fori_loop(0, nsub, drain, 0)
        zsrc = ybuf_ref.at[pl.ds(0, MOE_RB * ROW_SLAB)]
        zsrc[...] = jnp.zeros(zsrc.shape, F32)
        _zero_tail(zsrc, ys_ref, tail_ref[0], tail_ref[1], osem)


def _moe(xs, n_items, item_e, item_row, item_nsub, tail, w_up, b_up3, w_down, b_down3):
    n_rows = xs.shape[0] * ROW_SLAB
    nt = MOE_NT
    assert MOE_SB <= nt
    up_off = D_FF // MOE_TF
    p1 = lambda j, n, w: jnp.where(n[w] > 0, jnp.minimum(j, nt - 1), nt - 1)
    p2 = lambda j, n, w: jnp.where(n[w] > 0, jnp.maximum(j - nt, 0), nt - 1)
    return pl.pallas_call(
        _moe_body,
        grid_spec=pltpu.PrefetchScalarGridSpec(
            num_scalar_prefetch=4,
            grid=(n_items, 2 * nt),
            in_specs=[
                pl.BlockSpec(memory_space=pl.ANY),
                pl.BlockSpec((None, D_MODEL, MOE_TF), lambda w, j, e, r, n, t: (e[w], 0, p1(j, n, w))),
                pl.BlockSpec((None, D_MODEL, MOE_TF),
                             lambda w, j, e, r, n, t: (e[w], 0, up_off + p1(j, n, w))),
                pl.BlockSpec((None, 1, MOE_TF), lambda w, j, e, r, n, t: (e[w], 0, p1(j, n, w))),
                pl.BlockSpec((None, 1, MOE_TF), lambda w, j, e, r, n, t: (e[w], 0, up_off + p1(j, n, w))),
                pl.BlockSpec((None, D_FF, MOE_TF), lambda w, j, e, r, n, t: (e[w], 0, p2(j, n, w))),
                pl.BlockSpec((None, 1, MOE_TF), lambda w, j, e, r, n, t: (e[w], 0, p2(j, n, w))),
            ],
            out_specs=pl.BlockSpec(memory_space=pl.ANY),
            scratch_shapes=[
                pltpu.VMEM((2, MOE_RB, D_MODEL), F32),
                pltpu.VMEM((MOE_R, D_MODEL), BF16),
                pltpu.VMEM((nt, MOE_R, MOE_TF), BF16),
                pltpu.VMEM((MOE_R * ROW_SLAB, LANES), F32),
                pltpu.VMEM((D_MODEL, MOE_TF), BF16),
                pltpu.VMEM((D_MODEL, MOE_TF), BF16),
                pltpu.VMEM((D_FF, MOE_TF), BF16),
                pltpu.SemaphoreType.DMA((2,)), pltpu.SemaphoreType.DMA(()),
            ],
        ),
        out_shape=jax.ShapeDtypeStruct((n_rows, LANES), F32),
        compiler_params=_cparams(("arbitrary", "arbitrary")),
        name="moe_experts",
    )(item_e, item_row, item_nsub, tail, xs, w_up, w_up, b_up3, b_up3, w_down, b_down3)


def _combine_body(slot_ref, ys_ref, x1_ref, gate_ref, g_ref, b_ref, yp_ref, ysm_ref, buf_ref, v_ref,
                  sem, *, tm, n_ptiles):
    def row_copy(r, kk, src):
        src0 = pl.multiple_of(src * ROW_SLAB, ROW_SLAB)
        dst0 = pl.multiple_of(r * ROW_SLAB, ROW_SLAB)
        return pltpu.make_async_copy(ys_ref.at[pl.ds(src0, ROW_SLAB)],
                                     buf_ref.at[kk, pl.ds(dst0, ROW_SLAB)], sem)

    def issue(r, _):
        for kk in range(TOP_K):
            row_copy(r, kk, slot_ref[0, r * TOP_K + kk]).start(priority=kk % 2)
        return 0

    lax.fori_loop(0, tm, issue, 0)

    def drain(r, _):
        for kk in range(TOP_K):
            row_copy(r, kk, 0).wait()
        return 0

    lax.fori_loop(0, tm, drain, 0)
    gates = gate_ref[...]
    for c in range(ROW_SLAB):
        acc = DN_ALPHA * _load_slab_chunk(x1_ref, 0, tm, c)
        for kk in range(TOP_K):
            acc = acc + gates[:, kk:kk + 1] * _load_slab_chunk(buf_ref.at[kk], 0, tm, c)
        v_ref[:, c * LANES:(c + 1) * LANES] = acc
    y = _layernorm_rows(v_ref[...], g_ref[...], b_ref[...])

    @pl.when(pl.program_id(0) < n_ptiles)
    def _():
        yp_ref[...] = y

    @pl.when(pl.program_id(0) >= n_ptiles)
    def _():
        ysm_ref[...] = y


def _combine(ys, slot_tiles, x1, gates, g_row, b_row, mp):
    m = x1.shape[0] // ROW_SLAB
    tm = TOK_TM
    assert mp % tm == 0 and (m - mp) % tm == 0
    n_pt = mp // tm
    wspec = lambda shape: pl.BlockSpec(shape, lambda i: (0, 0))
    return pl.pallas_call(
        functools.partial(_combine_body, tm=tm, n_ptiles=n_pt),
        grid=(m // tm,),
        in_specs=[
            pl.BlockSpec((None, 1, tm * TOP_K), lambda i: (i, 0, 0), memory_space=pltpu.SMEM),
            pl.BlockSpec(memory_space=pl.ANY),
            pl.BlockSpec((tm * ROW_SLAB, LANES), lambda i: (i, 0)),
            pl.BlockSpec((tm, LANES), lambda i: (i, 0)),
            wspec((1, D_MODEL)), wspec((1, D_MODEL)),
        ],
        out_specs=[pl.BlockSpec((tm, D_MODEL), lambda i: (jnp.minimum(i, n_pt - 1), 0)),
                   pl.BlockSpec((tm, D_MODEL), lambda i: (jnp.maximum(i - n_pt, 0), 0))],
        out_shape=[jax.ShapeDtypeStruct((mp, D_MODEL), F32),
                   jax.ShapeDtypeStruct((m - mp, D_MODEL), F32)],
        scratch_shapes=[pltpu.VMEM((TOP_K, tm * ROW_SLAB, LANES), F32),
                        pltpu.VMEM((tm, D_MODEL), F32), pltpu.SemaphoreType.DMA(())],
        compiler_params=_cparams(("arbitrary",)),
        name="moe_combine_ln",
    )(slot_tiles, ys, x1, gates, g_row, b_row)


def _routing_tables(top_i, m):
    e_ids = jnp.arange(N_EXPERTS, dtype=jnp.int32)
    onehot = (top_i[:, :, None] == e_ids[None, None, :]).astype(jnp.int32)
    mask = jnp.sum(onehot, axis=1)
    incl = jnp.cumsum(mask, axis=0)
    pos = incl - mask
    counts = incl[-1]
    nsub = (counts + MOE_RB - 1) // MOE_RB
    padded = nsub * MOE_RB
    gend = jnp.cumsum(padded)
    gstart = gend - padded
    slot = jnp.sum(onehot * (gstart[None, None, :] + pos[:, None, :]), axis=2)
    n_rows = (m * TOP_K // MOE_RB + N_EXPERTS) * MOE_RB
    tail = jnp.stack([gend[-1], (n_rows - gend[-1]) // MOE_RB]).astype(jnp.int32)
    zrow = jnp.concatenate([jnp.where(counts > 0, gend - MOE_RB, -1).astype(jnp.int32), tail])
    n_items = N_EXPERTS + (m * TOP_K // MOE_RB + N_EXPERTS) // MOE_SB
    ipe = (nsub + MOE_SB - 1) // MOE_SB
    iend = jnp.cumsum(ipe)
    istart = iend - ipe
    total = iend[-1]
    wid = jnp.arange(n_items, dtype=jnp.int32)
    wclamp = jnp.minimum(wid, total - 1)
    ie = jnp.minimum(jnp.searchsorted(iend, wclamp, side="right"), N_EXPERTS - 1).astype(jnp.int32)
    jn = wclamp - istart[ie]
    irow = (gstart[ie] + jn * MOE_R).astype(jnp.int32)
    insub = jnp.where(wid < total, jnp.clip(nsub[ie] - jn * MOE_SB, 0, MOE_SB), 0).astype(jnp.int32)
    return slot.astype(jnp.int32), zrow, tail, total.astype(jnp.int32), ie, irow, insub, n_rows


def kernel(x_prompt, x_sample, state_conv_a, state_delta, state_conv_b, state_lru, w_in, conv_a_w,
           a_log, dt_bias, norm_a_w, conv_b_w, conv_b_b, lru_wa, lru_ba, lru_wx, lru_bx, lru_lambda,
           norm_b_w, w_out, ln1_g, ln1_b, w_router, b_router, w_up, b_up, w_down, b_down, ln2_g, ln2_b):
    bp, tp, _ = x_prompt.shape
    bs, ts, _ = x_sample.shape
    mp, ms = bp * tp, bs * ts
    m = mp + ms
    l = 0
    xp2 = x_prompt.reshape(mp, D_MODEL)
    xs2 = x_sample.reshape(ms, D_MODEL)

    wi = w_in[l]
    c_qkvz = 4 * D_A
    wp = jnp.concatenate(
        [wi[:, :c_qkvz].astype(BF16), wi[:, c_qkvz + 2 * H_A:].astype(BF16),
         wi[:, c_qkvz:c_qkvz + 2 * H_A].astype(BF16),
         jnp.zeros((D_MODEL, 2 * LANES - 2 * H_A), BF16)], axis=1)
    wo = w_out[l].astype(BF16)
    pad_h = lambda v: jnp.zeros((1, LANES), F32).at[0, H_A:2 * H_A].set(v)
    alog_row = pad_h(a_log[l])
    dtb_row = pad_h(dt_bias[l])
    row = lambda v: v.reshape(1, -1)

    proj = _inproj(xp2, xs2, wp)

    qkv_act, xbc, bg = _convact_prompt(proj, bp, tp, conv_a_w[l], conv_b_w[l], row(conv_b_b[l]),
                                       alog_row, dtb_row)
    u, w, qe, kdt, aqk, el = _delta_prep(qkv_act, bg, bp, tp)
    oa_p, sd_p = _delta_seq(u, w, qe, kdt, aqk, el, proj, row(norm_a_w[l]), bp, tp)
    ob_p, h_p = _lru_prompt(xbc, proj, bp, tp, lru_wa[l], lru_wx[l], row(lru_ba[l]), row(lru_bx[l]),
                            row(lru_lambda[l]), row(norm_b_w[l]))
    oa_s, sd_s = _sample_a(proj, mp, bs, ts, state_conv_a[l], state_delta[l], conv_a_w[l],
                           alog_row, dtb_row, row(norm_a_w[l]))
    ob_s, h_s = _lru_sample(proj, mp, bs, ts, state_conv_b[l], state_lru[l], conv_b_w[l],
                            row(conv_b_b[l]), lru_wa[l], lru_wx[l], row(lru_ba[l]), row(lru_bx[l]),
                            row(lru_lambda[l]), row(norm_b_w[l]))

    x1, ti, tg = _outproj(oa_p, oa_s, ob_p, ob_s, xp2, xs2, wo, row(ln1_g[l]), row(ln1_b[l]),
                          w_router[l], row(b_router[l]))

    slot, zrow, tail, n_used, ie, irow, insub, n_rows = _routing_tables(ti[:, :TOP_K], m)
    slot_tiles = slot.reshape(m // TOK_TM, 1, TOK_TM * TOP_K)
    xs_sorted = _dispatch(x1, slot_tiles, zrow, n_rows)
    ys = _moe(xs_sorted, n_used, ie, irow, insub, tail, w_up[l], b_up[l].reshape(N_EXPERTS, 1, 2 * D_FF),
              w_down[l], b_down[l].reshape(N_EXPERTS, 1, D_MODEL))
    y_p, y_s = _combine(ys, slot_tiles, x1, tg, row(ln2_g[l]), row(ln2_b[l]), mp)

    y_prompt = y_p.reshape(bp, tp, D_MODEL)
    y_sample = y_s.reshape(bs, ts, D_MODEL)
    nh = CONV_W - 1
    assert tp % SUBLANES == 0 and ts == SUBLANES and nh <= SUBLANES
    pg = proj.reshape(m // SUBLANES, SUBLANES, N_PROJ)
    gp = tp // SUBLANES

    def last_rows(g0, g1, gstep, c0, width):
        return lax.slice(pg, (g0, SUBLANES - nh, c0), (g1, SUBLANES, c0 + width), (gstep, 1, 1))

    ca_p = last_rows(gp - 1, mp // SUBLANES, gp, COL_QKV, 3 * D_A)
    cb_p = last_rows(gp - 1, mp // SUBLANES, gp, COL_XB, D_B)
    ca_s = last_rows(mp // SUBLANES, m // SUBLANES, 1, COL_QKV, 3 * D_A)
    cb_s = last_rows(mp // SUBLANES, m // SUBLANES, 1, COL_XB, D_B)
    return (y_prompt, y_sample,
            ca_p[None], sd_p[None], cb_p[None], h_p.reshape(1, bp, D_B),
            ca_s[None], sd_s[None], cb_s[None], h_s[None])
```

```python
import functools
import math

import jax
import jax.numpy as jnp
from jax import lax
from jax.experimental import pallas as pl
from jax.experimental.pallas import tpu as pltpu

F32 = jnp.float32
BF16 = jnp.bfloat16
HIGHEST = lax.Precision.HIGHEST

D_MODEL = 2048
D_A = 1024
H_A = 8
DK = 128
DV = 128
CONV_W = 4
D_B = 1024
NB = 8
BW = 128
LRU_C = 8.0
N_EXPERTS = 32
TOP_K = 4
D_FF = 2048
SWIGLU_LIMIT = 7.0
SWIGLU_ALPHA = 1.702
DEPTH = 1
DN_ALPHA = (2.0 * DEPTH) ** 0.25
LN_EPS = 1e-5
RMS_EPS = 1e-6
L2_EPS = 1e-6

LANES = 128
SUBLANES = 8
VMEM_LIMIT_BYTES = 56 * 1024 * 1024

COL_QKV = 0
COL_Z = 3 * D_A
COL_XB = 4 * D_A
COL_YB = 4 * D_A + D_B
COL_BA = 4 * D_A + 2 * D_B
N_PROJ = COL_BA + 2 * LANES

INPROJ_TM = 512
INPROJ_TN = 1280
CONV_L = 256
DELTA_C = 128
OUT_TM = 256
MOE_RB = 256
MOE_SB = 6
MOE_R = MOE_RB * MOE_SB
MOE_TF = 256
MOE_NT = D_FF // MOE_TF
TOK_TM = 256
ROW_SLAB = D_MODEL // LANES


def _cparams(sem, vmem=VMEM_LIMIT_BYTES):
    return pltpu.CompilerParams(dimension_semantics=sem, vmem_limit_bytes=vmem)


def _sigmoid(x):
    return 1.0 / (1.0 + jnp.exp(-x))


def _softplus(x):
    return jnp.maximum(x, 0.0) + jnp.log(1.0 + jnp.exp(-jnp.abs(x)))


def _wprep_body(w_ref, o_ref):
    c = 4 * D_A
    nba = 2 * H_A
    o_ref[:, 0:c] = w_ref[:, 0:c].astype(BF16)
    o_ref[:, c:c + 2 * D_B] = w_ref[:, c + nba:c + nba + 2 * D_B].astype(BF16)
    ba = w_ref[:, c:c + LANES]
    lane = lax.broadcasted_iota(jnp.int32, ba.shape, 1)
    o_ref[:, COL_BA:COL_BA + LANES] = jnp.where(lane < nba, ba, 0.0).astype(BF16)
    o_ref[:, COL_BA + LANES:N_PROJ] = jnp.zeros((ba.shape[0], N_PROJ - COL_BA - LANES), BF16)


def _wprep(w_in, l):
    rows = 256
    d_in = w_in.shape[-1]
    assert d_in == 4 * D_A + 2 * H_A + 2 * D_B and D_MODEL % rows == 0
    return pl.pallas_call(
        _wprep_body,
        grid=(D_MODEL // rows,),
        in_specs=[pl.BlockSpec((None, rows, d_in), lambda i: (l, i, 0))],
        out_specs=pl.BlockSpec((rows, N_PROJ), lambda i: (i, 0)),
        out_shape=jax.ShapeDtypeStruct((D_MODEL, N_PROJ), BF16),
        compiler_params=_cparams(("arbitrary",)),
        name="wprep",
    )(w_in)


def _inproj_body(xp_ref, xs_ref, w_ref, o_ref, xb_ref, *, n_ptiles):
    i = pl.program_id(0)

    @pl.when(pl.program_id(1) == 0)
    def _():
        @pl.when(i < n_ptiles)
        def _():
            xb_ref[...] = xp_ref[...].astype(BF16)

        @pl.when(i >= n_ptiles)
        def _():
            xb_ref[...] = xs_ref[...].astype(BF16)

    o_ref[...] = jnp.dot(xb_ref[...], w_ref[...], preferred_element_type=F32)


def _inproj(xp2, xs2, wp):
    mp, ms = xp2.shape[0], xs2.shape[0]
    tm = min(INPROJ_TM, ms)
    assert mp % tm == 0 and ms % tm == 0
    n_pt, n_st = mp // tm, ms // tm
    n_nt = N_PROJ // INPROJ_TN
    return pl.pallas_call(
        functools.partial(_inproj_body, n_ptiles=n_pt),
        grid=(n_pt + n_st, n_nt),
        in_specs=[
            pl.BlockSpec((tm, D_MODEL), lambda i, j: (jnp.minimum(i, n_pt - 1), 0)),
            pl.BlockSpec((tm, D_MODEL), lambda i, j: (jnp.maximum(i - n_pt, 0), 0)),
            pl.BlockSpec((D_MODEL, INPROJ_TN), lambda i, j: (0, j)),
        ],
        out_specs=pl.BlockSpec((tm, INPROJ_TN), lambda i, j: (i, j)),
        out_shape=jax.ShapeDtypeStruct((mp + ms, N_PROJ), F32),
        scratch_shapes=[pltpu.VMEM((tm, D_MODEL), BF16)],
        compiler_params=_cparams(("arbitrary", "arbitrary")),
        name="inproj",
    )(xp2, xs2, wp)


def _conv_taps(h_ref, w_ref, c0, width, rows, base):
    acc = h_ref[base:base + rows, c0:c0 + width] * w_ref[CONV_W - 1:CONV_W, c0:c0 + width]
    for j in range(1, CONV_W):
        acc = acc + (h_ref[base - j:base - j + rows, c0:c0 + width]
                     * w_ref[CONV_W - 1 - j:CONV_W - j, c0:c0 + width])
    return acc


def _qkv_activation(acc, blk):
    a = acc * _sigmoid(acc)
    if blk < 2 * H_A:
        a = a * lax.rsqrt(jnp.sum(a * a, axis=-1, keepdims=True) + L2_EPS)
        if blk < H_A:
            a = a * (DK ** -0.5)
    return a


def _beta_g(ba, alog_ref, dtb_ref):
    lane = lax.broadcasted_iota(jnp.int32, ba.shape, 1)
    beta = _sigmoid(ba)
    g = -jnp.exp(alog_ref[...]) * _softplus(ba + dtb_ref[...])
    return jnp.where(lane < H_A, beta, jnp.where(lane < 2 * H_A, g, 0.0))


def _convact_prompt_body(qkv_ref, xb_ref, ba_ref, caw_ref, cbw_ref, cbb_ref, alog_ref, dtb_ref,
                         qkvo_ref, xbo_ref, bgo_ref, hq_ref, hx_ref, *, rows):
    t = pl.program_id(1)
    hdr = SUBLANES

    @pl.when(t == 0)
    def _():
        hq_ref[0:hdr, :] = jnp.zeros((hdr, 3 * D_A), F32)
        hx_ref[0:hdr, :] = jnp.zeros((hdr, D_B), F32)

    @pl.when(t > 0)
    def _():
        hq_ref[0:hdr, :] = hq_ref[rows:rows + hdr, :]
        hx_ref[0:hdr, :] = hx_ref[rows:rows + hdr, :]

    hq_ref[hdr:hdr + rows, :] = qkv_ref[...]
    hx_ref[hdr:hdr + rows, :] = xb_ref[...]
    for blk in range(3 * H_A):
        c0 = blk * DK
        acc = _conv_taps(hq_ref, caw_ref, c0, DK, rows, hdr)
        qkvo_ref[:, c0:c0 + DK] = _qkv_activation(acc, blk)
    for blk in range(NB):
        c0 = blk * BW
        xbo_ref[:, c0:c0 + BW] = _conv_taps(hx_ref, cbw_ref, c0, BW, rows, hdr) + cbb_ref[:, c0:c0 + BW]
    bgo_ref[...] = _beta_g(ba_ref[...], alog_ref, dtb_ref)


def _convact_prompt(proj, bsz, t_len, caw, cbw, cbb, alog_row, dtb_row):
    rows = min(CONV_L, t_len)
    assert t_len % rows == 0
    nt = t_len // rows
    mp = bsz * t_len
    wspec = lambda shape: pl.BlockSpec(shape, lambda b, t: (0, 0))
    return pl.pallas_call(
        functools.partial(_convact_prompt_body, rows=rows),
        grid=(bsz, nt),
        in_specs=[
            pl.BlockSpec((rows, 3 * D_A), lambda b, t: (b * nt + t, COL_QKV // (3 * D_A))),
            pl.BlockSpec((rows, D_B), lambda b, t: (b * nt + t, COL_XB // D_B)),
            pl.BlockSpec((rows, LANES), lambda b, t: (b * nt + t, COL_BA // LANES)),
            wspec((CONV_W, 3 * D_A)), wspec((CONV_W, D_B)), wspec((1, D_B)),
            wspec((1, LANES)), wspec((1, LANES)),
        ],
        out_specs=[
            pl.BlockSpec((rows, 3 * D_A), lambda b, t: (b * nt + t, 0)),
            pl.BlockSpec((rows, D_B), lambda b, t: (b * nt + t, 0)),
            pl.BlockSpec((rows, LANES), lambda b, t: (b * nt + t, 0)),
        ],
        out_shape=[
            jax.ShapeDtypeStruct((mp, 3 * D_A), F32),
            jax.ShapeDtypeStruct((mp, D_B), F32),
            jax.ShapeDtypeStruct((mp, LANES), F32),
        ],
        scratch_shapes=[pltpu.VMEM((SUBLANES + rows + SUBLANES, 3 * D_A), F32),
                        pltpu.VMEM((SUBLANES + rows + SUBLANES, D_B), F32)],
        compiler_params=_cparams(("arbitrary", "arbitrary")),
        name="convact_prompt",
    )(proj, proj, proj, caw, cbw, cbb, alog_row, dtb_row)


def _dot_nt(a, b, precision=None):
    return lax.dot_general(a, b, (((1,), (1,)), ((), ())), precision=precision,
                           preferred_element_type=F32)


def _split_bf16(x):
    hi = x.astype(BF16)
    return hi, (x - hi.astype(F32)).astype(BF16)


def _dot3(a, b):
    ah, al = _split_bf16(a)
    bh, bl = _split_bf16(b)
    return (jnp.dot(ah, bh, preferred_element_type=F32)
            + (jnp.dot(al, bh, preferred_element_type=F32) + jnp.dot(ah, bl, preferred_element_type=F32)))


def _delta_prep_body(qkv_ref, bg_ref, u_ref, w_ref, qe_ref, kdt_ref, aqk_ref, el_ref, *, c):
    row = lax.broadcasted_iota(jnp.int32, (c, c), 0)
    col = lax.broadcasted_iota(jnp.int32, (c, c), 1)
    incl = row >= col
    strict = row > col
    eye = (row == col).astype(F32)
    tril = incl.astype(F32)
    eye_l = (lax.broadcasted_iota(jnp.int32, (LANES, LANES), 0)
             == lax.broadcasted_iota(jnp.int32, (LANES, LANES), 1)).astype(F32)

    bg = bg_ref[...]
    gc_all = jnp.dot(tril, bg, precision=HIGHEST, preferred_element_type=F32)
    gc_t = _dot_nt(eye_l, gc_all, precision=HIGHEST)
    heads = range(H_A)
    lmats, rhss = [], []
    for h in heads:
        q = qkv_ref[:, h * DK:(h + 1) * DK]
        k = qkv_ref[:, (H_A + h) * DK:(H_A + h + 1) * DK]
        v = qkv_ref[:, (2 * H_A + h) * DK:(2 * H_A + h + 1) * DK]
        beta = bg[:, h:h + 1]
        gc = gc_all[:, H_A + h:H_A + h + 1]
        g_last = gc_all[c - 1:c, H_A + h:H_A + h + 1]
        diff = gc - gc_t[H_A + h:H_A + h + 1, :]
        decay = jnp.where(incl, jnp.exp(jnp.where(incl, diff, 0.0)), 0.0)
        kb = k * beta
        egc = jnp.exp(gc)
        kbf = k.astype(BF16)
        kk = _dot_nt(kbf, kbf)
        lmats.append(jnp.where(strict, kk * beta * decay, 0.0))
        rhss.append(jnp.concatenate([v * beta, kb * egc], axis=1))
        qe_ref[:, h * DK:(h + 1) * DK] = (q * egc).astype(BF16)
        aqk_ref[h] = (_dot_nt(q.astype(BF16), kbf) * decay).astype(BF16)
        kdec = k * jnp.exp(g_last - gc)
        kdt_ref[h] = _dot_nt(eye_l.astype(BF16), kdec.astype(BF16)).astype(BF16)
        el_ref[h] = jnp.broadcast_to(jnp.exp(g_last), (SUBLANES, LANES))
    xbs = [(-lm).astype(BF16) for lm in lmats]
    t0s = [eye - lm for lm in lmats]
    p = 2
    while p < c:
        xbs = [jnp.dot(xb, xb, preferred_element_type=F32).astype(BF16) for xb in xbs]
        t0s = [t0 + jnp.dot(t0.astype(BF16), xb, preferred_element_type=F32) for t0, xb in zip(t0s, xbs)]
        p *= 2
    t0bs = [t0.astype(BF16) for t0 in t0s]
    sol0s = [jnp.dot(t0b, rhs.astype(BF16), preferred_element_type=F32) for t0b, rhs in zip(t0bs, rhss)]
    resids = [rhs - sol0 - _dot3(lm, sol0) for rhs, sol0, lm in zip(rhss, sol0s, lmats)]
    for h in heads:
        sol = sol0s[h] + jnp.dot(t0bs[h], resids[h].astype(BF16), preferred_element_type=F32)
        u_ref[:, h * DV:(h + 1) * DV] = sol[:, :DV]
        w_ref[:, h * DK:(h + 1) * DK] = sol[:, DV:].astype(BF16)


def _delta_prep(qkv_act, bg, bsz, t_len):
    c = DELTA_C
    assert t_len % c == 0
    nc = t_len // c
    mp = bsz * t_len
    n_chunks = bsz * nc
    rowspec = lambda w: pl.BlockSpec((c, w), lambda i: (i, 0))
    return pl.pallas_call(
        functools.partial(_delta_prep_body, c=c),
        grid=(n_chunks,),
        in_specs=[rowspec(3 * D_A), rowspec(LANES)],
        out_specs=[
            rowspec(D_A), rowspec(D_A), rowspec(D_A),
            pl.BlockSpec((None, H_A, DK, c), lambda i: (i, 0, 0, 0)),
            pl.BlockSpec((None, H_A, c, c), lambda i: (i, 0, 0, 0)),
            pl.BlockSpec((None, H_A, SUBLANES, LANES), lambda i: (i, 0, 0, 0)),
        ],
        out_shape=[
            jax.ShapeDtypeStruct((mp, D_A), F32),
            jax.ShapeDtypeStruct((mp, D_A), BF16),
            jax.ShapeDtypeStruct((mp, D_A), BF16),
            jax.ShapeDtypeStruct((n_chunks, H_A, DK, c), BF16),
            jax.ShapeDtypeStruct((n_chunks, H_A, c, c), BF16),
            jax.ShapeDtypeStruct((n_chunks, H_A, SUBLANES, LANES), F32),
        ],
        compiler_params=_cparams(("arbitrary",)),
        name="delta_prep",
    )(qkv_act, bg)


def _gated_rmsnorm(o, z, nw):
    on = o * lax.rsqrt(jnp.mean(o * o, axis=-1, keepdims=True) + RMS_EPS) * nw
    return on * (z * _sigmoid(z))


def _delta_seq_body(u_ref, w_ref, qe_ref, kdt_ref, aqk_ref, el_ref, z_ref, nw_ref,
                    oa_ref, sfin_ref, s_ref):
    ci = pl.program_id(1)

    @pl.when(ci == 0)
    def _():
        s_ref[...] = jnp.zeros(s_ref.shape, F32)

    nw = nw_ref[...]
    heads = range(H_A)
    sls = [slice(h * DK, (h + 1) * DK) for h in heads]
    ss = [s_ref[h] for h in heads]
    sbs = [s.astype(BF16) for s in ss]
    vbs = [(u_ref[:, sl] - jnp.dot(w_ref[:, sl], sb, preferred_element_type=F32)).astype(BF16)
           for sl, sb in zip(sls, sbs)]
    for h in heads:
        s_ref[h] = ss[h] * el_ref[h][0:1, 0:1] + jnp.dot(kdt_ref[h], vbs[h], preferred_element_type=F32)
    for h in heads:
        o = (jnp.dot(qe_ref[:, sls[h]], sbs[h], preferred_element_type=F32)
             + jnp.dot(aqk_ref[h], vbs[h], preferred_element_type=F32))
        oa_ref[:, sls[h]] = _gated_rmsnorm(o, z_ref[:, sls[h]], nw).astype(BF16)

    @pl.when(ci == pl.num_programs(1) - 1)
    def _():
        sfin_ref[...] = s_ref[...]


def _delta_seq(u, w, qe, kdt, aqk, el, proj, nw_row, bsz, t_len):
    c = DELTA_C
    nc = t_len // c
    mp = bsz * t_len
    rowspec = lambda wd: pl.BlockSpec((c, wd), lambda b, i: (b * nc + i, 0))
    chunkspec = lambda a, bb: pl.BlockSpec((None, H_A, a, bb), lambda b, i: (b * nc + i, 0, 0, 0))
    return pl.pallas_call(
        _delta_seq_body,
        grid=(bsz, nc),
        in_specs=[
            rowspec(D_A), rowspec(D_A), rowspec(D_A),
            chunkspec(DK, c), chunkspec(c, c), chunkspec(SUBLANES, LANES),
            pl.BlockSpec((c, D_A), lambda b, i: (b * nc + i, COL_Z // D_A)),
            pl.BlockSpec((1, DV), lambda b, i: (0, 0)),
        ],
        out_specs=[
            rowspec(D_A),
            pl.BlockSpec((None, H_A, DK, DV), lambda b, i: (b, 0, 0, 0)),
        ],
        out_shape=[
            jax.ShapeDtypeStruct((mp, D_A), BF16),
            jax.ShapeDtypeStruct((bsz, H_A, DK, DV), F32),
        ],
        scratch_shapes=[pltpu.VMEM((H_A, DK, DV), F32)],
        compiler_params=_cparams(("arbitrary", "arbitrary")),
        name="delta_seq",
    )(u, w, qe, kdt, aqk, el, proj, nw_row)


def _sample_a_body(qkv_ref, z_ref, ba_ref, hist_ref, s0_ref, caw_ref, alog_ref, dtb_ref, nw_ref,
                   oa_ref, s1_ref, hq_ref, *, t_len):
    hdr = SUBLANES
    nh = CONV_W - 1
    hq_ref[0:hdr, :] = jnp.zeros((hdr, 3 * D_A), F32)
    hq_ref[hdr - nh:hdr, :] = hist_ref[...]
    hq_ref[hdr:hdr + t_len, :] = qkv_ref[...]
    bg = _beta_g(ba_ref[...], alog_ref, dtb_ref)
    rowi = lax.broadcasted_iota(jnp.int32, bg.shape, 0)
    gc_all = bg
    sft = 1
    while sft < t_len:
        gc_all = gc_all + jnp.where(rowi >= sft, pltpu.roll(gc_all, sft, axis=0), 0.0)
        sft *= 2
    nw = nw_ref[...]
    rowc = lax.broadcasted_iota(jnp.int32, (t_len, 1), 0)
    heads = range(H_A)
    act = lambda blk: _qkv_activation(_conv_taps(hq_ref, caw_ref, blk * DK, DK, t_len, hdr), blk)
    qs = [act(h) for h in heads]
    ks = [act(H_A + h) for h in heads]
    vs = [act(2 * H_A + h) for h in heads]
    betas = [bg[:, h:h + 1] for h in heads]
    gcs = [gc_all[:, H_A + h:H_A + h + 1] for h in heads]
    glasts = [gc[t_len - 1:t_len, :] for gc in gcs]
    kbs = [k * b for k, b in zip(ks, betas)]
    egcs = [jnp.exp(gc) for gc in gcs]
    sols = [jnp.concatenate([v * b, kb * e], axis=1) for v, b, kb, e in zip(vs, betas, kbs, egcs)]
    acols = [[] for _ in heads]
    lcols = [[] for _ in heads]
    for j in range(t_len):
        for h in heads:
            kj = ks[h][j:j + 1, :]
            dcol = jnp.exp(jnp.where(rowc >= j, gcs[h] - gcs[h][j:j + 1, :], 0.0))
            acols[h].append(jnp.where(rowc >= j, jnp.sum(qs[h] * kj, axis=-1, keepdims=True) * dcol, 0.0))
            lcols[h].append(jnp.where(rowc > j, jnp.sum(kbs[h] * kj, axis=-1, keepdims=True) * dcol, 0.0))
    for j in range(t_len - 1):
        sols = [sol - lcols[h][j] * sol[j:j + 1, :] for h, sol in enumerate(sols)]
    ss = [s0_ref[h] for h in heads]
    sbs = [s.astype(BF16) for s in ss]
    v_news = [sol[:, :DV] - jnp.dot(sol[:, DV:].astype(BF16), sb, preferred_element_type=F32)
              for sol, sb in zip(sols, sbs)]
    os_ = [jnp.dot((q * e).astype(BF16), sb, preferred_element_type=F32) for q, e, sb in zip(qs, egcs, sbs)]
    for j in range(t_len):
        os_ = [o + acols[h][j] * v_news[h][j:j + 1, :] for h, o in enumerate(os_)]
    upds = [lax.dot_general((k * jnp.exp(gl - gc)).astype(BF16), vn.astype(BF16), (((0,), (0,)), ((), ())),
                            preferred_element_type=F32)
            for k, gl, gc, vn in zip(ks, glasts, gcs, v_news)]
    for h in heads:
        s1_ref[h] = ss[h] * jnp.exp(glasts[h]) + upds[h]
        sl = slice(h * DV, (h + 1) * DV)
        oa_ref[:, sl] = _gated_rmsnorm(os_[h], z_ref[:, sl], nw).astype(BF16)


def _sample_a(proj, row0, bsz, t_len, hist, s0, caw, alog_row, dtb_row, nw_row):
    assert t_len == SUBLANES and row0 % t_len == 0
    r0 = row0 // t_len
    wspec = lambda shape: pl.BlockSpec(shape, lambda b: (0,) * len(shape))
    return pl.pallas_call(
        functools.partial(_sample_a_body, t_len=t_len),
        grid=(bsz,),
        in_specs=[
            pl.BlockSpec((t_len, 3 * D_A), lambda b: (r0 + b, COL_QKV // (3 * D_A))),
            pl.BlockSpec((t_len, D_A), lambda b: (r0 + b, COL_Z // D_A)),
            pl.BlockSpec((t_len, LANES), lambda b: (r0 + b, COL_BA // LANES)),
            pl.BlockSpec((None, CONV_W - 1, 3 * D_A), lambda b: (b, 0, 0)),
            pl.BlockSpec((None, H_A, DK, DV), lambda b: (b, 0, 0, 0)),
            wspec((CONV_W, 3 * D_A)), wspec((1, LANES)), wspec((1, LANES)), wspec((1, DV)),
        ],
        out_specs=[
            pl.BlockSpec((t_len, D_A), lambda b: (b, 0)),
            pl.BlockSpec((None, H_A, DK, DV), lambda b: (b, 0, 0, 0)),
        ],
        out_shape=[
            jax.ShapeDtypeStruct((bsz * t_len, D_A), BF16),
            jax.ShapeDtypeStruct((bsz, H_A, DK, DV), F32),
        ],
        scratch_shapes=[pltpu.VMEM((2 * SUBLANES, 3 * D_A), F32)],
        compiler_params=_cparams(("arbitrary",)),
        name="sample_a",
    )(proj, proj, proj, hist, s0, caw, alog_row, dtb_row, nw_row)


def _lru_coeffs(xc, wa_ref, wx_ref, lba_ref, lbx_ref, lam_ref):
    parts = []
    for n in range(NB):
        sl = slice(n * BW, (n + 1) * BW)
        xn = xc[:, sl]
        xnb = xn.astype(BF16)
        gr = _sigmoid(jnp.dot(xnb, wa_ref[n].astype(BF16), preferred_element_type=F32) + lba_ref[:, sl])
        gi = _sigmoid(jnp.dot(xnb, wx_ref[n].astype(BF16), preferred_element_type=F32) + lbx_ref[:, sl])
        log_a = -LRU_C * gr * _softplus(-lam_ref[:, sl])
        parts.append((jnp.exp(log_a), jnp.sqrt(1.0 - jnp.exp(2.0 * log_a)), gi * xn))
    return parts


def _group_scan(a, b):
    rowi = lax.broadcasted_iota(jnp.int32, a.shape, 0)
    sft = 1
    while sft < SUBLANES:
        keep = rowi >= sft
        b = b + a * jnp.where(keep, pltpu.roll(b, sft, axis=0), 0.0)
        a = a * jnp.where(keep, pltpu.roll(a, sft, axis=0), 1.0)
        sft *= 2
    return a, b


def _gelu_tanh(x):
    return 0.5 * x * (1.0 + jnp.tanh(math.sqrt(2.0 / math.pi) * (x + 0.044715 * x * x * x)))


def _lru_finish(h, yb, nbw):
    hg = h * _gelu_tanh(yb)
    return (hg * lax.rsqrt(jnp.mean(hg * hg, axis=-1, keepdims=True) + RMS_EPS) * nbw).astype(BF16)


def _lru_prompt_body(xc_ref, yb_ref, wa_ref, wx_ref, lba_ref, lbx_ref, lam_ref, nbw_ref,
                     ob_ref, hfin_ref, a_ref, b_ref, h_ref, carry_ref, *, rows):
    t = pl.program_id(1)
    parts = _lru_coeffs(xc_ref[...], wa_ref, wx_ref, lba_ref, lbx_ref, lam_ref)
    rowi = lax.broadcasted_iota(jnp.int32, (rows, BW), 0)
    first = jnp.logical_and(rowi == 0, t == 0)
    for n in range(NB):
        sl = slice(n * BW, (n + 1) * BW)
        a_n, mult_n, gix_n = parts[n]
        a_ref[:, sl] = jnp.where(first, 0.0, a_n)
        b_ref[:, sl] = jnp.where(first, 1.0, mult_n) * gix_n

    @pl.when(t == 0)
    def _():
        carry_ref[...] = jnp.zeros(carry_ref.shape, F32)

    def group(gidx, carry):
        r0 = pl.multiple_of(gidx * SUBLANES, SUBLANES)
        ag, bg = _group_scan(a_ref[pl.ds(r0, SUBLANES), :], b_ref[pl.ds(r0, SUBLANES), :])
        hg = ag * carry + bg
        h_ref[pl.ds(r0, SUBLANES), :] = hg
        return jnp.broadcast_to(hg[SUBLANES - 1:SUBLANES, :], hg.shape)

    carry = lax.fori_loop(0, rows // SUBLANES, group, carry_ref[...])
    carry_ref[...] = carry
    ob_ref[...] = _lru_finish(h_ref[...], yb_ref[...], nbw_ref[...])

    @pl.when(t == pl.num_programs(1) - 1)
    def _():
        hfin_ref[...] = carry[0:1, :]


def _lru_prompt(xbc, proj, bsz, t_len, wa, wx, lba, lbx, lam, nbw):
    rows = min(CONV_L, t_len)
    nt = t_len // rows
    mp = bsz * t_len
    wspec = lambda shape: pl.BlockSpec(shape, lambda b, t: (0,) * len(shape))
    return pl.pallas_call(
        functools.partial(_lru_prompt_body, rows=rows),
        grid=(bsz, nt),
        in_specs=[
            pl.BlockSpec((rows, D_B), lambda b, t: (b * nt + t, 0)),
            pl.BlockSpec((rows, D_B), lambda b, t: (b * nt + t, COL_YB // D_B)),
            wspec((NB, BW, BW)), wspec((NB, BW, BW)),
            wspec((1, D_B)), wspec((1, D_B)), wspec((1, D_B)), wspec((1, D_B)),
        ],
        out_specs=[
            pl.BlockSpec((rows, D_B), lambda b, t: (b * nt + t, 0)),
            pl.BlockSpec((None, 1, D_B), lambda b, t: (b, 0, 0)),
        ],
        out_shape=[
            jax.ShapeDtypeStruct((mp, D_B), BF16),
            jax.ShapeDtypeStruct((bsz, 1, D_B), F32),
        ],
        scratch_shapes=[pltpu.VMEM((rows, D_B), F32), pltpu.VMEM((rows, D_B), F32),
                        pltpu.VMEM((rows, D_B), F32), pltpu.VMEM((SUBLANES, D_B), F32)],
        compiler_params=_cparams(("arbitrary", "arbitrary")),
        name="lru_prompt",
    )(xbc, proj, wa, wx, lba, lbx, lam, nbw)


def _lru_sample_body(xb_ref, yb_ref, hist_ref, h0_ref, cbw_ref, cbb_ref, wa_ref, wx_ref, lba_ref,
                     lbx_ref, lam_ref, nbw_ref, ob_ref, h1_ref, hx_ref, a_ref, b_ref, h_ref,
                     *, nseq, t_len):
    hdr = SUBLANES
    nh = CONV_W - 1
    hx_ref[0:hdr, :] = jnp.zeros((hdr, D_B), F32)

    def conv_seq(si, _):
        r0 = pl.multiple_of(si * t_len, t_len)
        hx_ref[hdr - nh:hdr, :] = hist_ref[si]
        hx_ref[hdr:hdr + t_len, :] = xb_ref[pl.ds(r0, t_len), :]
        for n in range(NB):
            c0 = n * BW
            h_ref[pl.ds(r0, t_len), c0:c0 + BW] = (_conv_taps(hx_ref, cbw_ref, c0, BW, t_len, hdr)
                                                  + cbb_ref[:, c0:c0 + BW])
        return 0

    lax.fori_loop(0, nseq, conv_seq, 0)
    parts = _lru_coeffs(h_ref[...], wa_ref, wx_ref, lba_ref, lbx_ref, lam_ref)
    for n in range(NB):
        sl = slice(n * BW, (n + 1) * BW)
        a_n, mult_n, gix_n = parts[n]
        a_ref[:, sl] = a_n
        b_ref[:, sl] = mult_n * gix_n

    def seq(si, _):
        r0 = pl.multiple_of(si * t_len, t_len)
        ag, bg = _group_scan(a_ref[pl.ds(r0, t_len), :], b_ref[pl.ds(r0, t_len), :])
        hg = ag * h0_ref[pl.ds(si, 1), :] + bg
        h_ref[pl.ds(r0, t_len), :] = hg
        h1_ref[pl.ds(si, 1), :] = hg[t_len - 1:t_len, :]
        return 0

    lax.fori_loop(0, nseq, seq, 0)
    ob_ref[...] = _lru_finish(h_ref[...], yb_ref[...], nbw_ref[...])


def _lru_sample(proj, row0, bsz, t_len, hist, h0, cbw, cbb, wa, wx, lba, lbx, lam, nbw):
    assert t_len == SUBLANES
    ms = bsz * t_len
    assert row0 % ms == 0
    rblk = row0 // ms
    wspec = lambda shape: pl.BlockSpec(shape, lambda i: (0,) * len(shape))
    return pl.pallas_call(
        functools.partial(_lru_sample_body, nseq=bsz, t_len=t_len),
        grid=(1,),
        in_specs=[
            pl.BlockSpec((ms, D_B), lambda i: (rblk, COL_XB // D_B)),
            pl.BlockSpec((ms, D_B), lambda i: (rblk, COL_YB // D_B)),
            wspec((bsz, CONV_W - 1, D_B)), wspec((bsz, D_B)),
            wspec((CONV_W, D_B)), wspec((1, D_B)),
            wspec((NB, BW, BW)), wspec((NB, BW, BW)),
            wspec((1, D_B)), wspec((1, D_B)), wspec((1, D_B)), wspec((1, D_B)),
        ],
        out_specs=[wspec((ms, D_B)), wspec((bsz, D_B))],
        out_shape=[
            jax.ShapeDtypeStruct((ms, D_B), BF16),
            jax.ShapeDtypeStruct((bsz, D_B), F32),
        ],
        scratch_shapes=[pltpu.VMEM((2 * SUBLANES, D_B), F32), pltpu.VMEM((ms, D_B), F32),
                        pltpu.VMEM((ms, D_B), F32), pltpu.VMEM((ms, D_B), F32)],
        compiler_params=_cparams(("arbitrary",)),
        name="lru_sample",
    )(proj, proj, hist, h0, cbw, cbb, wa, wx, lba, lbx, lam, nbw)


def _layernorm_rows(v, g, b):
    mu = jnp.mean(v, axis=-1, keepdims=True)
    d = v - mu
    var = jnp.mean(d * d, axis=-1, keepdims=True)
    return d * lax.rsqrt(var + LN_EPS) * g + b


def _store_slabs(slab_ref, base, val):
    n = val.shape[0]
    for s in range(ROW_SLAB):
        slab_ref[pl.ds(base + s, n, stride=ROW_SLAB), :] = val[:, s * LANES:(s + 1) * LANES]


def _load_slab_chunk(slab_ref, base, n, s):
    return slab_ref[pl.ds(base + s, n, stride=ROW_SLAB), :]


def _outproj_body(oap_ref, oas_ref, obp_ref, obs_ref, xp_ref, xs_ref, wo_ref, g_ref, b_ref,
                  wr_ref, br_ref, x1_ref, ti_ref, tg_ref, *, n_ptiles):
    i = pl.program_id(0)

    def run(oa_ref, ob_ref, x_ref):
        mix = (jnp.dot(oa_ref[...], wo_ref[0:D_A, :], preferred_element_type=F32)
               + jnp.dot(ob_ref[...], wo_ref[D_A:D_A + D_B, :], preferred_element_type=F32))
        y = _layernorm_rows(DN_ALPHA * x_ref[...] + mix, g_ref[...], b_ref[...])
        _store_slabs(x1_ref, 0, y)
        logits = _dot3(y, wr_ref[...]) + br_ref[...]
        lane = lax.broadcasted_iota(jnp.int32, logits.shape, 1)
        lane_o = lax.broadcasted_iota(jnp.int32, ti_ref.shape, 1)
        cur = logits
        ti = jnp.zeros(ti_ref.shape, jnp.int32)
        tv = jnp.zeros(tg_ref.shape, F32)
        v0 = None
        den = None
        for kk in range(TOP_K):
            m = jnp.max(cur, axis=-1, keepdims=True)
            idx = jnp.min(jnp.where(cur == m, lane, N_EXPERTS), axis=-1, keepdims=True)
            cur = jnp.where(lane == idx, -jnp.inf, cur)
            if kk == 0:
                v0 = m
            e = jnp.exp(m - v0)
            den = e if den is None else den + e
            ti = jnp.where(lane_o == kk, idx, ti)
            tv = jnp.where(lane_o == kk, e, tv)
        ti_ref[...] = ti
        tg_ref[...] = tv / den

    @pl.when(i < n_ptiles)
    def _():
        run(oap_ref, obp_ref, xp_ref)

    @pl.when(i >= n_ptiles)
    def _():
        run(oas_ref, obs_ref, xs_ref)


def _outproj(oa_p, oa_s, ob_p, ob_s, xp2, xs2, wo, g_row, b_row, wr, br_row):
    mp, ms = xp2.shape[0], xs2.shape[0]
    tm = min(OUT_TM, ms)
    assert mp % tm == 0 and ms % tm == 0
    n_pt, n_st = mp // tm, ms // tm
    pmap = lambda i: (jnp.minimum(i, n_pt - 1), 0)
    smap = lambda i: (jnp.maximum(i - n_pt, 0), 0)
    wspec = lambda shape: pl.BlockSpec(shape, lambda i: (0, 0))
    m = mp + ms
    return pl.pallas_call(
        functools.partial(_outproj_body, n_ptiles=n_pt),
        grid=(n_pt + n_st,),
        in_specs=[
            pl.BlockSpec((tm, D_A), pmap), pl.BlockSpec((tm, D_A), smap),
            pl.BlockSpec((tm, D_B), pmap), pl.BlockSpec((tm, D_B), smap),
            pl.BlockSpec((tm, D_MODEL), pmap), pl.BlockSpec((tm, D_MODEL), smap),
            wspec((D_A + D_B, D_MODEL)), wspec((1, D_MODEL)), wspec((1, D_MODEL)),
            wspec((D_MODEL, N_EXPERTS)), wspec((1, N_EXPERTS)),
        ],
        out_specs=[
            pl.BlockSpec((tm * ROW_SLAB, LANES), lambda i: (i, 0)),
            pl.BlockSpec((tm, LANES), lambda i: (i, 0)),
            pl.BlockSpec((tm, LANES), lambda i: (i, 0)),
        ],
        out_shape=[
            jax.ShapeDtypeStruct((m * ROW_SLAB, LANES), F32),
            jax.ShapeDtypeStruct((m, LANES), jnp.int32),
            jax.ShapeDtypeStruct((m, LANES), F32),
        ],
        compiler_params=_cparams(("arbitrary",)),
        name="outproj_ln_router",
    )(oa_p, oa_s, ob_p, ob_s, xp2, xs2, wo, g_row, b_row, wr, br_row)


def _zero_tail(zbuf_ref, dst_ref, tail_row, n_blocks, sem):
    scale = ROW_SLAB if len(dst_ref.shape) == 2 else 1

    def tail_copy(t):
        z0 = pl.multiple_of((tail_row + t * MOE_RB) * scale, MOE_RB * scale)
        return pltpu.make_async_copy(zbuf_ref, dst_ref.at[pl.ds(z0, MOE_RB * scale)], sem)

    def start(t, _):
        tail_copy(t).start()
        return 0

    def wait(t, _):
        tail_copy(t).wait()
        return 0

    lax.fori_loop(0, n_blocks, start, 0)
    lax.fori_loop(0, n_blocks, wait, 0)


def _dispatch_body(zrow_ref, slot_ref, x_ref, xs_ref, zbuf_ref, zsem, rsem, *, tm):
    i = pl.program_id(0)

    @pl.when(i == 0)
    def _():
        zbuf_ref[...] = jnp.zeros(zbuf_ref.shape, F32)

        def zero_copy(e):
            z0 = pl.multiple_of(zrow_ref[e], MOE_RB)
            return pltpu.make_async_copy(zbuf_ref, xs_ref.at[pl.ds(z0, MOE_RB)], zsem)

        for e in range(N_EXPERTS):
            @pl.when(zrow_ref[e] >= 0)
            def _():
                zero_copy(e).start()
        for e in range(N_EXPERTS):
            @pl.when(zrow_ref[e] >= 0)
            def _():
                zero_copy(e).wait()
        _zero_tail(zbuf_ref, xs_ref, zrow_ref[N_EXPERTS], zrow_ref[N_EXPERTS + 1], zsem)

    def row_copy(r, dst):
        src0 = pl.multiple_of(r * ROW_SLAB, ROW_SLAB)
        return pltpu.make_async_copy(x_ref.at[pl.ds(src0, ROW_SLAB)], xs_ref.at[dst], rsem)

    def issue(r, _):
        for kk in range(TOP_K):
            row_copy(r, slot_ref[0, r * TOP_K + kk]).start(priority=kk % 2)
        return 0

    lax.fori_loop(0, tm, issue, 0)

    for kk in range(TOP_K):
        pltpu.make_async_copy(xs_ref.at[pl.ds(0, tm)], xs_ref.at[pl.ds(0, tm)], rsem).wait()


def _dispatch(x1, slot_tiles, zrow, n_rows):
    m = x1.shape[0] // ROW_SLAB
    tm = TOK_TM
    assert m % tm == 0
    return pl.pallas_call(
        functools.partial(_dispatch_body, tm=tm),
        grid_spec=pltpu.PrefetchScalarGridSpec(
            num_scalar_prefetch=1,
            grid=(m // tm,),
            in_specs=[
                pl.BlockSpec((None, 1, tm * TOP_K), lambda i, z: (i, 0, 0), memory_space=pltpu.SMEM),
                pl.BlockSpec((tm * ROW_SLAB, LANES), lambda i, z: (i, 0)),
            ],
            out_specs=pl.BlockSpec(memory_space=pl.ANY),
            scratch_shapes=[pltpu.VMEM((MOE_RB, ROW_SLAB, LANES), F32),
                            pltpu.SemaphoreType.DMA(()), pltpu.SemaphoreType.DMA(())],
        ),
        out_shape=jax.ShapeDtypeStruct((n_rows, ROW_SLAB, LANES), F32),
        compiler_params=_cparams(("arbitrary",)),
        name="moe_dispatch",
    )(zrow, slot_tiles, x1)


def _for_row_blocks(nsub, fn):
    for k in range(1, MOE_SB + 1):
        @pl.when(nsub == k)
        def _():
            fn(0, k * MOE_RB)


def _moe_body(ie_ref, irow_ref, insub_ref, tail_ref, xs_ref, wg_ref, wu_ref, bup_ref,
              wd_ref, bd_ref, ys_ref, xstage_ref, xb_ref, act_ref, ybuf_ref, wgb_ref, wub_ref, wdb_ref,
              isem, osem):
    wi = pl.program_id(0)
    j = pl.program_id(1)
    n_w = pl.num_programs(0)
    nsub = insub_ref[wi]
    row0 = irow_ref[wi]
    nt = MOE_NT
    is_last = wi == n_w - 1
    nxt = jnp.minimum(wi + 1, n_w - 1)
    nsub_next = jnp.where(is_last, 0, insub_ref[nxt])
    row_next = irow_ref[nxt]
    prv = jnp.maximum(wi - 1, 0)
    nsub_prev = jnp.where(wi == 0, 0, insub_ref[prv])
    row_prev = irow_ref[prv]

    class _XCopy:
        def __init__(self, row_base, s, slot):
            g0 = pl.multiple_of(row_base + s * MOE_RB, MOE_RB)
            self.copies = [
                pltpu.make_async_copy(xs_ref.at[pl.ds(g0, MOE_RB), c],
                                      xstage_ref.at[slot, :, pl.ds(c * LANES, LANES)], isem.at[slot])
                for c in range(ROW_SLAB)]

        def start(self):
            for cp in self.copies:
                cp.start()

        def wait(self):
            for cp in self.copies:
                cp.wait()

    x_copy = _XCopy

    def to_matmul_layout(s, slot):
        r0 = pl.multiple_of(s * MOE_RB, MOE_RB)
        xb_ref[pl.ds(r0, MOE_RB), :] = xstage_ref[slot].astype(BF16)

    def y_copy(row_base, s):
        b0 = pl.multiple_of(s * MOE_RB * ROW_SLAB, MOE_RB * ROW_SLAB)
        g0 = pl.multiple_of(row_base * ROW_SLAB, MOE_RB * ROW_SLAB) + b0
        return pltpu.make_async_copy(ybuf_ref.at[pl.ds(b0, MOE_RB * ROW_SLAB)],
                                     ys_ref.at[pl.ds(g0, MOE_RB * ROW_SLAB)], osem)

    @pl.when(jnp.logical_and(wi == 0, j == 0))
    def _():
        def load(s, _):
            cp = x_copy(row0, s, 0)
            cp.start()
            cp.wait()
            to_matmul_layout(s, 0)
            return 0

        lax.fori_loop(0, nsub, load, 0)

    @pl.when(jnp.logical_and(j == nt - 1, nsub_next > 0))
    def _():
        x_copy(row_next, 0, 0).start()

    @pl.when(jnp.logical_and(j >= nt, j - nt < nsub_next))
    def _():
        s = j - nt
        slot = lax.rem(s, 2)
        x_copy(row_next, s, slot).wait()

        @pl.when(s + 1 < nsub_next)
        def _():
            x_copy(row_next, s + 1, 1 - slot).start()

        to_matmul_layout(s, slot)

    @pl.when(jnp.logical_and(j == nt, nsub_prev > 0))
    def _():
        def drain(s, _):
            y_copy(row_prev, s).wait()
            return 0

        lax.fori_loop(0, nsub_prev, drain, 0)

    @pl.when(nsub > 0)
    def _():
        @pl.when(j < nt)
        def _():
            wgb_ref[...] = wg_ref[...].astype(BF16)
            wub_ref[...] = wu_ref[...].astype(BF16)
            b_gate = bup_ref[ie_ref[wi], pl.ds(j, 1), :]
            b_up = bup_ref[ie_ref[wi], pl.ds(nt + j, 1), :]

            def up_rows(r0, nrows):
                x = xb_ref[pl.ds(r0, nrows), :]
                hg = jnp.dot(x, wgb_ref[...], preferred_element_type=F32) + b_gate
                hu = jnp.dot(x, wub_ref[...], preferred_element_type=F32) + b_up
                gate = jnp.minimum(hg, SWIGLU_LIMIT)
                up = jnp.clip(hu, -SWIGLU_LIMIT, SWIGLU_LIMIT)
                glu = gate * _sigmoid(SWIGLU_ALPHA * gate)
                act_ref[j, pl.ds(r0, nrows), :] = (glu * (up + 1.0)).astype(BF16)

            _for_row_blocks(nsub, up_rows)

        @pl.when(j >= nt)
        def _():
            wdb_ref[...] = wd_ref[...].astype(BF16)
            n = j - nt
            b_down = bd_ref[ie_ref[wi], pl.ds(n, 1), :]

            def down_rows(r0, nrows):
                acc = jnp.dot(act_ref[0, pl.ds(r0, nrows), :], wdb_ref[0:MOE_TF, :],
                              preferred_element_type=F32)
                for jj in range(1, nt):
                    acc = acc + jnp.dot(act_ref[jj, pl.ds(r0, nrows), :],
                                        wdb_ref[jj * MOE_TF:(jj + 1) * MOE_TF, :],
                                        preferred_element_type=F32)
                acc = acc + b_down
                cpt = MOE_TF // LANES
                for cc in range(cpt):
                    ybuf_ref[pl.ds(r0 * ROW_SLAB + n * cpt + cc, nrows, stride=ROW_SLAB), :] = (
                        acc[:, cc * LANES:(cc + 1) * LANES])

            _for_row_blocks(nsub, down_rows)

        @pl.when(j == 2 * nt - 1)
        def _():
            def store(s, _):
                y_copy(row0, s).start()
                return 0

            lax.fori_loop(0, nsub, store, 0)

    @pl.when(jnp.logical_and(is_last, j == 2 * nt - 1))
    def _():
        def drain(s, _):
            y_copy(row0, s).wait()
            return 0

        lax.fori_loop(0, nsub, drain, 0)
        zsrc = ybuf_ref.at[pl.ds(0, MOE_RB * ROW_SLAB)]
        zsrc[...] = jnp.zeros(zsrc.shape, F32)
        _zero_tail(zsrc, ys_ref, tail_ref[0], tail_ref[1], osem)


def _moe(xs, n_items, item_e, item_row, item_nsub, tail, w_up, b_up3, w_down, b_down3):
    n_rows = xs.shape[0] * ROW_SLAB
    nt = MOE_NT
    assert MOE_SB <= nt
    up_off = D_FF // MOE_TF
    p1 = lambda j, n, w: jnp.where(n[w] > 0, jnp.minimum(j, nt - 1), nt - 1)
    p2 = lambda j, n, w: jnp.where(n[w] > 0, jnp.maximum(j - nt, 0), nt - 1)
    return pl.pallas_call(
        _moe_body,
        grid_spec=pltpu.PrefetchScalarGridSpec(
            num_scalar_prefetch=4,
            grid=(n_items, 2 * nt),
            in_specs=[
                pl.BlockSpec(memory_space=pl.ANY),
                pl.BlockSpec((None, D_MODEL, MOE_TF), lambda w, j, e, r, n, t: (e[w], 0, p1(j, n, w))),
                pl.BlockSpec((None, D_MODEL, MOE_TF),
                             lambda w, j, e, r, n, t: (e[w], 0, up_off + p1(j, n, w))),
                pl.BlockSpec((N_EXPERTS, 2 * nt, MOE_TF), lambda w, j, e, r, n, t: (0, 0, 0)),
                pl.BlockSpec((None, D_FF, MOE_TF), lambda w, j, e, r, n, t: (e[w], 0, p2(j, n, w))),
                pl.BlockSpec((N_EXPERTS, nt, MOE_TF), lambda w, j, e, r, n, t: (0, 0, 0)),
            ],
            out_specs=pl.BlockSpec(memory_space=pl.ANY),
            scratch_shapes=[
                pltpu.VMEM((2, MOE_RB, D_MODEL), F32),
                pltpu.VMEM((MOE_R, D_MODEL), BF16),
                pltpu.VMEM((nt, MOE_R, MOE_TF), BF16),
                pltpu.VMEM((MOE_R * ROW_SLAB, LANES), F32),
                pltpu.VMEM((D_MODEL, MOE_TF), BF16),
                pltpu.VMEM((D_MODEL, MOE_TF), BF16),
                pltpu.VMEM((D_FF, MOE_TF), BF16),
                pltpu.SemaphoreType.DMA((2,)), pltpu.SemaphoreType.DMA(()),
            ],
        ),
        out_shape=jax.ShapeDtypeStruct((n_rows, LANES), F32),
        compiler_params=_cparams(("arbitrary", "arbitrary")),
        name="moe_experts",
    )(item_e, item_row, item_nsub, tail, xs, w_up, w_up, b_up3, w_down, b_down3)


def _combine_body(slot_ref, slotn_ref, ys_ref, x1_ref, gate_ref, g_ref, b_ref, yp_ref, ysm_ref, buf_ref,
                  v_ref, sem, *, tm, n_ptiles):
    i = pl.program_id(0)
    cur = lax.rem(i, 2)

    def gather(table_ref, bslot):
        def issue(r, _):
            for kk in range(TOP_K):
                src0 = pl.multiple_of(table_ref[0, r * TOP_K + kk] * ROW_SLAB, ROW_SLAB)
                dst0 = pl.multiple_of(r * ROW_SLAB, ROW_SLAB)
                pltpu.make_async_copy(ys_ref.at[pl.ds(src0, ROW_SLAB)],
                                      buf_ref.at[bslot, kk, pl.ds(dst0, ROW_SLAB)],
                                      sem.at[bslot]).start(priority=kk % 2)
            return 0

        lax.fori_loop(0, tm, issue, 0)

    @pl.when(i == 0)
    def _():
        gather(slot_ref, 0)

    @pl.when(i + 1 < pl.num_programs(0))
    def _():
        gather(slotn_ref, 1 - cur)

    for kk in range(TOP_K):
        pltpu.make_async_copy(ys_ref.at[pl.ds(0, tm * ROW_SLAB)], buf_ref.at[cur, kk], sem.at[cur]).wait()
    gates = gate_ref[...]
    for c in range(ROW_SLAB):
        acc = DN_ALPHA * _load_slab_chunk(x1_ref, 0, tm, c)
        for kk in range(TOP_K):
            acc = acc + gates[:, kk:kk + 1] * _load_slab_chunk(buf_ref.at[cur, kk], 0, tm, c)
        v_ref[:, c * LANES:(c + 1) * LANES] = acc
    y = _layernorm_rows(v_ref[...], g_ref[...], b_ref[...])

    @pl.when(pl.program_id(0) < n_ptiles)
    def _():
        yp_ref[...] = y

    @pl.when(pl.program_id(0) >= n_ptiles)
    def _():
        ysm_ref[...] = y


def _combine(ys, slot_tiles, x1, gates, g_row, b_row, mp):
    m = x1.shape[0] // ROW_SLAB
    tm = TOK_TM
    assert mp % tm == 0 and (m - mp) % tm == 0
    n_pt = mp // tm
    wspec = lambda shape: pl.BlockSpec(shape, lambda i: (0, 0))
    n_t = m // tm
    return pl.pallas_call(
        functools.partial(_combine_body, tm=tm, n_ptiles=n_pt),
        grid=(n_t,),
        in_specs=[
            pl.BlockSpec((None, 1, tm * TOP_K), lambda i: (i, 0, 0), memory_space=pltpu.SMEM),
            pl.BlockSpec((None, 1, tm * TOP_K), lambda i: (jnp.minimum(i + 1, n_t - 1), 0, 0),
                         memory_space=pltpu.SMEM),
            pl.BlockSpec(memory_space=pl.ANY),
            pl.BlockSpec((tm * ROW_SLAB, LANES), lambda i: (i, 0)),
            pl.BlockSpec((tm, LANES), lambda i: (i, 0)),
            wspec((1, D_MODEL)), wspec((1, D_MODEL)),
        ],
        out_specs=[pl.BlockSpec((tm, D_MODEL), lambda i: (jnp.minimum(i, n_pt - 1), 0)),
                   pl.BlockSpec((tm, D_MODEL), lambda i: (jnp.maximum(i - n_pt, 0), 0))],
        out_shape=[jax.ShapeDtypeStruct((mp, D_MODEL), F32),
                   jax.ShapeDtypeStruct((m - mp, D_MODEL), F32)],
        scratch_shapes=[pltpu.VMEM((2, TOP_K, tm * ROW_SLAB, LANES), F32),
                        pltpu.VMEM((tm, D_MODEL), F32), pltpu.SemaphoreType.DMA((2,))],
        compiler_params=_cparams(("arbitrary",)),
        name="moe_combine_ln",
    )(slot_tiles, slot_tiles, ys, x1, gates, g_row, b_row)


def _routing_tables(top_i, m):
    e_ids = jnp.arange(N_EXPERTS, dtype=jnp.int32)
    onehot = (top_i[:, :, None] == e_ids[None, None, :]).astype(jnp.int32)
    mask = jnp.sum(onehot, axis=1)
    incl = jnp.cumsum(mask, axis=0)
    pos = incl - mask
    counts = incl[-1]
    nsub = (counts + MOE_RB - 1) // MOE_RB
    padded = nsub * MOE_RB
    gend = jnp.cumsum(padded)
    gstart = gend - padded
    slot = jnp.sum(onehot * (gstart[None, None, :] + pos[:, None, :]), axis=2)
    n_rows = (m * TOP_K // MOE_RB + N_EXPERTS) * MOE_RB
    tail = jnp.stack([gend[-1], (n_rows - gend[-1]) // MOE_RB]).astype(jnp.int32)
    zrow = jnp.concatenate([jnp.where(counts > 0, gend - MOE_RB, -1).astype(jnp.int32), tail])
    n_items = N_EXPERTS + (m * TOP_K // MOE_RB + N_EXPERTS) // MOE_SB
    ipe = (nsub + MOE_SB - 1) // MOE_SB
    iend = jnp.cumsum(ipe)
    istart = iend - ipe
    total = iend[-1]
    wid = jnp.arange(n_items, dtype=jnp.int32)
    wclamp = jnp.minimum(wid, total - 1)
    ie = jnp.minimum(jnp.searchsorted(iend, wclamp, side="right"), N_EXPERTS - 1).astype(jnp.int32)
    jn = wclamp - istart[ie]
    irow = (gstart[ie] + jn * MOE_R).astype(jnp.int32)
    insub = jnp.where(wid < total, jnp.clip(nsub[ie] - jn * MOE_SB, 0, MOE_SB), 0).astype(jnp.int32)
    return slot.astype(jnp.int32), zrow, tail, total.astype(jnp.int32), ie, irow, insub, n_rows


def kernel(x_prompt, x_sample, state_conv_a, state_delta, state_conv_b, state_lru, w_in, conv_a_w,
           a_log, dt_bias, norm_a_w, conv_b_w, conv_b_b, lru_wa, lru_ba, lru_wx, lru_bx, lru_lambda,
           norm_b_w, w_out, ln1_g, ln1_b, w_router, b_router, w_up, b_up, w_down, b_down, ln2_g, ln2_b):
    bp, tp, _ = x_prompt.shape
    bs, ts, _ = x_sample.shape
    mp, ms = bp * tp, bs * ts
    m = mp + ms
    l = 0
    xp2 = x_prompt.reshape(mp, D_MODEL)
    xs2 = x_sample.reshape(ms, D_MODEL)

    wp = _wprep(w_in, l)
    wo = w_out[l].astype(BF16)
    pad_h = lambda v: jnp.zeros((1, LANES), F32).at[0, H_A:2 * H_A].set(v)
    alog_row = pad_h(a_log[l])
    dtb_row = pad_h(dt_bias[l])
    row = lambda v: v.reshape(1, -1)

    proj = _inproj(xp2, xs2, wp)

    qkv_act, xbc, bg = _convact_prompt(proj, bp, tp, conv_a_w[l], conv_b_w[l], row(conv_b_b[l]),
                                       alog_row, dtb_row)
    u, w, qe, kdt, aqk, el = _delta_prep(qkv_act, bg, bp, tp)
    oa_p, sd_p = _delta_seq(u, w, qe, kdt, aqk, el, proj, row(norm_a_w[l]), bp, tp)
    ob_p, h_p = _lru_prompt(xbc, proj, bp, tp, lru_wa[l], lru_wx[l], row(lru_ba[l]), row(lru_bx[l]),
                            row(lru_lambda[l]), row(norm_b_w[l]))
    oa_s, sd_s = _sample_a(proj, mp, bs, ts, state_conv_a[l], state_delta[l], conv_a_w[l],
                           alog_row, dtb_row, row(norm_a_w[l]))
    ob_s, h_s = _lru_sample(proj, mp, bs, ts, state_conv_b[l], state_lru[l], conv_b_w[l],
                            row(conv_b_b[l]), lru_wa[l], lru_wx[l], row(lru_ba[l]), row(lru_bx[l]),
                            row(lru_lambda[l]), row(norm_b_w[l]))

    x1, ti, tg = _outproj(oa_p, oa_s, ob_p, ob_s, xp2, xs2, wo, row(ln1_g[l]), row(ln1_b[l]),
                          w_router[l], row(b_router[l]))

    slot, zrow, tail, n_used, ie, irow, insub, n_rows = _routing_tables(ti[:, :TOP_K], m)
    slot_tiles = slot.reshape(m // TOK_TM, 1, TOK_TM * TOP_K)
    xs_sorted = _dispatch(x1, slot_tiles, zrow, n_rows)
    ys = _moe(xs_sorted, n_used, ie, irow, insub, tail, w_up[l],
              b_up[l].reshape(N_EXPERTS, 2 * MOE_NT, MOE_TF), w_down[l],
              b_down[l].reshape(N_EXPERTS, MOE_NT, MOE_TF))
    y_p, y_s = _combine(ys, slot_tiles, x1, tg, row(ln2_g[l]), row(ln2_b[l]), mp)

    y_prompt = y_p.reshape(bp, tp, D_MODEL)
    y_sample = y_s.reshape(bs, ts, D_MODEL)
    nh = CONV_W - 1
    assert tp % SUBLANES == 0 and ts == SUBLANES and nh <= SUBLANES
    pg = proj.reshape(m // SUBLANES, SUBLANES, N_PROJ)
    gp = tp // SUBLANES

    def last_rows(g0, g1, gstep, c0, width):
        return lax.slice(pg, (g0, SUBLANES - nh, c0), (g1, SUBLANES, c0 + width), (gstep, 1, 1))

    ca_p = last_rows(gp - 1, mp // SUBLANES, gp, COL_QKV, 3 * D_A)
    cb_p = last_rows(gp - 1, mp // SUBLANES, gp, COL_XB, D_B)
    ca_s = last_rows(mp // SUBLANES, m // SUBLANES, 1, COL_QKV, 3 * D_A)
    cb_s = last_rows(mp // SUBLANES, m // SUBLANES, 1, COL_XB, D_B)
    return (y_prompt, y_sample,
            ca_p[None], sd_p[None], cb_p[None], h_p.reshape(1, bp, D_B),
            ca_s[None], sd_s[None], cb_s[None], h_s[None])
```

```python
import functools
import math

import jax
import jax.numpy as jnp
from jax import lax
from jax.experimental import pallas as pl
from jax.experimental.pallas import tpu as pltpu

F32 = jnp.float32
BF16 = jnp.bfloat16
HIGHEST = lax.Precision.HIGHEST

D_MODEL = 2048
D_A = 1024
H_A = 8
DK = 128
DV = 128
CONV_W = 4
D_B = 1024
NB = 8
BW = 128
LRU_C = 8.0
N_EXPERTS = 32
TOP_K = 4
D_FF = 2048
SWIGLU_LIMIT = 7.0
SWIGLU_ALPHA = 1.702
DEPTH = 1
DN_ALPHA = (2.0 * DEPTH) ** 0.25
LN_EPS = 1e-5
RMS_EPS = 1e-6
L2_EPS = 1e-6

LANES = 128
SUBLANES = 8
VMEM_LIMIT_BYTES = 56 * 1024 * 1024

COL_QKV = 0
COL_Z = 3 * D_A
COL_XB = 4 * D_A
COL_YB = 4 * D_A + D_B
COL_BA = 4 * D_A + 2 * D_B
N_PROJ = COL_BA + 2 * LANES

INPROJ_TM = 1024
INPROJ_TN = 1280
CONV_L = 256
DELTA_C = 128
OUT_TM = 256
MOE_RB = 256
MOE_SB = 6
MOE_R = MOE_RB * MOE_SB
MOE_TF = 256
MOE_NT = D_FF // MOE_TF
SAMPLE_NSEQ = 4
TOK_TM = 256
ROW_SLAB = D_MODEL // LANES


def _cparams(sem, vmem=VMEM_LIMIT_BYTES):
    return pltpu.CompilerParams(dimension_semantics=sem, vmem_limit_bytes=vmem)


def _sigmoid(x):
    return 1.0 / (1.0 + jnp.exp(-x))


def _softplus(x):
    return jnp.maximum(x, 0.0) + jnp.log(1.0 + jnp.exp(-jnp.abs(x)))


def _wprep_body(w_ref, o_ref):
    c = 4 * D_A
    nba = 2 * H_A
    o_ref[:, 0:c] = w_ref[:, 0:c].astype(BF16)
    o_ref[:, c:c + 2 * D_B] = w_ref[:, c + nba:c + nba + 2 * D_B].astype(BF16)
    ba = w_ref[:, c:c + LANES]
    lane = lax.broadcasted_iota(jnp.int32, ba.shape, 1)
    o_ref[:, COL_BA:COL_BA + LANES] = jnp.where(lane < nba, ba, 0.0).astype(BF16)
    o_ref[:, COL_BA + LANES:N_PROJ] = jnp.zeros((ba.shape[0], N_PROJ - COL_BA - LANES), BF16)


def _wprep(w_in):
    rows = 256
    d_in = w_in.shape[-1]
    assert d_in == 4 * D_A + 2 * H_A + 2 * D_B and D_MODEL % rows == 0
    return pl.pallas_call(
        _wprep_body,
        grid=(D_MODEL // rows,),
        in_specs=[pl.BlockSpec((rows, d_in), lambda i: (i, 0))],
        out_specs=pl.BlockSpec((rows, N_PROJ), lambda i: (i, 0)),
        out_shape=jax.ShapeDtypeStruct((D_MODEL, N_PROJ), BF16),
        compiler_params=_cparams(("arbitrary",)),
        name="wprep",
    )(w_in)


def _inproj_body(xp_ref, xs_ref, w_ref, o_ref, xb_ref, *, n_ptiles):
    i = pl.program_id(0)

    @pl.when(pl.program_id(1) == 0)
    def _():
        @pl.when(i < n_ptiles)
        def _():
            xb_ref[...] = xp_ref[...].astype(BF16)

        @pl.when(i >= n_ptiles)
        def _():
            xb_ref[...] = xs_ref[...].astype(BF16)

    o_ref[...] = jnp.dot(xb_ref[...], w_ref[...], preferred_element_type=F32)


def _inproj(xp2, xs2, wp):
    mp, ms = xp2.shape[0], xs2.shape[0]
    tm = min(INPROJ_TM, ms)
    assert mp % tm == 0 and ms % tm == 0
    n_pt, n_st = mp // tm, ms // tm
    n_nt = N_PROJ // INPROJ_TN
    return pl.pallas_call(
        functools.partial(_inproj_body, n_ptiles=n_pt),
        grid=(n_pt + n_st, n_nt),
        in_specs=[
            pl.BlockSpec((tm, D_MODEL), lambda i, j: (jnp.minimum(i, n_pt - 1), 0)),
            pl.BlockSpec((tm, D_MODEL), lambda i, j: (jnp.maximum(i - n_pt, 0), 0),
                         pipeline_mode=pl.Buffered(1)),
            pl.BlockSpec((D_MODEL, INPROJ_TN), lambda i, j: (0, j)),
        ],
        out_specs=pl.BlockSpec((tm, INPROJ_TN), lambda i, j: (i, j)),
        out_shape=jax.ShapeDtypeStruct((mp + ms, N_PROJ), F32),
        scratch_shapes=[pltpu.VMEM((tm, D_MODEL), BF16)],
        compiler_params=_cparams(("arbitrary", "arbitrary")),
        name="inproj",
    )(xp2, xs2, wp)


def _conv_taps(h_ref, w_ref, c0, width, rows, base):
    acc = h_ref[base:base + rows, c0:c0 + width] * w_ref[CONV_W - 1:CONV_W, c0:c0 + width]
    for j in range(1, CONV_W):
        acc = acc + (h_ref[base - j:base - j + rows, c0:c0 + width]
                     * w_ref[CONV_W - 1 - j:CONV_W - j, c0:c0 + width])
    return acc


def _qkv_activation(acc, blk):
    a = acc * _sigmoid(acc)
    if blk < 2 * H_A:
        a = a * lax.rsqrt(jnp.sum(a * a, axis=-1, keepdims=True) + L2_EPS)
        if blk < H_A:
            a = a * (DK ** -0.5)
    return a


def _beta_g(ba, alog_ref, dtb_ref):
    lane = lax.broadcasted_iota(jnp.int32, ba.shape, 1)
    beta = _sigmoid(ba)
    g = -jnp.exp(alog_ref[...]) * _softplus(ba + dtb_ref[...])
    return jnp.where(lane < H_A, beta, jnp.where(lane < 2 * H_A, g, 0.0))


def _convact_prompt_body(qkv_ref, xb_ref, ba_ref, caw_ref, cbw_ref, cbb_ref, alog_ref, dtb_ref,
                         qkvo_ref, xbo_ref, bgo_ref, hq_ref, hx_ref, *, rows):
    t = pl.program_id(1)
    hdr = SUBLANES

    @pl.when(t == 0)
    def _():
        hq_ref[0:hdr, :] = jnp.zeros((hdr, 3 * D_A), F32)
        hx_ref[0:hdr, :] = jnp.zeros((hdr, D_B), F32)

    @pl.when(t > 0)
    def _():
        hq_ref[0:hdr, :] = hq_ref[rows:rows + hdr, :]
        hx_ref[0:hdr, :] = hx_ref[rows:rows + hdr, :]

    hq_ref[hdr:hdr + rows, :] = qkv_ref[...]
    hx_ref[hdr:hdr + rows, :] = xb_ref[...]
    for blk in range(3 * H_A):
        c0 = blk * DK
        acc = _conv_taps(hq_ref, caw_ref, c0, DK, rows, hdr)
        qkvo_ref[:, c0:c0 + DK] = _qkv_activation(acc, blk)
    for blk in range(NB):
        c0 = blk * BW
        xbo_ref[:, c0:c0 + BW] = _conv_taps(hx_ref, cbw_ref, c0, BW, rows, hdr) + cbb_ref[:, c0:c0 + BW]
    bgo_ref[...] = _beta_g(ba_ref[...], alog_ref, dtb_ref)


def _convact_prompt(proj, bsz, t_len, caw, cbw, cbb, alog_row, dtb_row):
    rows = min(CONV_L, t_len)
    assert t_len % rows == 0
    nt = t_len // rows
    mp = bsz * t_len
    wspec = lambda shape: pl.BlockSpec(shape, lambda b, t: (0, 0))
    return pl.pallas_call(
        functools.partial(_convact_prompt_body, rows=rows),
        grid=(bsz, nt),
        in_specs=[
            pl.BlockSpec((rows, 3 * D_A), lambda b, t: (b * nt + t, COL_QKV // (3 * D_A))),
            pl.BlockSpec((rows, D_B), lambda b, t: (b * nt + t, COL_XB // D_B)),
            pl.BlockSpec((rows, LANES), lambda b, t: (b * nt + t, COL_BA // LANES)),
            wspec((CONV_W, 3 * D_A)), wspec((CONV_W, D_B)), wspec((1, D_B)),
            wspec((1, LANES)), wspec((1, LANES)),
        ],
        out_specs=[
            pl.BlockSpec((rows, 3 * D_A), lambda b, t: (b * nt + t, 0)),
            pl.BlockSpec((rows, D_B), lambda b, t: (b * nt + t, 0)),
            pl.BlockSpec((rows, LANES), lambda b, t: (b * nt + t, 0)),
        ],
        out_shape=[
            jax.ShapeDtypeStruct((mp, 3 * D_A), F32),
            jax.ShapeDtypeStruct((mp, D_B), F32),
            jax.ShapeDtypeStruct((mp, LANES), F32),
        ],
        scratch_shapes=[pltpu.VMEM((SUBLANES + rows + SUBLANES, 3 * D_A), F32),
                        pltpu.VMEM((SUBLANES + rows + SUBLANES, D_B), F32)],
        compiler_params=_cparams(("arbitrary", "arbitrary")),
        name="convact_prompt",
    )(proj, proj, proj, caw, cbw, cbb, alog_row, dtb_row)


def _dot_nt(a, b, precision=None):
    return lax.dot_general(a, b, (((1,), (1,)), ((), ())), precision=precision,
                           preferred_element_type=F32)


def _split_bf16(x):
    hi = x.astype(BF16)
    return hi, (x - hi.astype(F32)).astype(BF16)


def _dot3(a, b):
    ah, al = _split_bf16(a)
    bh, bl = _split_bf16(b)
    return (jnp.dot(ah, bh, preferred_element_type=F32)
            + (jnp.dot(al, bh, preferred_element_type=F32) + jnp.dot(ah, bl, preferred_element_type=F32)))


def _delta_prep_body(qkv_ref, bg_ref, u_ref, w_ref, qe_ref, kdt_ref, aqk_ref, el_ref, *, c):
    row = lax.broadcasted_iota(jnp.int32, (c, c), 0)
    col = lax.broadcasted_iota(jnp.int32, (c, c), 1)
    incl = row >= col
    strict = row > col
    eye = (row == col).astype(F32)
    tril = incl.astype(F32)
    eye_l = (lax.broadcasted_iota(jnp.int32, (LANES, LANES), 0)
             == lax.broadcasted_iota(jnp.int32, (LANES, LANES), 1)).astype(F32)

    bg = bg_ref[...]
    gc_all = jnp.dot(tril, bg, precision=HIGHEST, preferred_element_type=F32)
    gc_t = _dot_nt(eye_l, gc_all, precision=HIGHEST)
    heads = range(H_A)
    lmats, rhss = [], []
    for h in heads:
        q = qkv_ref[:, h * DK:(h + 1) * DK]
        k = qkv_ref[:, (H_A + h) * DK:(H_A + h + 1) * DK]
        v = qkv_ref[:, (2 * H_A + h) * DK:(2 * H_A + h + 1) * DK]
        beta = bg[:, h:h + 1]
        gc = gc_all[:, H_A + h:H_A + h + 1]
        g_last = gc_all[c - 1:c, H_A + h:H_A + h + 1]
        diff = gc - gc_t[H_A + h:H_A + h + 1, :]
        decay = jnp.where(incl, jnp.exp(jnp.where(incl, diff, 0.0)), 0.0)
        kb = k * beta
        egc = jnp.exp(gc)
        kbf = k.astype(BF16)
        kk = _dot_nt(kbf, kbf)
        lmats.append(jnp.where(strict, kk * beta * decay, 0.0))
        rhss.append(jnp.concatenate([v * beta, kb * egc], axis=1))
        qe_ref[:, h * DK:(h + 1) * DK] = (q * egc).astype(BF16)
        aqk_ref[h] = (_dot_nt(q.astype(BF16), kbf) * decay).astype(BF16)
        kdec = k * jnp.exp(g_last - gc)
        kdt_ref[h] = _dot_nt(eye_l.astype(BF16), kdec.astype(BF16)).astype(BF16)
        el_ref[h] = jnp.broadcast_to(jnp.exp(g_last), (SUBLANES, LANES))
    xbs = [(-lm).astype(BF16) for lm in lmats]
    t0s = [eye - lm for lm in lmats]
    p = 2
    while p < c:
        xbs = [jnp.dot(xb, xb, preferred_element_type=F32).astype(BF16) for xb in xbs]
        t0s = [t0 + jnp.dot(t0.astype(BF16), xb, preferred_element_type=F32) for t0, xb in zip(t0s, xbs)]
        p *= 2
    t0bs = [t0.astype(BF16) for t0 in t0s]
    sol0s = [jnp.dot(t0b, rhs.astype(BF16), preferred_element_type=F32) for t0b, rhs in zip(t0bs, rhss)]
    resids = [rhs - sol0 - _dot3(lm, sol0) for rhs, sol0, lm in zip(rhss, sol0s, lmats)]
    for h in heads:
        sol = sol0s[h] + jnp.dot(t0bs[h], resids[h].astype(BF16), preferred_element_type=F32)
        u_ref[:, h * DV:(h + 1) * DV] = sol[:, :DV]
        w_ref[:, h * DK:(h + 1) * DK] = sol[:, DV:].astype(BF16)


def _delta_prep(qkv_act, bg, bsz, t_len):
    c = DELTA_C
    assert t_len % c == 0
    nc = t_len // c
    mp = bsz * t_len
    n_chunks = bsz * nc
    rowspec = lambda w: pl.BlockSpec((c, w), lambda i: (i, 0))
    return pl.pallas_call(
        functools.partial(_delta_prep_body, c=c),
        grid=(n_chunks,),
        in_specs=[rowspec(3 * D_A), rowspec(LANES)],
        out_specs=[
            rowspec(D_A), rowspec(D_A), rowspec(D_A),
            pl.BlockSpec((None, H_A, DK, c), lambda i: (i, 0, 0, 0)),
            pl.BlockSpec((None, H_A, c, c), lambda i: (i, 0, 0, 0)),
            pl.BlockSpec((None, H_A, SUBLANES, LANES), lambda i: (i, 0, 0, 0)),
        ],
        out_shape=[
            jax.ShapeDtypeStruct((mp, D_A), F32),
            jax.ShapeDtypeStruct((mp, D_A), BF16),
            jax.ShapeDtypeStruct((mp, D_A), BF16),
            jax.ShapeDtypeStruct((n_chunks, H_A, DK, c), BF16),
            jax.ShapeDtypeStruct((n_chunks, H_A, c, c), BF16),
            jax.ShapeDtypeStruct((n_chunks, H_A, SUBLANES, LANES), F32),
        ],
        compiler_params=_cparams(("arbitrary",)),
        name="delta_prep",
    )(qkv_act, bg)


def _gated_rmsnorm(o, z, nw):
    on = o * lax.rsqrt(jnp.mean(o * o, axis=-1, keepdims=True) + RMS_EPS) * nw
    return on * (z * _sigmoid(z))


def _delta_seq_body(u_ref, w_ref, qe_ref, kdt_ref, aqk_ref, el_ref, z_ref, nw_ref,
                    oa_ref, sfin_ref, s_ref):
    ci = pl.program_id(1)

    @pl.when(ci == 0)
    def _():
        s_ref[...] = jnp.zeros(s_ref.shape, F32)

    nw = nw_ref[...]
    heads = range(H_A)
    sls = [slice(h * DK, (h + 1) * DK) for h in heads]
    ss = [s_ref[h] for h in heads]
    sbs = [s.astype(BF16) for s in ss]
    vbs = [(u_ref[:, sl] - jnp.dot(w_ref[:, sl], sb, preferred_element_type=F32)).astype(BF16)
           for sl, sb in zip(sls, sbs)]
    for h in heads:
        s_ref[h] = ss[h] * el_ref[h][0:1, 0:1] + jnp.dot(kdt_ref[h], vbs[h], preferred_element_type=F32)
    for h in heads:
        o = (jnp.dot(qe_ref[:, sls[h]], sbs[h], preferred_element_type=F32)
             + jnp.dot(aqk_ref[h], vbs[h], preferred_element_type=F32))
        oa_ref[:, sls[h]] = _gated_rmsnorm(o, z_ref[:, sls[h]], nw).astype(BF16)

    @pl.when(ci == pl.num_programs(1) - 1)
    def _():
        sfin_ref[...] = s_ref[...]


def _delta_seq(u, w, qe, kdt, aqk, el, proj, nw_row, bsz, t_len):
    c = DELTA_C
    nc = t_len // c
    mp = bsz * t_len
    rowspec = lambda wd: pl.BlockSpec((c, wd), lambda b, i: (b * nc + i, 0))
    chunkspec = lambda a, bb: pl.BlockSpec((None, H_A, a, bb), lambda b, i: (b * nc + i, 0, 0, 0))
    return pl.pallas_call(
        _delta_seq_body,
        grid=(bsz, nc),
        in_specs=[
            rowspec(D_A), rowspec(D_A), rowspec(D_A),
            chunkspec(DK, c), chunkspec(c, c), chunkspec(SUBLANES, LANES),
            pl.BlockSpec((c, D_A), lambda b, i: (b * nc + i, COL_Z // D_A)),
            pl.BlockSpec((1, DV), lambda b, i: (0, 0)),
        ],
        out_specs=[
            rowspec(D_A),
            pl.BlockSpec((None, H_A, DK, DV), lambda b, i: (b, 0, 0, 0)),
        ],
        out_shape=[
            jax.ShapeDtypeStruct((mp, D_A), BF16),
            jax.ShapeDtypeStruct((bsz, H_A, DK, DV), F32),
        ],
        scratch_shapes=[pltpu.VMEM((H_A, DK, DV), F32)],
        compiler_params=_cparams(("arbitrary", "arbitrary")),
        name="delta_seq",
    )(u, w, qe, kdt, aqk, el, proj, nw_row)


def _sample_a_body(qkv_ref, z_ref, ba_ref, hist_ref, s0_ref, caw_ref, alog_ref, dtb_ref, nw_ref,
                   oa_ref, s1_ref, hq_ref, *, t_len, nseq):
    hdr = SUBLANES
    nh = CONV_W - 1
    rows = [slice(si * t_len, (si + 1) * t_len) for si in range(nseq)]
    for si in range(nseq):
        hq_ref[si, 0:hdr, :] = jnp.zeros((hdr, 3 * D_A), F32)
        hq_ref[si, hdr - nh:hdr, :] = hist_ref[si]
        hq_ref[si, hdr:hdr + t_len, :] = qkv_ref[rows[si], :]
    bg = _beta_g(ba_ref[...], alog_ref, dtb_ref)
    rloc = lax.rem(lax.broadcasted_iota(jnp.int32, bg.shape, 0), t_len)
    gc_all = bg
    sft = 1
    while sft < t_len:
        gc_all = gc_all + jnp.where(rloc >= sft, pltpu.roll(gc_all, sft, axis=0), 0.0)
        sft *= 2
    nw = nw_ref[...]
    rowc = lax.broadcasted_iota(jnp.int32, (t_len, 1), 0)
    pairs = [(si, h) for si in range(nseq) for h in range(H_A)]
    npair = range(len(pairs))
    act = lambda si, blk: _qkv_activation(_conv_taps(hq_ref.at[si], caw_ref, blk * DK, DK, t_len, hdr), blk)
    qs = [act(si, h) for si, h in pairs]
    ks = [act(si, H_A + h) for si, h in pairs]
    vs = [act(si, 2 * H_A + h) for si, h in pairs]
    betas = [bg[rows[si], h:h + 1] for si, h in pairs]
    gcs = [gc_all[rows[si], H_A + h:H_A + h + 1] for si, h in pairs]
    glasts = [gc[t_len - 1:t_len, :] for gc in gcs]
    kbs = [k * b for k, b in zip(ks, betas)]
    egcs = [jnp.exp(gc) for gc in gcs]
    sols = [jnp.concatenate([v * b, kb * e], axis=1) for v, b, kb, e in zip(vs, betas, kbs, egcs)]
    acols = [[] for _ in npair]
    lcols = [[] for _ in npair]
    for j in range(t_len):
        for p in npair:
            kj = ks[p][j:j + 1, :]
            dcol = jnp.exp(jnp.where(rowc >= j, gcs[p] - gcs[p][j:j + 1, :], 0.0))
            acols[p].append(jnp.where(rowc >= j, jnp.sum(qs[p] * kj, axis=-1, keepdims=True) * dcol, 0.0))
            lcols[p].append(jnp.where(rowc > j, jnp.sum(kbs[p] * kj, axis=-1, keepdims=True) * dcol, 0.0))
    for j in range(t_len - 1):
        sols = [sol - lcols[p][j] * sol[j:j + 1, :] for p, sol in enumerate(sols)]
    ss = [s0_ref[si, h] for si, h in pairs]
    sbs = [s.astype(BF16) for s in ss]
    v_news = [sol[:, :DV] - jnp.dot(sol[:, DV:].astype(BF16), sb, preferred_element_type=F32)
              for sol, sb in zip(sols, sbs)]
    os_ = [jnp.dot((q * e).astype(BF16), sb, preferred_element_type=F32) for q, e, sb in zip(qs, egcs, sbs)]
    for j in range(t_len):
        os_ = [o + acols[p][j] * v_news[p][j:j + 1, :] for p, o in enumerate(os_)]
    upds = [lax.dot_general((k * jnp.exp(gl - gc)).astype(BF16), vn.astype(BF16), (((0,), (0,)), ((), ())),
                            preferred_element_type=F32)
            for k, gl, gc, vn in zip(ks, glasts, gcs, v_news)]
    for p, (si, h) in enumerate(pairs):
        s1_ref[si, h] = ss[p] * jnp.exp(glasts[p]) + upds[p]
        sl = slice(h * DV, (h + 1) * DV)
        oa_ref[rows[si], sl] = _gated_rmsnorm(os_[p], z_ref[rows[si], sl], nw).astype(BF16)


def _sample_a(proj, row0, bsz, t_len, hist, s0, caw, alog_row, dtb_row, nw_row):
    nseq = SAMPLE_NSEQ
    tr = nseq * t_len
    assert t_len == SUBLANES and row0 % tr == 0 and bsz % nseq == 0
    r0 = row0 // tr
    wspec = lambda shape: pl.BlockSpec(shape, lambda b: (0,) * len(shape))
    return pl.pallas_call(
        functools.partial(_sample_a_body, t_len=t_len, nseq=nseq),
        grid=(bsz // nseq,),
        in_specs=[
            pl.BlockSpec((tr, 3 * D_A), lambda b: (r0 + b, COL_QKV // (3 * D_A))),
            pl.BlockSpec((tr, D_A), lambda b: (r0 + b, COL_Z // D_A)),
            pl.BlockSpec((tr, LANES), lambda b: (r0 + b, COL_BA // LANES)),
            pl.BlockSpec((nseq, CONV_W - 1, 3 * D_A), lambda b: (b, 0, 0)),
            pl.BlockSpec((nseq, H_A, DK, DV), lambda b: (b, 0, 0, 0)),
            wspec((CONV_W, 3 * D_A)), wspec((1, LANES)), wspec((1, LANES)), wspec((1, DV)),
        ],
        out_specs=[
            pl.BlockSpec((tr, D_A), lambda b: (b, 0)),
            pl.BlockSpec((nseq, H_A, DK, DV), lambda b: (b, 0, 0, 0)),
        ],
        out_shape=[
            jax.ShapeDtypeStruct((bsz * t_len, D_A), BF16),
            jax.ShapeDtypeStruct((bsz, H_A, DK, DV), F32),
        ],
        scratch_shapes=[pltpu.VMEM((nseq, 2 * SUBLANES, 3 * D_A), F32)],
        compiler_params=_cparams(("arbitrary",)),
        name="sample_a",
    )(proj, proj, proj, hist, s0, caw, alog_row, dtb_row, nw_row)


def _lru_coeffs(xc, wa_ref, wx_ref, lba_ref, lbx_ref, lam_ref):
    parts = []
    for n in range(NB):
        sl = slice(n * BW, (n + 1) * BW)
        xn = xc[:, sl]
        xnb = xn.astype(BF16)
        gr = _sigmoid(jnp.dot(xnb, wa_ref[n].astype(BF16), preferred_element_type=F32) + lba_ref[:, sl])
        gi = _sigmoid(jnp.dot(xnb, wx_ref[n].astype(BF16), preferred_element_type=F32) + lbx_ref[:, sl])
        log_a = -LRU_C * gr * _softplus(-lam_ref[:, sl])
        parts.append((jnp.exp(log_a), jnp.sqrt(1.0 - jnp.exp(2.0 * log_a)), gi * xn))
    return parts


def _group_scan(a, b):
    rowi = lax.broadcasted_iota(jnp.int32, a.shape, 0)
    sft = 1
    while sft < SUBLANES:
        keep = rowi >= sft
        b = b + a * jnp.where(keep, pltpu.roll(b, sft, axis=0), 0.0)
        a = a * jnp.where(keep, pltpu.roll(a, sft, axis=0), 1.0)
        sft *= 2
    return a, b


def _gelu_tanh(x):
    return 0.5 * x * (1.0 + jnp.tanh(math.sqrt(2.0 / math.pi) * (x + 0.044715 * x * x * x)))


def _lru_finish(h, yb, nbw):
    hg = h * _gelu_tanh(yb)
    return (hg * lax.rsqrt(jnp.mean(hg * hg, axis=-1, keepdims=True) + RMS_EPS) * nbw).astype(BF16)


def _lru_prompt_body(xc_ref, yb_ref, wa_ref, wx_ref, lba_ref, lbx_ref, lam_ref, nbw_ref,
                     ob_ref, hfin_ref, a_ref, b_ref, h_ref, carry_ref, *, rows):
    t = pl.program_id(1)
    parts = _lru_coeffs(xc_ref[...], wa_ref, wx_ref, lba_ref, lbx_ref, lam_ref)
    rowi = lax.broadcasted_iota(jnp.int32, (rows, BW), 0)
    first = jnp.logical_and(rowi == 0, t == 0)
    for n in range(NB):
        sl = slice(n * BW, (n + 1) * BW)
        a_n, mult_n, gix_n = parts[n]
        a_ref[:, sl] = jnp.where(first, 0.0, a_n)
        b_ref[:, sl] = jnp.where(first, 1.0, mult_n) * gix_n

    @pl.when(t == 0)
    def _():
        carry_ref[...] = jnp.zeros(carry_ref.shape, F32)

    def group(gidx, carry):
        r0 = pl.multiple_of(gidx * SUBLANES, SUBLANES)
        ag, bg = _group_scan(a_ref[pl.ds(r0, SUBLANES), :], b_ref[pl.ds(r0, SUBLANES), :])
        hg = ag * carry + bg
        h_ref[pl.ds(r0, SUBLANES), :] = hg
        return jnp.broadcast_to(hg[SUBLANES - 1:SUBLANES, :], hg.shape)

    carry = lax.fori_loop(0, rows // SUBLANES, group, carry_ref[...])
    carry_ref[...] = carry
    ob_ref[...] = _lru_finish(h_ref[...], yb_ref[...], nbw_ref[...])

    @pl.when(t == pl.num_programs(1) - 1)
    def _():
        hfin_ref[...] = carry[0:1, :]


def _lru_prompt(xbc, proj, bsz, t_len, wa, wx, lba, lbx, lam, nbw):
    rows = min(CONV_L, t_len)
    nt = t_len // rows
    mp = bsz * t_len
    wspec = lambda shape: pl.BlockSpec(shape, lambda b, t: (0,) * len(shape))
    return pl.pallas_call(
        functools.partial(_lru_prompt_body, rows=rows),
        grid=(bsz, nt),
        in_specs=[
            pl.BlockSpec((rows, D_B), lambda b, t: (b * nt + t, 0)),
            pl.BlockSpec((rows, D_B), lambda b, t: (b * nt + t, COL_YB // D_B)),
            wspec((NB, BW, BW)), wspec((NB, BW, BW)),
            wspec((1, D_B)), wspec((1, D_B)), wspec((1, D_B)), wspec((1, D_B)),
        ],
        out_specs=[
            pl.BlockSpec((rows, D_B), lambda b, t: (b * nt + t, 0)),
            pl.BlockSpec((None, 1, D_B), lambda b, t: (b, 0, 0)),
        ],
        out_shape=[
            jax.ShapeDtypeStruct((mp, D_B), BF16),
            jax.ShapeDtypeStruct((bsz, 1, D_B), F32),
        ],
        scratch_shapes=[pltpu.VMEM((rows, D_B), F32), pltpu.VMEM((rows, D_B), F32),
                        pltpu.VMEM((rows, D_B), F32), pltpu.VMEM((SUBLANES, D_B), F32)],
        compiler_params=_cparams(("arbitrary", "arbitrary")),
        name="lru_prompt",
    )(xbc, proj, wa, wx, lba, lbx, lam, nbw)


def _lru_sample_body(xb_ref, yb_ref, hist_ref, h0_ref, cbw_ref, cbb_ref, wa_ref, wx_ref, lba_ref,
                     lbx_ref, lam_ref, nbw_ref, ob_ref, h1_ref, hx_ref, a_ref, b_ref, h_ref,
                     *, nseq, t_len):
    hdr = SUBLANES
    nh = CONV_W - 1
    hx_ref[0:hdr, :] = jnp.zeros((hdr, D_B), F32)

    def conv_seq(si, _):
        r0 = pl.multiple_of(si * t_len, t_len)
        hx_ref[hdr - nh:hdr, :] = hist_ref[si]
        hx_ref[hdr:hdr + t_len, :] = xb_ref[pl.ds(r0, t_len), :]
        for n in range(NB):
            c0 = n * BW
            h_ref[pl.ds(r0, t_len), c0:c0 + BW] = (_conv_taps(hx_ref, cbw_ref, c0, BW, t_len, hdr)
                                                  + cbb_ref[:, c0:c0 + BW])
        return 0

    lax.fori_loop(0, nseq, conv_seq, 0)
    parts = _lru_coeffs(h_ref[...], wa_ref, wx_ref, lba_ref, lbx_ref, lam_ref)
    for n in range(NB):
        sl = slice(n * BW, (n + 1) * BW)
        a_n, mult_n, gix_n = parts[n]
        a_ref[:, sl] = a_n
        b_ref[:, sl] = mult_n * gix_n

    def seq(si, _):
        r0 = pl.multiple_of(si * t_len, t_len)
        ag, bg = _group_scan(a_ref[pl.ds(r0, t_len), :], b_ref[pl.ds(r0, t_len), :])
        hg = ag * h0_ref[pl.ds(si, 1), :] + bg
        h_ref[pl.ds(r0, t_len), :] = hg
        h1_ref[pl.ds(si, 1), :] = hg[t_len - 1:t_len, :]
        return 0

    lax.fori_loop(0, nseq, seq, 0)
    ob_ref[...] = _lru_finish(h_ref[...], yb_ref[...], nbw_ref[...])


def _lru_sample(proj, row0, bsz, t_len, hist, h0, cbw, cbb, wa, wx, lba, lbx, lam, nbw):
    assert t_len == SUBLANES
    ms = bsz * t_len
    assert row0 % ms == 0
    rblk = row0 // ms
    wspec = lambda shape: pl.BlockSpec(shape, lambda i: (0,) * len(shape))
    return pl.pallas_call(
        functools.partial(_lru_sample_body, nseq=bsz, t_len=t_len),
        grid=(1,),
        in_specs=[
            pl.BlockSpec((ms, D_B), lambda i: (rblk, COL_XB // D_B)),
            pl.BlockSpec((ms, D_B), lambda i: (rblk, COL_YB // D_B)),
            wspec((bsz, CONV_W - 1, D_B)), wspec((bsz, D_B)),
            wspec((CONV_W, D_B)), wspec((1, D_B)),
            wspec((NB, BW, BW)), wspec((NB, BW, BW)),
            wspec((1, D_B)), wspec((1, D_B)), wspec((1, D_B)), wspec((1, D_B)),
        ],
        out_specs=[wspec((ms, D_B)), wspec((bsz, D_B))],
        out_shape=[
            jax.ShapeDtypeStruct((ms, D_B), BF16),
            jax.ShapeDtypeStruct((bsz, D_B), F32),
        ],
        scratch_shapes=[pltpu.VMEM((2 * SUBLANES, D_B), F32), pltpu.VMEM((ms, D_B), F32),
                        pltpu.VMEM((ms, D_B), F32), pltpu.VMEM((ms, D_B), F32)],
        compiler_params=_cparams(("arbitrary",)),
        name="lru_sample",
    )(proj, proj, hist, h0, cbw, cbb, wa, wx, lba, lbx, lam, nbw)


def _layernorm_rows(v, g, b):
    mu = jnp.mean(v, axis=-1, keepdims=True)
    d = v - mu
    var = jnp.mean(d * d, axis=-1, keepdims=True)
    return d * lax.rsqrt(var + LN_EPS) * g + b


def _store_slabs(slab_ref, base, val):
    n = val.shape[0]
    for s in range(ROW_SLAB):
        slab_ref[pl.ds(base + s, n, stride=ROW_SLAB), :] = val[:, s * LANES:(s + 1) * LANES]


def _load_slab_chunk(slab_ref, base, n, s):
    return slab_ref[pl.ds(base + s, n, stride=ROW_SLAB), :]


def _outproj_body(oap_ref, oas_ref, obp_ref, obs_ref, xp_ref, xs_ref, wo_ref, g_ref, b_ref,
                  wr_ref, br_ref, x1_ref, ti_ref, tg_ref, *, n_ptiles):
    i = pl.program_id(0)

    def run(oa_ref, ob_ref, x_ref):
        mix = (jnp.dot(oa_ref[...], wo_ref[0:D_A, :], preferred_element_type=F32)
               + jnp.dot(ob_ref[...], wo_ref[D_A:D_A + D_B, :], preferred_element_type=F32))
        y = _layernorm_rows(DN_ALPHA * x_ref[...] + mix, g_ref[...], b_ref[...])
        _store_slabs(x1_ref, 0, y)
        logits = _dot3(y, wr_ref[...]) + br_ref[...]
        lane = lax.broadcasted_iota(jnp.int32, logits.shape, 1)
        lane_o = lax.broadcasted_iota(jnp.int32, ti_ref.shape, 1)
        cur = logits
        ti = jnp.zeros(ti_ref.shape, jnp.int32)
        tv = jnp.zeros(tg_ref.shape, F32)
        v0 = None
        den = None
        for kk in range(TOP_K):
            m = jnp.max(cur, axis=-1, keepdims=True)
            idx = jnp.min(jnp.where(cur == m, lane, N_EXPERTS), axis=-1, keepdims=True)
            cur = jnp.where(lane == idx, -jnp.inf, cur)
            if kk == 0:
                v0 = m
            e = jnp.exp(m - v0)
            den = e if den is None else den + e
            ti = jnp.where(lane_o == kk, idx, ti)
            tv = jnp.where(lane_o == kk, e, tv)
        ti_ref[...] = ti
        tg_ref[...] = tv / den

    @pl.when(i < n_ptiles)
    def _():
        run(oap_ref, obp_ref, xp_ref)

    @pl.when(i >= n_ptiles)
    def _():
        run(oas_ref, obs_ref, xs_ref)


def _outproj(oa_p, oa_s, ob_p, ob_s, xp2, xs2, wo, g_row, b_row, wr, br_row):
    mp, ms = xp2.shape[0], xs2.shape[0]
    tm = min(OUT_TM, ms)
    assert mp % tm == 0 and ms % tm == 0
    n_pt, n_st = mp // tm, ms // tm
    pmap = lambda i: (jnp.minimum(i, n_pt - 1), 0)
    smap = lambda i: (jnp.maximum(i - n_pt, 0), 0)
    wspec = lambda shape: pl.BlockSpec(shape, lambda i: (0, 0))
    m = mp + ms
    return pl.pallas_call(
        functools.partial(_outproj_body, n_ptiles=n_pt),
        grid=(n_pt + n_st,),
        in_specs=[
            pl.BlockSpec((tm, D_A), pmap), pl.BlockSpec((tm, D_A), smap),
            pl.BlockSpec((tm, D_B), pmap), pl.BlockSpec((tm, D_B), smap),
            pl.BlockSpec((tm, D_MODEL), pmap), pl.BlockSpec((tm, D_MODEL), smap),
            wspec((D_A + D_B, D_MODEL)), wspec((1, D_MODEL)), wspec((1, D_MODEL)),
            wspec((D_MODEL, N_EXPERTS)), wspec((1, N_EXPERTS)),
        ],
        out_specs=[
            pl.BlockSpec((tm * ROW_SLAB, LANES), lambda i: (i, 0)),
            pl.BlockSpec((tm, LANES), lambda i: (i, 0)),
            pl.BlockSpec((tm, LANES), lambda i: (i, 0)),
        ],
        out_shape=[
            jax.ShapeDtypeStruct((m * ROW_SLAB, LANES), F32),
            jax.ShapeDtypeStruct((m, LANES), jnp.int32),
            jax.ShapeDtypeStruct((m, LANES), F32),
        ],
        compiler_params=_cparams(("arbitrary",)),
        name="outproj_ln_router",
    )(oa_p, oa_s, ob_p, ob_s, xp2, xs2, wo, g_row, b_row, wr, br_row)


def _zero_tail(zbuf_ref, dst_ref, tail_row, n_blocks, sem):
    scale = ROW_SLAB if len(dst_ref.shape) == 2 else 1

    def tail_copy(t):
        z0 = pl.multiple_of((tail_row + t * MOE_RB) * scale, MOE_RB * scale)
        return pltpu.make_async_copy(zbuf_ref, dst_ref.at[pl.ds(z0, MOE_RB * scale)], sem)

    def start(t, _):
        tail_copy(t).start()
        return 0

    def wait(t, _):
        tail_copy(t).wait()
        return 0

    lax.fori_loop(0, n_blocks, start, 0)
    lax.fori_loop(0, n_blocks, wait, 0)


def _dispatch_body(zrow_ref, slot_ref, x_ref, xs_ref, zbuf_ref, zsem, rsem, *, tm):
    i = pl.program_id(0)

    @pl.when(i == 0)
    def _():
        zbuf_ref[...] = jnp.zeros(zbuf_ref.shape, F32)

        def zero_copy(e):
            z0 = pl.multiple_of(zrow_ref[e], MOE_RB)
            return pltpu.make_async_copy(zbuf_ref, xs_ref.at[pl.ds(z0, MOE_RB)], zsem)

        for e in range(N_EXPERTS):
            @pl.when(zrow_ref[e] >= 0)
            def _():
                zero_copy(e).start()
        for e in range(N_EXPERTS):
            @pl.when(zrow_ref[e] >= 0)
            def _():
                zero_copy(e).wait()
        _zero_tail(zbuf_ref, xs_ref, zrow_ref[N_EXPERTS], zrow_ref[N_EXPERTS + 1], zsem)

    def row_copy(r, dst):
        src0 = pl.multiple_of(r * ROW_SLAB, ROW_SLAB)
        return pltpu.make_async_copy(x_ref.at[pl.ds(src0, ROW_SLAB)], xs_ref.at[dst], rsem)

    def issue(r, _):
        for kk in range(TOP_K):
            row_copy(r, slot_ref[0, r * TOP_K + kk]).start(priority=kk % 2)
        return 0

    lax.fori_loop(0, tm, issue, 0, unroll=8)

    for kk in range(TOP_K):
        pltpu.make_async_copy(xs_ref.at[pl.ds(0, tm)], xs_ref.at[pl.ds(0, tm)], rsem).wait()


def _dispatch(x1, slot_tiles, zrow, n_rows):
    m = x1.shape[0] // ROW_SLAB
    tm = TOK_TM
    assert m % tm == 0
    return pl.pallas_call(
        functools.partial(_dispatch_body, tm=tm),
        grid_spec=pltpu.PrefetchScalarGridSpec(
            num_scalar_prefetch=1,
            grid=(m // tm,),
            in_specs=[
                pl.BlockSpec((None, 1, tm * TOP_K), lambda i, z: (i, 0, 0), memory_space=pltpu.SMEM),
                pl.BlockSpec((tm * ROW_SLAB, LANES), lambda i, z: (i, 0)),
            ],
            out_specs=pl.BlockSpec(memory_space=pl.ANY),
            scratch_shapes=[pltpu.VMEM((MOE_RB, ROW_SLAB, LANES), F32),
                            pltpu.SemaphoreType.DMA(()), pltpu.SemaphoreType.DMA(())],
        ),
        out_shape=jax.ShapeDtypeStruct((n_rows, ROW_SLAB, LANES), F32),
        compiler_params=_cparams(("arbitrary",)),
        name="moe_dispatch",
    )(zrow, slot_tiles, x1)


def _for_row_blocks(nsub, fn):
    for k in range(1, MOE_SB + 1):
        @pl.when(nsub == k)
        def _():
            fn(0, k * MOE_RB)


def _moe_body(ie_ref, irow_ref, insub_ref, tail_ref, xs_ref, wg_ref, wu_ref, bup_ref,
              wd_ref, bd_ref, ys_ref, xstage_ref, xb_ref, act_ref, ybuf_ref, wgb_ref, wub_ref, wdb_ref,
              isem, osem):
    wi = pl.program_id(0)
    j = pl.program_id(1)
    n_w = pl.num_programs(0)
    nsub = insub_ref[wi]
    row0 = irow_ref[wi]
    nt = MOE_NT
    is_last = wi == n_w - 1
    nxt = jnp.minimum(wi + 1, n_w - 1)
    nsub_next = jnp.where(is_last, 0, insub_ref[nxt])
    row_next = irow_ref[nxt]
    prv = jnp.maximum(wi - 1, 0)
    nsub_prev = jnp.where(wi == 0, 0, insub_ref[prv])
    row_prev = irow_ref[prv]

    class _XCopy:
        def __init__(self, row_base, s, slot):
            g0 = pl.multiple_of(row_base + s * MOE_RB, MOE_RB)
            self.copies = [
                pltpu.make_async_copy(xs_ref.at[pl.ds(g0, MOE_RB), c],
                                      xstage_ref.at[slot, :, pl.ds(c * LANES, LANES)], isem.at[slot])
                for c in range(ROW_SLAB)]

        def start(self):
            for cp in self.copies:
                cp.start()

        def wait(self):
            for cp in self.copies:
                cp.wait()

    x_copy = _XCopy

    def to_matmul_layout(s, slot):
        r0 = pl.multiple_of(s * MOE_RB, MOE_RB)
        xb_ref[pl.ds(r0, MOE_RB), :] = xstage_ref[slot].astype(BF16)

    def y_copy(row_base, s):
        b0 = pl.multiple_of(s * MOE_RB * ROW_SLAB, MOE_RB * ROW_SLAB)
        g0 = pl.multiple_of(row_base * ROW_SLAB, MOE_RB * ROW_SLAB) + b0
        return pltpu.make_async_copy(ybuf_ref.at[pl.ds(b0, MOE_RB * ROW_SLAB)],
                                     ys_ref.at[pl.ds(g0, MOE_RB * ROW_SLAB)], osem)

    @pl.when(jnp.logical_and(wi == 0, j == 0))
    def _():
        def load(s, _):
            cp = x_copy(row0, s, 0)
            cp.start()
            cp.wait()
            to_matmul_layout(s, 0)
            return 0

        lax.fori_loop(0, nsub, load, 0)

    @pl.when(jnp.logical_and(j == nt - 1, nsub_next > 0))
    def _():
        x_copy(row_next, 0, 0).start()

    @pl.when(jnp.logical_and(j >= nt, j - nt < nsub_next))
    def _():
        s = j - nt
        slot = lax.rem(s, 2)
        x_copy(row_next, s, slot).wait()

        @pl.when(s + 1 < nsub_next)
        def _():
            x_copy(row_next, s + 1, 1 - slot).start()

        to_matmul_layout(s, slot)

    @pl.when(jnp.logical_and(j == nt, nsub_prev > 0))
    def _():
        def drain(s, _):
            y_copy(row_prev, s).wait()
            return 0

        lax.fori_loop(0, nsub_prev, drain, 0)

    @pl.when(nsub > 0)
    def _():
        @pl.when(j < nt)
        def _():
            wgb_ref[...] = wg_ref[...].astype(BF16)
            wub_ref[...] = wu_ref[...].astype(BF16)
            b_gate = bup_ref[ie_ref[wi], pl.ds(j, 1), :]
            b_up = bup_ref[ie_ref[wi], pl.ds(nt + j, 1), :]

            def up_rows(r0, nrows):
                x = xb_ref[pl.ds(r0, nrows), :]
                hg = jnp.dot(x, wgb_ref[...], preferred_element_type=F32) + b_gate
                hu = jnp.dot(x, wub_ref[...], preferred_element_type=F32) + b_up
                gate = jnp.minimum(hg, SWIGLU_LIMIT)
                up = jnp.clip(hu, -SWIGLU_LIMIT, SWIGLU_LIMIT)
                glu = gate * _sigmoid(SWIGLU_ALPHA * gate)
                act_ref[j, pl.ds(r0, nrows), :] = (glu * (up + 1.0)).astype(BF16)

            _for_row_blocks(nsub, up_rows)

        @pl.when(j >= nt)
        def _():
            wdb_ref[...] = wd_ref[...].astype(BF16)
            n = j - nt
            b_down = bd_ref[ie_ref[wi], pl.ds(n, 1), :]

            def down_rows(r0, nrows):
                acc = jnp.dot(act_ref[0, pl.ds(r0, nrows), :], wdb_ref[0:MOE_TF, :],
                              preferred_element_type=F32)
                for jj in range(1, nt):
                    acc = acc + jnp.dot(act_ref[jj, pl.ds(r0, nrows), :],
                                        wdb_ref[jj * MOE_TF:(jj + 1) * MOE_TF, :],
                                        preferred_element_type=F32)
                acc = acc + b_down
                cpt = MOE_TF // LANES
                for cc in range(cpt):
                    ybuf_ref[pl.ds(r0 * ROW_SLAB + n * cpt + cc, nrows, stride=ROW_SLAB), :] = (
                        acc[:, cc * LANES:(cc + 1) * LANES])

            _for_row_blocks(nsub, down_rows)

        @pl.when(j == 2 * nt - 1)
        def _():
            def store(s, _):
                y_copy(row0, s).start()
                return 0

            lax.fori_loop(0, nsub, store, 0)

    @pl.when(jnp.logical_and(is_last, j == 2 * nt - 1))
    def _():
        def drain(s, _):
            y_copy(row0, s).wait()
            return 0

        lax.fori_loop(0, nsub, drain, 0)
        zsrc = ybuf_ref.at[pl.ds(0, MOE_RB * ROW_SLAB)]
        zsrc[...] = jnp.zeros(zsrc.shape, F32)
        _zero_tail(zsrc, ys_ref, tail_ref[0], tail_ref[1], osem)


def _moe(xs, n_items, item_e, item_row, item_nsub, tail, w_up, b_up3, w_down, b_down3):
    n_rows = xs.shape[0] * ROW_SLAB
    nt = MOE_NT
    assert MOE_SB <= nt
    up_off = D_FF // MOE_TF
    p1 = lambda j, n, w: jnp.where(n[w] > 0, jnp.minimum(j, nt - 1), nt - 1)
    p2 = lambda j, n, w: jnp.where(n[w] > 0, jnp.maximum(j - nt, 0), nt - 1)
    return pl.pallas_call(
        _moe_body,
        grid_spec=pltpu.PrefetchScalarGridSpec(
            num_scalar_prefetch=4,
            grid=(n_items, 2 * nt),
            in_specs=[
                pl.BlockSpec(memory_space=pl.ANY),
                pl.BlockSpec((None, D_MODEL, MOE_TF), lambda w, j, e, r, n, t: (e[w], 0, p1(j, n, w))),
                pl.BlockSpec((None, D_MODEL, MOE_TF),
                             lambda w, j, e, r, n, t: (e[w], 0, up_off + p1(j, n, w))),
                pl.BlockSpec((N_EXPERTS, 2 * nt, MOE_TF), lambda w, j, e, r, n, t: (0, 0, 0)),
                pl.BlockSpec((None, D_FF, MOE_TF), lambda w, j, e, r, n, t: (e[w], 0, p2(j, n, w))),
                pl.BlockSpec((N_EXPERTS, nt, MOE_TF), lambda w, j, e, r, n, t: (0, 0, 0)),
            ],
            out_specs=pl.BlockSpec(memory_space=pl.ANY),
            scratch_shapes=[
                pltpu.VMEM((2, MOE_RB, D_MODEL), F32),
                pltpu.VMEM((MOE_R, D_MODEL), BF16),
                pltpu.VMEM((nt, MOE_R, MOE_TF), BF16),
                pltpu.VMEM((MOE_R * ROW_SLAB, LANES), F32),
                pltpu.VMEM((D_MODEL, MOE_TF), BF16),
                pltpu.VMEM((D_MODEL, MOE_TF), BF16),
                pltpu.VMEM((D_FF, MOE_TF), BF16),
                pltpu.SemaphoreType.DMA((2,)), pltpu.SemaphoreType.DMA(()),
            ],
        ),
        out_shape=jax.ShapeDtypeStruct((n_rows, LANES), F32),
        compiler_params=_cparams(("arbitrary", "arbitrary")),
        name="moe_experts",
    )(item_e, item_row, item_nsub, tail, xs, w_up, w_up, b_up3, w_down, b_down3)


def _combine_body(slot_ref, slotn_ref, ys_ref, x1_ref, gate_ref, g_ref, b_ref, yp_ref, ysm_ref, buf_ref,
                  v_ref, sem, *, tm, n_ptiles):
    i = pl.program_id(0)
    cur = lax.rem(i, 2)

    def gather(table_ref, bslot):
        def issue(r, _):
            for kk in range(TOP_K):
                src0 = pl.multiple_of(table_ref[0, r * TOP_K + kk] * ROW_SLAB, ROW_SLAB)
                dst0 = pl.multiple_of(r * ROW_SLAB, ROW_SLAB)
                pltpu.make_async_copy(ys_ref.at[pl.ds(src0, ROW_SLAB)],
                                      buf_ref.at[bslot, kk, pl.ds(dst0, ROW_SLAB)],
                                      sem.at[bslot]).start(priority=kk % 2)
            return 0

        lax.fori_loop(0, tm, issue, 0, unroll=8)

    @pl.when(i == 0)
    def _():
        gather(slot_ref, 0)

    @pl.when(i + 1 < pl.num_programs(0))
    def _():
        gather(slotn_ref, 1 - cur)

    for kk in range(TOP_K):
        pltpu.make_async_copy(ys_ref.at[pl.ds(0, tm * ROW_SLAB)], buf_ref.at[cur, kk], sem.at[cur]).wait()
    gates = gate_ref[...]
    for c in range(ROW_SLAB):
        acc = DN_ALPHA * _load_slab_chunk(x1_ref, 0, tm, c)
        for kk in range(TOP_K):
            acc = acc + gates[:, kk:kk + 1] * _load_slab_chunk(buf_ref.at[cur, kk], 0, tm, c)
        v_ref[:, c * LANES:(c + 1) * LANES] = acc
    y = _layernorm_rows(v_ref[...], g_ref[...], b_ref[...])

    @pl.when(pl.program_id(0) < n_ptiles)
    def _():
        yp_ref[...] = y

    @pl.when(pl.program_id(0) >= n_ptiles)
    def _():
        ysm_ref[...] = y


def _combine(ys, slot_tiles, x1, gates, g_row, b_row, mp):
    m = x1.shape[0] // ROW_SLAB
    tm = TOK_TM
    assert mp % tm == 0 and (m - mp) % tm == 0
    n_pt = mp // tm
    wspec = lambda shape: pl.BlockSpec(shape, lambda i: (0, 0))
    n_t = m // tm
    return pl.pallas_call(
        functools.partial(_combine_body, tm=tm, n_ptiles=n_pt),
        grid=(n_t,),
        in_specs=[
            pl.BlockSpec((None, 1, tm * TOP_K), lambda i: (i, 0, 0), memory_space=pltpu.SMEM),
            pl.BlockSpec((None, 1, tm * TOP_K), lambda i: (jnp.minimum(i + 1, n_t - 1), 0, 0),
                         memory_space=pltpu.SMEM),
            pl.BlockSpec(memory_space=pl.ANY),
            pl.BlockSpec((tm * ROW_SLAB, LANES), lambda i: (i, 0)),
            pl.BlockSpec((tm, LANES), lambda i: (i, 0)),
            wspec((1, D_MODEL)), wspec((1, D_MODEL)),
        ],
        out_specs=[pl.BlockSpec((tm, D_MODEL), lambda i: (jnp.minimum(i, n_pt - 1), 0)),
                   pl.BlockSpec((tm, D_MODEL), lambda i: (jnp.maximum(i - n_pt, 0), 0))],
        out_shape=[jax.ShapeDtypeStruct((mp, D_MODEL), F32),
                   jax.ShapeDtypeStruct((m - mp, D_MODEL), F32)],
        scratch_shapes=[pltpu.VMEM((2, TOP_K, tm * ROW_SLAB, LANES), F32),
                        pltpu.VMEM((tm, D_MODEL), F32), pltpu.SemaphoreType.DMA((2,))],
        compiler_params=_cparams(("arbitrary",)),
        name="moe_combine_ln",
    )(slot_tiles, slot_tiles, ys, x1, gates, g_row, b_row)


def _routing_tables(top_i, m):
    e_ids = jnp.arange(N_EXPERTS, dtype=jnp.int32)
    onehot = (top_i[:, :, None] == e_ids[None, None, :]).astype(jnp.int32)
    mask = jnp.sum(onehot, axis=1)
    incl = jnp.cumsum(mask, axis=0)
    pos = incl - mask
    counts = incl[-1]
    nsub = (counts + MOE_RB - 1) // MOE_RB
    padded = nsub * MOE_RB
    gend = jnp.cumsum(padded)
    gstart = gend - padded
    slot = jnp.sum(onehot * (gstart[None, None, :] + pos[:, None, :]), axis=2)
    n_rows = (m * TOP_K // MOE_RB + N_EXPERTS) * MOE_RB
    tail = jnp.stack([gend[-1], (n_rows - gend[-1]) // MOE_RB]).astype(jnp.int32)
    zrow = jnp.concatenate([jnp.where(counts > 0, gend - MOE_RB, -1).astype(jnp.int32), tail])
    n_items = N_EXPERTS + (m * TOP_K // MOE_RB + N_EXPERTS) // MOE_SB
    ipe = (nsub + MOE_SB - 1) // MOE_SB
    iend = jnp.cumsum(ipe)
    istart = iend - ipe
    total = iend[-1]
    wid = jnp.arange(n_items, dtype=jnp.int32)
    wclamp = jnp.minimum(wid, total - 1)
    ie = jnp.minimum(jnp.searchsorted(iend, wclamp, side="right"), N_EXPERTS - 1).astype(jnp.int32)
    jn = wclamp - istart[ie]
    irow = (gstart[ie] + jn * MOE_R).astype(jnp.int32)
    insub = jnp.where(wid < total, jnp.clip(nsub[ie] - jn * MOE_SB, 0, MOE_SB), 0).astype(jnp.int32)
    return slot.astype(jnp.int32), zrow, tail, total.astype(jnp.int32), ie, irow, insub, n_rows


def kernel(x_prompt, x_sample, state_conv_a, state_delta, state_conv_b, state_lru, w_in, conv_a_w,
           a_log, dt_bias, norm_a_w, conv_b_w, conv_b_b, lru_wa, lru_ba, lru_wx, lru_bx, lru_lambda,
           norm_b_w, w_out, ln1_g, ln1_b, w_router, b_router, w_up, b_up, w_down, b_down, ln2_g, ln2_b):
    bp, tp, _ = x_prompt.shape
    bs, ts, _ = x_sample.shape
    mp, ms = bp * tp, bs * ts
    m = mp + ms
    l = 0
    xp2 = x_prompt.reshape(mp, D_MODEL)
    xs2 = x_sample.reshape(ms, D_MODEL)

    wp = _wprep(w_in[l])
    wo = w_out[l].astype(BF16)
    pad_h = lambda v: jnp.zeros((1, LANES), F32).at[0, H_A:2 * H_A].set(v)
    alog_row = pad_h(a_log[l])
    dtb_row = pad_h(dt_bias[l])
    row = lambda v: v.reshape(1, -1)

    proj = _inproj(xp2, xs2, wp)

    qkv_act, xbc, bg = _convact_prompt(proj, bp, tp, conv_a_w[l], conv_b_w[l], row(conv_b_b[l]),
                                       alog_row, dtb_row)
    u, w, qe, kdt, aqk, el = _delta_prep(qkv_act, bg, bp, tp)
    oa_p, sd_p = _delta_seq(u, w, qe, kdt, aqk, el, proj, row(norm_a_w[l]), bp, tp)
    ob_p, h_p = _lru_prompt(xbc, proj, bp, tp, lru_wa[l], lru_wx[l], row(lru_ba[l]), row(lru_bx[l]),
                            row(lru_lambda[l]), row(norm_b_w[l]))
    oa_s, sd_s = _sample_a(proj, mp, bs, ts, state_conv_a[l], state_delta[l], conv_a_w[l],
                           alog_row, dtb_row, row(norm_a_w[l]))
    ob_s, h_s = _lru_sample(proj, mp, bs, ts, state_conv_b[l], state_lru[l], conv_b_w[l],
                            row(conv_b_b[l]), lru_wa[l], lru_wx[l], row(lru_ba[l]), row(lru_bx[l]),
                            row(lru_lambda[l]), row(norm_b_w[l]))

    x1, ti, tg = _outproj(oa_p, oa_s, ob_p, ob_s, xp2, xs2, wo, row(ln1_g[l]), row(ln1_b[l]),
                          w_router[l], row(b_router[l]))

    slot, zrow, tail, n_used, ie, irow, insub, n_rows = _routing_tables(ti[:, :TOP_K], m)
    slot_tiles = slot.reshape(m // TOK_TM, 1, TOK_TM * TOP_K)
    xs_sorted = _dispatch(x1, slot_tiles, zrow, n_rows)
    ys = _moe(xs_sorted, n_used, ie, irow, insub, tail, w_up[l],
              b_up[l].reshape(N_EXPERTS, 2 * MOE_NT, MOE_TF), w_down[l],
              b_down[l].reshape(N_EXPERTS, MOE_NT, MOE_TF))
    y_p, y_s = _combine(ys, slot_tiles, x1, tg, row(ln2_g[l]), row(ln2_b[l]), mp)

    y_prompt = y_p.reshape(bp, tp, D_MODEL)
    y_sample = y_s.reshape(bs, ts, D_MODEL)
    nh = CONV_W - 1
    assert tp % SUBLANES == 0 and ts == SUBLANES and nh <= SUBLANES
    pg = proj.reshape(m // SUBLANES, SUBLANES, N_PROJ)
    gp = tp // SUBLANES

    def last_rows(g0, g1, gstep, c0, width):
        return lax.slice(pg, (g0, SUBLANES - nh, c0), (g1, SUBLANES, c0 + width), (gstep, 1, 1))

    ca_p = last_rows(gp - 1, mp // SUBLANES, gp, COL_QKV, 3 * D_A)
    cb_p = last_rows(gp - 1, mp // SUBLANES, gp, COL_XB, D_B)
    ca_s = last_rows(mp // SUBLANES, m // SUBLANES, 1, COL_QKV, 3 * D_A)
    cb_s = last_rows(mp // SUBLANES, m // SUBLANES, 1, COL_XB, D_B)
    return (y_prompt, y_sample,
            ca_p[None], sd_p[None], cb_p[None], h_p.reshape(1, bp, D_B),
            ca_s[None], sd_s[None], cb_s[None], h_s[None])
```

```python
import functools
import math

import jax
import jax.numpy as jnp
from jax import lax
from jax.experimental import pallas as pl
from jax.experimental.pallas import tpu as pltpu

F32 = jnp.float32
BF16 = jnp.bfloat16
HIGHEST = lax.Precision.HIGHEST

D_MODEL = 2048
D_A = 1024
H_A = 8
DK = 128
DV = 128
CONV_W = 4
D_B = 1024
NB = 8
BW = 128
LRU_C = 8.0
N_EXPERTS = 32
TOP_K = 4
D_FF = 2048
SWIGLU_LIMIT = 7.0
SWIGLU_ALPHA = 1.702
DEPTH = 1
DN_ALPHA = (2.0 * DEPTH) ** 0.25
LN_EPS = 1e-5
RMS_EPS = 1e-6
L2_EPS = 1e-6

LANES = 128
SUBLANES = 8
VMEM_LIMIT_BYTES = 56 * 1024 * 1024

COL_QKV = 0
COL_Z = 3 * D_A
COL_XB = 4 * D_A
COL_YB = 4 * D_A + D_B
COL_BA = 4 * D_A + 2 * D_B
N_PROJ = COL_BA + 2 * LANES

INPROJ_TM = 1024
INPROJ_TN = 1280
CONV_L = 256
DELTA_C = 128
OUT_TM = 256
MOE_PB = 128
MOE_RB = 256
MOE_SB = 6
MOE_R = MOE_RB * MOE_SB
MOE_NPB = MOE_R // MOE_PB
MOE_EXACT_ODD = (7, 9, 11)
MOE_TF = 256
MOE_NT = D_FF // MOE_TF
SAMPLE_NSEQ = 4
TOK_TM = 256
ROW_SLAB = D_MODEL // LANES


def _cparams(sem, vmem=VMEM_LIMIT_BYTES):
    return pltpu.CompilerParams(dimension_semantics=sem, vmem_limit_bytes=vmem)


def _sigmoid(x):
    return 1.0 / (1.0 + jnp.exp(-x))


def _softplus(x):
    return jnp.maximum(x, 0.0) + jnp.log(1.0 + jnp.exp(-jnp.abs(x)))


def _wprep_body(w_ref, o_ref):
    c = 4 * D_A
    nba = 2 * H_A
    o_ref[0:c, :] = w_ref[0:c, :].astype(BF16)
    o_ref[c:c + 2 * D_B, :] = w_ref[c + nba:c + nba + 2 * D_B, :].astype(BF16)
    o_ref[COL_BA:COL_BA + nba, :] = w_ref[c:c + nba, :].astype(BF16)
    o_ref[COL_BA + nba:N_PROJ, :] = jnp.zeros((N_PROJ - COL_BA - nba, o_ref.shape[1]), BF16)


def _wprep(w_t):
    cols = 256
    d_in = w_t.shape[0]
    assert d_in == 4 * D_A + 2 * H_A + 2 * D_B and D_MODEL % cols == 0
    return pl.pallas_call(
        _wprep_body,
        grid=(D_MODEL // cols,),
        in_specs=[pl.BlockSpec((d_in, cols), lambda i: (0, i))],
        out_specs=pl.BlockSpec((N_PROJ, cols), lambda i: (0, i)),
        out_shape=jax.ShapeDtypeStruct((N_PROJ, D_MODEL), BF16),
        compiler_params=_cparams(("arbitrary",)),
        name="wprep",
    )(w_t)


def _inproj_body(xp_ref, xs_ref, w_ref, o_ref, xb_ref, *, n_ptiles):
    i = pl.program_id(0)

    @pl.when(pl.program_id(1) == 0)
    def _():
        @pl.when(i < n_ptiles)
        def _():
            xb_ref[...] = xp_ref[...].astype(BF16)

        @pl.when(i >= n_ptiles)
        def _():
            xb_ref[...] = xs_ref[...].astype(BF16)

    o_ref[...] = _dot_nt(xb_ref[...], w_ref[...])


def _inproj(xp2, xs2, wp):
    mp, ms = xp2.shape[0], xs2.shape[0]
    tm = min(INPROJ_TM, ms)
    assert mp % tm == 0 and ms % tm == 0
    n_pt, n_st = mp // tm, ms // tm
    n_nt = N_PROJ // INPROJ_TN
    return pl.pallas_call(
        functools.partial(_inproj_body, n_ptiles=n_pt),
        grid=(n_pt + n_st, n_nt),
        in_specs=[
            pl.BlockSpec((tm, D_MODEL), lambda i, j: (jnp.minimum(i, n_pt - 1), 0)),
            pl.BlockSpec((tm, D_MODEL), lambda i, j: (jnp.maximum(i - n_pt, 0), 0),
                         pipeline_mode=pl.Buffered(1)),
            pl.BlockSpec((INPROJ_TN, D_MODEL), lambda i, j: (j, 0)),
        ],
        out_specs=pl.BlockSpec((tm, INPROJ_TN), lambda i, j: (i, j)),
        out_shape=jax.ShapeDtypeStruct((mp + ms, N_PROJ), F32),
        scratch_shapes=[pltpu.VMEM((tm, D_MODEL), BF16)],
        compiler_params=_cparams(("arbitrary", "arbitrary")),
        name="inproj",
    )(xp2, xs2, wp)


def _conv_taps(h_ref, w_ref, c0, width, rows, base):
    acc = h_ref[base:base + rows, c0:c0 + width] * w_ref[CONV_W - 1:CONV_W, c0:c0 + width]
    for j in range(1, CONV_W):
        acc = acc + (h_ref[base - j:base - j + rows, c0:c0 + width]
                     * w_ref[CONV_W - 1 - j:CONV_W - j, c0:c0 + width])
    return acc


def _qkv_activation(acc, blk):
    a = acc * _sigmoid(acc)
    if blk < 2 * H_A:
        a = a * lax.rsqrt(jnp.sum(a * a, axis=-1, keepdims=True) + L2_EPS)
        if blk < H_A:
            a = a * (DK ** -0.5)
    return a


def _beta_g(ba, alog_ref, dtb_ref):
    lane = lax.broadcasted_iota(jnp.int32, ba.shape, 1)
    beta = _sigmoid(ba)
    g = -jnp.exp(alog_ref[...]) * _softplus(ba + dtb_ref[...])
    return jnp.where(lane < H_A, beta, jnp.where(lane < 2 * H_A, g, 0.0))


def _convact_prompt_body(qkv_ref, xb_ref, ba_ref, caw_ref, cbw_ref, cbb_ref, alog_ref, dtb_ref,
                         qkvo_ref, xbo_ref, bgo_ref, hq_ref, hx_ref, *, rows):
    t = pl.program_id(1)
    hdr = SUBLANES

    @pl.when(t == 0)
    def _():
        hq_ref[0:hdr, :] = jnp.zeros((hdr, 3 * D_A), F32)
        hx_ref[0:hdr, :] = jnp.zeros((hdr, D_B), F32)

    @pl.when(t > 0)
    def _():
        hq_ref[0:hdr, :] = hq_ref[rows:rows + hdr, :]
        hx_ref[0:hdr, :] = hx_ref[rows:rows + hdr, :]

    hq_ref[hdr:hdr + rows, :] = qkv_ref[...]
    hx_ref[hdr:hdr + rows, :] = xb_ref[...]
    for blk in range(3 * H_A):
        c0 = blk * DK
        acc = _conv_taps(hq_ref, caw_ref, c0, DK, rows, hdr)
        qkvo_ref[:, c0:c0 + DK] = _qkv_activation(acc, blk)
    for blk in range(NB):
        c0 = blk * BW
        xbo_ref[:, c0:c0 + BW] = _conv_taps(hx_ref, cbw_ref, c0, BW, rows, hdr) + cbb_ref[:, c0:c0 + BW]
    bgo_ref[...] = _beta_g(ba_ref[...], alog_ref, dtb_ref)


def _convact_prompt(proj, bsz, t_len, caw, cbw, cbb, alog_row, dtb_row):
    rows = min(CONV_L, t_len)
    assert t_len % rows == 0
    nt = t_len // rows
    mp = bsz * t_len
    wspec = lambda shape: pl.BlockSpec(shape, lambda b, t: (0, 0))
    return pl.pallas_call(
        functools.partial(_convact_prompt_body, rows=rows),
        grid=(bsz, nt),
        in_specs=[
            pl.BlockSpec((rows, 3 * D_A), lambda b, t: (b * nt + t, COL_QKV // (3 * D_A))),
            pl.BlockSpec((rows, D_B), lambda b, t: (b * nt + t, COL_XB // D_B)),
            pl.BlockSpec((rows, LANES), lambda b, t: (b * nt + t, COL_BA // LANES)),
            wspec((CONV_W, 3 * D_A)), wspec((CONV_W, D_B)), wspec((1, D_B)),
            wspec((1, LANES)), wspec((1, LANES)),
        ],
        out_specs=[
            pl.BlockSpec((rows, 3 * D_A), lambda b, t: (b * nt + t, 0)),
            pl.BlockSpec((rows, D_B), lambda b, t: (b * nt + t, 0)),
            pl.BlockSpec((rows, LANES), lambda b, t: (b * nt + t, 0)),
        ],
        out_shape=[
            jax.ShapeDtypeStruct((mp, 3 * D_A), F32),
            jax.ShapeDtypeStruct((mp, D_B), F32),
            jax.ShapeDtypeStruct((mp, LANES), F32),
        ],
        scratch_shapes=[pltpu.VMEM((SUBLANES + rows + SUBLANES, 3 * D_A), F32),
                        pltpu.VMEM((SUBLANES + rows + SUBLANES, D_B), F32)],
        compiler_params=_cparams(("arbitrary", "arbitrary")),
        name="convact_prompt",
    )(proj, proj, proj, caw, cbw, cbb, alog_row, dtb_row)


def _dot_nt(a, b, precision=None):
    return lax.dot_general(a, b, (((1,), (1,)), ((), ())), precision=precision,
                           preferred_element_type=F32)


def _split_bf16(x):
    hi = x.astype(BF16)
    return hi, (x - hi.astype(F32)).astype(BF16)


def _dot3(a, b):
    ah, al = _split_bf16(a)
    bh, bl = _split_bf16(b)
    return (jnp.dot(ah, bh, preferred_element_type=F32)
            + (jnp.dot(al, bh, preferred_element_type=F32) + jnp.dot(ah, bl, preferred_element_type=F32)))


def _delta_prep_body(qkv_ref, bg_ref, u_ref, w_ref, qe_ref, kdt_ref, aqk_ref, el_ref, *, c):
    row = lax.broadcasted_iota(jnp.int32, (c, c), 0)
    col = lax.broadcasted_iota(jnp.int32, (c, c), 1)
    incl = row >= col
    strict = row > col
    eye = (row == col).astype(F32)
    tril = incl.astype(F32)
    eye_l = (lax.broadcasted_iota(jnp.int32, (LANES, LANES), 0)
             == lax.broadcasted_iota(jnp.int32, (LANES, LANES), 1)).astype(F32)

    bg = bg_ref[...]
    gc_all = jnp.dot(tril, bg, precision=HIGHEST, preferred_element_type=F32)
    gc_t = _dot_nt(eye_l, gc_all, precision=HIGHEST)
    heads = range(H_A)
    lmats, rhss = [], []
    for h in heads:
        q = qkv_ref[:, h * DK:(h + 1) * DK]
        k = qkv_ref[:, (H_A + h) * DK:(H_A + h + 1) * DK]
        v = qkv_ref[:, (2 * H_A + h) * DK:(2 * H_A + h + 1) * DK]
        beta = bg[:, h:h + 1]
        gc = gc_all[:, H_A + h:H_A + h + 1]
        g_last = gc_all[c - 1:c, H_A + h:H_A + h + 1]
        diff = gc - gc_t[H_A + h:H_A + h + 1, :]
        decay = jnp.where(incl, jnp.exp(jnp.where(incl, diff, 0.0)), 0.0)
        kb = k * beta
        egc = jnp.exp(gc)
        kbf = k.astype(BF16)
        kk = _dot_nt(kbf, kbf)
        lmats.append(jnp.where(strict, kk * beta * decay, 0.0))
        rhss.append(jnp.concatenate([v * beta, kb * egc], axis=1))
        qe_ref[:, h * DK:(h + 1) * DK] = (q * egc).astype(BF16)
        aqk_ref[h] = (_dot_nt(q.astype(BF16), kbf) * decay).astype(BF16)
        kdec = k * jnp.exp(g_last - gc)
        kdt_ref[h] = kdec.astype(BF16)
        el_ref[h] = jnp.broadcast_to(jnp.exp(g_last), (SUBLANES, LANES))
    xbs = [(-lm).astype(BF16) for lm in lmats]
    t0s = [eye - lm for lm in lmats]
    p = 2
    while p < c:
        xbs = [jnp.dot(xb, xb, preferred_element_type=F32).astype(BF16) for xb in xbs]
        t0s = [t0 + jnp.dot(t0.astype(BF16), xb, preferred_element_type=F32) for t0, xb in zip(t0s, xbs)]
        p *= 2
    t0bs = [t0.astype(BF16) for t0 in t0s]
    sol0s = [jnp.dot(t0b, rhs.astype(BF16), preferred_element_type=F32) for t0b, rhs in zip(t0bs, rhss)]
    resids = [rhs - sol0 - _dot3(lm, sol0) for rhs, sol0, lm in zip(rhss, sol0s, lmats)]
    for h in heads:
        sol = sol0s[h] + jnp.dot(t0bs[h], resids[h].astype(BF16), preferred_element_type=F32)
        u_ref[:, h * DV:(h + 1) * DV] = sol[:, :DV]
        w_ref[:, h * DK:(h + 1) * DK] = sol[:, DV:].astype(BF16)


def _delta_prep(qkv_act, bg, bsz, t_len):
    c = DELTA_C
    assert t_len % c == 0
    nc = t_len // c
    mp = bsz * t_len
    n_chunks = bsz * nc
    rowspec = lambda w: pl.BlockSpec((c, w), lambda i: (i, 0))
    return pl.pallas_call(
        functools.partial(_delta_prep_body, c=c),
        grid=(n_chunks,),
        in_specs=[rowspec(3 * D_A), rowspec(LANES)],
        out_specs=[
            rowspec(D_A), rowspec(D_A), rowspec(D_A),
            pl.BlockSpec((None, H_A, c, DK), lambda i: (i, 0, 0, 0)),
            pl.BlockSpec((None, H_A, c, c), lambda i: (i, 0, 0, 0)),
            pl.BlockSpec((None, H_A, SUBLANES, LANES), lambda i: (i, 0, 0, 0)),
        ],
        out_shape=[
            jax.ShapeDtypeStruct((mp, D_A), F32),
            jax.ShapeDtypeStruct((mp, D_A), BF16),
            jax.ShapeDtypeStruct((mp, D_A), BF16),
            jax.ShapeDtypeStruct((n_chunks, H_A, c, DK), BF16),
            jax.ShapeDtypeStruct((n_chunks, H_A, c, c), BF16),
            jax.ShapeDtypeStruct((n_chunks, H_A, SUBLANES, LANES), F32),
        ],
        compiler_params=_cparams(("arbitrary",)),
        name="delta_prep",
    )(qkv_act, bg)


def _gated_rmsnorm(o, z, nw):
    on = o * lax.rsqrt(jnp.mean(o * o, axis=-1, keepdims=True) + RMS_EPS) * nw
    return on * (z * _sigmoid(z))


def _delta_seq_body(u_ref, w_ref, qe_ref, kdt_ref, aqk_ref, el_ref, z_ref, nw_ref,
                    oa_ref, sfin_ref, s_ref):
    ci = pl.program_id(1)

    @pl.when(ci == 0)
    def _():
        s_ref[...] = jnp.zeros(s_ref.shape, F32)

    nw = nw_ref[...]
    heads = range(H_A)
    sls = [slice(h * DK, (h + 1) * DK) for h in heads]
    ss = [s_ref[h] for h in heads]
    sbs = [s.astype(BF16) for s in ss]
    vbs = [(u_ref[:, sl] - jnp.dot(w_ref[:, sl], sb, preferred_element_type=F32)).astype(BF16)
           for sl, sb in zip(sls, sbs)]
    for h in heads:
        s_ref[h] = ss[h] * el_ref[h][0:1, 0:1] + lax.dot_general(
            kdt_ref[h], vbs[h], (((0,), (0,)), ((), ())), preferred_element_type=F32)
    for h in heads:
        o = (jnp.dot(qe_ref[:, sls[h]], sbs[h], preferred_element_type=F32)
             + jnp.dot(aqk_ref[h], vbs[h], preferred_element_type=F32))
        oa_ref[:, sls[h]] = _gated_rmsnorm(o, z_ref[:, sls[h]], nw).astype(BF16)

    @pl.when(ci == pl.num_programs(1) - 1)
    def _():
        sfin_ref[...] = s_ref[...]


def _delta_seq(u, w, qe, kdt, aqk, el, proj, nw_row, bsz, t_len):
    c = DELTA_C
    nc = t_len // c
    mp = bsz * t_len
    rowspec = lambda wd: pl.BlockSpec((c, wd), lambda b, i: (b * nc + i, 0))
    chunkspec = lambda a, bb: pl.BlockSpec((None, H_A, a, bb), lambda b, i: (b * nc + i, 0, 0, 0))
    return pl.pallas_call(
        _delta_seq_body,
        grid=(bsz, nc),
        in_specs=[
            rowspec(D_A), rowspec(D_A), rowspec(D_A),
            chunkspec(c, DK), chunkspec(c, c), chunkspec(SUBLANES, LANES),
            pl.BlockSpec((c, D_A), lambda b, i: (b * nc + i, COL_Z // D_A)),
            pl.BlockSpec((1, DV), lambda b, i: (0, 0)),
        ],
        out_specs=[
            rowspec(D_A),
            pl.BlockSpec((None, H_A, DK, DV), lambda b, i: (b, 0, 0, 0)),
        ],
        out_shape=[
            jax.ShapeDtypeStruct((mp, D_A), BF16),
            jax.ShapeDtypeStruct((bsz, H_A, DK, DV), F32),
        ],
        scratch_shapes=[pltpu.VMEM((H_A, DK, DV), F32)],
        compiler_params=_cparams(("arbitrary", "arbitrary")),
        name="delta_seq",
    )(u, w, qe, kdt, aqk, el, proj, nw_row)


def _sample_a_body(qkv_ref, z_ref, ba_ref, hist_ref, s0_ref, caw_ref, alog_ref, dtb_ref, nw_ref,
                   oa_ref, s1_ref, hq_ref, *, t_len, nseq):
    hdr = SUBLANES
    nh = CONV_W - 1
    rows = [slice(si * t_len, (si + 1) * t_len) for si in range(nseq)]
    for si in range(nseq):
        hq_ref[si, 0:hdr, :] = jnp.zeros((hdr, 3 * D_A), F32)
        hq_ref[si, hdr - nh:hdr, :] = hist_ref[si]
        hq_ref[si, hdr:hdr + t_len, :] = qkv_ref[rows[si], :]
    bg = _beta_g(ba_ref[...], alog_ref, dtb_ref)
    rloc = lax.rem(lax.broadcasted_iota(jnp.int32, bg.shape, 0), t_len)
    gc_all = bg
    sft = 1
    while sft < t_len:
        gc_all = gc_all + jnp.where(rloc >= sft, pltpu.roll(gc_all, sft, axis=0), 0.0)
        sft *= 2
    nw = nw_ref[...]
    rowc = lax.broadcasted_iota(jnp.int32, (t_len, 1), 0)
    pairs = [(si, h) for si in range(nseq) for h in range(H_A)]
    npair = range(len(pairs))
    act = lambda si, blk: _qkv_activation(_conv_taps(hq_ref.at[si], caw_ref, blk * DK, DK, t_len, hdr), blk)
    qs = [act(si, h) for si, h in pairs]
    ks = [act(si, H_A + h) for si, h in pairs]
    vs = [act(si, 2 * H_A + h) for si, h in pairs]
    betas = [bg[rows[si], h:h + 1] for si, h in pairs]
    gcs = [gc_all[rows[si], H_A + h:H_A + h + 1] for si, h in pairs]
    glasts = [gc[t_len - 1:t_len, :] for gc in gcs]
    kbs = [k * b for k, b in zip(ks, betas)]
    egcs = [jnp.exp(gc) for gc in gcs]
    sols = [jnp.concatenate([v * b, kb * e], axis=1) for v, b, kb, e in zip(vs, betas, kbs, egcs)]
    acols = [[] for _ in npair]
    lcols = [[] for _ in npair]
    for j in range(t_len):
        for p in npair:
            kj = ks[p][j:j + 1, :]
            dcol = jnp.exp(jnp.where(rowc >= j, gcs[p] - gcs[p][j:j + 1, :], 0.0))
            acols[p].append(jnp.where(rowc >= j, jnp.sum(qs[p] * kj, axis=-1, keepdims=True) * dcol, 0.0))
            lcols[p].append(jnp.where(rowc > j, jnp.sum(kbs[p] * kj, axis=-1, keepdims=True) * dcol, 0.0))
    for j in range(t_len - 1):
        sols = [sol - lcols[p][j] * sol[j:j + 1, :] for p, sol in enumerate(sols)]
    ss = [s0_ref[si, h] for si, h in pairs]
    sbs = [s.astype(BF16) for s in ss]
    v_news = [sol[:, :DV] - jnp.dot(sol[:, DV:].astype(BF16), sb, preferred_element_type=F32)
              for sol, sb in zip(sols, sbs)]
    os_ = [jnp.dot((q * e).astype(BF16), sb, preferred_element_type=F32) for q, e, sb in zip(qs, egcs, sbs)]
    for j in range(t_len):
        os_ = [o + acols[p][j] * v_news[p][j:j + 1, :] for p, o in enumerate(os_)]
    upds = [lax.dot_general((k * jnp.exp(gl - gc)).astype(BF16), vn.astype(BF16), (((0,), (0,)), ((), ())),
                            preferred_element_type=F32)
            for k, gl, gc, vn in zip(ks, glasts, gcs, v_news)]
    for p, (si, h) in enumerate(pairs):
        s1_ref[si, h] = ss[p] * jnp.exp(glasts[p]) + upds[p]
        sl = slice(h * DV, (h + 1) * DV)
        oa_ref[rows[si], sl] = _gated_rmsnorm(os_[p], z_ref[rows[si], sl], nw).astype(BF16)


def _sample_a(proj, row0, bsz, t_len, hist, s0, caw, alog_row, dtb_row, nw_row):
    nseq = SAMPLE_NSEQ
    tr = nseq * t_len
    assert t_len == SUBLANES and row0 % tr == 0 and bsz % nseq == 0
    r0 = row0 // tr
    wspec = lambda shape: pl.BlockSpec(shape, lambda b: (0,) * len(shape))
    return pl.pallas_call(
        functools.partial(_sample_a_body, t_len=t_len, nseq=nseq),
        grid=(bsz // nseq,),
        in_specs=[
            pl.BlockSpec((tr, 3 * D_A), lambda b: (r0 + b, COL_QKV // (3 * D_A))),
            pl.BlockSpec((tr, D_A), lambda b: (r0 + b, COL_Z // D_A)),
            pl.BlockSpec((tr, LANES), lambda b: (r0 + b, COL_BA // LANES)),
            pl.BlockSpec((nseq, CONV_W - 1, 3 * D_A), lambda b: (b, 0, 0)),
            pl.BlockSpec((nseq, H_A, DK, DV), lambda b: (b, 0, 0, 0)),
            wspec((CONV_W, 3 * D_A)), wspec((1, LANES)), wspec((1, LANES)), wspec((1, DV)),
        ],
        out_specs=[
            pl.BlockSpec((tr, D_A), lambda b: (b, 0)),
            pl.BlockSpec((nseq, H_A, DK, DV), lambda b: (b, 0, 0, 0)),
        ],
        out_shape=[
            jax.ShapeDtypeStruct((bsz * t_len, D_A), BF16),
            jax.ShapeDtypeStruct((bsz, H_A, DK, DV), F32),
        ],
        scratch_shapes=[pltpu.VMEM((nseq, 2 * SUBLANES, 3 * D_A), F32)],
        compiler_params=_cparams(("arbitrary",)),
        name="sample_a",
    )(proj, proj, proj, hist, s0, caw, alog_row, dtb_row, nw_row)


def _lru_coeffs(xc, wa_ref, wx_ref, lba_ref, lbx_ref, lam_ref):
    parts = []
    for n in range(NB):
        sl = slice(n * BW, (n + 1) * BW)
        xn = xc[:, sl]
        xnb = xn.astype(BF16)
        gr = _sigmoid(jnp.dot(xnb, wa_ref[n].astype(BF16), preferred_element_type=F32) + lba_ref[:, sl])
        gi = _sigmoid(jnp.dot(xnb, wx_ref[n].astype(BF16), preferred_element_type=F32) + lbx_ref[:, sl])
        log_a = -LRU_C * gr * _softplus(-lam_ref[:, sl])
        parts.append((jnp.exp(log_a), jnp.sqrt(1.0 - jnp.exp(2.0 * log_a)), gi * xn))
    return parts


def _group_scan(a, b):
    rowi = lax.broadcasted_iota(jnp.int32, a.shape, 0)
    sft = 1
    while sft < SUBLANES:
        keep = rowi >= sft
        b = b + a * jnp.where(keep, pltpu.roll(b, sft, axis=0), 0.0)
        a = a * jnp.where(keep, pltpu.roll(a, sft, axis=0), 1.0)
        sft *= 2
    return a, b


def _gelu_tanh(x):
    return 0.5 * x * (1.0 + jnp.tanh(math.sqrt(2.0 / math.pi) * (x + 0.044715 * x * x * x)))


def _lru_finish(h, yb, nbw):
    hg = h * _gelu_tanh(yb)
    return (hg * lax.rsqrt(jnp.mean(hg * hg, axis=-1, keepdims=True) + RMS_EPS) * nbw).astype(BF16)


def _lru_prompt_body(xc_ref, yb_ref, wa_ref, wx_ref, lba_ref, lbx_ref, lam_ref, nbw_ref,
                     ob_ref, hfin_ref, a_ref, b_ref, h_ref, carry_ref, *, rows):
    t = pl.program_id(1)
    parts = _lru_coeffs(xc_ref[...], wa_ref, wx_ref, lba_ref, lbx_ref, lam_ref)
    rowi = lax.broadcasted_iota(jnp.int32, (rows, BW), 0)
    first = jnp.logical_and(rowi == 0, t == 0)
    for n in range(NB):
        sl = slice(n * BW, (n + 1) * BW)
        a_n, mult_n, gix_n = parts[n]
        a_ref[:, sl] = jnp.where(first, 0.0, a_n)
        b_ref[:, sl] = jnp.where(first, 1.0, mult_n) * gix_n

    @pl.when(t == 0)
    def _():
        carry_ref[...] = jnp.zeros(carry_ref.shape, F32)

    def group(gidx, carry):
        r0 = pl.multiple_of(gidx * SUBLANES, SUBLANES)
        ag, bg = _group_scan(a_ref[pl.ds(r0, SUBLANES), :], b_ref[pl.ds(r0, SUBLANES), :])
        hg = ag * carry + bg
        h_ref[pl.ds(r0, SUBLANES), :] = hg
        return jnp.broadcast_to(hg[SUBLANES - 1:SUBLANES, :], hg.shape)

    carry = lax.fori_loop(0, rows // SUBLANES, group, carry_ref[...])
    carry_ref[...] = carry
    ob_ref[...] = _lru_finish(h_ref[...], yb_ref[...], nbw_ref[...])

    @pl.when(t == pl.num_programs(1) - 1)
    def _():
        hfin_ref[...] = carry[0:1, :]


def _lru_prompt(xbc, proj, bsz, t_len, wa, wx, lba, lbx, lam, nbw):
    rows = min(CONV_L, t_len)
    nt = t_len // rows
    mp = bsz * t_len
    wspec = lambda shape: pl.BlockSpec(shape, lambda b, t: (0,) * len(shape))
    return pl.pallas_call(
        functools.partial(_lru_prompt_body, rows=rows),
        grid=(bsz, nt),
        in_specs=[
            pl.BlockSpec((rows, D_B), lambda b, t: (b * nt + t, 0)),
            pl.BlockSpec((rows, D_B), lambda b, t: (b * nt + t, COL_YB // D_B)),
            wspec((NB, BW, BW)), wspec((NB, BW, BW)),
            wspec((1, D_B)), wspec((1, D_B)), wspec((1, D_B)), wspec((1, D_B)),
        ],
        out_specs=[
            pl.BlockSpec((rows, D_B), lambda b, t: (b * nt + t, 0)),
            pl.BlockSpec((None, 1, D_B), lambda b, t: (b, 0, 0)),
        ],
        out_shape=[
            jax.ShapeDtypeStruct((mp, D_B), BF16),
            jax.ShapeDtypeStruct((bsz, 1, D_B), F32),
        ],
        scratch_shapes=[pltpu.VMEM((rows, D_B), F32), pltpu.VMEM((rows, D_B), F32),
                        pltpu.VMEM((rows, D_B), F32), pltpu.VMEM((SUBLANES, D_B), F32)],
        compiler_params=_cparams(("arbitrary", "arbitrary")),
        name="lru_prompt",
    )(xbc, proj, wa, wx, lba, lbx, lam, nbw)


def _lru_sample_body(xb_ref, yb_ref, hist_ref, h0_ref, cbw_ref, cbb_ref, wa_ref, wx_ref, lba_ref,
                     lbx_ref, lam_ref, nbw_ref, ob_ref, h1_ref, hx_ref, a_ref, b_ref, h_ref,
                     *, nseq, t_len):
    hdr = SUBLANES
    nh = CONV_W - 1
    hx_ref[0:hdr, :] = jnp.zeros((hdr, D_B), F32)

    def conv_seq(si, _):
        r0 = pl.multiple_of(si * t_len, t_len)
        hx_ref[hdr - nh:hdr, :] = hist_ref[si]
        hx_ref[hdr:hdr + t_len, :] = xb_ref[pl.ds(r0, t_len), :]
        for n in range(NB):
            c0 = n * BW
            h_ref[pl.ds(r0, t_len), c0:c0 + BW] = (_conv_taps(hx_ref, cbw_ref, c0, BW, t_len, hdr)
                                                  + cbb_ref[:, c0:c0 + BW])
        return 0

    lax.fori_loop(0, nseq, conv_seq, 0)
    parts = _lru_coeffs(h_ref[...], wa_ref, wx_ref, lba_ref, lbx_ref, lam_ref)
    for n in range(NB):
        sl = slice(n * BW, (n + 1) * BW)
        a_n, mult_n, gix_n = parts[n]
        a_ref[:, sl] = a_n
        b_ref[:, sl] = mult_n * gix_n

    def seq(si, _):
        r0 = pl.multiple_of(si * t_len, t_len)
        ag, bg = _group_scan(a_ref[pl.ds(r0, t_len), :], b_ref[pl.ds(r0, t_len), :])
        hg = ag * h0_ref[pl.ds(si, 1), :] + bg
        h_ref[pl.ds(r0, t_len), :] = hg
        h1_ref[pl.ds(si, 1), :] = hg[t_len - 1:t_len, :]
        return 0

    lax.fori_loop(0, nseq, seq, 0)
    ob_ref[...] = _lru_finish(h_ref[...], yb_ref[...], nbw_ref[...])


def _lru_sample(proj, row0, bsz, t_len, hist, h0, cbw, cbb, wa, wx, lba, lbx, lam, nbw):
    assert t_len == SUBLANES
    ms = bsz * t_len
    assert row0 % ms == 0
    rblk = row0 // ms
    wspec = lambda shape: pl.BlockSpec(shape, lambda i: (0,) * len(shape))
    return pl.pallas_call(
        functools.partial(_lru_sample_body, nseq=bsz, t_len=t_len),
        grid=(1,),
        in_specs=[
            pl.BlockSpec((ms, D_B), lambda i: (rblk, COL_XB // D_B)),
            pl.BlockSpec((ms, D_B), lambda i: (rblk, COL_YB // D_B)),
            wspec((bsz, CONV_W - 1, D_B)), wspec((bsz, D_B)),
            wspec((CONV_W, D_B)), wspec((1, D_B)),
            wspec((NB, BW, BW)), wspec((NB, BW, BW)),
            wspec((1, D_B)), wspec((1, D_B)), wspec((1, D_B)), wspec((1, D_B)),
        ],
        out_specs=[wspec((ms, D_B)), wspec((bsz, D_B))],
        out_shape=[
            jax.ShapeDtypeStruct((ms, D_B), BF16),
            jax.ShapeDtypeStruct((bsz, D_B), F32),
        ],
        scratch_shapes=[pltpu.VMEM((2 * SUBLANES, D_B), F32), pltpu.VMEM((ms, D_B), F32),
                        pltpu.VMEM((ms, D_B), F32), pltpu.VMEM((ms, D_B), F32)],
        compiler_params=_cparams(("arbitrary",)),
        name="lru_sample",
    )(proj, proj, hist, h0, cbw, cbb, wa, wx, lba, lbx, lam, nbw)


def _layernorm_rows(v, g, b):
    mu = jnp.mean(v, axis=-1, keepdims=True)
    d = v - mu
    var = jnp.mean(d * d, axis=-1, keepdims=True)
    return d * lax.rsqrt(var + LN_EPS) * g + b


def _store_slabs(slab_ref, base, val):
    n = val.shape[0]
    for s in range(ROW_SLAB):
        slab_ref[pl.ds(base + s, n, stride=ROW_SLAB), :] = val[:, s * LANES:(s + 1) * LANES]


def _load_slab_chunk(slab_ref, base, n, s):
    return slab_ref[pl.ds(base + s, n, stride=ROW_SLAB), :]


def _outproj_body(oap_ref, oas_ref, obp_ref, obs_ref, xp_ref, xs_ref, wo_ref, g_ref, b_ref,
                  wr_ref, br_ref, x1_ref, ti_ref, tg_ref, *, n_ptiles):
    i = pl.program_id(0)

    def run(oa_ref, ob_ref, x_ref):
        mix = (jnp.dot(oa_ref[...], wo_ref[0:D_A, :], preferred_element_type=F32)
               + jnp.dot(ob_ref[...], wo_ref[D_A:D_A + D_B, :], preferred_element_type=F32))
        y = _layernorm_rows(DN_ALPHA * x_ref[...] + mix, g_ref[...], b_ref[...])
        _store_slabs(x1_ref, 0, y)
        logits = _dot3(y, wr_ref[...]) + br_ref[...]
        lane = lax.broadcasted_iota(jnp.int32, logits.shape, 1)
        lane_o = lax.broadcasted_iota(jnp.int32, ti_ref.shape, 1)
        cur = logits
        ti = jnp.zeros(ti_ref.shape, jnp.int32)
        tv = jnp.zeros(tg_ref.shape, F32)
        v0 = None
        den = None
        for kk in range(TOP_K):
            m = jnp.max(cur, axis=-1, keepdims=True)
            idx = jnp.min(jnp.where(cur == m, lane, N_EXPERTS), axis=-1, keepdims=True)
            cur = jnp.where(lane == idx, -jnp.inf, cur)
            if kk == 0:
                v0 = m
            e = jnp.exp(m - v0)
            den = e if den is None else den + e
            ti = jnp.where(lane_o == kk, idx, ti)
            tv = jnp.where(lane_o == kk, e, tv)
        ti_ref[...] = ti
        tg_ref[...] = tv / den

    @pl.when(i < n_ptiles)
    def _():
        run(oap_ref, obp_ref, xp_ref)

    @pl.when(i >= n_ptiles)
    def _():
        run(oas_ref, obs_ref, xs_ref)


def _outproj(oa_p, oa_s, ob_p, ob_s, xp2, xs2, wo, g_row, b_row, wr, br_row):
    mp, ms = xp2.shape[0], xs2.shape[0]
    tm = min(OUT_TM, ms)
    assert mp % tm == 0 and ms % tm == 0
    n_pt, n_st = mp // tm, ms // tm
    pmap = lambda i: (jnp.minimum(i, n_pt - 1), 0)
    smap = lambda i: (jnp.maximum(i - n_pt, 0), 0)
    wspec = lambda shape: pl.BlockSpec(shape, lambda i: (0, 0))
    m = mp + ms
    return pl.pallas_call(
        functools.partial(_outproj_body, n_ptiles=n_pt),
        grid=(n_pt + n_st,),
        in_specs=[
            pl.BlockSpec((tm, D_A), pmap), pl.BlockSpec((tm, D_A), smap),
            pl.BlockSpec((tm, D_B), pmap), pl.BlockSpec((tm, D_B), smap),
            pl.BlockSpec((tm, D_MODEL), pmap), pl.BlockSpec((tm, D_MODEL), smap),
            wspec((D_A + D_B, D_MODEL)), wspec((1, D_MODEL)), wspec((1, D_MODEL)),
            wspec((D_MODEL, N_EXPERTS)), wspec((1, N_EXPERTS)),
        ],
        out_specs=[
            pl.BlockSpec((tm * ROW_SLAB, LANES), lambda i: (i, 0)),
            pl.BlockSpec((tm, LANES), lambda i: (i, 0)),
            pl.BlockSpec((tm, LANES), lambda i: (i, 0)),
        ],
        out_shape=[
            jax.ShapeDtypeStruct((m * ROW_SLAB, LANES), F32),
            jax.ShapeDtypeStruct((m, LANES), jnp.int32),
            jax.ShapeDtypeStruct((m, LANES), F32),
        ],
        compiler_params=_cparams(("arbitrary",)),
        name="outproj_ln_router",
    )(oa_p, oa_s, ob_p, ob_s, xp2, xs2, wo, g_row, b_row, wr, br_row)


def _zero_tail(zbuf_ref, dst_ref, tail_row, n_blocks, sem):
    scale = ROW_SLAB if len(dst_ref.shape) == 2 else 1

    def tail_copy(t):
        z0 = pl.multiple_of((tail_row + t * MOE_PB) * scale, MOE_PB * scale)
        return pltpu.make_async_copy(zbuf_ref, dst_ref.at[pl.ds(z0, MOE_PB * scale)], sem)

    def start(t, _):
        tail_copy(t).start()
        return 0

    def wait(t, _):
        tail_copy(t).wait()
        return 0

    lax.fori_loop(0, n_blocks, start, 0)
    lax.fori_loop(0, n_blocks, wait, 0)


def _dispatch_body(zrow_ref, slot_ref, x_ref, xs_ref, zbuf_ref, zsem, rsem, *, tm):
    i = pl.program_id(0)

    @pl.when(i == 0)
    def _():
        zbuf_ref[...] = jnp.zeros(zbuf_ref.shape, F32)

        def zero_copy(e):
            z0 = pl.multiple_of(zrow_ref[e], MOE_PB)
            return pltpu.make_async_copy(zbuf_ref, xs_ref.at[pl.ds(z0, MOE_PB)], zsem)

        for e in range(N_EXPERTS):
            @pl.when(zrow_ref[e] >= 0)
            def _():
                zero_copy(e).start()
        for e in range(N_EXPERTS):
            @pl.when(zrow_ref[e] >= 0)
            def _():
                zero_copy(e).wait()
        _zero_tail(zbuf_ref, xs_ref, zrow_ref[N_EXPERTS], zrow_ref[N_EXPERTS + 1], zsem)

    def row_copy(r, dst):
        src0 = pl.multiple_of(r * ROW_SLAB, ROW_SLAB)
        return pltpu.make_async_copy(x_ref.at[pl.ds(src0, ROW_SLAB)], xs_ref.at[dst], rsem)

    def issue(r, _):
        for kk in range(TOP_K):
            row_copy(r, slot_ref[0, r * TOP_K + kk]).start(priority=kk % 2)
        return 0

    lax.fori_loop(0, tm, issue, 0, unroll=8)

    for kk in range(TOP_K):
        pltpu.make_async_copy(xs_ref.at[pl.ds(0, tm)], xs_ref.at[pl.ds(0, tm)], rsem).wait()


def _dispatch(x1, slot_tiles, zrow, n_rows):
    m = x1.shape[0] // ROW_SLAB
    tm = TOK_TM
    assert m % tm == 0
    return pl.pallas_call(
        functools.partial(_dispatch_body, tm=tm),
        grid_spec=pltpu.PrefetchScalarGridSpec(
            num_scalar_prefetch=1,
            grid=(m // tm,),
            in_specs=[
                pl.BlockSpec((None, 1, tm * TOP_K), lambda i, z: (i, 0, 0), memory_space=pltpu.SMEM),
                pl.BlockSpec((tm * ROW_SLAB, LANES), lambda i, z: (i, 0)),
            ],
            out_specs=pl.BlockSpec(memory_space=pl.ANY),
            scratch_shapes=[pltpu.VMEM((MOE_PB, ROW_SLAB, LANES), F32),
                            pltpu.SemaphoreType.DMA(()), pltpu.SemaphoreType.DMA(())],
        ),
        out_shape=jax.ShapeDtypeStruct((n_rows, ROW_SLAB, LANES), F32),
        compiler_params=_cparams(("arbitrary",)),
        name="moe_dispatch",
    )(zrow, slot_tiles, x1)


def _for_row_blocks(npb, fn):
    for k in range(1, MOE_SB + 1):
        odd = 2 * k - 1
        hit = npb == 2 * k
        if odd not in MOE_EXACT_ODD:
            hit = jnp.logical_or(hit, npb == odd)

        @pl.when(hit)
        def _():
            fn(0, k * MOE_RB)
    for q in MOE_EXACT_ODD:
        @pl.when(npb == q)
        def _():
            fn(0, q * MOE_PB)


def _moe_body(ie_ref, irow_ref, insub_ref, tail_ref, xs_ref, wg_ref, wu_ref, bup_ref,
              wd_ref, bd_ref, ys_ref, xstage_ref, xb_ref, act_ref, ybuf_ref, wgb_ref, wub_ref, wdb_ref,
              isem, osem):
    wi = pl.program_id(0)
    j = pl.program_id(1)
    n_w = pl.num_programs(0)
    to_sub = lambda n: lax.shift_right_logical(n + (MOE_RB // MOE_PB - 1), MOE_RB // MOE_PB - 1)
    npb = insub_ref[wi]
    nsub = to_sub(npb)
    row0 = irow_ref[wi]
    nt = MOE_NT
    is_last = wi == n_w - 1
    nxt = jnp.minimum(wi + 1, n_w - 1)
    nsub_next = to_sub(jnp.where(is_last, 0, insub_ref[nxt]))
    row_next = irow_ref[nxt]
    prv = jnp.maximum(wi - 1, 0)
    npb_prev = jnp.where(wi == 0, 0, insub_ref[prv])
    row_prev = irow_ref[prv]

    class _XCopy:
        def __init__(self, row_base, s, slot):
            g0 = pl.multiple_of(row_base + s * MOE_RB, MOE_PB)
            self.copies = [
                pltpu.make_async_copy(xs_ref.at[pl.ds(g0, MOE_RB), c],
                                      xstage_ref.at[slot, :, pl.ds(c * LANES, LANES)], isem.at[slot])
                for c in range(ROW_SLAB)]

        def start(self):
            for cp in self.copies:
                cp.start()

        def wait(self):
            for cp in self.copies:
                cp.wait()

    x_copy = _XCopy

    def to_matmul_layout(s, slot):
        r0 = pl.multiple_of(s * MOE_RB, MOE_RB)
        xb_ref[pl.ds(r0, MOE_RB), :] = xstage_ref[slot].astype(BF16)

    def y_copy(row_base, s):
        b0 = pl.multiple_of(s * MOE_PB * ROW_SLAB, MOE_PB * ROW_SLAB)
        g0 = pl.multiple_of(row_base * ROW_SLAB, MOE_PB * ROW_SLAB) + b0
        return pltpu.make_async_copy(ybuf_ref.at[pl.ds(b0, MOE_PB * ROW_SLAB)],
                                     ys_ref.at[pl.ds(g0, MOE_PB * ROW_SLAB)], osem)

    @pl.when(jnp.logical_and(wi == 0, j == 0))
    def _():
        def load(s, _):
            cp = x_copy(row0, s, 0)
            cp.start()
            cp.wait()
            to_matmul_layout(s, 0)
            return 0

        lax.fori_loop(0, nsub, load, 0)

    @pl.when(jnp.logical_and(j == nt - 1, nsub_next > 0))
    def _():
        x_copy(row_next, 0, 0).start()

    @pl.when(jnp.logical_and(j >= nt, j - nt < nsub_next))
    def _():
        s = j - nt
        slot = lax.rem(s, 2)
        x_copy(row_next, s, slot).wait()

        @pl.when(s + 1 < nsub_next)
        def _():
            x_copy(row_next, s + 1, 1 - slot).start()

        to_matmul_layout(s, slot)

    @pl.when(jnp.logical_and(j == nt, npb_prev > 0))
    def _():
        def drain(s, _):
            y_copy(row_prev, s).wait()
            return 0

        lax.fori_loop(0, npb_prev, drain, 0)

    @pl.when(npb > 0)
    def _():
        @pl.when(j < nt)
        def _():
            wgb_ref[...] = wg_ref[...].astype(BF16)
            wub_ref[...] = wu_ref[...].astype(BF16)
            b_gate = bup_ref[ie_ref[wi], pl.ds(j, 1), :]
            b_up = bup_ref[ie_ref[wi], pl.ds(nt + j, 1), :]

            def up_rows(r0, nrows):
                x = xb_ref[pl.ds(r0, nrows), :]
                hg = jnp.dot(x, wgb_ref[...], preferred_element_type=F32) + b_gate
                hu = jnp.dot(x, wub_ref[...], preferred_element_type=F32) + b_up
                gate = jnp.minimum(hg, SWIGLU_LIMIT)
                up = jnp.clip(hu, -SWIGLU_LIMIT, SWIGLU_LIMIT)
                glu = gate * _sigmoid(SWIGLU_ALPHA * gate)
                act_ref[j, pl.ds(r0, nrows), :] = (glu * (up + 1.0)).astype(BF16)

            _for_row_blocks(npb, up_rows)

        @pl.when(j >= nt)
        def _():
            wdb_ref[...] = wd_ref[...].astype(BF16)
            n = j - nt
            b_down = bd_ref[ie_ref[wi], pl.ds(n, 1), :]

            def down_rows(r0, nrows):
                acc = jnp.dot(act_ref[0, pl.ds(r0, nrows), :], wdb_ref[0:MOE_TF, :],
                              preferred_element_type=F32)
                for jj in range(1, nt):
                    acc = acc + jnp.dot(act_ref[jj, pl.ds(r0, nrows), :],
                                        wdb_ref[jj * MOE_TF:(jj + 1) * MOE_TF, :],
                                        preferred_element_type=F32)
                acc = acc + b_down
                cpt = MOE_TF // LANES
                for cc in range(cpt):
                    ybuf_ref[pl.ds(r0 * ROW_SLAB + n * cpt + cc, nrows, stride=ROW_SLAB), :] = (
                        acc[:, cc * LANES:(cc + 1) * LANES])

            _for_row_blocks(npb, down_rows)

        @pl.when(j == 2 * nt - 1)
        def _():
            def store(s, _):
                y_copy(row0, s).start()
                return 0

            lax.fori_loop(0, npb, store, 0)

    @pl.when(jnp.logical_and(is_last, j == 2 * nt - 1))
    def _():
        def drain(s, _):
            y_copy(row0, s).wait()
            return 0

        lax.fori_loop(0, npb, drain, 0)
        zsrc = ybuf_ref.at[pl.ds(0, MOE_PB * ROW_SLAB)]
        zsrc[...] = jnp.zeros(zsrc.shape, F32)
        _zero_tail(zsrc, ys_ref, tail_ref[0], tail_ref[1], osem)


def _moe(xs, n_items, item_e, item_row, item_nsub, tail, w_up, b_up3, w_down, b_down3):
    n_rows = xs.shape[0] * ROW_SLAB
    nt = MOE_NT
    assert MOE_SB <= nt
    up_off = D_FF // MOE_TF
    p1 = lambda j, n, w: jnp.where(n[w] > 0, jnp.minimum(j, nt - 1), nt - 1)
    p2 = lambda j, n, w: jnp.where(n[w] > 0, jnp.maximum(j - nt, 0), nt - 1)
    return pl.pallas_call(
        _moe_body,
        grid_spec=pltpu.PrefetchScalarGridSpec(
            num_scalar_prefetch=4,
            grid=(n_items, 2 * nt),
            in_specs=[
                pl.BlockSpec(memory_space=pl.ANY),
                pl.BlockSpec((None, D_MODEL, MOE_TF), lambda w, j, e, r, n, t: (e[w], 0, p1(j, n, w))),
                pl.BlockSpec((None, D_MODEL, MOE_TF),
                             lambda w, j, e, r, n, t: (e[w], 0, up_off + p1(j, n, w))),
                pl.BlockSpec((N_EXPERTS, 2 * nt, MOE_TF), lambda w, j, e, r, n, t: (0, 0, 0)),
                pl.BlockSpec((None, D_FF, MOE_TF), lambda w, j, e, r, n, t: (e[w], 0, p2(j, n, w))),
                pl.BlockSpec((N_EXPERTS, nt, MOE_TF), lambda w, j, e, r, n, t: (0, 0, 0)),
            ],
            out_specs=pl.BlockSpec(memory_space=pl.ANY),
            scratch_shapes=[
                pltpu.VMEM((2, MOE_RB, D_MODEL), F32),
                pltpu.VMEM((MOE_R, D_MODEL), BF16),
                pltpu.VMEM((nt, MOE_R, MOE_TF), BF16),
                pltpu.VMEM((MOE_R * ROW_SLAB, LANES), F32),
                pltpu.VMEM((D_MODEL, MOE_TF), BF16),
                pltpu.VMEM((D_MODEL, MOE_TF), BF16),
                pltpu.VMEM((D_FF, MOE_TF), BF16),
                pltpu.SemaphoreType.DMA((2,)), pltpu.SemaphoreType.DMA(()),
            ],
        ),
        out_shape=jax.ShapeDtypeStruct((n_rows, LANES), F32),
        compiler_params=_cparams(("arbitrary", "arbitrary")),
        name="moe_experts",
    )(item_e, item_row, item_nsub, tail, xs, w_up, w_up, b_up3, w_down, b_down3)


def _combine_body(slot_ref, slotn_ref, ys_ref, x1_ref, gate_ref, g_ref, b_ref, yp_ref, ysm_ref, buf_ref,
                  v_ref, sem, *, tm, n_ptiles):
    i = pl.program_id(0)
    cur = lax.rem(i, 2)

    def gather(table_ref, bslot):
        def issue(r, _):
            for kk in range(TOP_K):
                src0 = pl.multiple_of(table_ref[0, r * TOP_K + kk] * ROW_SLAB, ROW_SLAB)
                dst0 = pl.multiple_of(r * ROW_SLAB, ROW_SLAB)
                pltpu.make_async_copy(ys_ref.at[pl.ds(src0, ROW_SLAB)],
                                      buf_ref.at[bslot, kk, pl.ds(dst0, ROW_SLAB)],
                                      sem.at[bslot]).start(priority=kk % 2)
            return 0

        lax.fori_loop(0, tm, issue, 0, unroll=8)

    @pl.when(i == 0)
    def _():
        gather(slot_ref, 0)

    @pl.when(i + 1 < pl.num_programs(0))
    def _():
        gather(slotn_ref, 1 - cur)

    for kk in range(TOP_K):
        pltpu.make_async_copy(ys_ref.at[pl.ds(0, tm * ROW_SLAB)], buf_ref.at[cur, kk], sem.at[cur]).wait()
    gates = gate_ref[...]
    for c in range(ROW_SLAB):
        acc = DN_ALPHA * _load_slab_chunk(x1_ref, 0, tm, c)
        for kk in range(TOP_K):
            acc = acc + gates[:, kk:kk + 1] * _load_slab_chunk(buf_ref.at[cur, kk], 0, tm, c)
        v_ref[:, c * LANES:(c + 1) * LANES] = acc
    y = _layernorm_rows(v_ref[...], g_ref[...], b_ref[...])

    @pl.when(pl.program_id(0) < n_ptiles)
    def _():
        yp_ref[...] = y

    @pl.when(pl.program_id(0) >= n_ptiles)
    def _():
        ysm_ref[...] = y


def _combine(ys, slot_tiles, x1, gates, g_row, b_row, mp):
    m = x1.shape[0] // ROW_SLAB
    tm = TOK_TM
    assert mp % tm == 0 and (m - mp) % tm == 0
    n_pt = mp // tm
    wspec = lambda shape: pl.BlockSpec(shape, lambda i: (0, 0))
    n_t = m // tm
    return pl.pallas_call(
        functools.partial(_combine_body, tm=tm, n_ptiles=n_pt),
        grid=(n_t,),
        in_specs=[
            pl.BlockSpec((None, 1, tm * TOP_K), lambda i: (i, 0, 0), memory_space=pltpu.SMEM),
            pl.BlockSpec((None, 1, tm * TOP_K), lambda i: (jnp.minimum(i + 1, n_t - 1), 0, 0),
                         memory_space=pltpu.SMEM),
            pl.BlockSpec(memory_space=pl.ANY),
            pl.BlockSpec((tm * ROW_SLAB, LANES), lambda i: (i, 0)),
            pl.BlockSpec((tm, LANES), lambda i: (i, 0)),
            wspec((1, D_MODEL)), wspec((1, D_MODEL)),
        ],
        out_specs=[pl.BlockSpec((tm, D_MODEL), lambda i: (jnp.minimum(i, n_pt - 1), 0)),
                   pl.BlockSpec((tm, D_MODEL), lambda i: (jnp.maximum(i - n_pt, 0), 0))],
        out_shape=[jax.ShapeDtypeStruct((mp, D_MODEL), F32),
                   jax.ShapeDtypeStruct((m - mp, D_MODEL), F32)],
        scratch_shapes=[pltpu.VMEM((2, TOP_K, tm * ROW_SLAB, LANES), F32),
                        pltpu.VMEM((tm, D_MODEL), F32), pltpu.SemaphoreType.DMA((2,))],
        compiler_params=_cparams(("arbitrary",)),
        name="moe_combine_ln",
    )(slot_tiles, slot_tiles, ys, x1, gates, g_row, b_row)


def _routing_tables(top_i, m):
    e_ids = jnp.arange(N_EXPERTS, dtype=jnp.int32)
    onehot = (top_i[:, :, None] == e_ids[None, None, :]).astype(jnp.int32)
    mask = jnp.sum(onehot, axis=1)
    incl = jnp.cumsum(mask, axis=0)
    pos = incl - mask
    counts = incl[-1]
    npb = (counts + MOE_PB - 1) // MOE_PB
    padded = npb * MOE_PB
    gend = jnp.cumsum(padded)
    gstart = gend - padded
    slot = jnp.sum(onehot * (gstart[None, None, :] + pos[:, None, :]), axis=2)
    n_rows = (m * TOP_K // MOE_PB + N_EXPERTS) * MOE_PB + MOE_PB
    tail = jnp.stack([gend[-1], (n_rows - gend[-1]) // MOE_PB]).astype(jnp.int32)
    zrow = jnp.concatenate([jnp.where(counts > 0, gend - MOE_PB, -1).astype(jnp.int32), tail])
    n_items = N_EXPERTS + (m * TOP_K // MOE_PB + N_EXPERTS) // MOE_NPB
    ipe = (npb + MOE_NPB - 1) // MOE_NPB
    iend = jnp.cumsum(ipe)
    istart = iend - ipe
    total = iend[-1]
    wid = jnp.arange(n_items, dtype=jnp.int32)
    wclamp = jnp.minimum(wid, total - 1)
    ie = jnp.minimum(jnp.searchsorted(iend, wclamp, side="right"), N_EXPERTS - 1).astype(jnp.int32)
    jn = wclamp - istart[ie]
    irow = (gstart[ie] + jn * MOE_R).astype(jnp.int32)
    insub = jnp.where(wid < total, jnp.clip(npb[ie] - jn * MOE_NPB, 0, MOE_NPB), 0).astype(jnp.int32)
    return slot.astype(jnp.int32), zrow, tail, total.astype(jnp.int32), ie, irow, insub, n_rows


def kernel(x_prompt, x_sample, state_conv_a, state_delta, state_conv_b, state_lru, w_in, conv_a_w,
           a_log, dt_bias, norm_a_w, conv_b_w, conv_b_b, lru_wa, lru_ba, lru_wx, lru_bx, lru_lambda,
           norm_b_w, w_out, ln1_g, ln1_b, w_router, b_router, w_up, b_up, w_down, b_down, ln2_g, ln2_b):
    bp, tp, _ = x_prompt.shape
    bs, ts, _ = x_sample.shape
    mp, ms = bp * tp, bs * ts
    m = mp + ms
    l = 0
    xp2 = x_prompt.reshape(mp, D_MODEL)
    xs2 = x_sample.reshape(ms, D_MODEL)

    wp = _wprep(jnp.transpose(w_in[l]))
    wo = w_out[l].astype(BF16)
    pad_h = lambda v: jnp.zeros((1, LANES), F32).at[0, H_A:2 * H_A].set(v)
    alog_row = pad_h(a_log[l])
    dtb_row = pad_h(dt_bias[l])
    row = lambda v: v.reshape(1, -1)

    proj = _inproj(xp2, xs2, wp)

    qkv_act, xbc, bg = _convact_prompt(proj, bp, tp, conv_a_w[l], conv_b_w[l], row(conv_b_b[l]),
                                       alog_row, dtb_row)
    u, w, qe, kdt, aqk, el = _delta_prep(qkv_act, bg, bp, tp)
    oa_p, sd_p = _delta_seq(u, w, qe, kdt, aqk, el, proj, row(norm_a_w[l]), bp, tp)
    ob_p, h_p = _lru_prompt(xbc, proj, bp, tp, lru_wa[l], lru_wx[l], row(lru_ba[l]), row(lru_bx[l]),
                            row(lru_lambda[l]), row(norm_b_w[l]))
    oa_s, sd_s = _sample_a(proj, mp, bs, ts, state_conv_a[l], state_delta[l], conv_a_w[l],
                           alog_row, dtb_row, row(norm_a_w[l]))
    ob_s, h_s = _lru_sample(proj, mp, bs, ts, state_conv_b[l], state_lru[l], conv_b_w[l],
                            row(conv_b_b[l]), lru_wa[l], lru_wx[l], row(lru_ba[l]), row(lru_bx[l]),
                            row(lru_lambda[l]), row(norm_b_w[l]))

    x1, ti, tg = _outproj(oa_p, oa_s, ob_p, ob_s, xp2, xs2, wo, row(ln1_g[l]), row(ln1_b[l]),
                          w_router[l], row(b_router[l]))

    slot, zrow, tail, n_used, ie, irow, insub, n_rows = _routing_tables(ti[:, :TOP_K], m)
    slot_tiles = slot.reshape(m // TOK_TM, 1, TOK_TM * TOP_K)
    xs_sorted = _dispatch(x1, slot_tiles, zrow, n_rows)
    ys = _moe(xs_sorted, n_used, ie, irow, insub, tail, w_up[l],
              b_up[l].reshape(N_EXPERTS, 2 * MOE_NT, MOE_TF), w_down[l],
              b_down[l].reshape(N_EXPERTS, MOE_NT, MOE_TF))
    y_p, y_s = _combine(ys, slot_tiles, x1, tg, row(ln2_g[l]), row(ln2_b[l]), mp)

    y_prompt = y_p.reshape(bp, tp, D_MODEL)
    y_sample = y_s.reshape(bs, ts, D_MODEL)
    nh = CONV_W - 1
    assert tp % SUBLANES == 0 and ts == SUBLANES and nh <= SUBLANES
    pg = proj.reshape(m // SUBLANES, SUBLANES, N_PROJ)
    gp = tp // SUBLANES

    def last_rows(g0, g1, gstep, c0, width):
        return lax.slice(pg, (g0, SUBLANES - nh, c0), (g1, SUBLANES, c0 + width), (gstep, 1, 1))

    ca_p = last_rows(gp - 1, mp // SUBLANES, gp, COL_QKV, 3 * D_A)
    cb_p = last_rows(gp - 1, mp // SUBLANES, gp, COL_XB, D_B)
    ca_s = last_rows(mp // SUBLANES, m // SUBLANES, 1, COL_QKV, 3 * D_A)
    cb_s = last_rows(mp // SUBLANES, m // SUBLANES, 1, COL_XB, D_B)
    return (y_prompt, y_sample,
            ca_p[None], sd_p[None], cb_p[None], h_p.reshape(1, bp, D_B),
            ca_s[None], sd_s[None], cb_s[None], h_s[None])
```

```python
import functools
import math

import jax
import jax.numpy as jnp
from jax import lax
from jax.experimental import pallas as pl
from jax.experimental.pallas import tpu as pltpu

F32 = jnp.float32
BF16 = jnp.bfloat16
HIGHEST = lax.Precision.HIGHEST

D_MODEL = 2048
D_A = 1024
H_A = 8
DK = 128
DV = 128
CONV_W = 4
D_B = 1024
NB = 8
BW = 128
LRU_C = 8.0
N_EXPERTS = 32
TOP_K = 4
D_FF = 2048
SWIGLU_LIMIT = 7.0
SWIGLU_ALPHA = 1.702
DEPTH = 1
DN_ALPHA = (2.0 * DEPTH) ** 0.25
LN_EPS = 1e-5
RMS_EPS = 1e-6
L2_EPS = 1e-6

LANES = 128
SUBLANES = 8
VMEM_LIMIT_BYTES = 56 * 1024 * 1024

COL_QKV = 0
COL_Z = 3 * D_A
COL_XB = 4 * D_A
COL_YB = 4 * D_A + D_B
COL_BA = 4 * D_A + 2 * D_B
N_PROJ = COL_BA + 2 * LANES

INPROJ_TM = 1024
INPROJ_TN = 1280
CONV_L = 256
DELTA_C = 128
OUT_TM = 512
MOE_PB = 128
MOE_RB = 256
MOE_SB = 5
MOE_R = MOE_RB * MOE_SB
MOE_NPB = MOE_R // MOE_PB
MOE_EXACT_ODD = (7, 9)
MOE_TF = 256
MOE_NT = D_FF // MOE_TF
MOE_TF1 = 512
MOE_NT1 = D_FF // MOE_TF1
MOE_VMEM_LIMIT_BYTES = 60 * 1024 * 1024
SAMPLE_NSEQ = 4
TOK_TM = 256
ROW_SLAB = D_MODEL // LANES


def _cparams(sem, vmem=VMEM_LIMIT_BYTES):
    return pltpu.CompilerParams(dimension_semantics=sem, vmem_limit_bytes=vmem)


def _sigmoid(x):
    return 1.0 / (1.0 + jnp.exp(-x))


def _softplus(x):
    return jnp.maximum(x, 0.0) + jnp.log(1.0 + jnp.exp(-jnp.abs(x)))


def _wprep_body(w_ref, o_ref):
    c = 4 * D_A
    nba = 2 * H_A
    o_ref[0:c, :] = w_ref[0:c, :].astype(BF16)
    o_ref[c:c + 2 * D_B, :] = w_ref[c + nba:c + nba + 2 * D_B, :].astype(BF16)
    o_ref[COL_BA:COL_BA + nba, :] = w_ref[c:c + nba, :].astype(BF16)
    o_ref[COL_BA + nba:N_PROJ, :] = jnp.zeros((N_PROJ - COL_BA - nba, o_ref.shape[1]), BF16)


def _wprep(w_t):
    cols = 256
    d_in = w_t.shape[0]
    assert d_in == 4 * D_A + 2 * H_A + 2 * D_B and D_MODEL % cols == 0
    return pl.pallas_call(
        _wprep_body,
        grid=(D_MODEL // cols,),
        in_specs=[pl.BlockSpec((d_in, cols), lambda i: (0, i))],
        out_specs=pl.BlockSpec((N_PROJ, cols), lambda i: (0, i)),
        out_shape=jax.ShapeDtypeStruct((N_PROJ, D_MODEL), BF16),
        compiler_params=_cparams(("arbitrary",)),
        name="wprep",
    )(w_t)


def _inproj_body(xp_ref, xs_ref, w_ref, o_ref, xb_ref, *, n_ptiles):
    i = pl.program_id(0)

    @pl.when(pl.program_id(1) == 0)
    def _():
        @pl.when(i < n_ptiles)
        def _():
            xb_ref[...] = xp_ref[...].astype(BF16)

        @pl.when(i >= n_ptiles)
        def _():
            xb_ref[...] = xs_ref[...].astype(BF16)

    o_ref[...] = _dot_nt(xb_ref[...], w_ref[...])


def _inproj(xp2, xs2, wp):
    mp, ms = xp2.shape[0], xs2.shape[0]
    tm = min(INPROJ_TM, ms)
    assert mp % tm == 0 and ms % tm == 0
    n_pt, n_st = mp // tm, ms // tm
    n_nt = N_PROJ // INPROJ_TN
    return pl.pallas_call(
        functools.partial(_inproj_body, n_ptiles=n_pt),
        grid=(n_pt + n_st, n_nt),
        in_specs=[
            pl.BlockSpec((tm, D_MODEL), lambda i, j: (jnp.minimum(i, n_pt - 1), 0)),
            pl.BlockSpec((tm, D_MODEL), lambda i, j: (jnp.maximum(i - n_pt, 0), 0),
                         pipeline_mode=pl.Buffered(1)),
            pl.BlockSpec((INPROJ_TN, D_MODEL), lambda i, j: (j, 0)),
        ],
        out_specs=pl.BlockSpec((tm, INPROJ_TN), lambda i, j: (i, j)),
        out_shape=jax.ShapeDtypeStruct((mp + ms, N_PROJ), F32),
        scratch_shapes=[pltpu.VMEM((tm, D_MODEL), BF16)],
        compiler_params=_cparams(("arbitrary", "arbitrary")),
        name="inproj",
    )(xp2, xs2, wp)


def _conv_taps(h_ref, w_ref, c0, width, rows, base):
    acc = h_ref[base:base + rows, c0:c0 + width] * w_ref[CONV_W - 1:CONV_W, c0:c0 + width]
    for j in range(1, CONV_W):
        acc = acc + (h_ref[base - j:base - j + rows, c0:c0 + width]
                     * w_ref[CONV_W - 1 - j:CONV_W - j, c0:c0 + width])
    return acc


def _qkv_activation(acc, blk):
    a = acc * _sigmoid(acc)
    if blk < 2 * H_A:
        a = a * lax.rsqrt(jnp.sum(a * a, axis=-1, keepdims=True) + L2_EPS)
        if blk < H_A:
            a = a * (DK ** -0.5)
    return a


def _beta_g(ba, alog_ref, dtb_ref):
    lane = lax.broadcasted_iota(jnp.int32, ba.shape, 1)
    beta = _sigmoid(ba)
    g = -jnp.exp(alog_ref[...]) * _softplus(ba + dtb_ref[...])
    return jnp.where(lane < H_A, beta, jnp.where(lane < 2 * H_A, g, 0.0))


def _convact_prompt_body(qkv_ref, xb_ref, ba_ref, caw_ref, cbw_ref, cbb_ref, alog_ref, dtb_ref,
                         qkvo_ref, xbo_ref, bgo_ref, hq_ref, hx_ref, *, rows):
    t = pl.program_id(1)
    hdr = SUBLANES

    @pl.when(t == 0)
    def _():
        hq_ref[0:hdr, :] = jnp.zeros((hdr, 3 * D_A), F32)
        hx_ref[0:hdr, :] = jnp.zeros((hdr, D_B), F32)

    @pl.when(t > 0)
    def _():
        hq_ref[0:hdr, :] = hq_ref[rows:rows + hdr, :]
        hx_ref[0:hdr, :] = hx_ref[rows:rows + hdr, :]

    hq_ref[hdr:hdr + rows, :] = qkv_ref[...]
    hx_ref[hdr:hdr + rows, :] = xb_ref[...]
    for blk in range(3 * H_A):
        c0 = blk * DK
        acc = _conv_taps(hq_ref, caw_ref, c0, DK, rows, hdr)
        qkvo_ref[:, c0:c0 + DK] = _qkv_activation(acc, blk)
    for blk in range(NB):
        c0 = blk * BW
        xbo_ref[:, c0:c0 + BW] = _conv_taps(hx_ref, cbw_ref, c0, BW, rows, hdr) + cbb_ref[:, c0:c0 + BW]
    bgo_ref[...] = _beta_g(ba_ref[...], alog_ref, dtb_ref)


def _convact_prompt(proj, bsz, t_len, caw, cbw, cbb, alog_row, dtb_row):
    rows = min(CONV_L, t_len)
    assert t_len % rows == 0
    nt = t_len // rows
    mp = bsz * t_len
    wspec = lambda shape: pl.BlockSpec(shape, lambda b, t: (0, 0))
    return pl.pallas_call(
        functools.partial(_convact_prompt_body, rows=rows),
        grid=(bsz, nt),
        in_specs=[
            pl.BlockSpec((rows, 3 * D_A), lambda b, t: (b * nt + t, COL_QKV // (3 * D_A))),
            pl.BlockSpec((rows, D_B), lambda b, t: (b * nt + t, COL_XB // D_B)),
            pl.BlockSpec((rows, LANES), lambda b, t: (b * nt + t, COL_BA // LANES)),
            wspec((CONV_W, 3 * D_A)), wspec((CONV_W, D_B)), wspec((1, D_B)),
            wspec((1, LANES)), wspec((1, LANES)),
        ],
        out_specs=[
            pl.BlockSpec((rows, 3 * D_A), lambda b, t: (b * nt + t, 0)),
            pl.BlockSpec((rows, D_B), lambda b, t: (b * nt + t, 0)),
            pl.BlockSpec((rows, LANES), lambda b, t: (b * nt + t, 0)),
        ],
        out_shape=[
            jax.ShapeDtypeStruct((mp, 3 * D_A), F32),
            jax.ShapeDtypeStruct((mp, D_B), F32),
            jax.ShapeDtypeStruct((mp, LANES), F32),
        ],
        scratch_shapes=[pltpu.VMEM((SUBLANES + rows + SUBLANES, 3 * D_A), F32),
                        pltpu.VMEM((SUBLANES + rows + SUBLANES, D_B), F32)],
        compiler_params=_cparams(("arbitrary", "arbitrary")),
        name="convact_prompt",
    )(proj, proj, proj, caw, cbw, cbb, alog_row, dtb_row)


def _dot_nt(a, b, precision=None):
    return lax.dot_general(a, b, (((1,), (1,)), ((), ())), precision=precision,
                           preferred_element_type=F32)


def _split_bf16(x):
    hi = x.astype(BF16)
    return hi, (x - hi.astype(F32)).astype(BF16)


def _dot3(a, b):
    ah, al = _split_bf16(a)
    bh, bl = _split_bf16(b)
    return (jnp.dot(ah, bh, preferred_element_type=F32)
            + (jnp.dot(al, bh, preferred_element_type=F32) + jnp.dot(ah, bl, preferred_element_type=F32)))


def _delta_prep_body(qkv_ref, bg_ref, u_ref, w_ref, qe_ref, kdt_ref, aqk_ref, el_ref, *, c):
    row = lax.broadcasted_iota(jnp.int32, (c, c), 0)
    col = lax.broadcasted_iota(jnp.int32, (c, c), 1)
    incl = row >= col
    strict = row > col
    eye = (row == col).astype(F32)
    tril = incl.astype(F32)
    eye_l = (lax.broadcasted_iota(jnp.int32, (LANES, LANES), 0)
             == lax.broadcasted_iota(jnp.int32, (LANES, LANES), 1)).astype(F32)

    bg = bg_ref[...]
    gc_all = jnp.dot(tril, bg, precision=HIGHEST, preferred_element_type=F32)
    gc_t = _dot_nt(eye_l, gc_all, precision=HIGHEST)
    heads = range(H_A)
    lmats, rhss = [], []
    for h in heads:
        q = qkv_ref[:, h * DK:(h + 1) * DK]
        k = qkv_ref[:, (H_A + h) * DK:(H_A + h + 1) * DK]
        v = qkv_ref[:, (2 * H_A + h) * DK:(2 * H_A + h + 1) * DK]
        beta = bg[:, h:h + 1]
        gc = gc_all[:, H_A + h:H_A + h + 1]
        g_last = gc_all[c - 1:c, H_A + h:H_A + h + 1]
        diff = gc - gc_t[H_A + h:H_A + h + 1, :]
        decay = jnp.where(incl, jnp.exp(jnp.where(incl, diff, 0.0)), 0.0)
        kb = k * beta
        egc = jnp.exp(gc)
        kbf = k.astype(BF16)
        kk = _dot_nt(kbf, kbf)
        lmats.append(jnp.where(strict, kk * beta * decay, 0.0))
        rhss.append(jnp.concatenate([v * beta, kb * egc], axis=1))
        qe_ref[:, h * DK:(h + 1) * DK] = (q * egc).astype(BF16)
        aqk_ref[h] = (_dot_nt(q.astype(BF16), kbf) * decay).astype(BF16)
        kdec = k * jnp.exp(g_last - gc)
        kdt_ref[h] = kdec.astype(BF16)
        el_ref[h] = jnp.broadcast_to(jnp.exp(g_last), (SUBLANES, LANES))
    xbs = [(-lm).astype(BF16) for lm in lmats]
    t0s = [eye - lm for lm in lmats]
    p = 2
    while p < c:
        xbs = [jnp.dot(xb, xb, preferred_element_type=F32).astype(BF16) for xb in xbs]
        t0s = [t0 + jnp.dot(t0.astype(BF16), xb, preferred_element_type=F32) for t0, xb in zip(t0s, xbs)]
        p *= 2
    t0bs = [t0.astype(BF16) for t0 in t0s]
    sol0s = [jnp.dot(t0b, rhs.astype(BF16), preferred_element_type=F32) for t0b, rhs in zip(t0bs, rhss)]
    resids = [rhs - sol0 - _dot3(lm, sol0) for rhs, sol0, lm in zip(rhss, sol0s, lmats)]
    for h in heads:
        sol = sol0s[h] + jnp.dot(t0bs[h], resids[h].astype(BF16), preferred_element_type=F32)
        u_ref[:, h * DV:(h + 1) * DV] = sol[:, :DV]
        w_ref[:, h * DK:(h + 1) * DK] = sol[:, DV:].astype(BF16)


def _delta_prep(qkv_act, bg, bsz, t_len):
    c = DELTA_C
    assert t_len % c == 0
    nc = t_len // c
    mp = bsz * t_len
    n_chunks = bsz * nc
    rowspec = lambda w: pl.BlockSpec((c, w), lambda i: (i, 0))
    return pl.pallas_call(
        functools.partial(_delta_prep_body, c=c),
        grid=(n_chunks,),
        in_specs=[rowspec(3 * D_A), rowspec(LANES)],
        out_specs=[
            rowspec(D_A), rowspec(D_A), rowspec(D_A),
            pl.BlockSpec((None, H_A, c, DK), lambda i: (i, 0, 0, 0)),
            pl.BlockSpec((None, H_A, c, c), lambda i: (i, 0, 0, 0)),
            pl.BlockSpec((None, H_A, SUBLANES, LANES), lambda i: (i, 0, 0, 0)),
        ],
        out_shape=[
            jax.ShapeDtypeStruct((mp, D_A), F32),
            jax.ShapeDtypeStruct((mp, D_A), BF16),
            jax.ShapeDtypeStruct((mp, D_A), BF16),
            jax.ShapeDtypeStruct((n_chunks, H_A, c, DK), BF16),
            jax.ShapeDtypeStruct((n_chunks, H_A, c, c), BF16),
            jax.ShapeDtypeStruct((n_chunks, H_A, SUBLANES, LANES), F32),
        ],
        compiler_params=_cparams(("arbitrary",)),
        name="delta_prep",
    )(qkv_act, bg)


def _gated_rmsnorm(o, z, nw):
    on = o * lax.rsqrt(jnp.mean(o * o, axis=-1, keepdims=True) + RMS_EPS) * nw
    return on * (z * _sigmoid(z))


def _delta_seq_body(u_ref, w_ref, qe_ref, kdt_ref, aqk_ref, el_ref, z_ref, nw_ref,
                    oa_ref, sfin_ref, s_ref):
    ci = pl.program_id(1)

    @pl.when(ci == 0)
    def _():
        s_ref[...] = jnp.zeros(s_ref.shape, F32)

    nw = nw_ref[...]
    heads = range(H_A)
    sls = [slice(h * DK, (h + 1) * DK) for h in heads]
    ss = [s_ref[h] for h in heads]
    sbs = [s.astype(BF16) for s in ss]
    vbs = [(u_ref[:, sl] - jnp.dot(w_ref[:, sl], sb, preferred_element_type=F32)).astype(BF16)
           for sl, sb in zip(sls, sbs)]
    for h in heads:
        s_ref[h] = ss[h] * el_ref[h][0:1, 0:1] + lax.dot_general(
            kdt_ref[h], vbs[h], (((0,), (0,)), ((), ())), preferred_element_type=F32)
    for h in heads:
        o = (jnp.dot(qe_ref[:, sls[h]], sbs[h], preferred_element_type=F32)
             + jnp.dot(aqk_ref[h], vbs[h], preferred_element_type=F32))
        oa_ref[:, sls[h]] = _gated_rmsnorm(o, z_ref[:, sls[h]], nw).astype(BF16)

    @pl.when(ci == pl.num_programs(1) - 1)
    def _():
        sfin_ref[...] = s_ref[...]


def _delta_seq(u, w, qe, kdt, aqk, el, proj, nw_row, bsz, t_len):
    c = DELTA_C
    nc = t_len // c
    mp = bsz * t_len
    rowspec = lambda wd: pl.BlockSpec((c, wd), lambda b, i: (b * nc + i, 0))
    chunkspec = lambda a, bb: pl.BlockSpec((None, H_A, a, bb), lambda b, i: (b * nc + i, 0, 0, 0))
    return pl.pallas_call(
        _delta_seq_body,
        grid=(bsz, nc),
        in_specs=[
            rowspec(D_A), rowspec(D_A), rowspec(D_A),
            chunkspec(c, DK), chunkspec(c, c), chunkspec(SUBLANES, LANES),
            pl.BlockSpec((c, D_A), lambda b, i: (b * nc + i, COL_Z // D_A)),
            pl.BlockSpec((1, DV), lambda b, i: (0, 0)),
        ],
        out_specs=[
            rowspec(D_A),
            pl.BlockSpec((None, H_A, DK, DV), lambda b, i: (b, 0, 0, 0)),
        ],
        out_shape=[
            jax.ShapeDtypeStruct((mp, D_A), BF16),
            jax.ShapeDtypeStruct((bsz, H_A, DK, DV), F32),
        ],
        scratch_shapes=[pltpu.VMEM((H_A, DK, DV), F32)],
        compiler_params=_cparams(("arbitrary", "arbitrary")),
        name="delta_seq",
    )(u, w, qe, kdt, aqk, el, proj, nw_row)


def _sample_a_body(qkv_ref, z_ref, ba_ref, hist_ref, s0_ref, caw_ref, alog_ref, dtb_ref, nw_ref,
                   oa_ref, s1_ref, hq_ref, *, t_len, nseq):
    hdr = SUBLANES
    nh = CONV_W - 1
    rows = [slice(si * t_len, (si + 1) * t_len) for si in range(nseq)]
    for si in range(nseq):
        hq_ref[si, 0:hdr, :] = jnp.zeros((hdr, 3 * D_A), F32)
        hq_ref[si, hdr - nh:hdr, :] = hist_ref[si]
        hq_ref[si, hdr:hdr + t_len, :] = qkv_ref[rows[si], :]
    bg = _beta_g(ba_ref[...], alog_ref, dtb_ref)
    rloc = lax.rem(lax.broadcasted_iota(jnp.int32, bg.shape, 0), t_len)
    gc_all = bg
    sft = 1
    while sft < t_len:
        gc_all = gc_all + jnp.where(rloc >= sft, pltpu.roll(gc_all, sft, axis=0), 0.0)
        sft *= 2
    nw = nw_ref[...]
    rowc = lax.broadcasted_iota(jnp.int32, (t_len, 1), 0)
    pairs = [(si, h) for si in range(nseq) for h in range(H_A)]
    npair = range(len(pairs))
    act = lambda si, blk: _qkv_activation(_conv_taps(hq_ref.at[si], caw_ref, blk * DK, DK, t_len, hdr), blk)
    qs = [act(si, h) for si, h in pairs]
    ks = [act(si, H_A + h) for si, h in pairs]
    vs = [act(si, 2 * H_A + h) for si, h in pairs]
    betas = [bg[rows[si], h:h + 1] for si, h in pairs]
    gcs = [gc_all[rows[si], H_A + h:H_A + h + 1] for si, h in pairs]
    glasts = [gc[t_len - 1:t_len, :] for gc in gcs]
    kbs = [k * b for k, b in zip(ks, betas)]
    egcs = [jnp.exp(gc) for gc in gcs]
    sols = [jnp.concatenate([v * b, kb * e], axis=1) for v, b, kb, e in zip(vs, betas, kbs, egcs)]
    acols = [[] for _ in npair]
    lcols = [[] for _ in npair]
    for j in range(t_len):
        for p in npair:
            kj = ks[p][j:j + 1, :]
            dcol = jnp.exp(jnp.where(rowc >= j, gcs[p] - gcs[p][j:j + 1, :], 0.0))
            acols[p].append(jnp.where(rowc >= j, jnp.sum(qs[p] * kj, axis=-1, keepdims=True) * dcol, 0.0))
            lcols[p].append(jnp.where(rowc > j, jnp.sum(kbs[p] * kj, axis=-1, keepdims=True) * dcol, 0.0))
    for j in range(t_len - 1):
        sols = [sol - lcols[p][j] * sol[j:j + 1, :] for p, sol in enumerate(sols)]
    ss = [s0_ref[si, h] for si, h in pairs]
    sbs = [s.astype(BF16) for s in ss]
    v_news = [sol[:, :DV] - jnp.dot(sol[:, DV:].astype(BF16), sb, preferred_element_type=F32)
              for sol, sb in zip(sols, sbs)]
    os_ = [jnp.dot((q * e).astype(BF16), sb, preferred_element_type=F32) for q, e, sb in zip(qs, egcs, sbs)]
    for j in range(t_len):
        os_ = [o + acols[p][j] * v_news[p][j:j + 1, :] for p, o in enumerate(os_)]
    upds = [lax.dot_general((k * jnp.exp(gl - gc)).astype(BF16), vn.astype(BF16), (((0,), (0,)), ((), ())),
                            preferred_element_type=F32)
            for k, gl, gc, vn in zip(ks, glasts, gcs, v_news)]
    for p, (si, h) in enumerate(pairs):
        s1_ref[si, h] = ss[p] * jnp.exp(glasts[p]) + upds[p]
        sl = slice(h * DV, (h + 1) * DV)
        oa_ref[rows[si], sl] = _gated_rmsnorm(os_[p], z_ref[rows[si], sl], nw).astype(BF16)


def _sample_a(proj, row0, bsz, t_len, hist, s0, caw, alog_row, dtb_row, nw_row):
    nseq = SAMPLE_NSEQ
    tr = nseq * t_len
    assert t_len == SUBLANES and row0 % tr == 0 and bsz % nseq == 0
    r0 = row0 // tr
    wspec = lambda shape: pl.BlockSpec(shape, lambda b: (0,) * len(shape))
    return pl.pallas_call(
        functools.partial(_sample_a_body, t_len=t_len, nseq=nseq),
        grid=(bsz // nseq,),
        in_specs=[
            pl.BlockSpec((tr, 3 * D_A), lambda b: (r0 + b, COL_QKV // (3 * D_A))),
            pl.BlockSpec((tr, D_A), lambda b: (r0 + b, COL_Z // D_A)),
            pl.BlockSpec((tr, LANES), lambda b: (r0 + b, COL_BA // LANES)),
            pl.BlockSpec((nseq, CONV_W - 1, 3 * D_A), lambda b: (b, 0, 0)),
            pl.BlockSpec((nseq, H_A, DK, DV), lambda b: (b, 0, 0, 0)),
            wspec((CONV_W, 3 * D_A)), wspec((1, LANES)), wspec((1, LANES)), wspec((1, DV)),
        ],
        out_specs=[
            pl.BlockSpec((tr, D_A), lambda b: (b, 0)),
            pl.BlockSpec((nseq, H_A, DK, DV), lambda b: (b, 0, 0, 0)),
        ],
        out_shape=[
            jax.ShapeDtypeStruct((bsz * t_len, D_A), BF16),
            jax.ShapeDtypeStruct((bsz, H_A, DK, DV), F32),
        ],
        scratch_shapes=[pltpu.VMEM((nseq, 2 * SUBLANES, 3 * D_A), F32)],
        compiler_params=_cparams(("arbitrary",)),
        name="sample_a",
    )(proj, proj, proj, hist, s0, caw, alog_row, dtb_row, nw_row)


def _lru_coeffs(xc, wa_ref, wx_ref, lba_ref, lbx_ref, lam_ref):
    parts = []
    for n in range(NB):
        sl = slice(n * BW, (n + 1) * BW)
        xn = xc[:, sl]
        xnb = xn.astype(BF16)
        gr = _sigmoid(jnp.dot(xnb, wa_ref[n].astype(BF16), preferred_element_type=F32) + lba_ref[:, sl])
        gi = _sigmoid(jnp.dot(xnb, wx_ref[n].astype(BF16), preferred_element_type=F32) + lbx_ref[:, sl])
        log_a = -LRU_C * gr * _softplus(-lam_ref[:, sl])
        parts.append((jnp.exp(log_a), jnp.sqrt(1.0 - jnp.exp(2.0 * log_a)), gi * xn))
    return parts


def _group_scan(a, b):
    rowi = lax.broadcasted_iota(jnp.int32, a.shape, 0)
    sft = 1
    while sft < SUBLANES:
        keep = rowi >= sft
        b = b + a * jnp.where(keep, pltpu.roll(b, sft, axis=0), 0.0)
        a = a * jnp.where(keep, pltpu.roll(a, sft, axis=0), 1.0)
        sft *= 2
    return a, b


def _gelu_tanh(x):
    return 0.5 * x * (1.0 + jnp.tanh(math.sqrt(2.0 / math.pi) * (x + 0.044715 * x * x * x)))


def _lru_finish(h, yb, nbw):
    hg = h * _gelu_tanh(yb)
    return (hg * lax.rsqrt(jnp.mean(hg * hg, axis=-1, keepdims=True) + RMS_EPS) * nbw).astype(BF16)


def _lru_prompt_body(xc_ref, yb_ref, wa_ref, wx_ref, lba_ref, lbx_ref, lam_ref, nbw_ref,
                     ob_ref, hfin_ref, a_ref, b_ref, h_ref, carry_ref, *, rows):
    t = pl.program_id(1)
    parts = _lru_coeffs(xc_ref[...], wa_ref, wx_ref, lba_ref, lbx_ref, lam_ref)
    rowi = lax.broadcasted_iota(jnp.int32, (rows, BW), 0)
    first = jnp.logical_and(rowi == 0, t == 0)
    for n in range(NB):
        sl = slice(n * BW, (n + 1) * BW)
        a_n, mult_n, gix_n = parts[n]
        a_ref[:, sl] = jnp.where(first, 0.0, a_n)
        b_ref[:, sl] = jnp.where(first, 1.0, mult_n) * gix_n

    @pl.when(t == 0)
    def _():
        carry_ref[...] = jnp.zeros(carry_ref.shape, F32)

    def group(gidx, carry):
        r0 = pl.multiple_of(gidx * SUBLANES, SUBLANES)
        ag, bg = _group_scan(a_ref[pl.ds(r0, SUBLANES), :], b_ref[pl.ds(r0, SUBLANES), :])
        hg = ag * carry + bg
        h_ref[pl.ds(r0, SUBLANES), :] = hg
        return jnp.broadcast_to(hg[SUBLANES - 1:SUBLANES, :], hg.shape)

    carry = lax.fori_loop(0, rows // SUBLANES, group, carry_ref[...])
    carry_ref[...] = carry
    ob_ref[...] = _lru_finish(h_ref[...], yb_ref[...], nbw_ref[...])

    @pl.when(t == pl.num_programs(1) - 1)
    def _():
        hfin_ref[...] = carry[0:1, :]


def _lru_prompt(xbc, proj, bsz, t_len, wa, wx, lba, lbx, lam, nbw):
    rows = min(CONV_L, t_len)
    nt = t_len // rows
    mp = bsz * t_len
    wspec = lambda shape: pl.BlockSpec(shape, lambda b, t: (0,) * len(shape))
    return pl.pallas_call(
        functools.partial(_lru_prompt_body, rows=rows),
        grid=(bsz, nt),
        in_specs=[
            pl.BlockSpec((rows, D_B), lambda b, t: (b * nt + t, 0)),
            pl.BlockSpec((rows, D_B), lambda b, t: (b * nt + t, COL_YB // D_B)),
            wspec((NB, BW, BW)), wspec((NB, BW, BW)),
            wspec((1, D_B)), wspec((1, D_B)), wspec((1, D_B)), wspec((1, D_B)),
        ],
        out_specs=[
            pl.BlockSpec((rows, D_B), lambda b, t: (b * nt + t, 0)),
            pl.BlockSpec((None, 1, D_B), lambda b, t: (b, 0, 0)),
        ],
        out_shape=[
            jax.ShapeDtypeStruct((mp, D_B), BF16),
            jax.ShapeDtypeStruct((bsz, 1, D_B), F32),
        ],
        scratch_shapes=[pltpu.VMEM((rows, D_B), F32), pltpu.VMEM((rows, D_B), F32),
                        pltpu.VMEM((rows, D_B), F32), pltpu.VMEM((SUBLANES, D_B), F32)],
        compiler_params=_cparams(("arbitrary", "arbitrary")),
        name="lru_prompt",
    )(xbc, proj, wa, wx, lba, lbx, lam, nbw)


def _lru_sample_body(xb_ref, yb_ref, hist_ref, h0_ref, cbw_ref, cbb_ref, wa_ref, wx_ref, lba_ref,
                     lbx_ref, lam_ref, nbw_ref, ob_ref, h1_ref, hx_ref, a_ref, b_ref, h_ref,
                     *, nseq, t_len):
    hdr = SUBLANES
    nh = CONV_W - 1
    hx_ref[0:hdr, :] = jnp.zeros((hdr, D_B), F32)

    def conv_seq(si, _):
        r0 = pl.multiple_of(si * t_len, t_len)
        hx_ref[hdr - nh:hdr, :] = hist_ref[si]
        hx_ref[hdr:hdr + t_len, :] = xb_ref[pl.ds(r0, t_len), :]
        for n in range(NB):
            c0 = n * BW
            h_ref[pl.ds(r0, t_len), c0:c0 + BW] = (_conv_taps(hx_ref, cbw_ref, c0, BW, t_len, hdr)
                                                  + cbb_ref[:, c0:c0 + BW])
        return 0

    lax.fori_loop(0, nseq, conv_seq, 0)
    parts = _lru_coeffs(h_ref[...], wa_ref, wx_ref, lba_ref, lbx_ref, lam_ref)
    for n in range(NB):
        sl = slice(n * BW, (n + 1) * BW)
        a_n, mult_n, gix_n = parts[n]
        a_ref[:, sl] = a_n
        b_ref[:, sl] = mult_n * gix_n

    def seq(si, _):
        r0 = pl.multiple_of(si * t_len, t_len)
        ag, bg = _group_scan(a_ref[pl.ds(r0, t_len), :], b_ref[pl.ds(r0, t_len), :])
        hg = ag * h0_ref[pl.ds(si, 1), :] + bg
        h_ref[pl.ds(r0, t_len), :] = hg
        h1_ref[pl.ds(si, 1), :] = hg[t_len - 1:t_len, :]
        return 0

    lax.fori_loop(0, nseq, seq, 0)
    ob_ref[...] = _lru_finish(h_ref[...], yb_ref[...], nbw_ref[...])


def _lru_sample(proj, row0, bsz, t_len, hist, h0, cbw, cbb, wa, wx, lba, lbx, lam, nbw):
    assert t_len == SUBLANES
    ms = bsz * t_len
    assert row0 % ms == 0
    rblk = row0 // ms
    wspec = lambda shape: pl.BlockSpec(shape, lambda i: (0,) * len(shape))
    return pl.pallas_call(
        functools.partial(_lru_sample_body, nseq=bsz, t_len=t_len),
        grid=(1,),
        in_specs=[
            pl.BlockSpec((ms, D_B), lambda i: (rblk, COL_XB // D_B)),
            pl.BlockSpec((ms, D_B), lambda i: (rblk, COL_YB // D_B)),
            wspec((bsz, CONV_W - 1, D_B)), wspec((bsz, D_B)),
            wspec((CONV_W, D_B)), wspec((1, D_B)),
            wspec((NB, BW, BW)), wspec((NB, BW, BW)),
            wspec((1, D_B)), wspec((1, D_B)), wspec((1, D_B)), wspec((1, D_B)),
        ],
        out_specs=[wspec((ms, D_B)), wspec((bsz, D_B))],
        out_shape=[
            jax.ShapeDtypeStruct((ms, D_B), BF16),
            jax.ShapeDtypeStruct((bsz, D_B), F32),
        ],
        scratch_shapes=[pltpu.VMEM((2 * SUBLANES, D_B), F32), pltpu.VMEM((ms, D_B), F32),
                        pltpu.VMEM((ms, D_B), F32), pltpu.VMEM((ms, D_B), F32)],
        compiler_params=_cparams(("arbitrary",)),
        name="lru_sample",
    )(proj, proj, hist, h0, cbw, cbb, wa, wx, lba, lbx, lam, nbw)


def _layernorm_rows(v, g, b):
    mu = jnp.mean(v, axis=-1, keepdims=True)
    d = v - mu
    var = jnp.mean(d * d, axis=-1, keepdims=True)
    return d * lax.rsqrt(var + LN_EPS) * g + b


def _store_slabs(slab_ref, base, val):
    n = val.shape[0]
    for s in range(ROW_SLAB):
        slab_ref[pl.ds(base + s, n, stride=ROW_SLAB), :] = val[:, s * LANES:(s + 1) * LANES]


def _load_slab_chunk(slab_ref, base, n, s):
    return slab_ref[pl.ds(base + s, n, stride=ROW_SLAB), :]


def _outproj_body(oap_ref, oas_ref, obp_ref, obs_ref, xp_ref, xs_ref, wo_ref, g_ref, b_ref,
                  wr_ref, br_ref, x1_ref, ti_ref, tg_ref, *, n_ptiles):
    i = pl.program_id(0)

    def run(oa_ref, ob_ref, x_ref):
        mix = (jnp.dot(oa_ref[...], wo_ref[0:D_A, :], preferred_element_type=F32)
               + jnp.dot(ob_ref[...], wo_ref[D_A:D_A + D_B, :], preferred_element_type=F32))
        y = _layernorm_rows(DN_ALPHA * x_ref[...] + mix, g_ref[...], b_ref[...])
        _store_slabs(x1_ref, 0, y)
        logits = _dot3(y, wr_ref[...]) + br_ref[...]
        lane = lax.broadcasted_iota(jnp.int32, logits.shape, 1)
        lane_o = lax.broadcasted_iota(jnp.int32, ti_ref.shape, 1)
        cur = logits
        ti = jnp.zeros(ti_ref.shape, jnp.int32)
        tv = jnp.zeros(tg_ref.shape, F32)
        v0 = None
        den = None
        for kk in range(TOP_K):
            m = jnp.max(cur, axis=-1, keepdims=True)
            idx = jnp.min(jnp.where(cur == m, lane, N_EXPERTS), axis=-1, keepdims=True)
            cur = jnp.where(lane == idx, -jnp.inf, cur)
            if kk == 0:
                v0 = m
            e = jnp.exp(m - v0)
            den = e if den is None else den + e
            ti = jnp.where(lane_o == kk, idx, ti)
            tv = jnp.where(lane_o == kk, e, tv)
        ti_ref[...] = ti
        tg_ref[...] = tv / den

    @pl.when(i < n_ptiles)
    def _():
        run(oap_ref, obp_ref, xp_ref)

    @pl.when(i >= n_ptiles)
    def _():
        run(oas_ref, obs_ref, xs_ref)


def _outproj(oa_p, oa_s, ob_p, ob_s, xp2, xs2, wo, g_row, b_row, wr, br_row):
    mp, ms = xp2.shape[0], xs2.shape[0]
    tm = min(OUT_TM, ms)
    assert mp % tm == 0 and ms % tm == 0
    n_pt, n_st = mp // tm, ms // tm
    pmap = lambda i: (jnp.minimum(i, n_pt - 1), 0)
    smap = lambda i: (jnp.maximum(i - n_pt, 0), 0)
    wspec = lambda shape: pl.BlockSpec(shape, lambda i: (0, 0))
    m = mp + ms
    return pl.pallas_call(
        functools.partial(_outproj_body, n_ptiles=n_pt),
        grid=(n_pt + n_st,),
        in_specs=[
            pl.BlockSpec((tm, D_A), pmap), pl.BlockSpec((tm, D_A), smap),
            pl.BlockSpec((tm, D_B), pmap), pl.BlockSpec((tm, D_B), smap),
            pl.BlockSpec((tm, D_MODEL), pmap), pl.BlockSpec((tm, D_MODEL), smap),
            wspec((D_A + D_B, D_MODEL)), wspec((1, D_MODEL)), wspec((1, D_MODEL)),
            wspec((D_MODEL, N_EXPERTS)), wspec((1, N_EXPERTS)),
        ],
        out_specs=[
            pl.BlockSpec((tm * ROW_SLAB, LANES), lambda i: (i, 0)),
            pl.BlockSpec((tm, LANES), lambda i: (i, 0)),
            pl.BlockSpec((tm, LANES), lambda i: (i, 0)),
        ],
        out_shape=[
            jax.ShapeDtypeStruct((m * ROW_SLAB, LANES), F32),
            jax.ShapeDtypeStruct((m, LANES), jnp.int32),
            jax.ShapeDtypeStruct((m, LANES), F32),
        ],
        compiler_params=_cparams(("arbitrary",)),
        name="outproj_ln_router",
    )(oa_p, oa_s, ob_p, ob_s, xp2, xs2, wo, g_row, b_row, wr, br_row)


def _zero_tail(zbuf_ref, dst_ref, tail_row, n_blocks, sem):
    scale = ROW_SLAB if len(dst_ref.shape) == 2 else 1

    def tail_copy(t):
        z0 = pl.multiple_of((tail_row + t * MOE_PB) * scale, MOE_PB * scale)
        return pltpu.make_async_copy(zbuf_ref, dst_ref.at[pl.ds(z0, MOE_PB * scale)], sem)

    def start(t, _):
        tail_copy(t).start()
        return 0

    def wait(t, _):
        tail_copy(t).wait()
        return 0

    lax.fori_loop(0, n_blocks, start, 0)
    lax.fori_loop(0, n_blocks, wait, 0)


def _dispatch_body(zrow_ref, slot_ref, x_ref, xs_ref, zbuf_ref, zsem, rsem, *, tm):
    i = pl.program_id(0)

    @pl.when(i == 0)
    def _():
        zbuf_ref[...] = jnp.zeros(zbuf_ref.shape, F32)

        def zero_copy(e):
            z0 = pl.multiple_of(zrow_ref[e], MOE_PB)
            return pltpu.make_async_copy(zbuf_ref, xs_ref.at[pl.ds(z0, MOE_PB)], zsem)

        for e in range(N_EXPERTS):
            @pl.when(zrow_ref[e] >= 0)
            def _():
                zero_copy(e).start()
        for e in range(N_EXPERTS):
            @pl.when(zrow_ref[e] >= 0)
            def _():
                zero_copy(e).wait()
        _zero_tail(zbuf_ref, xs_ref, zrow_ref[N_EXPERTS], zrow_ref[N_EXPERTS + 1], zsem)

    def row_copy(r, dst):
        src0 = pl.multiple_of(r * ROW_SLAB, ROW_SLAB)
        return pltpu.make_async_copy(x_ref.at[pl.ds(src0, ROW_SLAB)], xs_ref.at[dst], rsem)

    def issue(r, _):
        for kk in range(TOP_K):
            row_copy(r, slot_ref[0, r * TOP_K + kk]).start(priority=kk % 2)
        return 0

    lax.fori_loop(0, tm, issue, 0, unroll=8)

    for kk in range(TOP_K):
        pltpu.make_async_copy(xs_ref.at[pl.ds(0, tm)], xs_ref.at[pl.ds(0, tm)], rsem).wait()


def _dispatch(x1, slot_tiles, zrow, n_rows):
    m = x1.shape[0] // ROW_SLAB
    tm = TOK_TM
    assert m % tm == 0
    return pl.pallas_call(
        functools.partial(_dispatch_body, tm=tm),
        grid_spec=pltpu.PrefetchScalarGridSpec(
            num_scalar_prefetch=1,
            grid=(m // tm,),
            in_specs=[
                pl.BlockSpec((None, 1, tm * TOP_K), lambda i, z: (i, 0, 0), memory_space=pltpu.SMEM),
                pl.BlockSpec((tm * ROW_SLAB, LANES), lambda i, z: (i, 0)),
            ],
            out_specs=pl.BlockSpec(memory_space=pl.ANY),
            scratch_shapes=[pltpu.VMEM((MOE_PB, ROW_SLAB, LANES), F32),
                            pltpu.SemaphoreType.DMA(()), pltpu.SemaphoreType.DMA(())],
        ),
        out_shape=jax.ShapeDtypeStruct((n_rows, ROW_SLAB, LANES), F32),
        compiler_params=_cparams(("arbitrary",)),
        name="moe_dispatch",
    )(zrow, slot_tiles, x1)


def _for_row_blocks(npb, fn):
    for k in range(1, MOE_SB + 1):
        odd = 2 * k - 1
        hit = npb == 2 * k
        if odd not in MOE_EXACT_ODD:
            hit = jnp.logical_or(hit, npb == odd)

        @pl.when(hit)
        def _():
            fn(0, k * MOE_RB)
    for q in MOE_EXACT_ODD:
        @pl.when(npb == q)
        def _():
            fn(0, q * MOE_PB)


def _moe_body(ie_ref, irow_ref, insub_ref, tail_ref, xs_ref, wg_ref, wu_ref, bup_ref,
              wd_ref, bd_ref, ys_ref, xstage_ref, xb_ref, act_ref, ybuf_ref, wgb_ref, wub_ref, wdb_ref,
              isem, osem):
    wi = pl.program_id(0)
    j = pl.program_id(1)
    n_w = pl.num_programs(0)
    to_sub = lambda n: lax.shift_right_logical(n + (MOE_RB // MOE_PB - 1), MOE_RB // MOE_PB - 1)
    npb = insub_ref[wi]
    nsub = to_sub(npb)
    row0 = irow_ref[wi]
    nt = MOE_NT
    n1 = MOE_NT1
    is_last = wi == n_w - 1
    nxt = jnp.minimum(wi + 1, n_w - 1)
    nsub_next = to_sub(jnp.where(is_last, 0, insub_ref[nxt]))
    row_next = irow_ref[nxt]
    prv = jnp.maximum(wi - 1, 0)
    npb_prev = jnp.where(wi == 0, 0, insub_ref[prv])
    row_prev = irow_ref[prv]

    class _XCopy:
        def __init__(self, row_base, s, slot):
            g0 = pl.multiple_of(row_base + s * MOE_RB, MOE_PB)
            self.copies = [
                pltpu.make_async_copy(xs_ref.at[pl.ds(g0, MOE_RB), c],
                                      xstage_ref.at[slot, :, pl.ds(c * LANES, LANES)], isem.at[slot])
                for c in range(ROW_SLAB)]

        def start(self):
            for cp in self.copies:
                cp.start()

        def wait(self):
            for cp in self.copies:
                cp.wait()

    x_copy = _XCopy

    def to_matmul_layout(s, slot):
        r0 = pl.multiple_of(s * MOE_RB, MOE_RB)
        xb_ref[pl.ds(r0, MOE_RB), :] = xstage_ref[slot].astype(BF16)

    def y_copy(row_base, s):
        b0 = pl.multiple_of(s * MOE_PB * ROW_SLAB, MOE_PB * ROW_SLAB)
        g0 = pl.multiple_of(row_base * ROW_SLAB, MOE_PB * ROW_SLAB) + b0
        return pltpu.make_async_copy(ybuf_ref.at[pl.ds(b0, MOE_PB * ROW_SLAB)],
                                     ys_ref.at[pl.ds(g0, MOE_PB * ROW_SLAB)], osem)

    @pl.when(jnp.logical_and(wi == 0, j == 0))
    def _():
        def load(s, _):
            cp = x_copy(row0, s, 0)
            cp.start()
            cp.wait()
            to_matmul_layout(s, 0)
            return 0

        lax.fori_loop(0, nsub, load, 0)

    @pl.when(jnp.logical_and(j == n1 - 1, nsub_next > 0))
    def _():
        x_copy(row_next, 0, 0).start()

    @pl.when(jnp.logical_and(j >= n1, j - n1 < nsub_next))
    def _():
        s = j - n1
        slot = lax.rem(s, 2)
        x_copy(row_next, s, slot).wait()

        @pl.when(s + 1 < nsub_next)
        def _():
            x_copy(row_next, s + 1, 1 - slot).start()

        to_matmul_layout(s, slot)

    @pl.when(jnp.logical_and(j == n1, npb_prev > 0))
    def _():
        def drain(s, _):
            y_copy(row_prev, s).wait()
            return 0

        lax.fori_loop(0, npb_prev, drain, 0)

    @pl.when(npb > 0)
    def _():
        @pl.when(j < n1)
        def _():
            wgb_ref[...] = wg_ref[...].astype(BF16)
            wub_ref[...] = wu_ref[...].astype(BF16)
            b_gate = bup_ref[ie_ref[wi], pl.ds(j, 1), :]
            b_up = bup_ref[ie_ref[wi], pl.ds(n1 + j, 1), :]
            tpw = MOE_TF1 // MOE_TF

            def up_rows(r0, nrows):
                x = xb_ref[pl.ds(r0, nrows), :]
                hg = jnp.dot(x, wgb_ref[...], preferred_element_type=F32) + b_gate
                hu = jnp.dot(x, wub_ref[...], preferred_element_type=F32) + b_up
                gate = jnp.minimum(hg, SWIGLU_LIMIT)
                up = jnp.clip(hu, -SWIGLU_LIMIT, SWIGLU_LIMIT)
                glu = gate * _sigmoid(SWIGLU_ALPHA * gate)
                a = (glu * (up + 1.0)).astype(BF16)
                for tt in range(tpw):
                    act_ref[j * tpw + tt, pl.ds(r0, nrows), :] = a[:, tt * MOE_TF:(tt + 1) * MOE_TF]

            _for_row_blocks(npb, up_rows)

        @pl.when(j >= n1)
        def _():
            wdb_ref[...] = wd_ref[...].astype(BF16)
            n = j - n1
            b_down = bd_ref[ie_ref[wi], pl.ds(n, 1), :]

            def down_rows(r0, nrows):
                acc = jnp.dot(act_ref[0, pl.ds(r0, nrows), :], wdb_ref[0:MOE_TF, :],
                              preferred_element_type=F32)
                for jj in range(1, nt):
                    acc = acc + jnp.dot(act_ref[jj, pl.ds(r0, nrows), :],
                                        wdb_ref[jj * MOE_TF:(jj + 1) * MOE_TF, :],
                                        preferred_element_type=F32)
                acc = acc + b_down
                cpt = MOE_TF // LANES
                for cc in range(cpt):
                    ybuf_ref[pl.ds(r0 * ROW_SLAB + n * cpt + cc, nrows, stride=ROW_SLAB), :] = (
                        acc[:, cc * LANES:(cc + 1) * LANES])

            _for_row_blocks(npb, down_rows)

        @pl.when(j == n1 + nt - 1)
        def _():
            def store(s, _):
                y_copy(row0, s).start()
                return 0

            lax.fori_loop(0, npb, store, 0)

    @pl.when(jnp.logical_and(is_last, j == n1 + nt - 1))
    def _():
        def drain(s, _):
            y_copy(row0, s).wait()
            return 0

        lax.fori_loop(0, npb, drain, 0)
        zsrc = ybuf_ref.at[pl.ds(0, MOE_PB * ROW_SLAB)]
        zsrc[...] = jnp.zeros(zsrc.shape, F32)
        _zero_tail(zsrc, ys_ref, tail_ref[0], tail_ref[1], osem)


def _moe(xs, n_items, item_e, item_row, item_nsub, tail, w_up, b_up3, w_down, b_down3):
    n_rows = xs.shape[0] * ROW_SLAB
    nt = MOE_NT
    n1 = MOE_NT1
    assert MOE_SB <= nt and MOE_RB == 2 * MOE_PB
    up_off = D_FF // MOE_TF1
    p1 = lambda j, n, w: jnp.where(n[w] > 0, jnp.minimum(j, n1 - 1), n1 - 1)
    p2 = lambda j, n, w: jnp.where(n[w] > 0, jnp.maximum(j - n1, 0), nt - 1)
    return pl.pallas_call(
        _moe_body,
        grid_spec=pltpu.PrefetchScalarGridSpec(
            num_scalar_prefetch=4,
            grid=(n_items, n1 + nt),
            in_specs=[
                pl.BlockSpec(memory_space=pl.ANY),
                pl.BlockSpec((None, D_MODEL, MOE_TF1), lambda w, j, e, r, n, t: (e[w], 0, p1(j, n, w))),
                pl.BlockSpec((None, D_MODEL, MOE_TF1),
                             lambda w, j, e, r, n, t: (e[w], 0, up_off + p1(j, n, w))),
                pl.BlockSpec((N_EXPERTS, 2 * n1, MOE_TF1), lambda w, j, e, r, n, t: (0, 0, 0)),
                pl.BlockSpec((None, D_FF, MOE_TF), lambda w, j, e, r, n, t: (e[w], 0, p2(j, n, w))),
                pl.BlockSpec((N_EXPERTS, nt, MOE_TF), lambda w, j, e, r, n, t: (0, 0, 0)),
            ],
            out_specs=pl.BlockSpec(memory_space=pl.ANY),
            scratch_shapes=[
                pltpu.VMEM((2, MOE_RB, D_MODEL), F32),
                pltpu.VMEM((MOE_R, D_MODEL), BF16),
                pltpu.VMEM((nt, MOE_R, MOE_TF), BF16),
                pltpu.VMEM((MOE_R * ROW_SLAB, LANES), F32),
                pltpu.VMEM((D_MODEL, MOE_TF1), BF16),
                pltpu.VMEM((D_MODEL, MOE_TF1), BF16),
                pltpu.VMEM((D_FF, MOE_TF), BF16),
                pltpu.SemaphoreType.DMA((2,)), pltpu.SemaphoreType.DMA(()),
            ],
        ),
        out_shape=jax.ShapeDtypeStruct((n_rows, LANES), F32),
        compiler_params=_cparams(("arbitrary", "arbitrary"), MOE_VMEM_LIMIT_BYTES),
        name="moe_experts",
    )(item_e, item_row, item_nsub, tail, xs, w_up, w_up, b_up3, w_down, b_down3)


def _combine_body(slot_ref, slotn_ref, ys_ref, x1_ref, gate_ref, g_ref, b_ref, yp_ref, ysm_ref, buf_ref,
                  v_ref, sem, *, tm, n_ptiles):
    i = pl.program_id(0)
    cur = lax.rem(i, 2)

    def gather(table_ref, bslot):
        def issue(r, _):
            for kk in range(TOP_K):
                src0 = pl.multiple_of(table_ref[0, r * TOP_K + kk] * ROW_SLAB, ROW_SLAB)
                dst0 = pl.multiple_of(r * ROW_SLAB, ROW_SLAB)
                pltpu.make_async_copy(ys_ref.at[pl.ds(src0, ROW_SLAB)],
                                      buf_ref.at[bslot, kk, pl.ds(dst0, ROW_SLAB)],
                                      sem.at[bslot]).start(priority=kk % 2)
            return 0

        lax.fori_loop(0, tm, issue, 0, unroll=8)

    @pl.when(i == 0)
    def _():
        gather(slot_ref, 0)

    @pl.when(i + 1 < pl.num_programs(0))
    def _():
        gather(slotn_ref, 1 - cur)

    for kk in range(TOP_K):
        pltpu.make_async_copy(ys_ref.at[pl.ds(0, tm * ROW_SLAB)], buf_ref.at[cur, kk], sem.at[cur]).wait()
    gates = gate_ref[...]
    for c in range(ROW_SLAB):
        acc = DN_ALPHA * _load_slab_chunk(x1_ref, 0, tm, c)
        for kk in range(TOP_K):
            acc = acc + gates[:, kk:kk + 1] * _load_slab_chunk(buf_ref.at[cur, kk], 0, tm, c)
        v_ref[:, c * LANES:(c + 1) * LANES] = acc
    y = _layernorm_rows(v_ref[...], g_ref[...], b_ref[...])

    @pl.when(pl.program_id(0) < n_ptiles)
    def _():
        yp_ref[...] = y

    @pl.when(pl.program_id(0) >= n_ptiles)
    def _():
        ysm_ref[...] = y


def _combine(ys, slot_tiles, x1, gates, g_row, b_row, mp):
    m = x1.shape[0] // ROW_SLAB
    tm = TOK_TM
    assert mp % tm == 0 and (m - mp) % tm == 0
    n_pt = mp // tm
    wspec = lambda shape: pl.BlockSpec(shape, lambda i: (0, 0))
    n_t = m // tm
    return pl.pallas_call(
        functools.partial(_combine_body, tm=tm, n_ptiles=n_pt),
        grid=(n_t,),
        in_specs=[
            pl.BlockSpec((None, 1, tm * TOP_K), lambda i: (i, 0, 0), memory_space=pltpu.SMEM),
            pl.BlockSpec((None, 1, tm * TOP_K), lambda i: (jnp.minimum(i + 1, n_t - 1), 0, 0),
                         memory_space=pltpu.SMEM),
            pl.BlockSpec(memory_space=pl.ANY),
            pl.BlockSpec((tm * ROW_SLAB, LANES), lambda i: (i, 0)),
            pl.BlockSpec((tm, LANES), lambda i: (i, 0)),
            wspec((1, D_MODEL)), wspec((1, D_MODEL)),
        ],
        out_specs=[pl.BlockSpec((tm, D_MODEL), lambda i: (jnp.minimum(i, n_pt - 1), 0)),
                   pl.BlockSpec((tm, D_MODEL), lambda i: (jnp.maximum(i - n_pt, 0), 0))],
        out_shape=[jax.ShapeDtypeStruct((mp, D_MODEL), F32),
                   jax.ShapeDtypeStruct((m - mp, D_MODEL), F32)],
        scratch_shapes=[pltpu.VMEM((2, TOP_K, tm * ROW_SLAB, LANES), F32),
                        pltpu.VMEM((tm, D_MODEL), F32), pltpu.SemaphoreType.DMA((2,))],
        compiler_params=_cparams(("arbitrary",)),
        name="moe_combine_ln",
    )(slot_tiles, slot_tiles, ys, x1, gates, g_row, b_row)


def _routing_tables(top_i, m):
    e_ids = jnp.arange(N_EXPERTS, dtype=jnp.int32)
    onehot = (top_i[:, :, None] == e_ids[None, None, :]).astype(jnp.int32)
    mask = jnp.sum(onehot, axis=1)
    incl = jnp.cumsum(mask, axis=0)
    pos = incl - mask
    counts = incl[-1]
    npb = (counts + MOE_PB - 1) // MOE_PB
    padded = npb * MOE_PB
    gend = jnp.cumsum(padded)
    gstart = gend - padded
    slot = jnp.sum(onehot * (gstart[None, None, :] + pos[:, None, :]), axis=2)
    n_rows = (m * TOP_K // MOE_PB + N_EXPERTS) * MOE_PB + MOE_PB
    tail = jnp.stack([gend[-1], (n_rows - gend[-1]) // MOE_PB]).astype(jnp.int32)
    zrow = jnp.concatenate([jnp.where(counts > 0, gend - MOE_PB, -1).astype(jnp.int32), tail])
    n_items = N_EXPERTS + (m * TOP_K // MOE_PB + N_EXPERTS) // MOE_NPB
    ipe = (npb + MOE_NPB - 1) // MOE_NPB
    iend = jnp.cumsum(ipe)
    istart = iend - ipe
    total = iend[-1]
    wid = jnp.arange(n_items, dtype=jnp.int32)
    wclamp = jnp.minimum(wid, total - 1)
    ie = jnp.minimum(jnp.searchsorted(iend, wclamp, side="right"), N_EXPERTS - 1).astype(jnp.int32)
    jn = wclamp - istart[ie]
    irow = (gstart[ie] + jn * MOE_R).astype(jnp.int32)
    insub = jnp.where(wid < total, jnp.clip(npb[ie] - jn * MOE_NPB, 0, MOE_NPB), 0).astype(jnp.int32)
    return slot.astype(jnp.int32), zrow, tail, total.astype(jnp.int32), ie, irow, insub, n_rows


def kernel(x_prompt, x_sample, state_conv_a, state_delta, state_conv_b, state_lru, w_in, conv_a_w,
           a_log, dt_bias, norm_a_w, conv_b_w, conv_b_b, lru_wa, lru_ba, lru_wx, lru_bx, lru_lambda,
           norm_b_w, w_out, ln1_g, ln1_b, w_router, b_router, w_up, b_up, w_down, b_down, ln2_g, ln2_b):
    bp, tp, _ = x_prompt.shape
    bs, ts, _ = x_sample.shape
    mp, ms = bp * tp, bs * ts
    m = mp + ms
    l = 0
    xp2 = x_prompt.reshape(mp, D_MODEL)
    xs2 = x_sample.reshape(ms, D_MODEL)

    wp = _wprep(jnp.transpose(w_in[l]))
    wo = w_out[l].astype(BF16)
    pad_h = lambda v: jnp.zeros((1, LANES), F32).at[0, H_A:2 * H_A].set(v)
    alog_row = pad_h(a_log[l])
    dtb_row = pad_h(dt_bias[l])
    row = lambda v: v.reshape(1, -1)

    proj = _inproj(xp2, xs2, wp)

    qkv_act, xbc, bg = _convact_prompt(proj, bp, tp, conv_a_w[l], conv_b_w[l], row(conv_b_b[l]),
                                       alog_row, dtb_row)
    u, w, qe, kdt, aqk, el = _delta_prep(qkv_act, bg, bp, tp)
    oa_p, sd_p = _delta_seq(u, w, qe, kdt, aqk, el, proj, row(norm_a_w[l]), bp, tp)
    ob_p, h_p = _lru_prompt(xbc, proj, bp, tp, lru_wa[l], lru_wx[l], row(lru_ba[l]), row(lru_bx[l]),
                            row(lru_lambda[l]), row(norm_b_w[l]))
    oa_s, sd_s = _sample_a(proj, mp, bs, ts, state_conv_a[l], state_delta[l], conv_a_w[l],
                           alog_row, dtb_row, row(norm_a_w[l]))
    ob_s, h_s = _lru_sample(proj, mp, bs, ts, state_conv_b[l], state_lru[l], conv_b_w[l],
                            row(conv_b_b[l]), lru_wa[l], lru_wx[l], row(lru_ba[l]), row(lru_bx[l]),
                            row(lru_lambda[l]), row(norm_b_w[l]))

    x1, ti, tg = _outproj(oa_p, oa_s, ob_p, ob_s, xp2, xs2, wo, row(ln1_g[l]), row(ln1_b[l]),
                          w_router[l], row(b_router[l]))

    slot, zrow, tail, n_used, ie, irow, insub, n_rows = _routing_tables(ti[:, :TOP_K], m)
    slot_tiles = slot.reshape(m // TOK_TM, 1, TOK_TM * TOP_K)
    xs_sorted = _dispatch(x1, slot_tiles, zrow, n_rows)
    ys = _moe(xs_sorted, n_used, ie, irow, insub, tail, w_up[l],
              b_up[l].reshape(N_EXPERTS, 2 * MOE_NT1, MOE_TF1), w_down[l],
              b_down[l].reshape(N_EXPERTS, MOE_NT, MOE_TF))
    y_p, y_s = _combine(ys, slot_tiles, x1, tg, row(ln2_g[l]), row(ln2_b[l]), mp)

    y_prompt = y_p.reshape(bp, tp, D_MODEL)
    y_sample = y_s.reshape(bs, ts, D_MODEL)
    nh = CONV_W - 1
    assert tp % SUBLANES == 0 and ts == SUBLANES and nh <= SUBLANES
    pg = proj.reshape(m // SUBLANES, SUBLANES, N_PROJ)
    gp = tp // SUBLANES

    def last_rows(g0, g1, gstep, c0, width):
        return lax.slice(pg, (g0, SUBLANES - nh, c0), (g1, SUBLANES, c0 + width), (gstep, 1, 1))

    ca_p = last_rows(gp - 1, mp // SUBLANES, gp, COL_QKV, 3 * D_A)
    cb_p = last_rows(gp - 1, mp // SUBLANES, gp, COL_XB, D_B)
    ca_s = last_rows(mp // SUBLANES, m // SUBLANES, 1, COL_QKV, 3 * D_A)
    cb_s = last_rows(mp // SUBLANES, m // SUBLANES, 1, COL_XB, D_B)
    return (y_prompt, y_sample,
            ca_p[None], sd_p[None], cb_p[None], h_p.reshape(1, bp, D_B),
            ca_s[None], sd_s[None], cb_s[None], h_s[None])
```

```python
import functools
import math

import jax
import jax.numpy as jnp
from jax import lax
from jax.experimental import pallas as pl
from jax.experimental.pallas import tpu as pltpu

F32 = jnp.float32
BF16 = jnp.bfloat16
HIGHEST = lax.Precision.HIGHEST

D_MODEL = 2048
D_A = 1024
H_A = 8
DK = 128
DV = 128
CONV_W = 4
D_B = 1024
NB = 8
BW = 128
LRU_C = 8.0
N_EXPERTS = 32
TOP_K = 4
D_FF = 2048
SWIGLU_LIMIT = 7.0
SWIGLU_ALPHA = 1.702
DEPTH = 1
DN_ALPHA = (2.0 * DEPTH) ** 0.25
LN_EPS = 1e-5
RMS_EPS = 1e-6
L2_EPS = 1e-6

LANES = 128
SUBLANES = 8
VMEM_LIMIT_BYTES = 56 * 1024 * 1024

COL_QKV = 0
COL_Z = 3 * D_A
COL_XB = 4 * D_A
COL_YB = 4 * D_A + D_B
COL_BA = 4 * D_A + 2 * D_B
N_PROJ = COL_BA + 2 * LANES

INPROJ_TM = 1024
INPROJ_TN = 1280
CONV_L = 256
DELTA_C = 128
OUT_TM = 512
MOE_PB = 128
MOE_RB = 256
MOE_SB = 6
MOE_R = MOE_RB * MOE_SB
MOE_NPB = MOE_R // MOE_PB
MOE_EXACT_ODD = (7, 9, 11)
MOE_TF = 256
MOE_NT = D_FF // MOE_TF
MOE_TF1 = 256
MOE_NT1 = D_FF // MOE_TF1
MOE_VMEM_LIMIT_BYTES = VMEM_LIMIT_BYTES
SAMPLE_NSEQ = 4
TOK_TM = 256
ROW_SLAB = D_MODEL // LANES


def _cparams(sem, vmem=VMEM_LIMIT_BYTES):
    return pltpu.CompilerParams(dimension_semantics=sem, vmem_limit_bytes=vmem)


def _sigmoid(x):
    return 1.0 / (1.0 + jnp.exp(-x))


def _softplus(x):
    return jnp.maximum(x, 0.0) + jnp.log(1.0 + jnp.exp(-jnp.abs(x)))


def _wprep_body(w_ref, o_ref):
    c = 4 * D_A
    nba = 2 * H_A
    o_ref[0:c, :] = w_ref[0:c, :].astype(BF16)
    o_ref[c:c + 2 * D_B, :] = w_ref[c + nba:c + nba + 2 * D_B, :].astype(BF16)
    o_ref[COL_BA:COL_BA + nba, :] = w_ref[c:c + nba, :].astype(BF16)
    o_ref[COL_BA + nba:N_PROJ, :] = jnp.zeros((N_PROJ - COL_BA - nba, o_ref.shape[1]), BF16)


def _wprep(w_t):
    cols = 256
    d_in = w_t.shape[0]
    assert d_in == 4 * D_A + 2 * H_A + 2 * D_B and D_MODEL % cols == 0
    return pl.pallas_call(
        _wprep_body,
        grid=(D_MODEL // cols,),
        in_specs=[pl.BlockSpec((d_in, cols), lambda i: (0, i))],
        out_specs=pl.BlockSpec((N_PROJ, cols), lambda i: (0, i)),
        out_shape=jax.ShapeDtypeStruct((N_PROJ, D_MODEL), BF16),
        compiler_params=_cparams(("arbitrary",)),
        name="wprep",
    )(w_t)


def _inproj_body(xp_ref, xs_ref, w_ref, o_ref, xb_ref, *, n_ptiles):
    i = pl.program_id(0)

    @pl.when(pl.program_id(1) == 0)
    def _():
        @pl.when(i < n_ptiles)
        def _():
            xb_ref[...] = xp_ref[...].astype(BF16)

        @pl.when(i >= n_ptiles)
        def _():
            xb_ref[...] = xs_ref[...].astype(BF16)

    o_ref[...] = _dot_nt(xb_ref[...], w_ref[...])


def _inproj(xp2, xs2, wp):
    mp, ms = xp2.shape[0], xs2.shape[0]
    tm = min(INPROJ_TM, ms)
    assert mp % tm == 0 and ms % tm == 0
    n_pt, n_st = mp // tm, ms // tm
    n_nt = N_PROJ // INPROJ_TN
    return pl.pallas_call(
        functools.partial(_inproj_body, n_ptiles=n_pt),
        grid=(n_pt + n_st, n_nt),
        in_specs=[
            pl.BlockSpec((tm, D_MODEL), lambda i, j: (jnp.minimum(i, n_pt - 1), 0)),
            pl.BlockSpec((tm, D_MODEL), lambda i, j: (jnp.maximum(i - n_pt, 0), 0),
                         pipeline_mode=pl.Buffered(1)),
            pl.BlockSpec((INPROJ_TN, D_MODEL), lambda i, j: (j, 0)),
        ],
        out_specs=pl.BlockSpec((tm, INPROJ_TN), lambda i, j: (i, j)),
        out_shape=jax.ShapeDtypeStruct((mp + ms, N_PROJ), F32),
        scratch_shapes=[pltpu.VMEM((tm, D_MODEL), BF16)],
        compiler_params=_cparams(("arbitrary", "arbitrary")),
        name="inproj",
    )(xp2, xs2, wp)


def _conv_taps(h_ref, w_ref, c0, width, rows, base):
    acc = h_ref[base:base + rows, c0:c0 + width] * w_ref[CONV_W - 1:CONV_W, c0:c0 + width]
    for j in range(1, CONV_W):
        acc = acc + (h_ref[base - j:base - j + rows, c0:c0 + width]
                     * w_ref[CONV_W - 1 - j:CONV_W - j, c0:c0 + width])
    return acc


def _qkv_activation(acc, blk):
    a = acc * _sigmoid(acc)
    if blk < 2 * H_A:
        a = a * lax.rsqrt(jnp.sum(a * a, axis=-1, keepdims=True) + L2_EPS)
        if blk < H_A:
            a = a * (DK ** -0.5)
    return a


def _beta_g(ba, alog_ref, dtb_ref):
    lane = lax.broadcasted_iota(jnp.int32, ba.shape, 1)
    beta = _sigmoid(ba)
    g = -jnp.exp(alog_ref[...]) * _softplus(ba + dtb_ref[...])
    return jnp.where(lane < H_A, beta, jnp.where(lane < 2 * H_A, g, 0.0))


def _convact_prompt_body(qkv_ref, xb_ref, ba_ref, caw_ref, cbw_ref, cbb_ref, alog_ref, dtb_ref,
                         qkvo_ref, xbo_ref, bgo_ref, hq_ref, hx_ref, *, rows):
    t = pl.program_id(1)
    hdr = SUBLANES

    @pl.when(t == 0)
    def _():
        hq_ref[0:hdr, :] = jnp.zeros((hdr, 3 * D_A), F32)
        hx_ref[0:hdr, :] = jnp.zeros((hdr, D_B), F32)

    @pl.when(t > 0)
    def _():
        hq_ref[0:hdr, :] = hq_ref[rows:rows + hdr, :]
        hx_ref[0:hdr, :] = hx_ref[rows:rows + hdr, :]

    hq_ref[hdr:hdr + rows, :] = qkv_ref[...]
    hx_ref[hdr:hdr + rows, :] = xb_ref[...]
    for blk in range(3 * H_A):
        c0 = blk * DK
        acc = _conv_taps(hq_ref, caw_ref, c0, DK, rows, hdr)
        qkvo_ref[:, c0:c0 + DK] = _qkv_activation(acc, blk)
    for blk in range(NB):
        c0 = blk * BW
        xbo_ref[:, c0:c0 + BW] = _conv_taps(hx_ref, cbw_ref, c0, BW, rows, hdr) + cbb_ref[:, c0:c0 + BW]
    bgo_ref[...] = _beta_g(ba_ref[...], alog_ref, dtb_ref)


def _convact_prompt(proj, bsz, t_len, caw, cbw, cbb, alog_row, dtb_row):
    rows = min(CONV_L, t_len)
    assert t_len % rows == 0
    nt = t_len // rows
    mp = bsz * t_len
    wspec = lambda shape: pl.BlockSpec(shape, lambda b, t: (0, 0))
    return pl.pallas_call(
        functools.partial(_convact_prompt_body, rows=rows),
        grid=(bsz, nt),
        in_specs=[
            pl.BlockSpec((rows, 3 * D_A), lambda b, t: (b * nt + t, COL_QKV // (3 * D_A))),
            pl.BlockSpec((rows, D_B), lambda b, t: (b * nt + t, COL_XB // D_B)),
            pl.BlockSpec((rows, LANES), lambda b, t: (b * nt + t, COL_BA // LANES)),
            wspec((CONV_W, 3 * D_A)), wspec((CONV_W, D_B)), wspec((1, D_B)),
            wspec((1, LANES)), wspec((1, LANES)),
        ],
        out_specs=[
            pl.BlockSpec((rows, 3 * D_A), lambda b, t: (b * nt + t, 0)),
            pl.BlockSpec((rows, D_B), lambda b, t: (b * nt + t, 0)),
            pl.BlockSpec((rows, LANES), lambda b, t: (b * nt + t, 0)),
        ],
        out_shape=[
            jax.ShapeDtypeStruct((mp, 3 * D_A), F32),
            jax.ShapeDtypeStruct((mp, D_B), F32),
            jax.ShapeDtypeStruct((mp, LANES), F32),
        ],
        scratch_shapes=[pltpu.VMEM((SUBLANES + rows + SUBLANES, 3 * D_A), F32),
                        pltpu.VMEM((SUBLANES + rows + SUBLANES, D_B), F32)],
        compiler_params=_cparams(("arbitrary", "arbitrary")),
        name="convact_prompt",
    )(proj, proj, proj, caw, cbw, cbb, alog_row, dtb_row)


def _dot_nt(a, b, precision=None):
    return lax.dot_general(a, b, (((1,), (1,)), ((), ())), precision=precision,
                           preferred_element_type=F32)


def _split_bf16(x):
    hi = x.astype(BF16)
    return hi, (x - hi.astype(F32)).astype(BF16)


def _dot3(a, b):
    ah, al = _split_bf16(a)
    bh, bl = _split_bf16(b)
    return (jnp.dot(ah, bh, preferred_element_type=F32)
            + (jnp.dot(al, bh, preferred_element_type=F32) + jnp.dot(ah, bl, preferred_element_type=F32)))


def _delta_prep_body(qkv_ref, bg_ref, u_ref, w_ref, qe_ref, kdt_ref, aqk_ref, el_ref, *, c):
    row = lax.broadcasted_iota(jnp.int32, (c, c), 0)
    col = lax.broadcasted_iota(jnp.int32, (c, c), 1)
    incl = row >= col
    strict = row > col
    eye = (row == col).astype(F32)
    tril = incl.astype(F32)
    eye_l = (lax.broadcasted_iota(jnp.int32, (LANES, LANES), 0)
             == lax.broadcasted_iota(jnp.int32, (LANES, LANES), 1)).astype(F32)

    bg = bg_ref[...]
    gc_all = jnp.dot(tril, bg, precision=HIGHEST, preferred_element_type=F32)
    gc_t = _dot_nt(eye_l, gc_all, precision=HIGHEST)
    heads = range(H_A)
    lmats, rhss = [], []
    for h in heads:
        q = qkv_ref[:, h * DK:(h + 1) * DK]
        k = qkv_ref[:, (H_A + h) * DK:(H_A + h + 1) * DK]
        v = qkv_ref[:, (2 * H_A + h) * DK:(2 * H_A + h + 1) * DK]
        beta = bg[:, h:h + 1]
        gc = gc_all[:, H_A + h:H_A + h + 1]
        g_last = gc_all[c - 1:c, H_A + h:H_A + h + 1]
        diff = gc - gc_t[H_A + h:H_A + h + 1, :]
        decay = jnp.where(incl, jnp.exp(jnp.where(incl, diff, 0.0)), 0.0)
        kb = k * beta
        egc = jnp.exp(gc)
        kbf = k.astype(BF16)
        kk = _dot_nt(kbf, kbf)
        lmats.append(jnp.where(strict, kk * beta * decay, 0.0))
        rhss.append(jnp.concatenate([v * beta, kb * egc], axis=1))
        qe_ref[:, h * DK:(h + 1) * DK] = (q * egc).astype(BF16)
        aqk_ref[h] = (_dot_nt(q.astype(BF16), kbf) * decay).astype(BF16)
        kdec = k * jnp.exp(g_last - gc)
        kdt_ref[h] = kdec.astype(BF16)
        el_ref[h] = jnp.broadcast_to(jnp.exp(g_last), (SUBLANES, LANES))
    xbs = [(-lm).astype(BF16) for lm in lmats]
    t0s = [eye - lm for lm in lmats]
    p = 2
    while p < c:
        xbs = [jnp.dot(xb, xb, preferred_element_type=F32).astype(BF16) for xb in xbs]
        t0s = [t0 + jnp.dot(t0.astype(BF16), xb, preferred_element_type=F32) for t0, xb in zip(t0s, xbs)]
        p *= 2
    t0bs = [t0.astype(BF16) for t0 in t0s]
    sol0s = [jnp.dot(t0b, rhs.astype(BF16), preferred_element_type=F32) for t0b, rhs in zip(t0bs, rhss)]
    resids = [rhs - sol0 - _dot3(lm, sol0) for rhs, sol0, lm in zip(rhss, sol0s, lmats)]
    for h in heads:
        sol = sol0s[h] + jnp.dot(t0bs[h], resids[h].astype(BF16), preferred_element_type=F32)
        u_ref[:, h * DV:(h + 1) * DV] = sol[:, :DV]
        w_ref[:, h * DK:(h + 1) * DK] = sol[:, DV:].astype(BF16)


def _delta_prep(qkv_act, bg, bsz, t_len):
    c = DELTA_C
    assert t_len % c == 0
    nc = t_len // c
    mp = bsz * t_len
    n_chunks = bsz * nc
    rowspec = lambda w: pl.BlockSpec((c, w), lambda i: (i, 0))
    return pl.pallas_call(
        functools.partial(_delta_prep_body, c=c),
        grid=(n_chunks,),
        in_specs=[rowspec(3 * D_A), rowspec(LANES)],
        out_specs=[
            rowspec(D_A), rowspec(D_A), rowspec(D_A),
            pl.BlockSpec((None, H_A, c, DK), lambda i: (i, 0, 0, 0)),
            pl.BlockSpec((None, H_A, c, c), lambda i: (i, 0, 0, 0)),
            pl.BlockSpec((None, H_A, SUBLANES, LANES), lambda i: (i, 0, 0, 0)),
        ],
        out_shape=[
            jax.ShapeDtypeStruct((mp, D_A), F32),
            jax.ShapeDtypeStruct((mp, D_A), BF16),
            jax.ShapeDtypeStruct((mp, D_A), BF16),
            jax.ShapeDtypeStruct((n_chunks, H_A, c, DK), BF16),
            jax.ShapeDtypeStruct((n_chunks, H_A, c, c), BF16),
            jax.ShapeDtypeStruct((n_chunks, H_A, SUBLANES, LANES), F32),
        ],
        compiler_params=_cparams(("arbitrary",)),
        name="delta_prep",
    )(qkv_act, bg)


def _gated_rmsnorm(o, z, nw):
    on = o * lax.rsqrt(jnp.mean(o * o, axis=-1, keepdims=True) + RMS_EPS) * nw
    return on * (z * _sigmoid(z))


def _delta_seq_body(*refs, bsz):
    u_ref, w_ref, qe_ref, kdt_ref, aqk_ref, el_ref = refs[:6]
    z_refs = refs[6:6 + bsz]
    nw_ref, oa_ref, sfin_ref, s_ref = refs[6 + bsz:]
    ci = pl.program_id(0)

    @pl.when(ci == 0)
    def _():
        s_ref[...] = jnp.zeros(s_ref.shape, F32)

    nw = nw_ref[...]
    pairs = [(b, h) for b in range(bsz) for h in range(H_A)]
    sl = lambda h: slice(h * DK, (h + 1) * DK)
    ss = [s_ref[b, h] for b, h in pairs]
    sbs = [s.astype(BF16) for s in ss]
    vbs = [(u_ref[b, :, sl(h)] - jnp.dot(w_ref[b, :, sl(h)], sb, preferred_element_type=F32)).astype(BF16)
           for (b, h), sb in zip(pairs, sbs)]
    for p, (b, h) in enumerate(pairs):
        s_ref[b, h] = ss[p] * el_ref[b, h][0:1, 0:1] + lax.dot_general(
            kdt_ref[b, h], vbs[p], (((0,), (0,)), ((), ())), preferred_element_type=F32)
    for p, (b, h) in enumerate(pairs):
        o = (jnp.dot(qe_ref[b, :, sl(h)], sbs[p], preferred_element_type=F32)
             + jnp.dot(aqk_ref[b, h], vbs[p], preferred_element_type=F32))
        oa_ref[b, :, sl(h)] = _gated_rmsnorm(o, z_refs[b][:, sl(h)], nw).astype(BF16)

    @pl.when(ci == pl.num_programs(0) - 1)
    def _():
        sfin_ref[...] = s_ref[...]


def _delta_seq(u, w, qe, kdt, aqk, el, proj, nw_row, bsz, t_len):
    c = DELTA_C
    nc = t_len // c
    mp = bsz * t_len
    rows4 = lambda a: a.reshape(bsz, nc, c, a.shape[-1])
    chunk5 = lambda a: a.reshape((bsz, nc) + a.shape[1:])
    rowspec = pl.BlockSpec((bsz, None, c, D_A), lambda i: (0, i, 0, 0))
    chunkspec = lambda a, bb: pl.BlockSpec((bsz, None, H_A, a, bb), lambda i: (0, i, 0, 0, 0))
    zspecs = [pl.BlockSpec((c, D_A), functools.partial(lambda i, b: (b * nc + i, COL_Z // D_A), b=b))
              for b in range(bsz)]
    oa, sfin = pl.pallas_call(
        functools.partial(_delta_seq_body, bsz=bsz),
        grid=(nc,),
        in_specs=[rowspec, rowspec, rowspec,
                  chunkspec(c, DK), chunkspec(c, c), chunkspec(SUBLANES, LANES)]
                 + zspecs + [pl.BlockSpec((1, DV), lambda i: (0, 0))],
        out_specs=[
            rowspec,
            pl.BlockSpec((bsz, H_A, DK, DV), lambda i: (0, 0, 0, 0)),
        ],
        out_shape=[
            jax.ShapeDtypeStruct((bsz, nc, c, D_A), BF16),
            jax.ShapeDtypeStruct((bsz, H_A, DK, DV), F32),
        ],
        scratch_shapes=[pltpu.VMEM((bsz, H_A, DK, DV), F32)],
        compiler_params=_cparams(("arbitrary",)),
        name="delta_seq",
    )(rows4(u), rows4(w), rows4(qe), chunk5(kdt), chunk5(aqk), chunk5(el), *([proj] * bsz), nw_row)
    return oa.reshape(mp, D_A), sfin


def _sample_a_body(qkv_ref, z_ref, ba_ref, hist_ref, s0_ref, caw_ref, alog_ref, dtb_ref, nw_ref,
                   oa_ref, s1_ref, hq_ref, *, t_len, nseq):
    hdr = SUBLANES
    nh = CONV_W - 1
    rows = [slice(si * t_len, (si + 1) * t_len) for si in range(nseq)]
    for si in range(nseq):
        hq_ref[si, 0:hdr, :] = jnp.zeros((hdr, 3 * D_A), F32)
        hq_ref[si, hdr - nh:hdr, :] = hist_ref[si]
        hq_ref[si, hdr:hdr + t_len, :] = qkv_ref[rows[si], :]
    bg = _beta_g(ba_ref[...], alog_ref, dtb_ref)
    rloc = lax.rem(lax.broadcasted_iota(jnp.int32, bg.shape, 0), t_len)
    gc_all = bg
    sft = 1
    while sft < t_len:
        gc_all = gc_all + jnp.where(rloc >= sft, pltpu.roll(gc_all, sft, axis=0), 0.0)
        sft *= 2
    nw = nw_ref[...]
    rowc = lax.broadcasted_iota(jnp.int32, (t_len, 1), 0)
    pairs = [(si, h) for si in range(nseq) for h in range(H_A)]
    npair = range(len(pairs))
    act = lambda si, blk: _qkv_activation(_conv_taps(hq_ref.at[si], caw_ref, blk * DK, DK, t_len, hdr), blk)
    qs = [act(si, h) for si, h in pairs]
    ks = [act(si, H_A + h) for si, h in pairs]
    vs = [act(si, 2 * H_A + h) for si, h in pairs]
    betas = [bg[rows[si], h:h + 1] for si, h in pairs]
    gcs = [gc_all[rows[si], H_A + h:H_A + h + 1] for si, h in pairs]
    glasts = [gc[t_len - 1:t_len, :] for gc in gcs]
    kbs = [k * b for k, b in zip(ks, betas)]
    egcs = [jnp.exp(gc) for gc in gcs]
    sols = [jnp.concatenate([v * b, kb * e], axis=1) for v, b, kb, e in zip(vs, betas, kbs, egcs)]
    acols = [[] for _ in npair]
    lcols = [[] for _ in npair]
    for j in range(t_len):
        for p in npair:
            kj = ks[p][j:j + 1, :]
            dcol = jnp.exp(jnp.where(rowc >= j, gcs[p] - gcs[p][j:j + 1, :], 0.0))
            acols[p].append(jnp.where(rowc >= j, jnp.sum(qs[p] * kj, axis=-1, keepdims=True) * dcol, 0.0))
            lcols[p].append(jnp.where(rowc > j, jnp.sum(kbs[p] * kj, axis=-1, keepdims=True) * dcol, 0.0))
    for j in range(t_len - 1):
        sols = [sol - lcols[p][j] * sol[j:j + 1, :] for p, sol in enumerate(sols)]
    ss = [s0_ref[si, h] for si, h in pairs]
    sbs = [s.astype(BF16) for s in ss]
    v_news = [sol[:, :DV] - jnp.dot(sol[:, DV:].astype(BF16), sb, preferred_element_type=F32)
              for sol, sb in zip(sols, sbs)]
    os_ = [jnp.dot((q * e).astype(BF16), sb, preferred_element_type=F32) for q, e, sb in zip(qs, egcs, sbs)]
    for j in range(t_len):
        os_ = [o + acols[p][j] * v_news[p][j:j + 1, :] for p, o in enumerate(os_)]
    upds = [lax.dot_general((k * jnp.exp(gl - gc)).astype(BF16), vn.astype(BF16), (((0,), (0,)), ((), ())),
                            preferred_element_type=F32)
            for k, gl, gc, vn in zip(ks, glasts, gcs, v_news)]
    for p, (si, h) in enumerate(pairs):
        s1_ref[si, h] = ss[p] * jnp.exp(glasts[p]) + upds[p]
        sl = slice(h * DV, (h + 1) * DV)
        oa_ref[rows[si], sl] = _gated_rmsnorm(os_[p], z_ref[rows[si], sl], nw).astype(BF16)


def _sample_a(proj, row0, bsz, t_len, hist, s0, caw, alog_row, dtb_row, nw_row):
    nseq = SAMPLE_NSEQ
    tr = nseq * t_len
    assert t_len == SUBLANES and row0 % tr == 0 and bsz % nseq == 0
    r0 = row0 // tr
    wspec = lambda shape: pl.BlockSpec(shape, lambda b: (0,) * len(shape))
    return pl.pallas_call(
        functools.partial(_sample_a_body, t_len=t_len, nseq=nseq),
        grid=(bsz // nseq,),
        in_specs=[
            pl.BlockSpec((tr, 3 * D_A), lambda b: (r0 + b, COL_QKV // (3 * D_A))),
            pl.BlockSpec((tr, D_A), lambda b: (r0 + b, COL_Z // D_A)),
            pl.BlockSpec((tr, LANES), lambda b: (r0 + b, COL_BA // LANES)),
            pl.BlockSpec((nseq, CONV_W - 1, 3 * D_A), lambda b: (b, 0, 0)),
            pl.BlockSpec((nseq, H_A, DK, DV), lambda b: (b, 0, 0, 0)),
            wspec((CONV_W, 3 * D_A)), wspec((1, LANES)), wspec((1, LANES)), wspec((1, DV)),
        ],
        out_specs=[
            pl.BlockSpec((tr, D_A), lambda b: (b, 0)),
            pl.BlockSpec((nseq, H_A, DK, DV), lambda b: (b, 0, 0, 0)),
        ],
        out_shape=[
            jax.ShapeDtypeStruct((bsz * t_len, D_A), BF16),
            jax.ShapeDtypeStruct((bsz, H_A, DK, DV), F32),
        ],
        scratch_shapes=[pltpu.VMEM((nseq, 2 * SUBLANES, 3 * D_A), F32)],
        compiler_params=_cparams(("arbitrary",)),
        name="sample_a",
    )(proj, proj, proj, hist, s0, caw, alog_row, dtb_row, nw_row)


def _lru_coeffs(xc, wa_ref, wx_ref, lba_ref, lbx_ref, lam_ref):
    parts = []
    for n in range(NB):
        sl = slice(n * BW, (n + 1) * BW)
        xn = xc[:, sl]
        xnb = xn.astype(BF16)
        gr = _sigmoid(jnp.dot(xnb, wa_ref[n].astype(BF16), preferred_element_type=F32) + lba_ref[:, sl])
        gi = _sigmoid(jnp.dot(xnb, wx_ref[n].astype(BF16), preferred_element_type=F32) + lbx_ref[:, sl])
        log_a = -LRU_C * gr * _softplus(-lam_ref[:, sl])
        parts.append((jnp.exp(log_a), jnp.sqrt(1.0 - jnp.exp(2.0 * log_a)), gi * xn))
    return parts


def _group_scan(a, b):
    rowi = lax.broadcasted_iota(jnp.int32, a.shape, 0)
    sft = 1
    while sft < SUBLANES:
        keep = rowi >= sft
        b = b + a * jnp.where(keep, pltpu.roll(b, sft, axis=0), 0.0)
        a = a * jnp.where(keep, pltpu.roll(a, sft, axis=0), 1.0)
        sft *= 2
    return a, b


def _gelu_tanh(x):
    return 0.5 * x * (1.0 + jnp.tanh(math.sqrt(2.0 / math.pi) * (x + 0.044715 * x * x * x)))


def _lru_finish(h, yb, nbw):
    hg = h * _gelu_tanh(yb)
    return (hg * lax.rsqrt(jnp.mean(hg * hg, axis=-1, keepdims=True) + RMS_EPS) * nbw).astype(BF16)


def _lru_prompt_body(xc_ref, yb_ref, wa_ref, wx_ref, lba_ref, lbx_ref, lam_ref, nbw_ref,
                     ob_ref, hfin_ref, a_ref, b_ref, h_ref, carry_ref, *, rows):
    t = pl.program_id(1)
    parts = _lru_coeffs(xc_ref[...], wa_ref, wx_ref, lba_ref, lbx_ref, lam_ref)
    rowi = lax.broadcasted_iota(jnp.int32, (rows, BW), 0)
    first = jnp.logical_and(rowi == 0, t == 0)
    for n in range(NB):
        sl = slice(n * BW, (n + 1) * BW)
        a_n, mult_n, gix_n = parts[n]
        a_ref[:, sl] = jnp.where(first, 0.0, a_n)
        b_ref[:, sl] = jnp.where(first, 1.0, mult_n) * gix_n

    @pl.when(t == 0)
    def _():
        carry_ref[...] = jnp.zeros(carry_ref.shape, F32)

    def group(gidx, carry):
        r0 = pl.multiple_of(gidx * SUBLANES, SUBLANES)
        ag, bg = _group_scan(a_ref[pl.ds(r0, SUBLANES), :], b_ref[pl.ds(r0, SUBLANES), :])
        hg = ag * carry + bg
        h_ref[pl.ds(r0, SUBLANES), :] = hg
        return jnp.broadcast_to(hg[SUBLANES - 1:SUBLANES, :], hg.shape)

    carry = lax.fori_loop(0, rows // SUBLANES, group, carry_ref[...])
    carry_ref[...] = carry
    ob_ref[...] = _lru_finish(h_ref[...], yb_ref[...], nbw_ref[...])

    @pl.when(t == pl.num_programs(1) - 1)
    def _():
        hfin_ref[...] = carry[0:1, :]


def _lru_prompt(xbc, proj, bsz, t_len, wa, wx, lba, lbx, lam, nbw):
    rows = min(CONV_L, t_len)
    nt = t_len // rows
    mp = bsz * t_len
    wspec = lambda shape: pl.BlockSpec(shape, lambda b, t: (0,) * len(shape))
    return pl.pallas_call(
        functools.partial(_lru_prompt_body, rows=rows),
        grid=(bsz, nt),
        in_specs=[
            pl.BlockSpec((rows, D_B), lambda b, t: (b * nt + t, 0)),
            pl.BlockSpec((rows, D_B), lambda b, t: (b * nt + t, COL_YB // D_B)),
            wspec((NB, BW, BW)), wspec((NB, BW, BW)),
            wspec((1, D_B)), wspec((1, D_B)), wspec((1, D_B)), wspec((1, D_B)),
        ],
        out_specs=[
            pl.BlockSpec((rows, D_B), lambda b, t: (b * nt + t, 0)),
            pl.BlockSpec((None, 1, D_B), lambda b, t: (b, 0, 0)),
        ],
        out_shape=[
            jax.ShapeDtypeStruct((mp, D_B), BF16),
            jax.ShapeDtypeStruct((bsz, 1, D_B), F32),
        ],
        scratch_shapes=[pltpu.VMEM((rows, D_B), F32), pltpu.VMEM((rows, D_B), F32),
                        pltpu.VMEM((rows, D_B), F32), pltpu.VMEM((SUBLANES, D_B), F32)],
        compiler_params=_cparams(("arbitrary", "arbitrary")),
        name="lru_prompt",
    )(xbc, proj, wa, wx, lba, lbx, lam, nbw)


def _lru_sample_body(xb_ref, yb_ref, hist_ref, h0_ref, cbw_ref, cbb_ref, wa_ref, wx_ref, lba_ref,
                     lbx_ref, lam_ref, nbw_ref, ob_ref, h1_ref, hx_ref, a_ref, b_ref, h_ref,
                     *, nseq, t_len):
    hdr = SUBLANES
    nh = CONV_W - 1
    hx_ref[0:hdr, :] = jnp.zeros((hdr, D_B), F32)

    def conv_seq(si, _):
        r0 = pl.multiple_of(si * t_len, t_len)
        hx_ref[hdr - nh:hdr, :] = hist_ref[si]
        hx_ref[hdr:hdr + t_len, :] = xb_ref[pl.ds(r0, t_len), :]
        for n in range(NB):
            c0 = n * BW
            h_ref[pl.ds(r0, t_len), c0:c0 + BW] = (_conv_taps(hx_ref, cbw_ref, c0, BW, t_len, hdr)
                                                  + cbb_ref[:, c0:c0 + BW])
        return 0

    lax.fori_loop(0, nseq, conv_seq, 0)
    parts = _lru_coeffs(h_ref[...], wa_ref, wx_ref, lba_ref, lbx_ref, lam_ref)
    for n in range(NB):
        sl = slice(n * BW, (n + 1) * BW)
        a_n, mult_n, gix_n = parts[n]
        a_ref[:, sl] = a_n
        b_ref[:, sl] = mult_n * gix_n

    def seq(si, _):
        r0 = pl.multiple_of(si * t_len, t_len)
        ag, bg = _group_scan(a_ref[pl.ds(r0, t_len), :], b_ref[pl.ds(r0, t_len), :])
        hg = ag * h0_ref[pl.ds(si, 1), :] + bg
        h_ref[pl.ds(r0, t_len), :] = hg
        h1_ref[pl.ds(si, 1), :] = hg[t_len - 1:t_len, :]
        return 0

    lax.fori_loop(0, nseq, seq, 0)
    ob_ref[...] = _lru_finish(h_ref[...], yb_ref[...], nbw_ref[...])


def _lru_sample(proj, row0, bsz, t_len, hist, h0, cbw, cbb, wa, wx, lba, lbx, lam, nbw):
    assert t_len == SUBLANES
    ms = bsz * t_len
    assert row0 % ms == 0
    rblk = row0 // ms
    wspec = lambda shape: pl.BlockSpec(shape, lambda i: (0,) * len(shape))
    return pl.pallas_call(
        functools.partial(_lru_sample_body, nseq=bsz, t_len=t_len),
        grid=(1,),
        in_specs=[
            pl.BlockSpec((ms, D_B), lambda i: (rblk, COL_XB // D_B)),
            pl.BlockSpec((ms, D_B), lambda i: (rblk, COL_YB // D_B)),
            wspec((bsz, CONV_W - 1, D_B)), wspec((bsz, D_B)),
            wspec((CONV_W, D_B)), wspec((1, D_B)),
            wspec((NB, BW, BW)), wspec((NB, BW, BW)),
            wspec((1, D_B)), wspec((1, D_B)), wspec((1, D_B)), wspec((1, D_B)),
        ],
        out_specs=[wspec((ms, D_B)), wspec((bsz, D_B))],
        out_shape=[
            jax.ShapeDtypeStruct((ms, D_B), BF16),
            jax.ShapeDtypeStruct((bsz, D_B), F32),
        ],
        scratch_shapes=[pltpu.VMEM((2 * SUBLANES, D_B), F32), pltpu.VMEM((ms, D_B), F32),
                        pltpu.VMEM((ms, D_B), F32), pltpu.VMEM((ms, D_B), F32)],
        compiler_params=_cparams(("arbitrary",)),
        name="lru_sample",
    )(proj, proj, hist, h0, cbw, cbb, wa, wx, lba, lbx, lam, nbw)


def _layernorm_rows(v, g, b):
    mu = jnp.mean(v, axis=-1, keepdims=True)
    d = v - mu
    var = jnp.mean(d * d, axis=-1, keepdims=True)
    return d * lax.rsqrt(var + LN_EPS) * g + b


def _store_slabs(slab_ref, base, val):
    n = val.shape[0]
    for s in range(ROW_SLAB):
        slab_ref[pl.ds(base + s, n, stride=ROW_SLAB), :] = val[:, s * LANES:(s + 1) * LANES]


def _load_slab_chunk(slab_ref, base, n, s):
    return slab_ref[pl.ds(base + s, n, stride=ROW_SLAB), :]


def _outproj_body(oap_ref, oas_ref, obp_ref, obs_ref, xp_ref, xs_ref, wo_ref, g_ref, b_ref,
                  wr_ref, br_ref, x1_ref, ti_ref, tg_ref, *, n_ptiles):
    i = pl.program_id(0)

    def run(oa_ref, ob_ref, x_ref):
        mix = (jnp.dot(oa_ref[...], wo_ref[0:D_A, :], preferred_element_type=F32)
               + jnp.dot(ob_ref[...], wo_ref[D_A:D_A + D_B, :], preferred_element_type=F32))
        y = _layernorm_rows(DN_ALPHA * x_ref[...] + mix, g_ref[...], b_ref[...])
        _store_slabs(x1_ref, 0, y)
        logits = _dot3(y, wr_ref[...]) + br_ref[...]
        lane = lax.broadcasted_iota(jnp.int32, logits.shape, 1)
        lane_o = lax.broadcasted_iota(jnp.int32, ti_ref.shape, 1)
        cur = logits
        ti = jnp.zeros(ti_ref.shape, jnp.int32)
        tv = jnp.zeros(tg_ref.shape, F32)
        v0 = None
        den = None
        for kk in range(TOP_K):
            m = jnp.max(cur, axis=-1, keepdims=True)
            idx = jnp.min(jnp.where(cur == m, lane, N_EXPERTS), axis=-1, keepdims=True)
            cur = jnp.where(lane == idx, -jnp.inf, cur)
            if kk == 0:
                v0 = m
            e = jnp.exp(m - v0)
            den = e if den is None else den + e
            ti = jnp.where(lane_o == kk, idx, ti)
            tv = jnp.where(lane_o == kk, e, tv)
        ti_ref[...] = ti
        tg_ref[...] = tv / den

    @pl.when(i < n_ptiles)
    def _():
        run(oap_ref, obp_ref, xp_ref)

    @pl.when(i >= n_ptiles)
    def _():
        run(oas_ref, obs_ref, xs_ref)


def _outproj(oa_p, oa_s, ob_p, ob_s, xp2, xs2, wo, g_row, b_row, wr, br_row):
    mp, ms = xp2.shape[0], xs2.shape[0]
    tm = min(OUT_TM, ms)
    assert mp % tm == 0 and ms % tm == 0
    n_pt, n_st = mp // tm, ms // tm
    pmap = lambda i: (jnp.minimum(i, n_pt - 1), 0)
    smap = lambda i: (jnp.maximum(i - n_pt, 0), 0)
    wspec = lambda shape: pl.BlockSpec(shape, lambda i: (0, 0))
    m = mp + ms
    return pl.pallas_call(
        functools.partial(_outproj_body, n_ptiles=n_pt),
        grid=(n_pt + n_st,),
        in_specs=[
            pl.BlockSpec((tm, D_A), pmap), pl.BlockSpec((tm, D_A), smap),
            pl.BlockSpec((tm, D_B), pmap), pl.BlockSpec((tm, D_B), smap),
            pl.BlockSpec((tm, D_MODEL), pmap), pl.BlockSpec((tm, D_MODEL), smap),
            wspec((D_A + D_B, D_MODEL)), wspec((1, D_MODEL)), wspec((1, D_MODEL)),
            wspec((D_MODEL, N_EXPERTS)), wspec((1, N_EXPERTS)),
        ],
        out_specs=[
            pl.BlockSpec((tm * ROW_SLAB, LANES), lambda i: (i, 0)),
            pl.BlockSpec((tm, LANES), lambda i: (i, 0)),
            pl.BlockSpec((tm, LANES), lambda i: (i, 0)),
        ],
        out_shape=[
            jax.ShapeDtypeStruct((m * ROW_SLAB, LANES), F32),
            jax.ShapeDtypeStruct((m, LANES), jnp.int32),
            jax.ShapeDtypeStruct((m, LANES), F32),
        ],
        compiler_params=_cparams(("arbitrary",)),
        name="outproj_ln_router",
    )(oa_p, oa_s, ob_p, ob_s, xp2, xs2, wo, g_row, b_row, wr, br_row)


def _zero_tail(zbuf_ref, dst_ref, tail_row, n_blocks, sem):
    scale = ROW_SLAB if len(dst_ref.shape) == 2 else 1

    def tail_copy(t):
        z0 = pl.multiple_of((tail_row + t * MOE_PB) * scale, MOE_PB * scale)
        return pltpu.make_async_copy(zbuf_ref, dst_ref.at[pl.ds(z0, MOE_PB * scale)], sem)

    def start(t, _):
        tail_copy(t).start()
        return 0

    def wait(t, _):
        tail_copy(t).wait()
        return 0

    lax.fori_loop(0, n_blocks, start, 0)
    lax.fori_loop(0, n_blocks, wait, 0)


def _dispatch_body(zrow_ref, slot_ref, x_ref, xs_ref, zbuf_ref, zsem, rsem, *, tm):
    i = pl.program_id(0)

    @pl.when(i == 0)
    def _():
        zbuf_ref[...] = jnp.zeros(zbuf_ref.shape, F32)

        def zero_copy(e):
            z0 = pl.multiple_of(zrow_ref[e], MOE_PB)
            return pltpu.make_async_copy(zbuf_ref, xs_ref.at[pl.ds(z0, MOE_PB)], zsem)

        for e in range(N_EXPERTS):
            @pl.when(zrow_ref[e] >= 0)
            def _():
                zero_copy(e).start()
        for e in range(N_EXPERTS):
            @pl.when(zrow_ref[e] >= 0)
            def _():
                zero_copy(e).wait()
        _zero_tail(zbuf_ref, xs_ref, zrow_ref[N_EXPERTS], zrow_ref[N_EXPERTS + 1], zsem)

    def row_copy(r, dst):
        src0 = pl.multiple_of(r * ROW_SLAB, ROW_SLAB)
        return pltpu.make_async_copy(x_ref.at[pl.ds(src0, ROW_SLAB)], xs_ref.at[dst], rsem)

    def issue(r, _):
        for kk in range(TOP_K):
            row_copy(r, slot_ref[0, r * TOP_K + kk]).start(priority=kk % 2)
        return 0

    lax.fori_loop(0, tm, issue, 0, unroll=8)

    for kk in range(TOP_K):
        pltpu.make_async_copy(xs_ref.at[pl.ds(0, tm)], xs_ref.at[pl.ds(0, tm)], rsem).wait()


def _dispatch(x1, slot_tiles, zrow, n_rows):
    m = x1.shape[0] // ROW_SLAB
    tm = TOK_TM
    assert m % tm == 0
    return pl.pallas_call(
        functools.partial(_dispatch_body, tm=tm),
        grid_spec=pltpu.PrefetchScalarGridSpec(
            num_scalar_prefetch=1,
            grid=(m // tm,),
            in_specs=[
                pl.BlockSpec((None, 1, tm * TOP_K), lambda i, z: (i, 0, 0), memory_space=pltpu.SMEM),
                pl.BlockSpec((tm * ROW_SLAB, LANES), lambda i, z: (i, 0)),
            ],
            out_specs=pl.BlockSpec(memory_space=pl.ANY),
            scratch_shapes=[pltpu.VMEM((MOE_PB, ROW_SLAB, LANES), F32),
                            pltpu.SemaphoreType.DMA(()), pltpu.SemaphoreType.DMA(())],
        ),
        out_shape=jax.ShapeDtypeStruct((n_rows, ROW_SLAB, LANES), F32),
        compiler_params=_cparams(("arbitrary",)),
        name="moe_dispatch",
    )(zrow, slot_tiles, x1)


def _for_row_blocks(npb, fn):
    for k in range(1, MOE_SB + 1):
        odd = 2 * k - 1
        hit = npb == 2 * k
        if odd not in MOE_EXACT_ODD:
            hit = jnp.logical_or(hit, npb == odd)

        @pl.when(hit)
        def _():
            fn(0, k * MOE_RB)
    for q in MOE_EXACT_ODD:
        @pl.when(npb == q)
        def _():
            fn(0, q * MOE_PB)


def _moe_body(ie_ref, irow_ref, insub_ref, tail_ref, xs_ref, wg_ref, wu_ref, bup_ref,
              wd_ref, bd_ref, ys_ref, xstage_ref, xb_ref, act_ref, ybuf_ref, wgb_ref, wub_ref, wdb_ref,
              isem, osem):
    wi = pl.program_id(0)
    j = pl.program_id(1)
    n_w = pl.num_programs(0)
    to_sub = lambda n: lax.shift_right_logical(n + (MOE_RB // MOE_PB - 1), MOE_RB // MOE_PB - 1)
    npb = insub_ref[wi]
    nsub = to_sub(npb)
    row0 = irow_ref[wi]
    nt = MOE_NT
    n1 = MOE_NT1
    is_last = wi == n_w - 1
    nxt = jnp.minimum(wi + 1, n_w - 1)
    nsub_next = to_sub(jnp.where(is_last, 0, insub_ref[nxt]))
    row_next = irow_ref[nxt]
    prv = jnp.maximum(wi - 1, 0)
    npb_prev = jnp.where(wi == 0, 0, insub_ref[prv])
    row_prev = irow_ref[prv]

    class _XCopy:
        def __init__(self, row_base, s, slot):
            g0 = pl.multiple_of(row_base + s * MOE_RB, MOE_PB)
            self.copies = [
                pltpu.make_async_copy(xs_ref.at[pl.ds(g0, MOE_RB), c],
                                      xstage_ref.at[slot, :, pl.ds(c * LANES, LANES)], isem.at[slot])
                for c in range(ROW_SLAB)]

        def start(self):
            for cp in self.copies:
                cp.start()

        def wait(self):
            for cp in self.copies:
                cp.wait()

    x_copy = _XCopy

    def to_matmul_layout(s, slot):
        r0 = pl.multiple_of(s * MOE_RB, MOE_RB)
        xb_ref[pl.ds(r0, MOE_RB), :] = xstage_ref[slot].astype(BF16)

    def y_copy(row_base, s):
        b0 = pl.multiple_of(s * MOE_PB * ROW_SLAB, MOE_PB * ROW_SLAB)
        g0 = pl.multiple_of(row_base * ROW_SLAB, MOE_PB * ROW_SLAB) + b0
        return pltpu.make_async_copy(ybuf_ref.at[pl.ds(b0, MOE_PB * ROW_SLAB)],
                                     ys_ref.at[pl.ds(g0, MOE_PB * ROW_SLAB)], osem)

    @pl.when(jnp.logical_and(wi == 0, j == 0))
    def _():
        def load(s, _):
            cp = x_copy(row0, s, 0)
            cp.start()
            cp.wait()
            to_matmul_layout(s, 0)
            return 0

        lax.fori_loop(0, nsub, load, 0)

    @pl.when(jnp.logical_and(j == n1 - 1, nsub_next > 0))
    def _():
        x_copy(row_next, 0, 0).start()

    @pl.when(jnp.logical_and(j >= n1, j - n1 < nsub_next))
    def _():
        s = j - n1
        slot = lax.rem(s, 2)
        x_copy(row_next, s, slot).wait()

        @pl.when(s + 1 < nsub_next)
        def _():
            x_copy(row_next, s + 1, 1 - slot).start()

        to_matmul_layout(s, slot)

    @pl.when(jnp.logical_and(j == n1, npb_prev > 0))
    def _():
        def drain(s, _):
            y_copy(row_prev, s).wait()
            return 0

        lax.fori_loop(0, npb_prev, drain, 0)

    @pl.when(npb > 0)
    def _():
        @pl.when(j < n1)
        def _():
            wgb_ref[...] = wg_ref[...].astype(BF16)
            wub_ref[...] = wu_ref[...].astype(BF16)
            b_gate = bup_ref[ie_ref[wi], pl.ds(j, 1), :]
            b_up = bup_ref[ie_ref[wi], pl.ds(n1 + j, 1), :]
            tpw = MOE_TF1 // MOE_TF

            def up_rows(r0, nrows):
                x = xb_ref[pl.ds(r0, nrows), :]
                hg = jnp.dot(x, wgb_ref[...], preferred_element_type=F32) + b_gate
                hu = jnp.dot(x, wub_ref[...], preferred_element_type=F32) + b_up
                gate = jnp.minimum(hg, SWIGLU_LIMIT)
                up = jnp.clip(hu, -SWIGLU_LIMIT, SWIGLU_LIMIT)
                glu = gate * _sigmoid(SWIGLU_ALPHA * gate)
                a = (glu * (up + 1.0)).astype(BF16)
                for tt in range(tpw):
                    act_ref[j * tpw + tt, pl.ds(r0, nrows), :] = a[:, tt * MOE_TF:(tt + 1) * MOE_TF]

            _for_row_blocks(npb, up_rows)

        @pl.when(j >= n1)
        def _():
            wdb_ref[...] = wd_ref[...].astype(BF16)
            n = j - n1
            b_down = bd_ref[ie_ref[wi], pl.ds(n, 1), :]

            def down_rows(r0, nrows):
                acc = jnp.dot(act_ref[0, pl.ds(r0, nrows), :], wdb_ref[0:MOE_TF, :],
                              preferred_element_type=F32)
                for jj in range(1, nt):
                    acc = acc + jnp.dot(act_ref[jj, pl.ds(r0, nrows), :],
                                        wdb_ref[jj * MOE_TF:(jj + 1) * MOE_TF, :],
                                        preferred_element_type=F32)
                acc = acc + b_down
                cpt = MOE_TF // LANES
                for cc in range(cpt):
                    ybuf_ref[pl.ds(r0 * ROW_SLAB + n * cpt + cc, nrows, stride=ROW_SLAB), :] = (
                        acc[:, cc * LANES:(cc + 1) * LANES])

            _for_row_blocks(npb, down_rows)

        @pl.when(j == n1 + nt - 1)
        def _():
            def store(s, _):
                y_copy(row0, s).start()
                return 0

            lax.fori_loop(0, npb, store, 0)

    @pl.when(jnp.logical_and(is_last, j == n1 + nt - 1))
    def _():
        def drain(s, _):
            y_copy(row0, s).wait()
            return 0

        lax.fori_loop(0, npb, drain, 0)
        zsrc = ybuf_ref.at[pl.ds(0, MOE_PB * ROW_SLAB)]
        zsrc[...] = jnp.zeros(zsrc.shape, F32)
        _zero_tail(zsrc, ys_ref, tail_ref[0], tail_ref[1], osem)


def _moe(xs, n_items, item_e, item_row, item_nsub, tail, w_up, b_up3, w_down, b_down3):
    n_rows = xs.shape[0] * ROW_SLAB
    nt = MOE_NT
    n1 = MOE_NT1
    assert MOE_SB <= nt and MOE_RB == 2 * MOE_PB
    up_off = D_FF // MOE_TF1
    p1 = lambda j, n, w: jnp.where(n[w] > 0, jnp.minimum(j, n1 - 1), n1 - 1)
    p2 = lambda j, n, w: jnp.where(n[w] > 0, jnp.maximum(j - n1, 0), nt - 1)
    return pl.pallas_call(
        _moe_body,
        grid_spec=pltpu.PrefetchScalarGridSpec(
            num_scalar_prefetch=4,
            grid=(n_items, n1 + nt),
            in_specs=[
                pl.BlockSpec(memory_space=pl.ANY),
                pl.BlockSpec((None, D_MODEL, MOE_TF1), lambda w, j, e, r, n, t: (e[w], 0, p1(j, n, w))),
                pl.BlockSpec((None, D_MODEL, MOE_TF1),
                             lambda w, j, e, r, n, t: (e[w], 0, up_off + p1(j, n, w))),
                pl.BlockSpec((N_EXPERTS, 2 * n1, MOE_TF1), lambda w, j, e, r, n, t: (0, 0, 0)),
                pl.BlockSpec((None, D_FF, MOE_TF), lambda w, j, e, r, n, t: (e[w], 0, p2(j, n, w))),
                pl.BlockSpec((N_EXPERTS, nt, MOE_TF), lambda w, j, e, r, n, t: (0, 0, 0)),
            ],
            out_specs=pl.BlockSpec(memory_space=pl.ANY),
            scratch_shapes=[
                pltpu.VMEM((2, MOE_RB, D_MODEL), F32),
                pltpu.VMEM((MOE_R, D_MODEL), BF16),
                pltpu.VMEM((nt, MOE_R, MOE_TF), BF16),
                pltpu.VMEM((MOE_R * ROW_SLAB, LANES), F32),
                pltpu.VMEM((D_MODEL, MOE_TF1), BF16),
                pltpu.VMEM((D_MODEL, MOE_TF1), BF16),
                pltpu.VMEM((D_FF, MOE_TF), BF16),
                pltpu.SemaphoreType.DMA((2,)), pltpu.SemaphoreType.DMA(()),
            ],
        ),
        out_shape=jax.ShapeDtypeStruct((n_rows, LANES), F32),
        compiler_params=_cparams(("arbitrary", "arbitrary"), MOE_VMEM_LIMIT_BYTES),
        name="moe_experts",
    )(item_e, item_row, item_nsub, tail, xs, w_up, w_up, b_up3, w_down, b_down3)


def _combine_body(slot_ref, slotn_ref, ys_ref, x1_ref, gate_ref, g_ref, b_ref, yp_ref, ysm_ref, buf_ref,
                  v_ref, sem, *, tm, n_ptiles):
    i = pl.program_id(0)
    cur = lax.rem(i, 2)

    def gather(table_ref, bslot):
        def issue(r, _):
            for kk in range(TOP_K):
                src0 = pl.multiple_of(table_ref[0, r * TOP_K + kk] * ROW_SLAB, ROW_SLAB)
                dst0 = pl.multiple_of(r * ROW_SLAB, ROW_SLAB)
                pltpu.make_async_copy(ys_ref.at[pl.ds(src0, ROW_SLAB)],
                                      buf_ref.at[bslot, kk, pl.ds(dst0, ROW_SLAB)],
                                      sem.at[bslot]).start(priority=kk % 2)
            return 0

        lax.fori_loop(0, tm, issue, 0, unroll=8)

    @pl.when(i == 0)
    def _():
        gather(slot_ref, 0)

    @pl.when(i + 1 < pl.num_programs(0))
    def _():
        gather(slotn_ref, 1 - cur)

    for kk in range(TOP_K):
        pltpu.make_async_copy(ys_ref.at[pl.ds(0, tm * ROW_SLAB)], buf_ref.at[cur, kk], sem.at[cur]).wait()
    gates = gate_ref[...]
    for c in range(ROW_SLAB):
        acc = DN_ALPHA * _load_slab_chunk(x1_ref, 0, tm, c)
        for kk in range(TOP_K):
            acc = acc + gates[:, kk:kk + 1] * _load_slab_chunk(buf_ref.at[cur, kk], 0, tm, c)
        v_ref[:, c * LANES:(c + 1) * LANES] = acc
    y = _layernorm_rows(v_ref[...], g_ref[...], b_ref[...])

    @pl.when(pl.program_id(0) < n_ptiles)
    def _():
        yp_ref[...] = y

    @pl.when(pl.program_id(0) >= n_ptiles)
    def _():
        ysm_ref[...] = y


def _combine(ys, slot_tiles, x1, gates, g_row, b_row, mp):
    m = x1.shape[0] // ROW_SLAB
    tm = TOK_TM
    assert mp % tm == 0 and (m - mp) % tm == 0
    n_pt = mp // tm
    wspec = lambda shape: pl.BlockSpec(shape, lambda i: (0, 0))
    n_t = m // tm
    return pl.pallas_call(
        functools.partial(_combine_body, tm=tm, n_ptiles=n_pt),
        grid=(n_t,),
        in_specs=[
            pl.BlockSpec((None, 1, tm * TOP_K), lambda i: (i, 0, 0), memory_space=pltpu.SMEM),
            pl.BlockSpec((None, 1, tm * TOP_K), lambda i: (jnp.minimum(i + 1, n_t - 1), 0, 0),
                         memory_space=pltpu.SMEM),
            pl.BlockSpec(memory_space=pl.ANY),
            pl.BlockSpec((tm * ROW_SLAB, LANES), lambda i: (i, 0)),
            pl.BlockSpec((tm, LANES), lambda i: (i, 0)),
            wspec((1, D_MODEL)), wspec((1, D_MODEL)),
        ],
        out_specs=[pl.BlockSpec((tm, D_MODEL), lambda i: (jnp.minimum(i, n_pt - 1), 0)),
                   pl.BlockSpec((tm, D_MODEL), lambda i: (jnp.maximum(i - n_pt, 0), 0))],
        out_shape=[jax.ShapeDtypeStruct((mp, D_MODEL), F32),
                   jax.ShapeDtypeStruct((m - mp, D_MODEL), F32)],
        scratch_shapes=[pltpu.VMEM((2, TOP_K, tm * ROW_SLAB, LANES), F32),
                        pltpu.VMEM((tm, D_MODEL), F32), pltpu.SemaphoreType.DMA((2,))],
        compiler_params=_cparams(("arbitrary",)),
        name="moe_combine_ln",
    )(slot_tiles, slot_tiles, ys, x1, gates, g_row, b_row)


def _routing_tables(top_i, m):
    e_ids = jnp.arange(N_EXPERTS, dtype=jnp.int32)
    onehot = (top_i[:, :, None] == e_ids[None, None, :]).astype(jnp.int32)
    mask = jnp.sum(onehot, axis=1)
    incl = jnp.cumsum(mask, axis=0)
    pos = incl - mask
    counts = incl[-1]
    npb = (counts + MOE_PB - 1) // MOE_PB
    padded = npb * MOE_PB
    gend = jnp.cumsum(padded)
    gstart = gend - padded
    slot = jnp.sum(onehot * (gstart[None, None, :] + pos[:, None, :]), axis=2)
    n_rows = (m * TOP_K // MOE_PB + N_EXPERTS) * MOE_PB + MOE_PB
    tail = jnp.stack([gend[-1], (n_rows - gend[-1]) // MOE_PB]).astype(jnp.int32)
    zrow = jnp.concatenate([jnp.where(counts > 0, gend - MOE_PB, -1).astype(jnp.int32), tail])
    n_items = N_EXPERTS + (m * TOP_K // MOE_PB + N_EXPERTS) // MOE_NPB
    ipe = (npb + MOE_NPB - 1) // MOE_NPB
    iend = jnp.cumsum(ipe)
    istart = iend - ipe
    total = iend[-1]
    wid = jnp.arange(n_items, dtype=jnp.int32)
    wclamp = jnp.minimum(wid, total - 1)
    ie = jnp.minimum(jnp.searchsorted(iend, wclamp, side="right"), N_EXPERTS - 1).astype(jnp.int32)
    jn = wclamp - istart[ie]
    irow = (gstart[ie] + jn * MOE_R).astype(jnp.int32)
    insub = jnp.where(wid < total, jnp.clip(npb[ie] - jn * MOE_NPB, 0, MOE_NPB), 0).astype(jnp.int32)
    return slot.astype(jnp.int32), zrow, tail, total.astype(jnp.int32), ie, irow, insub, n_rows


def kernel(x_prompt, x_sample, state_conv_a, state_delta, state_conv_b, state_lru, w_in, conv_a_w,
           a_log, dt_bias, norm_a_w, conv_b_w, conv_b_b, lru_wa, lru_ba, lru_wx, lru_bx, lru_lambda,
           norm_b_w, w_out, ln1_g, ln1_b, w_router, b_router, w_up, b_up, w_down, b_down, ln2_g, ln2_b):
    bp, tp, _ = x_prompt.shape
    bs, ts, _ = x_sample.shape
    mp, ms = bp * tp, bs * ts
    m = mp + ms
    l = 0
    xp2 = x_prompt.reshape(mp, D_MODEL)
    xs2 = x_sample.reshape(ms, D_MODEL)

    wp = _wprep(jnp.transpose(w_in[l]))
    wo = w_out[l].astype(BF16)
    pad_h = lambda v: jnp.zeros((1, LANES), F32).at[0, H_A:2 * H_A].set(v)
    alog_row = pad_h(a_log[l])
    dtb_row = pad_h(dt_bias[l])
    row = lambda v: v.reshape(1, -1)

    proj = _inproj(xp2, xs2, wp)

    qkv_act, xbc, bg = _convact_prompt(proj, bp, tp, conv_a_w[l], conv_b_w[l], row(conv_b_b[l]),
                                       alog_row, dtb_row)
    u, w, qe, kdt, aqk, el = _delta_prep(qkv_act, bg, bp, tp)
    oa_p, sd_p = _delta_seq(u, w, qe, kdt, aqk, el, proj, row(norm_a_w[l]), bp, tp)
    ob_p, h_p = _lru_prompt(xbc, proj, bp, tp, lru_wa[l], lru_wx[l], row(lru_ba[l]), row(lru_bx[l]),
                            row(lru_lambda[l]), row(norm_b_w[l]))
    oa_s, sd_s = _sample_a(proj, mp, bs, ts, state_conv_a[l], state_delta[l], conv_a_w[l],
                           alog_row, dtb_row, row(norm_a_w[l]))
    ob_s, h_s = _lru_sample(proj, mp, bs, ts, state_conv_b[l], state_lru[l], conv_b_w[l],
                            row(conv_b_b[l]), lru_wa[l], lru_wx[l], row(lru_ba[l]), row(lru_bx[l]),
                            row(lru_lambda[l]), row(norm_b_w[l]))

    x1, ti, tg = _outproj(oa_p, oa_s, ob_p, ob_s, xp2, xs2, wo, row(ln1_g[l]), row(ln1_b[l]),
                          w_router[l], row(b_router[l]))

    slot, zrow, tail, n_used, ie, irow, insub, n_rows = _routing_tables(ti[:, :TOP_K], m)
    slot_tiles = slot.reshape(m // TOK_TM, 1, TOK_TM * TOP_K)
    xs_sorted = _dispatch(x1, slot_tiles, zrow, n_rows)
    ys = _moe(xs_sorted, n_used, ie, irow, insub, tail, w_up[l],
              b_up[l].reshape(N_EXPERTS, 2 * MOE_NT1, MOE_TF1), w_down[l],
              b_down[l].reshape(N_EXPERTS, MOE_NT, MOE_TF))
    y_p, y_s = _combine(ys, slot_tiles, x1, tg, row(ln2_g[l]), row(ln2_b[l]), mp)

    y_prompt = y_p.reshape(bp, tp, D_MODEL)
    y_sample = y_s.reshape(bs, ts, D_MODEL)
    nh = CONV_W - 1
    assert tp % SUBLANES == 0 and ts == SUBLANES and nh <= SUBLANES
    pg = proj.reshape(m // SUBLANES, SUBLANES, N_PROJ)
    gp = tp // SUBLANES

    def last_rows(g0, g1, gstep, c0, width):
        return lax.slice(pg, (g0, SUBLANES - nh, c0), (g1, SUBLANES, c0 + width), (gstep, 1, 1))

    ca_p = last_rows(gp - 1, mp // SUBLANES, gp, COL_QKV, 3 * D_A)
    cb_p = last_rows(gp - 1, mp // SUBLANES, gp, COL_XB, D_B)
    ca_s = last_rows(mp // SUBLANES, m // SUBLANES, 1, COL_QKV, 3 * D_A)
    cb_s = last_rows(mp // SUBLANES, m // SUBLANES, 1, COL_XB, D_B)
    return (y_prompt, y_sample,
            ca_p[None], sd_p[None], cb_p[None], h_p.reshape(1, bp, D_B),
            ca_s[None], sd_s[None], cb_s[None], h_s[None])
```

```python
import functools
import math

import jax
import jax.numpy as jnp
from jax import lax
from jax.experimental import pallas as pl
from jax.experimental.pallas import tpu as pltpu

F32 = jnp.float32
BF16 = jnp.bfloat16
HIGHEST = lax.Precision.HIGHEST

D_MODEL = 2048
D_A = 1024
H_A = 8
DK = 128
DV = 128
CONV_W = 4
D_B = 1024
NB = 8
BW = 128
LRU_C = 8.0
N_EXPERTS = 32
TOP_K = 4
D_FF = 2048
SWIGLU_LIMIT = 7.0
SWIGLU_ALPHA = 1.702
DEPTH = 1
DN_ALPHA = (2.0 * DEPTH) ** 0.25
LN_EPS = 1e-5
RMS_EPS = 1e-6
L2_EPS = 1e-6

LANES = 128
SUBLANES = 8
VMEM_LIMIT_BYTES = 56 * 1024 * 1024

COL_QKV = 0
COL_Z = 3 * D_A
COL_XB = 4 * D_A
COL_YB = 4 * D_A + D_B
COL_BA = 4 * D_A + 2 * D_B
N_PROJ = COL_BA + 2 * LANES

INPROJ_TM = 1024
INPROJ_TN = 1280
CONV_L = 256
DELTA_C = 128
DELTA_CPS = 2
OUT_TM = 512
MOE_PB = 128
MOE_RB = 256
MOE_SB = 6
MOE_R = MOE_RB * MOE_SB
MOE_NPB = MOE_R // MOE_PB
MOE_EXACT_ODD = (7, 9, 11)
MOE_TF = 256
MOE_NT = D_FF // MOE_TF
MOE_TF1 = 256
MOE_NT1 = D_FF // MOE_TF1
MOE_VMEM_LIMIT_BYTES = VMEM_LIMIT_BYTES
SAMPLE_NSEQ = 8
TOK_TM = 256
ROW_SLAB = D_MODEL // LANES


def _cparams(sem, vmem=VMEM_LIMIT_BYTES):
    return pltpu.CompilerParams(dimension_semantics=sem, vmem_limit_bytes=vmem)


def _sigmoid(x):
    return 1.0 / (1.0 + jnp.exp(-x))


def _softplus(x):
    return jnp.maximum(x, 0.0) + jnp.log(1.0 + jnp.exp(-jnp.abs(x)))


def _wprep_body(w_ref, o_ref):
    c = 4 * D_A
    nba = 2 * H_A
    o_ref[0:c, :] = w_ref[0:c, :].astype(BF16)
    o_ref[c:c + 2 * D_B, :] = w_ref[c + nba:c + nba + 2 * D_B, :].astype(BF16)
    o_ref[COL_BA:COL_BA + nba, :] = w_ref[c:c + nba, :].astype(BF16)
    o_ref[COL_BA + nba:N_PROJ, :] = jnp.zeros((N_PROJ - COL_BA - nba, o_ref.shape[1]), BF16)


def _wprep(w_t):
    cols = 256
    d_in = w_t.shape[0]
    assert d_in == 4 * D_A + 2 * H_A + 2 * D_B and D_MODEL % cols == 0
    return pl.pallas_call(
        _wprep_body,
        grid=(D_MODEL // cols,),
        in_specs=[pl.BlockSpec((d_in, cols), lambda i: (0, i))],
        out_specs=pl.BlockSpec((N_PROJ, cols), lambda i: (0, i)),
        out_shape=jax.ShapeDtypeStruct((N_PROJ, D_MODEL), BF16),
        compiler_params=_cparams(("arbitrary",)),
        name="wprep",
    )(w_t)


def _inproj_body(xp_ref, xs_ref, w_ref, o_ref, xb_ref, *, n_ptiles):
    i = pl.program_id(0)

    @pl.when(pl.program_id(1) == 0)
    def _():
        @pl.when(i < n_ptiles)
        def _():
            xb_ref[...] = xp_ref[...].astype(BF16)

        @pl.when(i >= n_ptiles)
        def _():
            xb_ref[...] = xs_ref[...].astype(BF16)

    o_ref[...] = _dot_nt(xb_ref[...], w_ref[...])


def _inproj(xp2, xs2, wp):
    mp, ms = xp2.shape[0], xs2.shape[0]
    tm = min(INPROJ_TM, ms)
    assert mp % tm == 0 and ms % tm == 0
    n_pt, n_st = mp // tm, ms // tm
    n_nt = N_PROJ // INPROJ_TN
    return pl.pallas_call(
        functools.partial(_inproj_body, n_ptiles=n_pt),
        grid=(n_pt + n_st, n_nt),
        in_specs=[
            pl.BlockSpec((tm, D_MODEL), lambda i, j: (jnp.minimum(i, n_pt - 1), 0)),
            pl.BlockSpec((tm, D_MODEL), lambda i, j: (jnp.maximum(i - n_pt, 0), 0),
                         pipeline_mode=pl.Buffered(1)),
            pl.BlockSpec((INPROJ_TN, D_MODEL), lambda i, j: (j, 0)),
        ],
        out_specs=pl.BlockSpec((tm, INPROJ_TN), lambda i, j: (i, j)),
        out_shape=jax.ShapeDtypeStruct((mp + ms, N_PROJ), F32),
        scratch_shapes=[pltpu.VMEM((tm, D_MODEL), BF16)],
        compiler_params=_cparams(("arbitrary", "arbitrary")),
        name="inproj",
    )(xp2, xs2, wp)


def _conv_taps(h_ref, w_ref, c0, width, rows, base):
    acc = h_ref[base:base + rows, c0:c0 + width] * w_ref[CONV_W - 1:CONV_W, c0:c0 + width]
    for j in range(1, CONV_W):
        acc = acc + (h_ref[base - j:base - j + rows, c0:c0 + width]
                     * w_ref[CONV_W - 1 - j:CONV_W - j, c0:c0 + width])
    return acc


def _qkv_activation(acc, blk):
    a = acc * _sigmoid(acc)
    if blk < 2 * H_A:
        a = a * lax.rsqrt(jnp.sum(a * a, axis=-1, keepdims=True) + L2_EPS)
        if blk < H_A:
            a = a * (DK ** -0.5)
    return a


def _beta_g(ba, alog_ref, dtb_ref):
    lane = lax.broadcasted_iota(jnp.int32, ba.shape, 1)
    beta = _sigmoid(ba)
    g = -jnp.exp(alog_ref[...]) * _softplus(ba + dtb_ref[...])
    return jnp.where(lane < H_A, beta, jnp.where(lane < 2 * H_A, g, 0.0))


def _convact_prompt_body(qkv_ref, xb_ref, ba_ref, caw_ref, cbw_ref, cbb_ref, alog_ref, dtb_ref,
                         qkvo_ref, xbo_ref, bgo_ref, hq_ref, hx_ref, *, rows):
    t = pl.program_id(1)
    hdr = SUBLANES

    @pl.when(t == 0)
    def _():
        hq_ref[0:hdr, :] = jnp.zeros((hdr, 3 * D_A), F32)
        hx_ref[0:hdr, :] = jnp.zeros((hdr, D_B), F32)

    @pl.when(t > 0)
    def _():
        hq_ref[0:hdr, :] = hq_ref[rows:rows + hdr, :]
        hx_ref[0:hdr, :] = hx_ref[rows:rows + hdr, :]

    hq_ref[hdr:hdr + rows, :] = qkv_ref[...]
    hx_ref[hdr:hdr + rows, :] = xb_ref[...]
    for blk in range(3 * H_A):
        c0 = blk * DK
        acc = _conv_taps(hq_ref, caw_ref, c0, DK, rows, hdr)
        qkvo_ref[:, c0:c0 + DK] = _qkv_activation(acc, blk)
    for blk in range(NB):
        c0 = blk * BW
        xbo_ref[:, c0:c0 + BW] = _conv_taps(hx_ref, cbw_ref, c0, BW, rows, hdr) + cbb_ref[:, c0:c0 + BW]
    bgo_ref[...] = _beta_g(ba_ref[...], alog_ref, dtb_ref)


def _convact_prompt(proj, bsz, t_len, caw, cbw, cbb, alog_row, dtb_row):
    rows = min(CONV_L, t_len)
    assert t_len % rows == 0
    nt = t_len // rows
    mp = bsz * t_len
    wspec = lambda shape: pl.BlockSpec(shape, lambda b, t: (0, 0))
    return pl.pallas_call(
        functools.partial(_convact_prompt_body, rows=rows),
        grid=(bsz, nt),
        in_specs=[
            pl.BlockSpec((rows, 3 * D_A), lambda b, t: (b * nt + t, COL_QKV // (3 * D_A))),
            pl.BlockSpec((rows, D_B), lambda b, t: (b * nt + t, COL_XB // D_B)),
            pl.BlockSpec((rows, LANES), lambda b, t: (b * nt + t, COL_BA // LANES)),
            wspec((CONV_W, 3 * D_A)), wspec((CONV_W, D_B)), wspec((1, D_B)),
            wspec((1, LANES)), wspec((1, LANES)),
        ],
        out_specs=[
            pl.BlockSpec((rows, 3 * D_A), lambda b, t: (b * nt + t, 0)),
            pl.BlockSpec((rows, D_B), lambda b, t: (b * nt + t, 0)),
            pl.BlockSpec((rows, LANES), lambda b, t: (b * nt + t, 0)),
        ],
        out_shape=[
            jax.ShapeDtypeStruct((mp, 3 * D_A), F32),
            jax.ShapeDtypeStruct((mp, D_B), F32),
            jax.ShapeDtypeStruct((mp, LANES), F32),
        ],
        scratch_shapes=[pltpu.VMEM((SUBLANES + rows + SUBLANES, 3 * D_A), F32),
                        pltpu.VMEM((SUBLANES + rows + SUBLANES, D_B), F32)],
        compiler_params=_cparams(("arbitrary", "arbitrary")),
        name="convact_prompt",
    )(proj, proj, proj, caw, cbw, cbb, alog_row, dtb_row)


def _dot_nt(a, b, precision=None):
    return lax.dot_general(a, b, (((1,), (1,)), ((), ())), precision=precision,
                           preferred_element_type=F32)


def _split_bf16(x):
    hi = x.astype(BF16)
    return hi, (x - hi.astype(F32)).astype(BF16)


def _dot3(a, b):
    ah, al = _split_bf16(a)
    bh, bl = _split_bf16(b)
    return (jnp.dot(ah, bh, preferred_element_type=F32)
            + (jnp.dot(al, bh, preferred_element_type=F32) + jnp.dot(ah, bl, preferred_element_type=F32)))


def _delta_prep_body(qkv_ref, bg_ref, u_ref, w_ref, qe_ref, kdt_ref, aqk_ref, el_ref, *, c, ncc):
    row = lax.broadcasted_iota(jnp.int32, (c, c), 0)
    col = lax.broadcasted_iota(jnp.int32, (c, c), 1)
    incl = row >= col
    strict = row > col
    eye = (row == col).astype(F32)
    tril = incl.astype(F32)
    eye_l = (lax.broadcasted_iota(jnp.int32, (LANES, LANES), 0)
             == lax.broadcasted_iota(jnp.int32, (LANES, LANES), 1)).astype(F32)

    pairs = [(cc, h) for cc in range(ncc) for h in range(H_A)]
    rows = [slice(cc * c, (cc + 1) * c) for cc in range(ncc)]
    bgs = [bg_ref[rs, :] for rs in rows]
    gc_alls = [jnp.dot(tril, bg, precision=HIGHEST, preferred_element_type=F32) for bg in bgs]
    gc_ts = [_dot_nt(eye_l, g, precision=HIGHEST) for g in gc_alls]
    lmats, rhss = [], []
    for cc, h in pairs:
        rs, bg, gc_all, gc_t = rows[cc], bgs[cc], gc_alls[cc], gc_ts[cc]
        q = qkv_ref[rs, h * DK:(h + 1) * DK]
        k = qkv_ref[rs, (H_A + h) * DK:(H_A + h + 1) * DK]
        v = qkv_ref[rs, (2 * H_A + h) * DK:(2 * H_A + h + 1) * DK]
        beta = bg[:, h:h + 1]
        gc = gc_all[:, H_A + h:H_A + h + 1]
        g_last = gc_all[c - 1:c, H_A + h:H_A + h + 1]
        diff = gc - gc_t[H_A + h:H_A + h + 1, :]
        decay = jnp.where(incl, jnp.exp(jnp.where(incl, diff, 0.0)), 0.0)
        kb = k * beta
        egc = jnp.exp(gc)
        kbf = k.astype(BF16)
        kk = _dot_nt(kbf, kbf)
        lmats.append(jnp.where(strict, kk * beta * decay, 0.0))
        rhss.append(jnp.concatenate([v * beta, kb * egc], axis=1))
        qe_ref[rs, h * DK:(h + 1) * DK] = (q * egc).astype(BF16)
        aqk_ref[cc, h] = (_dot_nt(q.astype(BF16), kbf) * decay).astype(BF16)
        kdec = k * jnp.exp(g_last - gc)
        kdt_ref[cc, h] = kdec.astype(BF16)
        el_ref[cc, h] = jnp.broadcast_to(jnp.exp(g_last), (SUBLANES, LANES))
    xbs = [(-lm).astype(BF16) for lm in lmats]
    t0s = [eye - lm for lm in lmats]
    p = 2
    while p < c:
        xbs = [jnp.dot(xb, xb, preferred_element_type=F32).astype(BF16) for xb in xbs]
        t0s = [t0 + jnp.dot(t0.astype(BF16), xb, preferred_element_type=F32) for t0, xb in zip(t0s, xbs)]
        p *= 2
    t0bs = [t0.astype(BF16) for t0 in t0s]
    sol0s = [jnp.dot(t0b, rhs.astype(BF16), preferred_element_type=F32) for t0b, rhs in zip(t0bs, rhss)]
    resids = [rhs - sol0 - _dot3(lm, sol0) for rhs, sol0, lm in zip(rhss, sol0s, lmats)]
    for p_, (cc, h) in enumerate(pairs):
        sol = sol0s[p_] + jnp.dot(t0bs[p_], resids[p_].astype(BF16), preferred_element_type=F32)
        u_ref[rows[cc], h * DV:(h + 1) * DV] = sol[:, :DV]
        w_ref[rows[cc], h * DK:(h + 1) * DK] = sol[:, DV:].astype(BF16)


def _delta_prep(qkv_act, bg, bsz, t_len):
    c = DELTA_C
    assert t_len % c == 0
    nc = t_len // c
    mp = bsz * t_len
    n_chunks = bsz * nc
    ncc = DELTA_CPS
    assert n_chunks % ncc == 0
    rowspec = lambda w: pl.BlockSpec((ncc * c, w), lambda i: (i, 0))
    return pl.pallas_call(
        functools.partial(_delta_prep_body, c=c, ncc=ncc),
        grid=(n_chunks // ncc,),
        in_specs=[rowspec(3 * D_A), rowspec(LANES)],
        out_specs=[
            rowspec(D_A), rowspec(D_A), rowspec(D_A),
            pl.BlockSpec((ncc, H_A, c, DK), lambda i: (i, 0, 0, 0)),
            pl.BlockSpec((ncc, H_A, c, c), lambda i: (i, 0, 0, 0)),
            pl.BlockSpec((ncc, H_A, SUBLANES, LANES), lambda i: (i, 0, 0, 0)),
        ],
        out_shape=[
            jax.ShapeDtypeStruct((mp, D_A), F32),
            jax.ShapeDtypeStruct((mp, D_A), BF16),
            jax.ShapeDtypeStruct((mp, D_A), BF16),
            jax.ShapeDtypeStruct((n_chunks, H_A, c, DK), BF16),
            jax.ShapeDtypeStruct((n_chunks, H_A, c, c), BF16),
            jax.ShapeDtypeStruct((n_chunks, H_A, SUBLANES, LANES), F32),
        ],
        compiler_params=_cparams(("arbitrary",)),
        name="delta_prep",
    )(qkv_act, bg)


def _gated_rmsnorm(o, z, nw):
    on = o * lax.rsqrt(jnp.mean(o * o, axis=-1, keepdims=True) + RMS_EPS) * nw
    return on * (z * _sigmoid(z))


def _delta_seq_body(*refs, bsz):
    u_ref, w_ref, qe_ref, kdt_ref, aqk_ref, el_ref = refs[:6]
    z_refs = refs[6:6 + bsz]
    nw_ref, oa_ref, sfin_ref, s_ref = refs[6 + bsz:]
    ci = pl.program_id(0)

    @pl.when(ci == 0)
    def _():
        s_ref[...] = jnp.zeros(s_ref.shape, F32)

    nw = nw_ref[...]
    pairs = [(b, h) for b in range(bsz) for h in range(H_A)]
    sl = lambda h: slice(h * DK, (h + 1) * DK)
    ss = [s_ref[b, h] for b, h in pairs]
    sbs = [s.astype(BF16) for s in ss]
    vbs = [(u_ref[b, :, sl(h)] - jnp.dot(w_ref[b, :, sl(h)], sb, preferred_element_type=F32)).astype(BF16)
           for (b, h), sb in zip(pairs, sbs)]
    for p, (b, h) in enumerate(pairs):
        s_ref[b, h] = ss[p] * el_ref[b, h][0:1, 0:1] + lax.dot_general(
            kdt_ref[b, h], vbs[p], (((0,), (0,)), ((), ())), preferred_element_type=F32)
    for p, (b, h) in enumerate(pairs):
        o = (jnp.dot(qe_ref[b, :, sl(h)], sbs[p], preferred_element_type=F32)
             + jnp.dot(aqk_ref[b, h], vbs[p], preferred_element_type=F32))
        oa_ref[b, :, sl(h)] = _gated_rmsnorm(o, z_refs[b][:, sl(h)], nw).astype(BF16)

    @pl.when(ci == pl.num_programs(0) - 1)
    def _():
        sfin_ref[...] = s_ref[...]


def _delta_seq(u, w, qe, kdt, aqk, el, proj, nw_row, bsz, t_len):
    c = DELTA_C
    nc = t_len // c
    mp = bsz * t_len
    rows4 = lambda a: a.reshape(bsz, nc, c, a.shape[-1])
    chunk5 = lambda a: a.reshape((bsz, nc) + a.shape[1:])
    rowspec = pl.BlockSpec((bsz, None, c, D_A), lambda i: (0, i, 0, 0))
    chunkspec = lambda a, bb: pl.BlockSpec((bsz, None, H_A, a, bb), lambda i: (0, i, 0, 0, 0))
    zspecs = [pl.BlockSpec((c, D_A), functools.partial(lambda i, b: (b * nc + i, COL_Z // D_A), b=b))
              for b in range(bsz)]
    oa, sfin = pl.pallas_call(
        functools.partial(_delta_seq_body, bsz=bsz),
        grid=(nc,),
        in_specs=[rowspec, rowspec, rowspec,
                  chunkspec(c, DK), chunkspec(c, c), chunkspec(SUBLANES, LANES)]
                 + zspecs + [pl.BlockSpec((1, DV), lambda i: (0, 0))],
        out_specs=[
            rowspec,
            pl.BlockSpec((bsz, H_A, DK, DV), lambda i: (0, 0, 0, 0)),
        ],
        out_shape=[
            jax.ShapeDtypeStruct((bsz, nc, c, D_A), BF16),
            jax.ShapeDtypeStruct((bsz, H_A, DK, DV), F32),
        ],
        scratch_shapes=[pltpu.VMEM((bsz, H_A, DK, DV), F32)],
        compiler_params=_cparams(("arbitrary",)),
        name="delta_seq",
    )(rows4(u), rows4(w), rows4(qe), chunk5(kdt), chunk5(aqk), chunk5(el), *([proj] * bsz), nw_row)
    return oa.reshape(mp, D_A), sfin


def _sample_a_body(qkv_ref, z_ref, ba_ref, hist_ref, s0_ref, caw_ref, alog_ref, dtb_ref, nw_ref,
                   oa_ref, s1_ref, hq_ref, *, t_len, nseq):
    hdr = SUBLANES
    nh = CONV_W - 1
    rows = [slice(si * t_len, (si + 1) * t_len) for si in range(nseq)]
    for si in range(nseq):
        hq_ref[si, 0:hdr, :] = jnp.zeros((hdr, 3 * D_A), F32)
        hq_ref[si, hdr - nh:hdr, :] = hist_ref[si]
        hq_ref[si, hdr:hdr + t_len, :] = qkv_ref[rows[si], :]
    bg = _beta_g(ba_ref[...], alog_ref, dtb_ref)
    rloc = lax.rem(lax.broadcasted_iota(jnp.int32, bg.shape, 0), t_len)
    gc_all = bg
    sft = 1
    while sft < t_len:
        gc_all = gc_all + jnp.where(rloc >= sft, pltpu.roll(gc_all, sft, axis=0), 0.0)
        sft *= 2
    nw = nw_ref[...]
    rowc = lax.broadcasted_iota(jnp.int32, (t_len, 1), 0)
    pairs = [(si, h) for si in range(nseq) for h in range(H_A)]
    npair = range(len(pairs))
    act = lambda si, blk: _qkv_activation(_conv_taps(hq_ref.at[si], caw_ref, blk * DK, DK, t_len, hdr), blk)
    qs = [act(si, h) for si, h in pairs]
    ks = [act(si, H_A + h) for si, h in pairs]
    vs = [act(si, 2 * H_A + h) for si, h in pairs]
    betas = [bg[rows[si], h:h + 1] for si, h in pairs]
    gcs = [gc_all[rows[si], H_A + h:H_A + h + 1] for si, h in pairs]
    glasts = [gc[t_len - 1:t_len, :] for gc in gcs]
    kbs = [k * b for k, b in zip(ks, betas)]
    egcs = [jnp.exp(gc) for gc in gcs]
    sols = [jnp.concatenate([v * b, kb * e], axis=1) for v, b, kb, e in zip(vs, betas, kbs, egcs)]
    acols = [[] for _ in npair]
    lcols = [[] for _ in npair]
    for j in range(t_len):
        for p in npair:
            kj = ks[p][j:j + 1, :]
            dcol = jnp.exp(jnp.where(rowc >= j, gcs[p] - gcs[p][j:j + 1, :], 0.0))
            acols[p].append(jnp.where(rowc >= j, jnp.sum(qs[p] * kj, axis=-1, keepdims=True) * dcol, 0.0))
            lcols[p].append(jnp.where(rowc > j, jnp.sum(kbs[p] * kj, axis=-1, keepdims=True) * dcol, 0.0))
    for j in range(t_len - 1):
        sols = [sol - lcols[p][j] * sol[j:j + 1, :] for p, sol in enumerate(sols)]
    ss = [s0_ref[si, h] for si, h in pairs]
    sbs = [s.astype(BF16) for s in ss]
    v_news = [sol[:, :DV] - jnp.dot(sol[:, DV:].astype(BF16), sb, preferred_element_type=F32)
              for sol, sb in zip(sols, sbs)]
    os_ = [jnp.dot((q * e).astype(BF16), sb, preferred_element_type=F32) for q, e, sb in zip(qs, egcs, sbs)]
    for j in range(t_len):
        os_ = [o + acols[p][j] * v_news[p][j:j + 1, :] for p, o in enumerate(os_)]
    upds = [lax.dot_general((k * jnp.exp(gl - gc)).astype(BF16), vn.astype(BF16), (((0,), (0,)), ((), ())),
                            preferred_element_type=F32)
            for k, gl, gc, vn in zip(ks, glasts, gcs, v_news)]
    for p, (si, h) in enumerate(pairs):
        s1_ref[si, h] = ss[p] * jnp.exp(glasts[p]) + upds[p]
        sl = slice(h * DV, (h + 1) * DV)
        oa_ref[rows[si], sl] = _gated_rmsnorm(os_[p], z_ref[rows[si], sl], nw).astype(BF16)


def _sample_a(proj, row0, bsz, t_len, hist, s0, caw, alog_row, dtb_row, nw_row):
    nseq = SAMPLE_NSEQ
    tr = nseq * t_len
    assert t_len == SUBLANES and row0 % tr == 0 and bsz % nseq == 0
    r0 = row0 // tr
    wspec = lambda shape: pl.BlockSpec(shape, lambda b: (0,) * len(shape))
    return pl.pallas_call(
        functools.partial(_sample_a_body, t_len=t_len, nseq=nseq),
        grid=(bsz // nseq,),
        in_specs=[
            pl.BlockSpec((tr, 3 * D_A), lambda b: (r0 + b, COL_QKV // (3 * D_A))),
            pl.BlockSpec((tr, D_A), lambda b: (r0 + b, COL_Z // D_A)),
            pl.BlockSpec((tr, LANES), lambda b: (r0 + b, COL_BA // LANES)),
            pl.BlockSpec((nseq, CONV_W - 1, 3 * D_A), lambda b: (b, 0, 0)),
            pl.BlockSpec((nseq, H_A, DK, DV), lambda b: (b, 0, 0, 0)),
            wspec((CONV_W, 3 * D_A)), wspec((1, LANES)), wspec((1, LANES)), wspec((1, DV)),
        ],
        out_specs=[
            pl.BlockSpec((tr, D_A), lambda b: (b, 0)),
            pl.BlockSpec((nseq, H_A, DK, DV), lambda b: (b, 0, 0, 0)),
        ],
        out_shape=[
            jax.ShapeDtypeStruct((bsz * t_len, D_A), BF16),
            jax.ShapeDtypeStruct((bsz, H_A, DK, DV), F32),
        ],
        scratch_shapes=[pltpu.VMEM((nseq, 2 * SUBLANES, 3 * D_A), F32)],
        compiler_params=_cparams(("arbitrary",)),
        name="sample_a",
    )(proj, proj, proj, hist, s0, caw, alog_row, dtb_row, nw_row)


def _lru_coeffs(xc, wa_ref, wx_ref, lba_ref, lbx_ref, lam_ref):
    parts = []
    for n in range(NB):
        sl = slice(n * BW, (n + 1) * BW)
        xn = xc[:, sl]
        xnb = xn.astype(BF16)
        gr = _sigmoid(jnp.dot(xnb, wa_ref[n].astype(BF16), preferred_element_type=F32) + lba_ref[:, sl])
        gi = _sigmoid(jnp.dot(xnb, wx_ref[n].astype(BF16), preferred_element_type=F32) + lbx_ref[:, sl])
        log_a = -LRU_C * gr * _softplus(-lam_ref[:, sl])
        parts.append((jnp.exp(log_a), jnp.sqrt(1.0 - jnp.exp(2.0 * log_a)), gi * xn))
    return parts


def _group_scan(a, b):
    rowi = lax.broadcasted_iota(jnp.int32, a.shape, 0)
    sft = 1
    while sft < SUBLANES:
        keep = rowi >= sft
        b = b + a * jnp.where(keep, pltpu.roll(b, sft, axis=0), 0.0)
        a = a * jnp.where(keep, pltpu.roll(a, sft, axis=0), 1.0)
        sft *= 2
    return a, b


def _gelu_tanh(x):
    return 0.5 * x * (1.0 + jnp.tanh(math.sqrt(2.0 / math.pi) * (x + 0.044715 * x * x * x)))


def _lru_finish(h, yb, nbw):
    hg = h * _gelu_tanh(yb)
    return (hg * lax.rsqrt(jnp.mean(hg * hg, axis=-1, keepdims=True) + RMS_EPS) * nbw).astype(BF16)


def _lru_prompt_body(xc_ref, yb_ref, wa_ref, wx_ref, lba_ref, lbx_ref, lam_ref, nbw_ref,
                     ob_ref, hfin_ref, a_ref, b_ref, h_ref, carry_ref, *, rows):
    t = pl.program_id(1)
    parts = _lru_coeffs(xc_ref[...], wa_ref, wx_ref, lba_ref, lbx_ref, lam_ref)
    rowi = lax.broadcasted_iota(jnp.int32, (rows, BW), 0)
    first = jnp.logical_and(rowi == 0, t == 0)
    for n in range(NB):
        sl = slice(n * BW, (n + 1) * BW)
        a_n, mult_n, gix_n = parts[n]
        a_ref[:, sl] = jnp.where(first, 0.0, a_n)
        b_ref[:, sl] = jnp.where(first, 1.0, mult_n) * gix_n

    @pl.when(t == 0)
    def _():
        carry_ref[...] = jnp.zeros(carry_ref.shape, F32)

    def group(gidx, carry):
        r0 = pl.multiple_of(gidx * SUBLANES, SUBLANES)
        ag, bg = _group_scan(a_ref[pl.ds(r0, SUBLANES), :], b_ref[pl.ds(r0, SUBLANES), :])
        hg = ag * carry + bg
        h_ref[pl.ds(r0, SUBLANES), :] = hg
        return jnp.broadcast_to(hg[SUBLANES - 1:SUBLANES, :], hg.shape)

    carry = lax.fori_loop(0, rows // SUBLANES, group, carry_ref[...])
    carry_ref[...] = carry
    ob_ref[...] = _lru_finish(h_ref[...], yb_ref[...], nbw_ref[...])

    @pl.when(t == pl.num_programs(1) - 1)
    def _():
        hfin_ref[...] = carry[0:1, :]


def _lru_prompt(xbc, proj, bsz, t_len, wa, wx, lba, lbx, lam, nbw):
    rows = min(CONV_L, t_len)
    nt = t_len // rows
    mp = bsz * t_len
    wspec = lambda shape: pl.BlockSpec(shape, lambda b, t: (0,) * len(shape))
    return pl.pallas_call(
        functools.partial(_lru_prompt_body, rows=rows),
        grid=(bsz, nt),
        in_specs=[
            pl.BlockSpec((rows, D_B), lambda b, t: (b * nt + t, 0)),
            pl.BlockSpec((rows, D_B), lambda b, t: (b * nt + t, COL_YB // D_B)),
            wspec((NB, BW, BW)), wspec((NB, BW, BW)),
            wspec((1, D_B)), wspec((1, D_B)), wspec((1, D_B)), wspec((1, D_B)),
        ],
        out_specs=[
            pl.BlockSpec((rows, D_B), lambda b, t: (b * nt + t, 0)),
            pl.BlockSpec((None, 1, D_B), lambda b, t: (b, 0, 0)),
        ],
        out_shape=[
            jax.ShapeDtypeStruct((mp, D_B), BF16),
            jax.ShapeDtypeStruct((bsz, 1, D_B), F32),
        ],
        scratch_shapes=[pltpu.VMEM((rows, D_B), F32), pltpu.VMEM((rows, D_B), F32),
                        pltpu.VMEM((rows, D_B), F32), pltpu.VMEM((SUBLANES, D_B), F32)],
        compiler_params=_cparams(("arbitrary", "arbitrary")),
        name="lru_prompt",
    )(xbc, proj, wa, wx, lba, lbx, lam, nbw)


def _lru_sample_body(xb_ref, yb_ref, hist_ref, h0_ref, cbw_ref, cbb_ref, wa_ref, wx_ref, lba_ref,
                     lbx_ref, lam_ref, nbw_ref, ob_ref, h1_ref, hx_ref, a_ref, b_ref, h_ref,
                     *, nseq, t_len):
    hdr = SUBLANES
    nh = CONV_W - 1
    hx_ref[0:hdr, :] = jnp.zeros((hdr, D_B), F32)

    def conv_seq(si, _):
        r0 = pl.multiple_of(si * t_len, t_len)
        hx_ref[hdr - nh:hdr, :] = hist_ref[si]
        hx_ref[hdr:hdr + t_len, :] = xb_ref[pl.ds(r0, t_len), :]
        for n in range(NB):
            c0 = n * BW
            h_ref[pl.ds(r0, t_len), c0:c0 + BW] = (_conv_taps(hx_ref, cbw_ref, c0, BW, t_len, hdr)
                                                  + cbb_ref[:, c0:c0 + BW])
        return 0

    lax.fori_loop(0, nseq, conv_seq, 0)
    parts = _lru_coeffs(h_ref[...], wa_ref, wx_ref, lba_ref, lbx_ref, lam_ref)
    for n in range(NB):
        sl = slice(n * BW, (n + 1) * BW)
        a_n, mult_n, gix_n = parts[n]
        a_ref[:, sl] = a_n
        b_ref[:, sl] = mult_n * gix_n

    def seq(si, _):
        r0 = pl.multiple_of(si * t_len, t_len)
        ag, bg = _group_scan(a_ref[pl.ds(r0, t_len), :], b_ref[pl.ds(r0, t_len), :])
        hg = ag * h0_ref[pl.ds(si, 1), :] + bg
        h_ref[pl.ds(r0, t_len), :] = hg
        h1_ref[pl.ds(si, 1), :] = hg[t_len - 1:t_len, :]
        return 0

    lax.fori_loop(0, nseq, seq, 0)
    ob_ref[...] = _lru_finish(h_ref[...], yb_ref[...], nbw_ref[...])


def _lru_sample(proj, row0, bsz, t_len, hist, h0, cbw, cbb, wa, wx, lba, lbx, lam, nbw):
    assert t_len == SUBLANES
    ms = bsz * t_len
    assert row0 % ms == 0
    rblk = row0 // ms
    wspec = lambda shape: pl.BlockSpec(shape, lambda i: (0,) * len(shape))
    return pl.pallas_call(
        functools.partial(_lru_sample_body, nseq=bsz, t_len=t_len),
        grid=(1,),
        in_specs=[
            pl.BlockSpec((ms, D_B), lambda i: (rblk, COL_XB // D_B)),
            pl.BlockSpec((ms, D_B), lambda i: (rblk, COL_YB // D_B)),
            wspec((bsz, CONV_W - 1, D_B)), wspec((bsz, D_B)),
            wspec((CONV_W, D_B)), wspec((1, D_B)),
            wspec((NB, BW, BW)), wspec((NB, BW, BW)),
            wspec((1, D_B)), wspec((1, D_B)), wspec((1, D_B)), wspec((1, D_B)),
        ],
        out_specs=[wspec((ms, D_B)), wspec((bsz, D_B))],
        out_shape=[
            jax.ShapeDtypeStruct((ms, D_B), BF16),
            jax.ShapeDtypeStruct((bsz, D_B), F32),
        ],
        scratch_shapes=[pltpu.VMEM((2 * SUBLANES, D_B), F32), pltpu.VMEM((ms, D_B), F32),
                        pltpu.VMEM((ms, D_B), F32), pltpu.VMEM((ms, D_B), F32)],
        compiler_params=_cparams(("arbitrary",)),
        name="lru_sample",
    )(proj, proj, hist, h0, cbw, cbb, wa, wx, lba, lbx, lam, nbw)


def _layernorm_rows(v, g, b):
    mu = jnp.mean(v, axis=-1, keepdims=True)
    d = v - mu
    var = jnp.mean(d * d, axis=-1, keepdims=True)
    return d * lax.rsqrt(var + LN_EPS) * g + b


def _store_slabs(slab_ref, base, val):
    n = val.shape[0]
    for s in range(ROW_SLAB):
        slab_ref[pl.ds(base + s, n, stride=ROW_SLAB), :] = val[:, s * LANES:(s + 1) * LANES]


def _load_slab_chunk(slab_ref, base, n, s):
    return slab_ref[pl.ds(base + s, n, stride=ROW_SLAB), :]


def _outproj_body(oap_ref, oas_ref, obp_ref, obs_ref, xp_ref, xs_ref, wo_ref, g_ref, b_ref,
                  wr_ref, br_ref, x1_ref, ti_ref, tg_ref, *, n_ptiles):
    i = pl.program_id(0)

    def run(oa_ref, ob_ref, x_ref):
        mix = (jnp.dot(oa_ref[...], wo_ref[0:D_A, :], preferred_element_type=F32)
               + jnp.dot(ob_ref[...], wo_ref[D_A:D_A + D_B, :], preferred_element_type=F32))
        y = _layernorm_rows(DN_ALPHA * x_ref[...] + mix, g_ref[...], b_ref[...])
        _store_slabs(x1_ref, 0, y)
        logits = _dot3(y, wr_ref[...]) + br_ref[...]
        lane = lax.broadcasted_iota(jnp.int32, logits.shape, 1)
        lane_o = lax.broadcasted_iota(jnp.int32, ti_ref.shape, 1)
        cur = logits
        ti = jnp.zeros(ti_ref.shape, jnp.int32)
        tv = jnp.zeros(tg_ref.shape, F32)
        v0 = None
        den = None
        for kk in range(TOP_K):
            m = jnp.max(cur, axis=-1, keepdims=True)
            idx = jnp.min(jnp.where(cur == m, lane, N_EXPERTS), axis=-1, keepdims=True)
            cur = jnp.where(lane == idx, -jnp.inf, cur)
            if kk == 0:
                v0 = m
            e = jnp.exp(m - v0)
            den = e if den is None else den + e
            ti = jnp.where(lane_o == kk, idx, ti)
            tv = jnp.where(lane_o == kk, e, tv)
        ti_ref[...] = ti
        tg_ref[...] = tv / den

    @pl.when(i < n_ptiles)
    def _():
        run(oap_ref, obp_ref, xp_ref)

    @pl.when(i >= n_ptiles)
    def _():
        run(oas_ref, obs_ref, xs_ref)


def _outproj(oa_p, oa_s, ob_p, ob_s, xp2, xs2, wo, g_row, b_row, wr, br_row):
    mp, ms = xp2.shape[0], xs2.shape[0]
    tm = min(OUT_TM, ms)
    assert mp % tm == 0 and ms % tm == 0
    n_pt, n_st = mp // tm, ms // tm
    pmap = lambda i: (jnp.minimum(i, n_pt - 1), 0)
    smap = lambda i: (jnp.maximum(i - n_pt, 0), 0)
    wspec = lambda shape: pl.BlockSpec(shape, lambda i: (0, 0))
    m = mp + ms
    return pl.pallas_call(
        functools.partial(_outproj_body, n_ptiles=n_pt),
        grid=(n_pt + n_st,),
        in_specs=[
            pl.BlockSpec((tm, D_A), pmap), pl.BlockSpec((tm, D_A), smap),
            pl.BlockSpec((tm, D_B), pmap), pl.BlockSpec((tm, D_B), smap),
            pl.BlockSpec((tm, D_MODEL), pmap), pl.BlockSpec((tm, D_MODEL), smap),
            wspec((D_A + D_B, D_MODEL)), wspec((1, D_MODEL)), wspec((1, D_MODEL)),
            wspec((D_MODEL, N_EXPERTS)), wspec((1, N_EXPERTS)),
        ],
        out_specs=[
            pl.BlockSpec((tm * ROW_SLAB, LANES), lambda i: (i, 0)),
            pl.BlockSpec((tm, LANES), lambda i: (i, 0)),
            pl.BlockSpec((tm, LANES), lambda i: (i, 0)),
        ],
        out_shape=[
            jax.ShapeDtypeStruct((m * ROW_SLAB, LANES), F32),
            jax.ShapeDtypeStruct((m, LANES), jnp.int32),
            jax.ShapeDtypeStruct((m, LANES), F32),
        ],
        compiler_params=_cparams(("arbitrary",)),
        name="outproj_ln_router",
    )(oa_p, oa_s, ob_p, ob_s, xp2, xs2, wo, g_row, b_row, wr, br_row)


def _zero_tail(zbuf_ref, dst_ref, tail_row, n_blocks, sem):
    scale = ROW_SLAB if len(dst_ref.shape) == 2 else 1

    def tail_copy(t):
        z0 = pl.multiple_of((tail_row + t * MOE_PB) * scale, MOE_PB * scale)
        return pltpu.make_async_copy(zbuf_ref, dst_ref.at[pl.ds(z0, MOE_PB * scale)], sem)

    def start(t, _):
        tail_copy(t).start()
        return 0

    def wait(t, _):
        tail_copy(t).wait()
        return 0

    lax.fori_loop(0, n_blocks, start, 0)
    lax.fori_loop(0, n_blocks, wait, 0)


def _dispatch_body(zrow_ref, slot_ref, x_ref, xs_ref, zbuf_ref, zsem, rsem, *, tm):
    i = pl.program_id(0)

    @pl.when(i == 0)
    def _():
        zbuf_ref[...] = jnp.zeros(zbuf_ref.shape, F32)

        def zero_copy(e):
            z0 = pl.multiple_of(zrow_ref[e], MOE_PB)
            return pltpu.make_async_copy(zbuf_ref, xs_ref.at[pl.ds(z0, MOE_PB)], zsem)

        for e in range(N_EXPERTS):
            @pl.when(zrow_ref[e] >= 0)
            def _():
                zero_copy(e).start()
        for e in range(N_EXPERTS):
            @pl.when(zrow_ref[e] >= 0)
            def _():
                zero_copy(e).wait()
        _zero_tail(zbuf_ref, xs_ref, zrow_ref[N_EXPERTS], zrow_ref[N_EXPERTS + 1], zsem)

    def row_copy(r, dst):
        src0 = pl.multiple_of(r * ROW_SLAB, ROW_SLAB)
        return pltpu.make_async_copy(x_ref.at[pl.ds(src0, ROW_SLAB)], xs_ref.at[dst], rsem)

    def issue(r, _):
        for kk in range(TOP_K):
            row_copy(r, slot_ref[0, r * TOP_K + kk]).start(priority=kk % 2)
        return 0

    lax.fori_loop(0, tm, issue, 0, unroll=8)

    for kk in range(TOP_K):
        pltpu.make_async_copy(xs_ref.at[pl.ds(0, tm)], xs_ref.at[pl.ds(0, tm)], rsem).wait()


def _dispatch(x1, slot_tiles, zrow, n_rows):
    m = x1.shape[0] // ROW_SLAB
    tm = TOK_TM
    assert m % tm == 0
    return pl.pallas_call(
        functools.partial(_dispatch_body, tm=tm),
        grid_spec=pltpu.PrefetchScalarGridSpec(
            num_scalar_prefetch=1,
            grid=(m // tm,),
            in_specs=[
                pl.BlockSpec((None, 1, tm * TOP_K), lambda i, z: (i, 0, 0), memory_space=pltpu.SMEM),
                pl.BlockSpec((tm * ROW_SLAB, LANES), lambda i, z: (i, 0)),
            ],
            out_specs=pl.BlockSpec(memory_space=pl.ANY),
            scratch_shapes=[pltpu.VMEM((MOE_PB, ROW_SLAB, LANES), F32),
                            pltpu.SemaphoreType.DMA(()), pltpu.SemaphoreType.DMA(())],
        ),
        out_shape=jax.ShapeDtypeStruct((n_rows, ROW_SLAB, LANES), F32),
        compiler_params=_cparams(("arbitrary",)),
        name="moe_dispatch",
    )(zrow, slot_tiles, x1)


def _for_row_blocks(npb, fn):
    for k in range(1, MOE_SB + 1):
        odd = 2 * k - 1
        hit = npb == 2 * k
        if odd not in MOE_EXACT_ODD:
            hit = jnp.logical_or(hit, npb == odd)

        @pl.when(hit)
        def _():
            fn(0, k * MOE_RB)
    for q in MOE_EXACT_ODD:
        @pl.when(npb == q)
        def _():
            fn(0, q * MOE_PB)


def _moe_body(ie_ref, irow_ref, insub_ref, tail_ref, xs_ref, wg_ref, wu_ref, bup_ref,
              wd_ref, bd_ref, ys_ref, xstage_ref, xb_ref, act_ref, ybuf_ref, wgb_ref, wub_ref, wdb_ref,
              isem, osem):
    wi = pl.program_id(0)
    j = pl.program_id(1)
    n_w = pl.num_programs(0)
    to_sub = lambda n: lax.shift_right_logical(n + (MOE_RB // MOE_PB - 1), MOE_RB // MOE_PB - 1)
    npb = insub_ref[wi]
    nsub = to_sub(npb)
    row0 = irow_ref[wi]
    nt = MOE_NT
    n1 = MOE_NT1
    is_last = wi == n_w - 1
    nxt = jnp.minimum(wi + 1, n_w - 1)
    nsub_next = to_sub(jnp.where(is_last, 0, insub_ref[nxt]))
    row_next = irow_ref[nxt]
    prv = jnp.maximum(wi - 1, 0)
    npb_prev = jnp.where(wi == 0, 0, insub_ref[prv])
    row_prev = irow_ref[prv]

    class _XCopy:
        def __init__(self, row_base, s, slot):
            g0 = pl.multiple_of(row_base + s * MOE_RB, MOE_PB)
            self.copies = [
                pltpu.make_async_copy(xs_ref.at[pl.ds(g0, MOE_RB), c],
                                      xstage_ref.at[slot, :, pl.ds(c * LANES, LANES)], isem.at[slot])
                for c in range(ROW_SLAB)]

        def start(self):
            for cp in self.copies:
                cp.start()

        def wait(self):
            for cp in self.copies:
                cp.wait()

    x_copy = _XCopy

    def to_matmul_layout(s, slot):
        r0 = pl.multiple_of(s * MOE_RB, MOE_RB)
        xb_ref[pl.ds(r0, MOE_RB), :] = xstage_ref[slot].astype(BF16)

    def y_copy(row_base, s):
        b0 = pl.multiple_of(s * MOE_PB * ROW_SLAB, MOE_PB * ROW_SLAB)
        g0 = pl.multiple_of(row_base * ROW_SLAB, MOE_PB * ROW_SLAB) + b0
        return pltpu.make_async_copy(ybuf_ref.at[pl.ds(b0, MOE_PB * ROW_SLAB)],
                                     ys_ref.at[pl.ds(g0, MOE_PB * ROW_SLAB)], osem)

    @pl.when(jnp.logical_and(wi == 0, j == 0))
    def _():
        def load(s, _):
            cp = x_copy(row0, s, 0)
            cp.start()
            cp.wait()
            to_matmul_layout(s, 0)
            return 0

        lax.fori_loop(0, nsub, load, 0)

    @pl.when(jnp.logical_and(j == n1 - 1, nsub_next > 0))
    def _():
        x_copy(row_next, 0, 0).start()

    @pl.when(jnp.logical_and(j >= n1, j - n1 < nsub_next))
    def _():
        s = j - n1
        slot = lax.rem(s, 2)
        x_copy(row_next, s, slot).wait()

        @pl.when(s + 1 < nsub_next)
        def _():
            x_copy(row_next, s + 1, 1 - slot).start()

        to_matmul_layout(s, slot)

    @pl.when(jnp.logical_and(j == n1, npb_prev > 0))
    def _():
        def drain(s, _):
            y_copy(row_prev, s).wait()
            return 0

        lax.fori_loop(0, npb_prev, drain, 0)

    @pl.when(npb > 0)
    def _():
        @pl.when(j < n1)
        def _():
            wgb_ref[...] = wg_ref[...].astype(BF16)
            wub_ref[...] = wu_ref[...].astype(BF16)
            b_gate = bup_ref[ie_ref[wi], pl.ds(j, 1), :]
            b_up = bup_ref[ie_ref[wi], pl.ds(n1 + j, 1), :]
            tpw = MOE_TF1 // MOE_TF

            def up_rows(r0, nrows):
                x = xb_ref[pl.ds(r0, nrows), :]
                hg = jnp.dot(x, wgb_ref[...], preferred_element_type=F32) + b_gate
                hu = jnp.dot(x, wub_ref[...], preferred_element_type=F32) + b_up
                gate = jnp.minimum(hg, SWIGLU_LIMIT)
                up = jnp.clip(hu, -SWIGLU_LIMIT, SWIGLU_LIMIT)
                glu = gate * _sigmoid(SWIGLU_ALPHA * gate)
                a = (glu * (up + 1.0)).astype(BF16)
                for tt in range(tpw):
                    act_ref[j * tpw + tt, pl.ds(r0, nrows), :] = a[:, tt * MOE_TF:(tt + 1) * MOE_TF]

            _for_row_blocks(npb, up_rows)

        @pl.when(j >= n1)
        def _():
            wdb_ref[...] = wd_ref[...].astype(BF16)
            n = j - n1
            b_down = bd_ref[ie_ref[wi], pl.ds(n, 1), :]

            def down_rows(r0, nrows):
                acc = jnp.dot(act_ref[0, pl.ds(r0, nrows), :], wdb_ref[0:MOE_TF, :],
                              preferred_element_type=F32)
                for jj in range(1, nt):
                    acc = acc + jnp.dot(act_ref[jj, pl.ds(r0, nrows), :],
                                        wdb_ref[jj * MOE_TF:(jj + 1) * MOE_TF, :],
                                        preferred_element_type=F32)
                acc = acc + b_down
                cpt = MOE_TF // LANES
                for cc in range(cpt):
                    ybuf_ref[pl.ds(r0 * ROW_SLAB + n * cpt + cc, nrows, stride=ROW_SLAB), :] = (
                        acc[:, cc * LANES:(cc + 1) * LANES])

            _for_row_blocks(npb, down_rows)

        @pl.when(j == n1 + nt - 1)
        def _():
            def store(s, _):
                y_copy(row0, s).start()
                return 0

            lax.fori_loop(0, npb, store, 0)

    @pl.when(jnp.logical_and(is_last, j == n1 + nt - 1))
    def _():
        def drain(s, _):
            y_copy(row0, s).wait()
            return 0

        lax.fori_loop(0, npb, drain, 0)
        zsrc = ybuf_ref.at[pl.ds(0, MOE_PB * ROW_SLAB)]
        zsrc[...] = jnp.zeros(zsrc.shape, F32)
        _zero_tail(zsrc, ys_ref, tail_ref[0], tail_ref[1], osem)


def _moe(xs, n_items, item_e, item_row, item_nsub, tail, w_up, b_up3, w_down, b_down3):
    n_rows = xs.shape[0] * ROW_SLAB
    nt = MOE_NT
    n1 = MOE_NT1
    assert MOE_SB <= nt and MOE_RB == 2 * MOE_PB
    up_off = D_FF // MOE_TF1
    p1 = lambda j, n, w: jnp.where(n[w] > 0, jnp.minimum(j, n1 - 1), n1 - 1)
    p2 = lambda j, n, w: jnp.where(n[w] > 0, jnp.maximum(j - n1, 0), nt - 1)
    return pl.pallas_call(
        _moe_body,
        grid_spec=pltpu.PrefetchScalarGridSpec(
            num_scalar_prefetch=4,
            grid=(n_items, n1 + nt),
            in_specs=[
                pl.BlockSpec(memory_space=pl.ANY),
                pl.BlockSpec((None, D_MODEL, MOE_TF1), lambda w, j, e, r, n, t: (e[w], 0, p1(j, n, w))),
                pl.BlockSpec((None, D_MODEL, MOE_TF1),
                             lambda w, j, e, r, n, t: (e[w], 0, up_off + p1(j, n, w))),
                pl.BlockSpec((N_EXPERTS, 2 * n1, MOE_TF1), lambda w, j, e, r, n, t: (0, 0, 0)),
                pl.BlockSpec((None, D_FF, MOE_TF), lambda w, j, e, r, n, t: (e[w], 0, p2(j, n, w))),
                pl.BlockSpec((N_EXPERTS, nt, MOE_TF), lambda w, j, e, r, n, t: (0, 0, 0)),
            ],
            out_specs=pl.BlockSpec(memory_space=pl.ANY),
            scratch_shapes=[
                pltpu.VMEM((2, MOE_RB, D_MODEL), F32),
                pltpu.VMEM((MOE_R, D_MODEL), BF16),
                pltpu.VMEM((nt, MOE_R, MOE_TF), BF16),
                pltpu.VMEM((MOE_R * ROW_SLAB, LANES), F32),
                pltpu.VMEM((D_MODEL, MOE_TF1), BF16),
                pltpu.VMEM((D_MODEL, MOE_TF1), BF16),
                pltpu.VMEM((D_FF, MOE_TF), BF16),
                pltpu.SemaphoreType.DMA((2,)), pltpu.SemaphoreType.DMA(()),
            ],
        ),
        out_shape=jax.ShapeDtypeStruct((n_rows, LANES), F32),
        compiler_params=_cparams(("arbitrary", "arbitrary"), MOE_VMEM_LIMIT_BYTES),
        name="moe_experts",
    )(item_e, item_row, item_nsub, tail, xs, w_up, w_up, b_up3, w_down, b_down3)


def _combine_body(slot_ref, slotn_ref, ys_ref, x1_ref, gate_ref, g_ref, b_ref, yp_ref, ysm_ref, buf_ref,
                  v_ref, sem, *, tm, n_ptiles):
    i = pl.program_id(0)
    cur = lax.rem(i, 2)

    def gather(table_ref, bslot):
        def issue(r, _):
            for kk in range(TOP_K):
                src0 = pl.multiple_of(table_ref[0, r * TOP_K + kk] * ROW_SLAB, ROW_SLAB)
                dst0 = pl.multiple_of(r * ROW_SLAB, ROW_SLAB)
                pltpu.make_async_copy(ys_ref.at[pl.ds(src0, ROW_SLAB)],
                                      buf_ref.at[bslot, kk, pl.ds(dst0, ROW_SLAB)],
                                      sem.at[bslot]).start(priority=kk % 2)
            return 0

        lax.fori_loop(0, tm, issue, 0, unroll=8)

    @pl.when(i == 0)
    def _():
        gather(slot_ref, 0)

    @pl.when(i + 1 < pl.num_programs(0))
    def _():
        gather(slotn_ref, 1 - cur)

    for kk in range(TOP_K):
        pltpu.make_async_copy(ys_ref.at[pl.ds(0, tm * ROW_SLAB)], buf_ref.at[cur, kk], sem.at[cur]).wait()
    gates = gate_ref[...]
    for c in range(ROW_SLAB):
        acc = DN_ALPHA * _load_slab_chunk(x1_ref, 0, tm, c)
        for kk in range(TOP_K):
            acc = acc + gates[:, kk:kk + 1] * _load_slab_chunk(buf_ref.at[cur, kk], 0, tm, c)
        v_ref[:, c * LANES:(c + 1) * LANES] = acc
    y = _layernorm_rows(v_ref[...], g_ref[...], b_ref[...])

    @pl.when(pl.program_id(0) < n_ptiles)
    def _():
        yp_ref[...] = y

    @pl.when(pl.program_id(0) >= n_ptiles)
    def _():
        ysm_ref[...] = y


def _combine(ys, slot_tiles, x1, gates, g_row, b_row, mp):
    m = x1.shape[0] // ROW_SLAB
    tm = TOK_TM
    assert mp % tm == 0 and (m - mp) % tm == 0
    n_pt = mp // tm
    wspec = lambda shape: pl.BlockSpec(shape, lambda i: (0, 0))
    n_t = m // tm
    return pl.pallas_call(
        functools.partial(_combine_body, tm=tm, n_ptiles=n_pt),
        grid=(n_t,),
        in_specs=[
            pl.BlockSpec((None, 1, tm * TOP_K), lambda i: (i, 0, 0), memory_space=pltpu.SMEM),
            pl.BlockSpec((None, 1, tm * TOP_K), lambda i: (jnp.minimum(i + 1, n_t - 1), 0, 0),
                         memory_space=pltpu.SMEM),
            pl.BlockSpec(memory_space=pl.ANY),
            pl.BlockSpec((tm * ROW_SLAB, LANES), lambda i: (i, 0)),
            pl.BlockSpec((tm, LANES), lambda i: (i, 0)),
            wspec((1, D_MODEL)), wspec((1, D_MODEL)),
        ],
        out_specs=[pl.BlockSpec((tm, D_MODEL), lambda i: (jnp.minimum(i, n_pt - 1), 0)),
                   pl.BlockSpec((tm, D_MODEL), lambda i: (jnp.maximum(i - n_pt, 0), 0))],
        out_shape=[jax.ShapeDtypeStruct((mp, D_MODEL), F32),
                   jax.ShapeDtypeStruct((m - mp, D_MODEL), F32)],
        scratch_shapes=[pltpu.VMEM((2, TOP_K, tm * ROW_SLAB, LANES), F32),
                        pltpu.VMEM((tm, D_MODEL), F32), pltpu.SemaphoreType.DMA((2,))],
        compiler_params=_cparams(("arbitrary",)),
        name="moe_combine_ln",
    )(slot_tiles, slot_tiles, ys, x1, gates, g_row, b_row)


def _routing_tables(top_i, m):
    e_ids = jnp.arange(N_EXPERTS, dtype=jnp.int32)
    onehot = (top_i[:, :, None] == e_ids[None, None, :]).astype(jnp.int32)
    mask = jnp.sum(onehot, axis=1)
    incl = jnp.cumsum(mask, axis=0)
    pos = incl - mask
    counts = incl[-1]
    npb = (counts + MOE_PB - 1) // MOE_PB
    padded = npb * MOE_PB
    gend = jnp.cumsum(padded)
    gstart = gend - padded
    slot = jnp.sum(onehot * (gstart[None, None, :] + pos[:, None, :]), axis=2)
    n_rows = (m * TOP_K // MOE_PB + N_EXPERTS) * MOE_PB + MOE_PB
    tail = jnp.stack([gend[-1], (n_rows - gend[-1]) // MOE_PB]).astype(jnp.int32)
    zrow = jnp.concatenate([jnp.where(counts > 0, gend - MOE_PB, -1).astype(jnp.int32), tail])
    n_items = N_EXPERTS + (m * TOP_K // MOE_PB + N_EXPERTS) // MOE_NPB
    ipe = (npb + MOE_NPB - 1) // MOE_NPB
    iend = jnp.cumsum(ipe)
    istart = iend - ipe
    total = iend[-1]
    wid = jnp.arange(n_items, dtype=jnp.int32)
    wclamp = jnp.minimum(wid, total - 1)
    ie = jnp.minimum(jnp.searchsorted(iend, wclamp, side="right"), N_EXPERTS - 1).astype(jnp.int32)
    jn = wclamp - istart[ie]
    irow = (gstart[ie] + jn * MOE_R).astype(jnp.int32)
    insub = jnp.where(wid < total, jnp.clip(npb[ie] - jn * MOE_NPB, 0, MOE_NPB), 0).astype(jnp.int32)
    return slot.astype(jnp.int32), zrow, tail, total.astype(jnp.int32), ie, irow, insub, n_rows


def kernel(x_prompt, x_sample, state_conv_a, state_delta, state_conv_b, state_lru, w_in, conv_a_w,
           a_log, dt_bias, norm_a_w, conv_b_w, conv_b_b, lru_wa, lru_ba, lru_wx, lru_bx, lru_lambda,
           norm_b_w, w_out, ln1_g, ln1_b, w_router, b_router, w_up, b_up, w_down, b_down, ln2_g, ln2_b):
    bp, tp, _ = x_prompt.shape
    bs, ts, _ = x_sample.shape
    mp, ms = bp * tp, bs * ts
    m = mp + ms
    l = 0
    xp2 = x_prompt.reshape(mp, D_MODEL)
    xs2 = x_sample.reshape(ms, D_MODEL)

    wp = _wprep(jnp.transpose(w_in[l]))
    wo = w_out[l].astype(BF16)
    pad_h = lambda v: jnp.zeros((1, LANES), F32).at[0, H_A:2 * H_A].set(v)
    alog_row = pad_h(a_log[l])
    dtb_row = pad_h(dt_bias[l])
    row = lambda v: v.reshape(1, -1)

    proj = _inproj(xp2, xs2, wp)

    qkv_act, xbc, bg = _convact_prompt(proj, bp, tp, conv_a_w[l], conv_b_w[l], row(conv_b_b[l]),
                                       alog_row, dtb_row)
    u, w, qe, kdt, aqk, el = _delta_prep(qkv_act, bg, bp, tp)
    oa_p, sd_p = _delta_seq(u, w, qe, kdt, aqk, el, proj, row(norm_a_w[l]), bp, tp)
    ob_p, h_p = _lru_prompt(xbc, proj, bp, tp, lru_wa[l], lru_wx[l], row(lru_ba[l]), row(lru_bx[l]),
                            row(lru_lambda[l]), row(norm_b_w[l]))
    oa_s, sd_s = _sample_a(proj, mp, bs, ts, state_conv_a[l], state_delta[l], conv_a_w[l],
                           alog_row, dtb_row, row(norm_a_w[l]))
    ob_s, h_s = _lru_sample(proj, mp, bs, ts, state_conv_b[l], state_lru[l], conv_b_w[l],
                            row(conv_b_b[l]), lru_wa[l], lru_wx[l], row(lru_ba[l]), row(lru_bx[l]),
                            row(lru_lambda[l]), row(norm_b_w[l]))

    x1, ti, tg = _outproj(oa_p, oa_s, ob_p, ob_s, xp2, xs2, wo, row(ln1_g[l]), row(ln1_b[l]),
                          w_router[l], row(b_router[l]))

    slot, zrow, tail, n_used, ie, irow, insub, n_rows = _routing_tables(ti[:, :TOP_K], m)
    slot_tiles = slot.reshape(m // TOK_TM, 1, TOK_TM * TOP_K)
    xs_sorted = _dispatch(x1, slot_tiles, zrow, n_rows)
    ys = _moe(xs_sorted, n_used, ie, irow, insub, tail, w_up[l],
              b_up[l].reshape(N_EXPERTS, 2 * MOE_NT1, MOE_TF1), w_down[l],
              b_down[l].reshape(N_EXPERTS, MOE_NT, MOE_TF))
    y_p, y_s = _combine(ys, slot_tiles, x1, tg, row(ln2_g[l]), row(ln2_b[l]), mp)

    y_prompt = y_p.reshape(bp, tp, D_MODEL)
    y_sample = y_s.reshape(bs, ts, D_MODEL)
    nh = CONV_W - 1
    assert tp % SUBLANES == 0 and ts == SUBLANES and nh <= SUBLANES
    pg = proj.reshape(m // SUBLANES, SUBLANES, N_PROJ)
    gp = tp // SUBLANES

    def last_rows(g0, g1, gstep, c0, width):
        return lax.slice(pg, (g0, SUBLANES - nh, c0), (g1, SUBLANES, c0 + width), (gstep, 1, 1))

    ca_p = last_rows(gp - 1, mp // SUBLANES, gp, COL_QKV, 3 * D_A)
    cb_p = last_rows(gp - 1, mp // SUBLANES, gp, COL_XB, D_B)
    ca_s = last_rows(mp // SUBLANES, m // SUBLANES, 1, COL_QKV, 3 * D_A)
    cb_s = last_rows(mp // SUBLANES, m // SUBLANES, 1, COL_XB, D_B)
    return (y_prompt, y_sample,
            ca_p[None], sd_p[None], cb_p[None], h_p.reshape(1, bp, D_B),
            ca_s[None], sd_s[None], cb_s[None], h_s[None])
```

```python
import functools
import math

import jax
import jax.numpy as jnp
from jax import lax
from jax.experimental import pallas as pl
from jax.experimental.pallas import tpu as pltpu

F32 = jnp.float32
BF16 = jnp.bfloat16
HIGHEST = lax.Precision.HIGHEST

D_MODEL = 2048
D_A = 1024
H_A = 8
DK = 128
DV = 128
CONV_W = 4
D_B = 1024
NB = 8
BW = 128
LRU_C = 8.0
N_EXPERTS = 32
TOP_K = 4
D_FF = 2048
SWIGLU_LIMIT = 7.0
SWIGLU_ALPHA = 1.702
DEPTH = 1
DN_ALPHA = (2.0 * DEPTH) ** 0.25
LN_EPS = 1e-5
RMS_EPS = 1e-6
L2_EPS = 1e-6

LANES = 128
SUBLANES = 8
VMEM_LIMIT_BYTES = 56 * 1024 * 1024

COL_QKV = 0
COL_Z = 3 * D_A
COL_XB = 4 * D_A
COL_YB = 4 * D_A + D_B
COL_BA = 4 * D_A + 2 * D_B
N_PROJ = COL_BA + 2 * LANES

INPROJ_TM = 1024
INPROJ_TN = 1280
CONV_L = 256
DELTA_C = 128
DELTA_CPS = 2
OUT_TM = 512
MOE_PB = 128
MOE_RB = 256
MOE_SB = 6
MOE_R = MOE_RB * MOE_SB
MOE_NPB = MOE_R // MOE_PB
MOE_EXACT_ODD = (7, 9, 11)
MOE_TF = 256
MOE_NT = D_FF // MOE_TF
MOE_TF1 = 256
MOE_NT1 = D_FF // MOE_TF1
MOE_VMEM_LIMIT_BYTES = VMEM_LIMIT_BYTES
SAMPLE_NSEQ = 8
TOK_TM = 256
ROW_SLAB = D_MODEL // LANES


def _cparams(sem, vmem=VMEM_LIMIT_BYTES):
    return pltpu.CompilerParams(dimension_semantics=sem, vmem_limit_bytes=vmem)


def _sigmoid(x):
    return 1.0 / (1.0 + jnp.exp(-x))


def _softplus(x):
    return jnp.maximum(x, 0.0) + jnp.log(1.0 + jnp.exp(-jnp.abs(x)))


def _wprep_body(w_ref, o_ref):
    c = 4 * D_A
    nba = 2 * H_A
    o_ref[0:c, :] = w_ref[0:c, :].astype(BF16)
    o_ref[c:c + 2 * D_B, :] = w_ref[c + nba:c + nba + 2 * D_B, :].astype(BF16)
    o_ref[COL_BA:COL_BA + nba, :] = w_ref[c:c + nba, :].astype(BF16)
    o_ref[COL_BA + nba:N_PROJ, :] = jnp.zeros((N_PROJ - COL_BA - nba, o_ref.shape[1]), BF16)


def _wprep(w_t):
    cols = 256
    d_in = w_t.shape[0]
    assert d_in == 4 * D_A + 2 * H_A + 2 * D_B and D_MODEL % cols == 0
    return pl.pallas_call(
        _wprep_body,
        grid=(D_MODEL // cols,),
        in_specs=[pl.BlockSpec((d_in, cols), lambda i: (0, i))],
        out_specs=pl.BlockSpec((N_PROJ, cols), lambda i: (0, i)),
        out_shape=jax.ShapeDtypeStruct((N_PROJ, D_MODEL), BF16),
        compiler_params=_cparams(("arbitrary",)),
        name="wprep",
    )(w_t)


def _inproj_body(xp_ref, xs_ref, w_ref, o_ref, xb_ref, *, n_ptiles):
    i = pl.program_id(0)

    @pl.when(pl.program_id(1) == 0)
    def _():
        @pl.when(i < n_ptiles)
        def _():
            xb_ref[...] = xp_ref[...].astype(BF16)

        @pl.when(i >= n_ptiles)
        def _():
            xb_ref[...] = xs_ref[...].astype(BF16)

    o_ref[...] = _dot_nt(xb_ref[...], w_ref[...])


def _inproj(xp2, xs2, wp):
    mp, ms = xp2.shape[0], xs2.shape[0]
    tm = min(INPROJ_TM, ms)
    assert mp % tm == 0 and ms % tm == 0
    n_pt, n_st = mp // tm, ms // tm
    n_nt = N_PROJ // INPROJ_TN
    return pl.pallas_call(
        functools.partial(_inproj_body, n_ptiles=n_pt),
        grid=(n_pt + n_st, n_nt),
        in_specs=[
            pl.BlockSpec((tm, D_MODEL), lambda i, j: (jnp.minimum(i, n_pt - 1), 0)),
            pl.BlockSpec((tm, D_MODEL), lambda i, j: (jnp.maximum(i - n_pt, 0), 0),
                         pipeline_mode=pl.Buffered(1)),
            pl.BlockSpec((INPROJ_TN, D_MODEL), lambda i, j: (j, 0)),
        ],
        out_specs=pl.BlockSpec((tm, INPROJ_TN), lambda i, j: (i, j)),
        out_shape=jax.ShapeDtypeStruct((mp + ms, N_PROJ), F32),
        scratch_shapes=[pltpu.VMEM((tm, D_MODEL), BF16)],
        compiler_params=_cparams(("arbitrary", "arbitrary")),
        name="inproj",
    )(xp2, xs2, wp)


def _conv_taps(h_ref, w_ref, c0, width, rows, base):
    acc = h_ref[base:base + rows, c0:c0 + width] * w_ref[CONV_W - 1:CONV_W, c0:c0 + width]
    for j in range(1, CONV_W):
        acc = acc + (h_ref[base - j:base - j + rows, c0:c0 + width]
                     * w_ref[CONV_W - 1 - j:CONV_W - j, c0:c0 + width])
    return acc


def _qkv_activation(acc, blk):
    a = acc * _sigmoid(acc)
    if blk < 2 * H_A:
        a = a * lax.rsqrt(jnp.sum(a * a, axis=-1, keepdims=True) + L2_EPS)
        if blk < H_A:
            a = a * (DK ** -0.5)
    return a


def _beta_g(ba, alog_ref, dtb_ref):
    lane = lax.broadcasted_iota(jnp.int32, ba.shape, 1)
    beta = _sigmoid(ba)
    g = -jnp.exp(alog_ref[...]) * _softplus(ba + dtb_ref[...])
    return jnp.where(lane < H_A, beta, jnp.where(lane < 2 * H_A, g, 0.0))


def _convact_prompt_body(qkv_ref, xb_ref, ba_ref, caw_ref, cbw_ref, cbb_ref, alog_ref, dtb_ref,
                         qkvo_ref, xbo_ref, bgo_ref, hq_ref, hx_ref, *, rows):
    t = pl.program_id(1)
    hdr = SUBLANES

    @pl.when(t == 0)
    def _():
        hq_ref[0:hdr, :] = jnp.zeros((hdr, 3 * D_A), F32)
        hx_ref[0:hdr, :] = jnp.zeros((hdr, D_B), F32)

    @pl.when(t > 0)
    def _():
        hq_ref[0:hdr, :] = hq_ref[rows:rows + hdr, :]
        hx_ref[0:hdr, :] = hx_ref[rows:rows + hdr, :]

    hq_ref[hdr:hdr + rows, :] = qkv_ref[...]
    hx_ref[hdr:hdr + rows, :] = xb_ref[...]
    for blk in range(3 * H_A):
        c0 = blk * DK
        acc = _conv_taps(hq_ref, caw_ref, c0, DK, rows, hdr)
        qkvo_ref[:, c0:c0 + DK] = _qkv_activation(acc, blk)
    for blk in range(NB):
        c0 = blk * BW
        xbo_ref[:, c0:c0 + BW] = _conv_taps(hx_ref, cbw_ref, c0, BW, rows, hdr) + cbb_ref[:, c0:c0 + BW]
    bgo_ref[...] = _beta_g(ba_ref[...], alog_ref, dtb_ref)


def _convact_prompt(proj, bsz, t_len, caw, cbw, cbb, alog_row, dtb_row):
    rows = min(CONV_L, t_len)
    assert t_len % rows == 0
    nt = t_len // rows
    mp = bsz * t_len
    wspec = lambda shape: pl.BlockSpec(shape, lambda b, t: (0, 0))
    return pl.pallas_call(
        functools.partial(_convact_prompt_body, rows=rows),
        grid=(bsz, nt),
        in_specs=[
            pl.BlockSpec((rows, 3 * D_A), lambda b, t: (b * nt + t, COL_QKV // (3 * D_A))),
            pl.BlockSpec((rows, D_B), lambda b, t: (b * nt + t, COL_XB // D_B)),
            pl.BlockSpec((rows, LANES), lambda b, t: (b * nt + t, COL_BA // LANES)),
            wspec((CONV_W, 3 * D_A)), wspec((CONV_W, D_B)), wspec((1, D_B)),
            wspec((1, LANES)), wspec((1, LANES)),
        ],
        out_specs=[
            pl.BlockSpec((rows, 3 * D_A), lambda b, t: (b * nt + t, 0)),
            pl.BlockSpec((rows, D_B), lambda b, t: (b * nt + t, 0)),
            pl.BlockSpec((rows, LANES), lambda b, t: (b * nt + t, 0)),
        ],
        out_shape=[
            jax.ShapeDtypeStruct((mp, 3 * D_A), F32),
            jax.ShapeDtypeStruct((mp, D_B), F32),
            jax.ShapeDtypeStruct((mp, LANES), F32),
        ],
        scratch_shapes=[pltpu.VMEM((SUBLANES + rows + SUBLANES, 3 * D_A), F32),
                        pltpu.VMEM((SUBLANES + rows + SUBLANES, D_B), F32)],
        compiler_params=_cparams(("arbitrary", "arbitrary")),
        name="convact_prompt",
    )(proj, proj, proj, caw, cbw, cbb, alog_row, dtb_row)


def _dot_nt(a, b, precision=None):
    return lax.dot_general(a, b, (((1,), (1,)), ((), ())), precision=precision,
                           preferred_element_type=F32)


def _split_bf16(x):
    hi = x.astype(BF16)
    return hi, (x - hi.astype(F32)).astype(BF16)


def _dot3(a, b):
    ah, al = _split_bf16(a)
    bh, bl = _split_bf16(b)
    return (jnp.dot(ah, bh, preferred_element_type=F32)
            + (jnp.dot(al, bh, preferred_element_type=F32) + jnp.dot(ah, bl, preferred_element_type=F32)))


def _delta_prep_body(qkv_ref, bg_ref, u_ref, w_ref, qe_ref, kdt_ref, aqk_ref, el_ref, *, c, ncc):
    row = lax.broadcasted_iota(jnp.int32, (c, c), 0)
    col = lax.broadcasted_iota(jnp.int32, (c, c), 1)
    incl = row >= col
    strict = row > col
    eye = (row == col).astype(F32)
    tril = incl.astype(F32)
    eye_l = (lax.broadcasted_iota(jnp.int32, (LANES, LANES), 0)
             == lax.broadcasted_iota(jnp.int32, (LANES, LANES), 1)).astype(F32)

    pairs = [(cc, h) for cc in range(ncc) for h in range(H_A)]
    rows = [slice(cc * c, (cc + 1) * c) for cc in range(ncc)]
    bgs = [bg_ref[rs, :] for rs in rows]
    gc_alls = [jnp.dot(tril, bg, precision=HIGHEST, preferred_element_type=F32) for bg in bgs]
    gc_ts = [_dot_nt(eye_l, g, precision=HIGHEST) for g in gc_alls]
    lmats, rhss = [], []
    for cc, h in pairs:
        rs, bg, gc_all, gc_t = rows[cc], bgs[cc], gc_alls[cc], gc_ts[cc]
        q = qkv_ref[rs, h * DK:(h + 1) * DK]
        k = qkv_ref[rs, (H_A + h) * DK:(H_A + h + 1) * DK]
        v = qkv_ref[rs, (2 * H_A + h) * DK:(2 * H_A + h + 1) * DK]
        beta = bg[:, h:h + 1]
        gc = gc_all[:, H_A + h:H_A + h + 1]
        g_last = gc_all[c - 1:c, H_A + h:H_A + h + 1]
        diff = gc - gc_t[H_A + h:H_A + h + 1, :]
        decay = jnp.where(incl, jnp.exp(jnp.where(incl, diff, 0.0)), 0.0)
        kb = k * beta
        egc = jnp.exp(gc)
        kbf = k.astype(BF16)
        kk = _dot_nt(kbf, kbf)
        lmats.append(jnp.where(strict, kk * beta * decay, 0.0))
        rhss.append(jnp.concatenate([v * beta, kb * egc], axis=1))
        qe_ref[rs, h * DK:(h + 1) * DK] = (q * egc).astype(BF16)
        aqk_ref[cc, h] = (_dot_nt(q.astype(BF16), kbf) * decay).astype(BF16)
        kdec = k * jnp.exp(g_last - gc)
        kdt_ref[cc, h] = kdec.astype(BF16)
        el_ref[cc, h] = jnp.broadcast_to(jnp.exp(g_last), (SUBLANES, LANES))
    xbs = [(-lm).astype(BF16) for lm in lmats]
    t0s = [eye - lm for lm in lmats]
    p = 2
    while p < c:
        xbs = [jnp.dot(xb, xb, preferred_element_type=F32).astype(BF16) for xb in xbs]
        t0s = [t0 + jnp.dot(t0.astype(BF16), xb, preferred_element_type=F32) for t0, xb in zip(t0s, xbs)]
        p *= 2
    t0bs = [t0.astype(BF16) for t0 in t0s]
    sol0s = [jnp.dot(t0b, rhs.astype(BF16), preferred_element_type=F32) for t0b, rhs in zip(t0bs, rhss)]
    resids = [rhs - sol0 - _dot3(lm, sol0) for rhs, sol0, lm in zip(rhss, sol0s, lmats)]
    for p_, (cc, h) in enumerate(pairs):
        sol = sol0s[p_] + jnp.dot(t0bs[p_], resids[p_].astype(BF16), preferred_element_type=F32)
        u_ref[rows[cc], h * DV:(h + 1) * DV] = sol[:, :DV]
        w_ref[rows[cc], h * DK:(h + 1) * DK] = sol[:, DV:].astype(BF16)


def _delta_prep(qkv_act, bg, bsz, t_len):
    c = DELTA_C
    assert t_len % c == 0
    nc = t_len // c
    mp = bsz * t_len
    n_chunks = bsz * nc
    ncc = DELTA_CPS
    assert n_chunks % ncc == 0
    rowspec = lambda w: pl.BlockSpec((ncc * c, w), lambda i: (i, 0))
    return pl.pallas_call(
        functools.partial(_delta_prep_body, c=c, ncc=ncc),
        grid=(n_chunks // ncc,),
        in_specs=[rowspec(3 * D_A), rowspec(LANES)],
        out_specs=[
            rowspec(D_A), rowspec(D_A), rowspec(D_A),
            pl.BlockSpec((ncc, H_A, c, DK), lambda i: (i, 0, 0, 0)),
            pl.BlockSpec((ncc, H_A, c, c), lambda i: (i, 0, 0, 0)),
            pl.BlockSpec((ncc, H_A, SUBLANES, LANES), lambda i: (i, 0, 0, 0)),
        ],
        out_shape=[
            jax.ShapeDtypeStruct((mp, D_A), F32),
            jax.ShapeDtypeStruct((mp, D_A), BF16),
            jax.ShapeDtypeStruct((mp, D_A), BF16),
            jax.ShapeDtypeStruct((n_chunks, H_A, c, DK), BF16),
            jax.ShapeDtypeStruct((n_chunks, H_A, c, c), BF16),
            jax.ShapeDtypeStruct((n_chunks, H_A, SUBLANES, LANES), F32),
        ],
        compiler_params=_cparams(("arbitrary",)),
        name="delta_prep",
    )(qkv_act, bg)


def _gated_rmsnorm(o, z, nw):
    on = o * lax.rsqrt(jnp.mean(o * o, axis=-1, keepdims=True) + RMS_EPS) * nw
    return on * (z * _sigmoid(z))


def _delta_seq_body(*refs, bsz):
    u_ref, w_ref, qe_ref, kdt_ref, aqk_ref, el_ref = refs[:6]
    z_refs = refs[6:6 + bsz]
    nw_ref, oa_ref, sfin_ref, s_ref = refs[6 + bsz:]
    ci = pl.program_id(0)

    @pl.when(ci == 0)
    def _():
        s_ref[...] = jnp.zeros(s_ref.shape, F32)

    nw = nw_ref[...]
    pairs = [(b, h) for b in range(bsz) for h in range(H_A)]
    sl = lambda h: slice(h * DK, (h + 1) * DK)
    ss = [s_ref[b, h] for b, h in pairs]
    sbs = [s.astype(BF16) for s in ss]
    vbs = [(u_ref[b, :, sl(h)] - jnp.dot(w_ref[b, :, sl(h)], sb, preferred_element_type=F32)).astype(BF16)
           for (b, h), sb in zip(pairs, sbs)]
    for p, (b, h) in enumerate(pairs):
        s_ref[b, h] = ss[p] * el_ref[b, h][0:1, 0:1] + lax.dot_general(
            kdt_ref[b, h], vbs[p], (((0,), (0,)), ((), ())), preferred_element_type=F32)
    for p, (b, h) in enumerate(pairs):
        o = (jnp.dot(qe_ref[b, :, sl(h)], sbs[p], preferred_element_type=F32)
             + jnp.dot(aqk_ref[b, h], vbs[p], preferred_element_type=F32))
        oa_ref[b, :, sl(h)] = _gated_rmsnorm(o, z_refs[b][:, sl(h)], nw).astype(BF16)

    @pl.when(ci == pl.num_programs(0) - 1)
    def _():
        sfin_ref[...] = s_ref[...]


def _delta_seq(u, w, qe, kdt, aqk, el, proj, nw_row, bsz, t_len):
    c = DELTA_C
    nc = t_len // c
    mp = bsz * t_len
    rows4 = lambda a: a.reshape(bsz, nc, c, a.shape[-1])
    chunk5 = lambda a: a.reshape((bsz, nc) + a.shape[1:])
    rowspec = pl.BlockSpec((bsz, None, c, D_A), lambda i: (0, i, 0, 0))
    chunkspec = lambda a, bb: pl.BlockSpec((bsz, None, H_A, a, bb), lambda i: (0, i, 0, 0, 0))
    zspecs = [pl.BlockSpec((c, D_A), functools.partial(lambda i, b: (b * nc + i, COL_Z // D_A), b=b))
              for b in range(bsz)]
    oa, sfin = pl.pallas_call(
        functools.partial(_delta_seq_body, bsz=bsz),
        grid=(nc,),
        in_specs=[rowspec, rowspec, rowspec,
                  chunkspec(c, DK), chunkspec(c, c), chunkspec(SUBLANES, LANES)]
                 + zspecs + [pl.BlockSpec((1, DV), lambda i: (0, 0))],
        out_specs=[
            rowspec,
            pl.BlockSpec((bsz, H_A, DK, DV), lambda i: (0, 0, 0, 0)),
        ],
        out_shape=[
            jax.ShapeDtypeStruct((bsz, nc, c, D_A), BF16),
            jax.ShapeDtypeStruct((bsz, H_A, DK, DV), F32),
        ],
        scratch_shapes=[pltpu.VMEM((bsz, H_A, DK, DV), F32)],
        compiler_params=_cparams(("arbitrary",)),
        name="delta_seq",
    )(rows4(u), rows4(w), rows4(qe), chunk5(kdt), chunk5(aqk), chunk5(el), *([proj] * bsz), nw_row)
    return oa.reshape(mp, D_A), sfin


def _sample_a_body(qkv_ref, z_ref, ba_ref, hist_ref, s0_ref, caw_ref, alog_ref, dtb_ref, nw_ref,
                   oa_ref, s1_ref, hq_ref, *, t_len, nseq):
    hdr = SUBLANES
    nh = CONV_W - 1
    rows = [slice(si * t_len, (si + 1) * t_len) for si in range(nseq)]
    for si in range(nseq):
        hq_ref[si, 0:hdr, :] = jnp.zeros((hdr, 3 * D_A), F32)
        hq_ref[si, hdr - nh:hdr, :] = hist_ref[si]
        hq_ref[si, hdr:hdr + t_len, :] = qkv_ref[rows[si], :]
    bg = _beta_g(ba_ref[...], alog_ref, dtb_ref)
    rloc = lax.rem(lax.broadcasted_iota(jnp.int32, bg.shape, 0), t_len)
    gc_all = bg
    sft = 1
    while sft < t_len:
        gc_all = gc_all + jnp.where(rloc >= sft, pltpu.roll(gc_all, sft, axis=0), 0.0)
        sft *= 2
    nw = nw_ref[...]
    rowc = lax.broadcasted_iota(jnp.int32, (t_len, 1), 0)
    pairs = [(si, h) for si in range(nseq) for h in range(H_A)]
    npair = range(len(pairs))
    act = lambda si, blk: _qkv_activation(_conv_taps(hq_ref.at[si], caw_ref, blk * DK, DK, t_len, hdr), blk)
    qs = [act(si, h) for si, h in pairs]
    ks = [act(si, H_A + h) for si, h in pairs]
    vs = [act(si, 2 * H_A + h) for si, h in pairs]
    betas = [bg[rows[si], h:h + 1] for si, h in pairs]
    gcs = [gc_all[rows[si], H_A + h:H_A + h + 1] for si, h in pairs]
    glasts = [gc[t_len - 1:t_len, :] for gc in gcs]
    kbs = [k * b for k, b in zip(ks, betas)]
    egcs = [jnp.exp(gc) for gc in gcs]
    sols = [jnp.concatenate([v * b, kb * e], axis=1) for v, b, kb, e in zip(vs, betas, kbs, egcs)]
    acols = [[] for _ in npair]
    lcols = [[] for _ in npair]
    for j in range(t_len):
        for p in npair:
            kj = ks[p][j:j + 1, :]
            dcol = jnp.exp(jnp.where(rowc >= j, gcs[p] - gcs[p][j:j + 1, :], 0.0))
            acols[p].append(jnp.where(rowc >= j, jnp.sum(qs[p] * kj, axis=-1, keepdims=True) * dcol, 0.0))
            lcols[p].append(jnp.where(rowc > j, jnp.sum(kbs[p] * kj, axis=-1, keepdims=True) * dcol, 0.0))
    for j in range(t_len - 1):
        sols = [sol - lcols[p][j] * sol[j:j + 1, :] for p, sol in enumerate(sols)]
    ss = [s0_ref[si, h] for si, h in pairs]
    sbs = [s.astype(BF16) for s in ss]
    v_news = [sol[:, :DV] - jnp.dot(sol[:, DV:].astype(BF16), sb, preferred_element_type=F32)
              for sol, sb in zip(sols, sbs)]
    os_ = [jnp.dot((q * e).astype(BF16), sb, preferred_element_type=F32) for q, e, sb in zip(qs, egcs, sbs)]
    for j in range(t_len):
        os_ = [o + acols[p][j] * v_news[p][j:j + 1, :] for p, o in enumerate(os_)]
    upds = [lax.dot_general((k * jnp.exp(gl - gc)).astype(BF16), vn.astype(BF16), (((0,), (0,)), ((), ())),
                            preferred_element_type=F32)
            for k, gl, gc, vn in zip(ks, glasts, gcs, v_news)]
    for p, (si, h) in enumerate(pairs):
        s1_ref[si, h] = ss[p] * jnp.exp(glasts[p]) + upds[p]
        sl = slice(h * DV, (h + 1) * DV)
        oa_ref[rows[si], sl] = _gated_rmsnorm(os_[p], z_ref[rows[si], sl], nw).astype(BF16)


def _sample_a(proj, row0, bsz, t_len, hist, s0, caw, alog_row, dtb_row, nw_row):
    nseq = SAMPLE_NSEQ
    tr = nseq * t_len
    assert t_len == SUBLANES and row0 % tr == 0 and bsz % nseq == 0
    r0 = row0 // tr
    wspec = lambda shape: pl.BlockSpec(shape, lambda b: (0,) * len(shape))
    return pl.pallas_call(
        functools.partial(_sample_a_body, t_len=t_len, nseq=nseq),
        grid=(bsz // nseq,),
        in_specs=[
            pl.BlockSpec((tr, 3 * D_A), lambda b: (r0 + b, COL_QKV // (3 * D_A))),
            pl.BlockSpec((tr, D_A), lambda b: (r0 + b, COL_Z // D_A)),
            pl.BlockSpec((tr, LANES), lambda b: (r0 + b, COL_BA // LANES)),
            pl.BlockSpec((nseq, CONV_W - 1, 3 * D_A), lambda b: (b, 0, 0)),
            pl.BlockSpec((nseq, H_A, DK, DV), lambda b: (b, 0, 0, 0)),
            wspec((CONV_W, 3 * D_A)), wspec((1, LANES)), wspec((1, LANES)), wspec((1, DV)),
        ],
        out_specs=[
            pl.BlockSpec((tr, D_A), lambda b: (b, 0)),
            pl.BlockSpec((nseq, H_A, DK, DV), lambda b: (b, 0, 0, 0)),
        ],
        out_shape=[
            jax.ShapeDtypeStruct((bsz * t_len, D_A), BF16),
            jax.ShapeDtypeStruct((bsz, H_A, DK, DV), F32),
        ],
        scratch_shapes=[pltpu.VMEM((nseq, 2 * SUBLANES, 3 * D_A), F32)],
        compiler_params=_cparams(("arbitrary",)),
        name="sample_a",
    )(proj, proj, proj, hist, s0, caw, alog_row, dtb_row, nw_row)


def _lru_coeffs(xc, wa_ref, wx_ref, lba_ref, lbx_ref, lam_ref):
    parts = []
    for n in range(NB):
        sl = slice(n * BW, (n + 1) * BW)
        xn = xc[:, sl]
        xnb = xn.astype(BF16)
        gr = _sigmoid(jnp.dot(xnb, wa_ref[n].astype(BF16), preferred_element_type=F32) + lba_ref[:, sl])
        gi = _sigmoid(jnp.dot(xnb, wx_ref[n].astype(BF16), preferred_element_type=F32) + lbx_ref[:, sl])
        log_a = -LRU_C * gr * _softplus(-lam_ref[:, sl])
        parts.append((jnp.exp(log_a), jnp.sqrt(1.0 - jnp.exp(2.0 * log_a)), gi * xn))
    return parts


def _group_scan(a, b):
    rowi = lax.broadcasted_iota(jnp.int32, a.shape, 0)
    sft = 1
    while sft < SUBLANES:
        keep = rowi >= sft
        b = b + a * jnp.where(keep, pltpu.roll(b, sft, axis=0), 0.0)
        a = a * jnp.where(keep, pltpu.roll(a, sft, axis=0), 1.0)
        sft *= 2
    return a, b


def _gelu_tanh(x):
    return 0.5 * x * (1.0 + jnp.tanh(math.sqrt(2.0 / math.pi) * (x + 0.044715 * x * x * x)))


def _lru_finish(h, yb, nbw):
    hg = h * _gelu_tanh(yb)
    return (hg * lax.rsqrt(jnp.mean(hg * hg, axis=-1, keepdims=True) + RMS_EPS) * nbw).astype(BF16)


def _lru_prompt_body(xc_ref, yb_ref, wa_ref, wx_ref, lba_ref, lbx_ref, lam_ref, nbw_ref,
                     ob_ref, hfin_ref, a_ref, b_ref, h_ref, carry_ref, *, rows):
    t = pl.program_id(1)
    parts = _lru_coeffs(xc_ref[...], wa_ref, wx_ref, lba_ref, lbx_ref, lam_ref)
    rowi = lax.broadcasted_iota(jnp.int32, (rows, BW), 0)
    first = jnp.logical_and(rowi == 0, t == 0)
    for n in range(NB):
        sl = slice(n * BW, (n + 1) * BW)
        a_n, mult_n, gix_n = parts[n]
        a_ref[:, sl] = jnp.where(first, 0.0, a_n)
        b_ref[:, sl] = jnp.where(first, 1.0, mult_n) * gix_n

    @pl.when(t == 0)
    def _():
        carry_ref[...] = jnp.zeros(carry_ref.shape, F32)

    def group(gidx, carry):
        r0 = pl.multiple_of(gidx * SUBLANES, SUBLANES)
        ag, bg = _group_scan(a_ref[pl.ds(r0, SUBLANES), :], b_ref[pl.ds(r0, SUBLANES), :])
        hg = ag * carry + bg
        h_ref[pl.ds(r0, SUBLANES), :] = hg
        return jnp.broadcast_to(hg[SUBLANES - 1:SUBLANES, :], hg.shape)

    carry = lax.fori_loop(0, rows // SUBLANES, group, carry_ref[...])
    carry_ref[...] = carry
    ob_ref[...] = _lru_finish(h_ref[...], yb_ref[...], nbw_ref[...])

    @pl.when(t == pl.num_programs(1) - 1)
    def _():
        hfin_ref[...] = carry[0:1, :]


def _lru_prompt(xbc, proj, bsz, t_len, wa, wx, lba, lbx, lam, nbw):
    rows = min(CONV_L, t_len)
    nt = t_len // rows
    mp = bsz * t_len
    wspec = lambda shape: pl.BlockSpec(shape, lambda b, t: (0,) * len(shape))
    return pl.pallas_call(
        functools.partial(_lru_prompt_body, rows=rows),
        grid=(bsz, nt),
        in_specs=[
            pl.BlockSpec((rows, D_B), lambda b, t: (b * nt + t, 0)),
            pl.BlockSpec((rows, D_B), lambda b, t: (b * nt + t, COL_YB // D_B)),
            wspec((NB, BW, BW)), wspec((NB, BW, BW)),
            wspec((1, D_B)), wspec((1, D_B)), wspec((1, D_B)), wspec((1, D_B)),
        ],
        out_specs=[
            pl.BlockSpec((rows, D_B), lambda b, t: (b * nt + t, 0)),
            pl.BlockSpec((None, 1, D_B), lambda b, t: (b, 0, 0)),
        ],
        out_shape=[
            jax.ShapeDtypeStruct((mp, D_B), BF16),
            jax.ShapeDtypeStruct((bsz, 1, D_B), F32),
        ],
        scratch_shapes=[pltpu.VMEM((rows, D_B), F32), pltpu.VMEM((rows, D_B), F32),
                        pltpu.VMEM((rows, D_B), F32), pltpu.VMEM((SUBLANES, D_B), F32)],
        compiler_params=_cparams(("arbitrary", "arbitrary")),
        name="lru_prompt",
    )(xbc, proj, wa, wx, lba, lbx, lam, nbw)


def _lru_sample_body(xb_ref, yb_ref, hist_ref, h0_ref, cbw_ref, cbb_ref, wa_ref, wx_ref, lba_ref,
                     lbx_ref, lam_ref, nbw_ref, ob_ref, h1_ref, hx_ref, a_ref, b_ref, h_ref,
                     *, nseq, t_len):
    hdr = SUBLANES
    nh = CONV_W - 1
    hx_ref[0:hdr, :] = jnp.zeros((hdr, D_B), F32)

    def conv_seq(si, _):
        r0 = pl.multiple_of(si * t_len, t_len)
        hx_ref[hdr - nh:hdr, :] = hist_ref[si]
        hx_ref[hdr:hdr + t_len, :] = xb_ref[pl.ds(r0, t_len), :]
        for n in range(NB):
            c0 = n * BW
            h_ref[pl.ds(r0, t_len), c0:c0 + BW] = (_conv_taps(hx_ref, cbw_ref, c0, BW, t_len, hdr)
                                                  + cbb_ref[:, c0:c0 + BW])
        return 0

    lax.fori_loop(0, nseq, conv_seq, 0)
    parts = _lru_coeffs(h_ref[...], wa_ref, wx_ref, lba_ref, lbx_ref, lam_ref)
    for n in range(NB):
        sl = slice(n * BW, (n + 1) * BW)
        a_n, mult_n, gix_n = parts[n]
        a_ref[:, sl] = a_n
        b_ref[:, sl] = mult_n * gix_n

    def seq(si, _):
        r0 = pl.multiple_of(si * t_len, t_len)
        ag, bg = _group_scan(a_ref[pl.ds(r0, t_len), :], b_ref[pl.ds(r0, t_len), :])
        hg = ag * h0_ref[pl.ds(si, 1), :] + bg
        h_ref[pl.ds(r0, t_len), :] = hg
        h1_ref[pl.ds(si, 1), :] = hg[t_len - 1:t_len, :]
        return 0

    lax.fori_loop(0, nseq, seq, 0)
    ob_ref[...] = _lru_finish(h_ref[...], yb_ref[...], nbw_ref[...])


def _lru_sample(proj, row0, bsz, t_len, hist, h0, cbw, cbb, wa, wx, lba, lbx, lam, nbw):
    assert t_len == SUBLANES
    ms = bsz * t_len
    assert row0 % ms == 0
    rblk = row0 // ms
    wspec = lambda shape: pl.BlockSpec(shape, lambda i: (0,) * len(shape))
    return pl.pallas_call(
        functools.partial(_lru_sample_body, nseq=bsz, t_len=t_len),
        grid=(1,),
        in_specs=[
            pl.BlockSpec((ms, D_B), lambda i: (rblk, COL_XB // D_B)),
            pl.BlockSpec((ms, D_B), lambda i: (rblk, COL_YB // D_B)),
            wspec((bsz, CONV_W - 1, D_B)), wspec((bsz, D_B)),
            wspec((CONV_W, D_B)), wspec((1, D_B)),
            wspec((NB, BW, BW)), wspec((NB, BW, BW)),
            wspec((1, D_B)), wspec((1, D_B)), wspec((1, D_B)), wspec((1, D_B)),
        ],
        out_specs=[wspec((ms, D_B)), wspec((bsz, D_B))],
        out_shape=[
            jax.ShapeDtypeStruct((ms, D_B), BF16),
            jax.ShapeDtypeStruct((bsz, D_B), F32),
        ],
        scratch_shapes=[pltpu.VMEM((2 * SUBLANES, D_B), F32), pltpu.VMEM((ms, D_B), F32),
                        pltpu.VMEM((ms, D_B), F32), pltpu.VMEM((ms, D_B), F32)],
        compiler_params=_cparams(("arbitrary",)),
        name="lru_sample",
    )(proj, proj, hist, h0, cbw, cbb, wa, wx, lba, lbx, lam, nbw)


def _layernorm_rows(v, g, b):
    mu = jnp.mean(v, axis=-1, keepdims=True)
    d = v - mu
    var = jnp.mean(d * d, axis=-1, keepdims=True)
    return d * lax.rsqrt(var + LN_EPS) * g + b


def _store_slabs(slab_ref, base, val):
    n = val.shape[0]
    for s in range(ROW_SLAB):
        slab_ref[pl.ds(base + s, n, stride=ROW_SLAB), :] = val[:, s * LANES:(s + 1) * LANES]


def _load_slab_chunk(slab_ref, base, n, s):
    return slab_ref[pl.ds(base + s, n, stride=ROW_SLAB), :]


def _outproj_body(oap_ref, oas_ref, obp_ref, obs_ref, xp_ref, xs_ref, wo_ref, g_ref, b_ref,
                  wr_ref, br_ref, x1_ref, ti_ref, tg_ref, *, n_ptiles):
    i = pl.program_id(0)

    def run(oa_ref, ob_ref, x_ref):
        mix = (jnp.dot(oa_ref[...], wo_ref[0:D_A, :], preferred_element_type=F32)
               + jnp.dot(ob_ref[...], wo_ref[D_A:D_A + D_B, :], preferred_element_type=F32))
        y = _layernorm_rows(DN_ALPHA * x_ref[...] + mix, g_ref[...], b_ref[...])
        _store_slabs(x1_ref, 0, y)
        logits = _dot3(y, wr_ref[...]) + br_ref[...]
        lane = lax.broadcasted_iota(jnp.int32, logits.shape, 1)
        lane_o = lax.broadcasted_iota(jnp.int32, ti_ref.shape, 1)
        cur = logits
        ti = jnp.zeros(ti_ref.shape, jnp.int32)
        tv = jnp.zeros(tg_ref.shape, F32)
        v0 = None
        den = None
        for kk in range(TOP_K):
            m = jnp.max(cur, axis=-1, keepdims=True)
            idx = jnp.min(jnp.where(cur == m, lane, N_EXPERTS), axis=-1, keepdims=True)
            cur = jnp.where(lane == idx, -jnp.inf, cur)
            if kk == 0:
                v0 = m
            e = jnp.exp(m - v0)
            den = e if den is None else den + e
            ti = jnp.where(lane_o == kk, idx, ti)
            tv = jnp.where(lane_o == kk, e, tv)
        ti_ref[...] = ti
        tg_ref[...] = tv / den

    @pl.when(i < n_ptiles)
    def _():
        run(oap_ref, obp_ref, xp_ref)

    @pl.when(i >= n_ptiles)
    def _():
        run(oas_ref, obs_ref, xs_ref)


def _outproj(oa_p, oa_s, ob_p, ob_s, xp2, xs2, wo, g_row, b_row, wr, br_row):
    mp, ms = xp2.shape[0], xs2.shape[0]
    tm = min(OUT_TM, ms)
    assert mp % tm == 0 and ms % tm == 0
    n_pt, n_st = mp // tm, ms // tm
    pmap = lambda i: (jnp.minimum(i, n_pt - 1), 0)
    smap = lambda i: (jnp.maximum(i - n_pt, 0), 0)
    wspec = lambda shape: pl.BlockSpec(shape, lambda i: (0, 0))
    m = mp + ms
    return pl.pallas_call(
        functools.partial(_outproj_body, n_ptiles=n_pt),
        grid=(n_pt + n_st,),
        in_specs=[
            pl.BlockSpec((tm, D_A), pmap), pl.BlockSpec((tm, D_A), smap),
            pl.BlockSpec((tm, D_B), pmap), pl.BlockSpec((tm, D_B), smap),
            pl.BlockSpec((tm, D_MODEL), pmap), pl.BlockSpec((tm, D_MODEL), smap),
            wspec((D_A + D_B, D_MODEL)), wspec((1, D_MODEL)), wspec((1, D_MODEL)),
            wspec((D_MODEL, N_EXPERTS)), wspec((1, N_EXPERTS)),
        ],
        out_specs=[
            pl.BlockSpec((tm * ROW_SLAB, LANES), lambda i: (i, 0)),
            pl.BlockSpec((tm, LANES), lambda i: (i, 0)),
            pl.BlockSpec((tm, LANES), lambda i: (i, 0)),
        ],
        out_shape=[
            jax.ShapeDtypeStruct((m * ROW_SLAB, LANES), F32),
            jax.ShapeDtypeStruct((m, LANES), jnp.int32),
            jax.ShapeDtypeStruct((m, LANES), F32),
        ],
        compiler_params=_cparams(("arbitrary",)),
        name="outproj_ln_router",
    )(oa_p, oa_s, ob_p, ob_s, xp2, xs2, wo, g_row, b_row, wr, br_row)


def _zero_tail(zbuf_ref, dst_ref, tail_row, n_blocks, sem):
    scale = ROW_SLAB if len(dst_ref.shape) == 2 else 1

    def tail_copy(t):
        z0 = pl.multiple_of((tail_row + t * MOE_PB) * scale, MOE_PB * scale)
        return pltpu.make_async_copy(zbuf_ref, dst_ref.at[pl.ds(z0, MOE_PB * scale)], sem)

    def start(t, _):
        tail_copy(t).start()
        return 0

    def wait(t, _):
        tail_copy(t).wait()
        return 0

    lax.fori_loop(0, n_blocks, start, 0)
    lax.fori_loop(0, n_blocks, wait, 0)


def _dispatch_body(zrow_ref, slot_ref, x_ref, xs_ref, zbuf_ref, zsem, rsem, *, tm):
    i = pl.program_id(0)

    @pl.when(i == 0)
    def _():
        zbuf_ref[...] = jnp.zeros(zbuf_ref.shape, F32)

        def zero_copy(e):
            z0 = pl.multiple_of(zrow_ref[e], MOE_PB)
            return pltpu.make_async_copy(zbuf_ref, xs_ref.at[pl.ds(z0, MOE_PB)], zsem)

        for e in range(N_EXPERTS):
            @pl.when(zrow_ref[e] >= 0)
            def _():
                zero_copy(e).start()
        for e in range(N_EXPERTS):
            @pl.when(zrow_ref[e] >= 0)
            def _():
                zero_copy(e).wait()
        _zero_tail(zbuf_ref, xs_ref, zrow_ref[N_EXPERTS], zrow_ref[N_EXPERTS + 1], zsem)

    def row_copy(r, dst):
        src0 = pl.multiple_of(r * ROW_SLAB, ROW_SLAB)
        return pltpu.make_async_copy(x_ref.at[pl.ds(src0, ROW_SLAB)], xs_ref.at[dst], rsem)

    def issue(r, _):
        for kk in range(TOP_K):
            row_copy(r, slot_ref[0, r * TOP_K + kk]).start(priority=kk % 2)
        return 0

    lax.fori_loop(0, tm, issue, 0, unroll=8)

    for kk in range(TOP_K):
        pltpu.make_async_copy(xs_ref.at[pl.ds(0, tm)], xs_ref.at[pl.ds(0, tm)], rsem).wait()


def _dispatch(x1, slot_tiles, zrow, n_rows):
    m = x1.shape[0] // ROW_SLAB
    tm = TOK_TM
    assert m % tm == 0
    return pl.pallas_call(
        functools.partial(_dispatch_body, tm=tm),
        grid_spec=pltpu.PrefetchScalarGridSpec(
            num_scalar_prefetch=1,
            grid=(m // tm,),
            in_specs=[
                pl.BlockSpec((None, 1, tm * TOP_K), lambda i, z: (i, 0, 0), memory_space=pltpu.SMEM),
                pl.BlockSpec((tm * ROW_SLAB, LANES), lambda i, z: (i, 0)),
            ],
            out_specs=pl.BlockSpec(memory_space=pl.ANY),
            scratch_shapes=[pltpu.VMEM((MOE_PB, ROW_SLAB, LANES), F32),
                            pltpu.SemaphoreType.DMA(()), pltpu.SemaphoreType.DMA(())],
        ),
        out_shape=jax.ShapeDtypeStruct((n_rows, ROW_SLAB, LANES), F32),
        compiler_params=_cparams(("arbitrary",)),
        name="moe_dispatch",
    )(zrow, slot_tiles, x1)


def _for_row_blocks(npb, fn):
    for k in range(1, MOE_SB + 1):
        odd = 2 * k - 1
        hit = npb == 2 * k
        if odd not in MOE_EXACT_ODD:
            hit = jnp.logical_or(hit, npb == odd)

        @pl.when(hit)
        def _():
            fn(0, k * MOE_RB)
    for q in MOE_EXACT_ODD:
        @pl.when(npb == q)
        def _():
            fn(0, q * MOE_PB)


def _moe_body(ie_ref, irow_ref, insub_ref, tail_ref, xs_ref, wg_ref, wu_ref, bup_ref,
              wd_ref, bd_ref, ys_ref, xstage_ref, xb_ref, act_ref, ybuf_ref, zbuf_ref, wgb_ref, wub_ref,
              wdb_ref, isem, osem):
    wi = pl.program_id(0)
    j = pl.program_id(1)
    n_w = pl.num_programs(0)
    to_sub = lambda n: lax.shift_right_logical(n + (MOE_RB // MOE_PB - 1), MOE_RB // MOE_PB - 1)
    npb = insub_ref[wi]
    nsub = to_sub(npb)
    row0 = irow_ref[wi]
    nt = MOE_NT
    n1 = MOE_NT1
    is_last = wi == n_w - 1
    nxt = jnp.minimum(wi + 1, n_w - 1)
    nsub_next = to_sub(jnp.where(is_last, 0, insub_ref[nxt]))
    row_next = irow_ref[nxt]
    prv = jnp.maximum(wi - 1, 0)
    npb_prev = jnp.where(wi == 0, 0, insub_ref[prv])
    row_prev = irow_ref[prv]

    class _XCopy:
        def __init__(self, row_base, s, slot):
            g0 = pl.multiple_of(row_base + s * MOE_RB, MOE_PB)
            self.copies = [
                pltpu.make_async_copy(xs_ref.at[pl.ds(g0, MOE_RB), c],
                                      xstage_ref.at[slot, :, pl.ds(c * LANES, LANES)], isem.at[slot])
                for c in range(ROW_SLAB)]

        def start(self):
            for cp in self.copies:
                cp.start()

        def wait(self):
            for cp in self.copies:
                cp.wait()

    x_copy = _XCopy

    def to_matmul_layout(s, slot):
        r0 = pl.multiple_of(s * MOE_RB, MOE_RB)
        xb_ref[pl.ds(r0, MOE_RB), :] = xstage_ref[slot].astype(BF16)

    class _YCopy:
        def __init__(self, row_base, s):
            r0 = pl.multiple_of(s * MOE_PB, MOE_PB)
            g0 = pl.multiple_of(row_base + r0, MOE_PB)
            cpt = MOE_TF // LANES
            self.copies = [
                pltpu.make_async_copy(
                    ybuf_ref.at[c // cpt, pl.ds(r0, MOE_PB), pl.ds((c % cpt) * LANES, LANES)],
                    ys_ref.at[pl.ds(g0, MOE_PB), c], osem)
                for c in range(ROW_SLAB)]

        def start(self):
            for cp in self.copies:
                cp.start()

        def wait(self):
            for cp in self.copies:
                cp.wait()

    y_copy = _YCopy

    @pl.when(jnp.logical_and(wi == 0, j == 0))
    def _():
        def load(s, _):
            cp = x_copy(row0, s, 0)
            cp.start()
            cp.wait()
            to_matmul_layout(s, 0)
            return 0

        lax.fori_loop(0, nsub, load, 0)

    @pl.when(jnp.logical_and(j == n1 - 1, nsub_next > 0))
    def _():
        x_copy(row_next, 0, 0).start()

    @pl.when(jnp.logical_and(j >= n1, j - n1 < nsub_next))
    def _():
        s = j - n1
        slot = lax.rem(s, 2)
        x_copy(row_next, s, slot).wait()

        @pl.when(s + 1 < nsub_next)
        def _():
            x_copy(row_next, s + 1, 1 - slot).start()

        to_matmul_layout(s, slot)

    @pl.when(jnp.logical_and(j == n1, npb_prev > 0))
    def _():
        def drain(s, _):
            y_copy(row_prev, s).wait()
            return 0

        lax.fori_loop(0, npb_prev, drain, 0)

    @pl.when(npb > 0)
    def _():
        @pl.when(j < n1)
        def _():
            wgb_ref[...] = wg_ref[...].astype(BF16)
            wub_ref[...] = wu_ref[...].astype(BF16)
            b_gate = bup_ref[ie_ref[wi], pl.ds(j, 1), :]
            b_up = bup_ref[ie_ref[wi], pl.ds(n1 + j, 1), :]
            tpw = MOE_TF1 // MOE_TF

            def up_rows(r0, nrows):
                x = xb_ref[pl.ds(r0, nrows), :]
                hg = jnp.dot(x, wgb_ref[...], preferred_element_type=F32) + b_gate
                hu = jnp.dot(x, wub_ref[...], preferred_element_type=F32) + b_up
                gate = jnp.minimum(hg, SWIGLU_LIMIT)
                up = jnp.clip(hu, -SWIGLU_LIMIT, SWIGLU_LIMIT)
                glu = gate * _sigmoid(SWIGLU_ALPHA * gate)
                a = (glu * (up + 1.0)).astype(BF16)
                for tt in range(tpw):
                    act_ref[j * tpw + tt, pl.ds(r0, nrows), :] = a[:, tt * MOE_TF:(tt + 1) * MOE_TF]

            _for_row_blocks(npb, up_rows)

        @pl.when(j >= n1)
        def _():
            wdb_ref[...] = wd_ref[...].astype(BF16)
            n = j - n1
            b_down = bd_ref[ie_ref[wi], pl.ds(n, 1), :]

            def down_rows(r0, nrows):
                acc = jnp.dot(act_ref[0, pl.ds(r0, nrows), :], wdb_ref[0:MOE_TF, :],
                              preferred_element_type=F32)
                for jj in range(1, nt):
                    acc = acc + jnp.dot(act_ref[jj, pl.ds(r0, nrows), :],
                                        wdb_ref[jj * MOE_TF:(jj + 1) * MOE_TF, :],
                                        preferred_element_type=F32)
                ybuf_ref[n, pl.ds(r0, nrows), :] = acc + b_down

            _for_row_blocks(npb, down_rows)

        @pl.when(j == n1 + nt - 1)
        def _():
            def store(s, _):
                y_copy(row0, s).start()
                return 0

            lax.fori_loop(0, npb, store, 0)

    @pl.when(jnp.logical_and(is_last, j == n1 + nt - 1))
    def _():
        def drain(s, _):
            y_copy(row0, s).wait()
            return 0

        lax.fori_loop(0, npb, drain, 0)
        zbuf_ref[...] = jnp.zeros(zbuf_ref.shape, F32)
        _zero_tail(zbuf_ref, ys_ref, tail_ref[0], tail_ref[1], osem)


def _moe(xs, n_items, item_e, item_row, item_nsub, tail, w_up, b_up3, w_down, b_down3):
    n_rows = xs.shape[0]
    nt = MOE_NT
    n1 = MOE_NT1
    assert MOE_SB <= nt and MOE_RB == 2 * MOE_PB
    up_off = D_FF // MOE_TF1
    p1 = lambda j, n, w: jnp.where(n[w] > 0, jnp.minimum(j, n1 - 1), n1 - 1)
    p2 = lambda j, n, w: jnp.where(n[w] > 0, jnp.maximum(j - n1, 0), nt - 1)
    return pl.pallas_call(
        _moe_body,
        grid_spec=pltpu.PrefetchScalarGridSpec(
            num_scalar_prefetch=4,
            grid=(n_items, n1 + nt),
            in_specs=[
                pl.BlockSpec(memory_space=pl.ANY),
                pl.BlockSpec((None, D_MODEL, MOE_TF1), lambda w, j, e, r, n, t: (e[w], 0, p1(j, n, w))),
                pl.BlockSpec((None, D_MODEL, MOE_TF1),
                             lambda w, j, e, r, n, t: (e[w], 0, up_off + p1(j, n, w))),
                pl.BlockSpec((N_EXPERTS, 2 * n1, MOE_TF1), lambda w, j, e, r, n, t: (0, 0, 0)),
                pl.BlockSpec((None, D_FF, MOE_TF), lambda w, j, e, r, n, t: (e[w], 0, p2(j, n, w))),
                pl.BlockSpec((N_EXPERTS, nt, MOE_TF), lambda w, j, e, r, n, t: (0, 0, 0)),
            ],
            out_specs=pl.BlockSpec(memory_space=pl.ANY),
            scratch_shapes=[
                pltpu.VMEM((2, MOE_RB, D_MODEL), F32),
                pltpu.VMEM((MOE_R, D_MODEL), BF16),
                pltpu.VMEM((nt, MOE_R, MOE_TF), BF16),
                pltpu.VMEM((nt, MOE_R, MOE_TF), F32),
                pltpu.VMEM((MOE_PB, ROW_SLAB, LANES), F32),
                pltpu.VMEM((D_MODEL, MOE_TF1), BF16),
                pltpu.VMEM((D_MODEL, MOE_TF1), BF16),
                pltpu.VMEM((D_FF, MOE_TF), BF16),
                pltpu.SemaphoreType.DMA((2,)), pltpu.SemaphoreType.DMA(()),
            ],
        ),
        out_shape=jax.ShapeDtypeStruct((n_rows, ROW_SLAB, LANES), F32),
        compiler_params=_cparams(("arbitrary", "arbitrary"), MOE_VMEM_LIMIT_BYTES),
        name="moe_experts",
    )(item_e, item_row, item_nsub, tail, xs, w_up, w_up, b_up3, w_down, b_down3)


def _combine_body(slot_ref, slotn_ref, ys_ref, x1_ref, gate_ref, g_ref, b_ref, yp_ref, ysm_ref, buf_ref,
                  v_ref, sem, *, tm, n_ptiles):
    i = pl.program_id(0)
    cur = lax.rem(i, 2)

    def gather(table_ref, bslot):
        def issue(r, _):
            for kk in range(TOP_K):
                dst0 = pl.multiple_of(r * ROW_SLAB, ROW_SLAB)
                pltpu.make_async_copy(ys_ref.at[table_ref[0, r * TOP_K + kk]],
                                      buf_ref.at[bslot, kk, pl.ds(dst0, ROW_SLAB)],
                                      sem.at[bslot]).start(priority=kk % 2)
            return 0

        lax.fori_loop(0, tm, issue, 0, unroll=8)

    @pl.when(i == 0)
    def _():
        gather(slot_ref, 0)

    @pl.when(i + 1 < pl.num_programs(0))
    def _():
        gather(slotn_ref, 1 - cur)

    for kk in range(TOP_K):
        pltpu.make_async_copy(buf_ref.at[cur, kk], buf_ref.at[cur, kk], sem.at[cur]).wait()
    gates = gate_ref[...]
    for c in range(ROW_SLAB):
        acc = DN_ALPHA * _load_slab_chunk(x1_ref, 0, tm, c)
        for kk in range(TOP_K):
            acc = acc + gates[:, kk:kk + 1] * _load_slab_chunk(buf_ref.at[cur, kk], 0, tm, c)
        v_ref[:, c * LANES:(c + 1) * LANES] = acc
    y = _layernorm_rows(v_ref[...], g_ref[...], b_ref[...])

    @pl.when(pl.program_id(0) < n_ptiles)
    def _():
        yp_ref[...] = y

    @pl.when(pl.program_id(0) >= n_ptiles)
    def _():
        ysm_ref[...] = y


def _combine(ys, slot_tiles, x1, gates, g_row, b_row, mp):
    m = x1.shape[0] // ROW_SLAB
    tm = TOK_TM
    assert mp % tm == 0 and (m - mp) % tm == 0
    n_pt = mp // tm
    wspec = lambda shape: pl.BlockSpec(shape, lambda i: (0, 0))
    n_t = m // tm
    return pl.pallas_call(
        functools.partial(_combine_body, tm=tm, n_ptiles=n_pt),
        grid=(n_t,),
        in_specs=[
            pl.BlockSpec((None, 1, tm * TOP_K), lambda i: (i, 0, 0), memory_space=pltpu.SMEM),
            pl.BlockSpec((None, 1, tm * TOP_K), lambda i: (jnp.minimum(i + 1, n_t - 1), 0, 0),
                         memory_space=pltpu.SMEM),
            pl.BlockSpec(memory_space=pl.ANY),
            pl.BlockSpec((tm * ROW_SLAB, LANES), lambda i: (i, 0)),
            pl.BlockSpec((tm, LANES), lambda i: (i, 0)),
            wspec((1, D_MODEL)), wspec((1, D_MODEL)),
        ],
        out_specs=[pl.BlockSpec((tm, D_MODEL), lambda i: (jnp.minimum(i, n_pt - 1), 0)),
                   pl.BlockSpec((tm, D_MODEL), lambda i: (jnp.maximum(i - n_pt, 0), 0))],
        out_shape=[jax.ShapeDtypeStruct((mp, D_MODEL), F32),
                   jax.ShapeDtypeStruct((m - mp, D_MODEL), F32)],
        scratch_shapes=[pltpu.VMEM((2, TOP_K, tm * ROW_SLAB, LANES), F32),
                        pltpu.VMEM((tm, D_MODEL), F32), pltpu.SemaphoreType.DMA((2,))],
        compiler_params=_cparams(("arbitrary",)),
        name="moe_combine_ln",
    )(slot_tiles, slot_tiles, ys, x1, gates, g_row, b_row)


def _routing_tables(top_i, m):
    e_ids = jnp.arange(N_EXPERTS, dtype=jnp.int32)
    onehot = (top_i[:, :, None] == e_ids[None, None, :]).astype(jnp.int32)
    mask = jnp.sum(onehot, axis=1)
    incl = jnp.cumsum(mask, axis=0)
    pos = incl - mask
    counts = incl[-1]
    npb = (counts + MOE_PB - 1) // MOE_PB
    padded = npb * MOE_PB
    gend = jnp.cumsum(padded)
    gstart = gend - padded
    slot = jnp.sum(onehot * (gstart[None, None, :] + pos[:, None, :]), axis=2)
    n_rows = (m * TOP_K // MOE_PB + N_EXPERTS) * MOE_PB + MOE_PB
    tail = jnp.stack([gend[-1], (n_rows - gend[-1]) // MOE_PB]).astype(jnp.int32)
    zrow = jnp.concatenate([jnp.where(counts > 0, gend - MOE_PB, -1).astype(jnp.int32), tail])
    n_items = N_EXPERTS + (m * TOP_K // MOE_PB + N_EXPERTS) // MOE_NPB
    ipe = (npb + MOE_NPB - 1) // MOE_NPB
    iend = jnp.cumsum(ipe)
    istart = iend - ipe
    total = iend[-1]
    wid = jnp.arange(n_items, dtype=jnp.int32)
    wclamp = jnp.minimum(wid, total - 1)
    ie = jnp.minimum(jnp.searchsorted(iend, wclamp, side="right"), N_EXPERTS - 1).astype(jnp.int32)
    jn = wclamp - istart[ie]
    irow = (gstart[ie] + jn * MOE_R).astype(jnp.int32)
    insub = jnp.where(wid < total, jnp.clip(npb[ie] - jn * MOE_NPB, 0, MOE_NPB), 0).astype(jnp.int32)
    return slot.astype(jnp.int32), zrow, tail, total.astype(jnp.int32), ie, irow, insub, n_rows


def kernel(x_prompt, x_sample, state_conv_a, state_delta, state_conv_b, state_lru, w_in, conv_a_w,
           a_log, dt_bias, norm_a_w, conv_b_w, conv_b_b, lru_wa, lru_ba, lru_wx, lru_bx, lru_lambda,
           norm_b_w, w_out, ln1_g, ln1_b, w_router, b_router, w_up, b_up, w_down, b_down, ln2_g, ln2_b):
    bp, tp, _ = x_prompt.shape
    bs, ts, _ = x_sample.shape
    mp, ms = bp * tp, bs * ts
    m = mp + ms
    l = 0
    xp2 = x_prompt.reshape(mp, D_MODEL)
    xs2 = x_sample.reshape(ms, D_MODEL)

    wp = _wprep(jnp.transpose(w_in[l]))
    wo = w_out[l].astype(BF16)
    pad_h = lambda v: jnp.zeros((1, LANES), F32).at[0, H_A:2 * H_A].set(v)
    alog_row = pad_h(a_log[l])
    dtb_row = pad_h(dt_bias[l])
    row = lambda v: v.reshape(1, -1)

    proj = _inproj(xp2, xs2, wp)

    qkv_act, xbc, bg = _convact_prompt(proj, bp, tp, conv_a_w[l], conv_b_w[l], row(conv_b_b[l]),
                                       alog_row, dtb_row)
    u, w, qe, kdt, aqk, el = _delta_prep(qkv_act, bg, bp, tp)
    oa_p, sd_p = _delta_seq(u, w, qe, kdt, aqk, el, proj, row(norm_a_w[l]), bp, tp)
    ob_p, h_p = _lru_prompt(xbc, proj, bp, tp, lru_wa[l], lru_wx[l], row(lru_ba[l]), row(lru_bx[l]),
                            row(lru_lambda[l]), row(norm_b_w[l]))
    oa_s, sd_s = _sample_a(proj, mp, bs, ts, state_conv_a[l], state_delta[l], conv_a_w[l],
                           alog_row, dtb_row, row(norm_a_w[l]))
    ob_s, h_s = _lru_sample(proj, mp, bs, ts, state_conv_b[l], state_lru[l], conv_b_w[l],
                            row(conv_b_b[l]), lru_wa[l], lru_wx[l], row(lru_ba[l]), row(lru_bx[l]),
                            row(lru_lambda[l]), row(norm_b_w[l]))

    x1, ti, tg = _outproj(oa_p, oa_s, ob_p, ob_s, xp2, xs2, wo, row(ln1_g[l]), row(ln1_b[l]),
                          w_router[l], row(b_router[l]))

    slot, zrow, tail, n_used, ie, irow, insub, n_rows = _routing_tables(ti[:, :TOP_K], m)
    slot_tiles = slot.reshape(m // TOK_TM, 1, TOK_TM * TOP_K)
    xs_sorted = _dispatch(x1, slot_tiles, zrow, n_rows)
    ys = _moe(xs_sorted, n_used, ie, irow, insub, tail, w_up[l],
              b_up[l].reshape(N_EXPERTS, 2 * MOE_NT1, MOE_TF1), w_down[l],
              b_down[l].reshape(N_EXPERTS, MOE_NT, MOE_TF))
    y_p, y_s = _combine(ys, slot_tiles, x1, tg, row(ln2_g[l]), row(ln2_b[l]), mp)

    y_prompt = y_p.reshape(bp, tp, D_MODEL)
    y_sample = y_s.reshape(bs, ts, D_MODEL)
    nh = CONV_W - 1
    assert tp % SUBLANES == 0 and ts == SUBLANES and nh <= SUBLANES
    pg = proj.reshape(m // SUBLANES, SUBLANES, N_PROJ)
    gp = tp // SUBLANES

    def last_rows(g0, g1, gstep, c0, width):
        return lax.slice(pg, (g0, SUBLANES - nh, c0), (g1, SUBLANES, c0 + width), (gstep, 1, 1))

    ca_p = last_rows(gp - 1, mp // SUBLANES, gp, COL_QKV, 3 * D_A)
    cb_p = last_rows(gp - 1, mp // SUBLANES, gp, COL_XB, D_B)
    ca_s = last_rows(mp // SUBLANES, m // SUBLANES, 1, COL_QKV, 3 * D_A)
    cb_s = last_rows(mp // SUBLANES, m // SUBLANES, 1, COL_XB, D_B)
    return (y_prompt, y_sample,
            ca_p[None], sd_p[None], cb_p[None], h_p.reshape(1, bp, D_B),
            ca_s[None], sd_s[None], cb_s[None], h_s[None])
```

```python
import functools
import math

import jax
import jax.numpy as jnp
from jax import lax
from jax.experimental import pallas as pl
from jax.experimental.pallas import tpu as pltpu

F32 = jnp.float32
BF16 = jnp.bfloat16
HIGHEST = lax.Precision.HIGHEST

D_MODEL = 2048
D_A = 1024
H_A = 8
DK = 128
DV = 128
CONV_W = 4
D_B = 1024
NB = 8
BW = 128
LRU_C = 8.0
N_EXPERTS = 32
TOP_K = 4
D_FF = 2048
SWIGLU_LIMIT = 7.0
SWIGLU_ALPHA = 1.702
DEPTH = 1
DN_ALPHA = (2.0 * DEPTH) ** 0.25
LN_EPS = 1e-5
RMS_EPS = 1e-6
L2_EPS = 1e-6

LANES = 128
SUBLANES = 8
VMEM_LIMIT_BYTES = 56 * 1024 * 1024

COL_QKV = 0
COL_Z = 3 * D_A
COL_XB = 4 * D_A
COL_YB = 4 * D_A + D_B
COL_BA = 4 * D_A + 2 * D_B
N_PROJ = COL_BA + 2 * LANES

INPROJ_TM = 1024
INPROJ_TN = 1280
CONV_L = 256
DELTA_C = 128
DELTA_CPS = 4
OUT_TM = 512
MOE_PB = 128
MOE_RB = 256
MOE_SB = 6
MOE_R = MOE_RB * MOE_SB
MOE_NPB = MOE_R // MOE_PB
MOE_EXACT_ODD = (7, 9, 11)
MOE_TF = 256
MOE_NT = D_FF // MOE_TF
MOE_TF1 = 256
MOE_NT1 = D_FF // MOE_TF1
SAMPLE_NSEQ = 8
TOK_TM = 256
ROW_SLAB = D_MODEL // LANES


def _cparams(sem, vmem=VMEM_LIMIT_BYTES):
    return pltpu.CompilerParams(dimension_semantics=sem, vmem_limit_bytes=vmem)


def _sigmoid(x):
    return 1.0 / (1.0 + jnp.exp(-x))


def _softplus(x):
    return jnp.maximum(x, 0.0) + jnp.log(1.0 + jnp.exp(-jnp.abs(x)))


def _wprep_body(w_ref, o_ref):
    c = 4 * D_A
    nba = 2 * H_A
    o_ref[0:c, :] = w_ref[0:c, :].astype(BF16)
    o_ref[c:c + 2 * D_B, :] = w_ref[c + nba:c + nba + 2 * D_B, :].astype(BF16)
    o_ref[COL_BA:COL_BA + nba, :] = w_ref[c:c + nba, :].astype(BF16)
    o_ref[COL_BA + nba:N_PROJ, :] = jnp.zeros((N_PROJ - COL_BA - nba, o_ref.shape[1]), BF16)


def _wprep(w_t):
    cols = 256
    d_in = w_t.shape[0]
    assert d_in == 4 * D_A + 2 * H_A + 2 * D_B and D_MODEL % cols == 0
    return pl.pallas_call(
        _wprep_body,
        grid=(D_MODEL // cols,),
        in_specs=[pl.BlockSpec((d_in, cols), lambda i: (0, i))],
        out_specs=pl.BlockSpec((N_PROJ, cols), lambda i: (0, i)),
        out_shape=jax.ShapeDtypeStruct((N_PROJ, D_MODEL), BF16),
        compiler_params=_cparams(("arbitrary",)),
        name="wprep",
    )(w_t)


def _inproj_body(xp_ref, xs_ref, w_ref, o_ref, xb_ref, *, n_ptiles):
    i = pl.program_id(0)

    @pl.when(pl.program_id(1) == 0)
    def _():
        @pl.when(i < n_ptiles)
        def _():
            xb_ref[...] = xp_ref[...].astype(BF16)

        @pl.when(i >= n_ptiles)
        def _():
            xb_ref[...] = xs_ref[...].astype(BF16)

    o_ref[...] = _dot_nt(xb_ref[...], w_ref[...])


def _inproj(xp2, xs2, wp):
    mp, ms = xp2.shape[0], xs2.shape[0]
    tm = min(INPROJ_TM, ms)
    assert mp % tm == 0 and ms % tm == 0
    n_pt, n_st = mp // tm, ms // tm
    n_nt = N_PROJ // INPROJ_TN
    return pl.pallas_call(
        functools.partial(_inproj_body, n_ptiles=n_pt),
        grid=(n_pt + n_st, n_nt),
        in_specs=[
            pl.BlockSpec((tm, D_MODEL), lambda i, j: (jnp.minimum(i, n_pt - 1), 0)),
            pl.BlockSpec((tm, D_MODEL), lambda i, j: (jnp.maximum(i - n_pt, 0), 0),
                         pipeline_mode=pl.Buffered(1)),
            pl.BlockSpec((INPROJ_TN, D_MODEL), lambda i, j: (j, 0)),
        ],
        out_specs=pl.BlockSpec((tm, INPROJ_TN), lambda i, j: (i, j)),
        out_shape=jax.ShapeDtypeStruct((mp + ms, N_PROJ), F32),
        scratch_shapes=[pltpu.VMEM((tm, D_MODEL), BF16)],
        compiler_params=_cparams(("arbitrary", "arbitrary")),
        name="inproj",
    )(xp2, xs2, wp)


def _conv_taps(h_ref, w_ref, c0, width, rows, base):
    acc = h_ref[base:base + rows, c0:c0 + width] * w_ref[CONV_W - 1:CONV_W, c0:c0 + width]
    for j in range(1, CONV_W):
        acc = acc + (h_ref[base - j:base - j + rows, c0:c0 + width]
                     * w_ref[CONV_W - 1 - j:CONV_W - j, c0:c0 + width])
    return acc


def _qkv_activation(acc, blk):
    a = acc * _sigmoid(acc)
    if blk < 2 * H_A:
        a = a * lax.rsqrt(jnp.sum(a * a, axis=-1, keepdims=True) + L2_EPS)
        if blk < H_A:
            a = a * (DK ** -0.5)
    return a


def _beta_g(ba, alog_ref, dtb_ref):
    lane = lax.broadcasted_iota(jnp.int32, ba.shape, 1)
    beta = _sigmoid(ba)
    g = -jnp.exp(alog_ref[...]) * _softplus(ba + dtb_ref[...])
    return jnp.where(lane < H_A, beta, jnp.where(lane < 2 * H_A, g, 0.0))


def _convact_prompt_body(qkv_ref, xb_ref, ba_ref, caw_ref, cbw_ref, cbb_ref, alog_ref, dtb_ref,
                         qkvo_ref, xbo_ref, bgo_ref, hq_ref, hx_ref, *, rows):
    t = pl.program_id(1)
    hdr = SUBLANES

    @pl.when(t == 0)
    def _():
        hq_ref[0:hdr, :] = jnp.zeros((hdr, 3 * D_A), F32)
        hx_ref[0:hdr, :] = jnp.zeros((hdr, D_B), F32)

    @pl.when(t > 0)
    def _():
        hq_ref[0:hdr, :] = hq_ref[rows:rows + hdr, :]
        hx_ref[0:hdr, :] = hx_ref[rows:rows + hdr, :]

    hq_ref[hdr:hdr + rows, :] = qkv_ref[...]
    hx_ref[hdr:hdr + rows, :] = xb_ref[...]
    for blk in range(3 * H_A):
        c0 = blk * DK
        acc = _conv_taps(hq_ref, caw_ref, c0, DK, rows, hdr)
        qkvo_ref[:, c0:c0 + DK] = _qkv_activation(acc, blk)
    for blk in range(NB):
        c0 = blk * BW
        xbo_ref[:, c0:c0 + BW] = _conv_taps(hx_ref, cbw_ref, c0, BW, rows, hdr) + cbb_ref[:, c0:c0 + BW]
    bgo_ref[...] = _beta_g(ba_ref[...], alog_ref, dtb_ref)


def _convact_prompt(proj, bsz, t_len, caw, cbw, cbb, alog_row, dtb_row):
    rows = min(CONV_L, t_len)
    assert t_len % rows == 0
    nt = t_len // rows
    mp = bsz * t_len
    wspec = lambda shape: pl.BlockSpec(shape, lambda b, t: (0, 0))
    return pl.pallas_call(
        functools.partial(_convact_prompt_body, rows=rows),
        grid=(bsz, nt),
        in_specs=[
            pl.BlockSpec((rows, 3 * D_A), lambda b, t: (b * nt + t, COL_QKV // (3 * D_A))),
            pl.BlockSpec((rows, D_B), lambda b, t: (b * nt + t, COL_XB // D_B)),
            pl.BlockSpec((rows, LANES), lambda b, t: (b * nt + t, COL_BA // LANES)),
            wspec((CONV_W, 3 * D_A)), wspec((CONV_W, D_B)), wspec((1, D_B)),
            wspec((1, LANES)), wspec((1, LANES)),
        ],
        out_specs=[
            pl.BlockSpec((rows, 3 * D_A), lambda b, t: (b * nt + t, 0)),
            pl.BlockSpec((rows, D_B), lambda b, t: (b * nt + t, 0)),
            pl.BlockSpec((rows, LANES), lambda b, t: (b * nt + t, 0)),
        ],
        out_shape=[
            jax.ShapeDtypeStruct((mp, 3 * D_A), F32),
            jax.ShapeDtypeStruct((mp, D_B), F32),
            jax.ShapeDtypeStruct((mp, LANES), F32),
        ],
        scratch_shapes=[pltpu.VMEM((SUBLANES + rows + SUBLANES, 3 * D_A), F32),
                        pltpu.VMEM((SUBLANES + rows + SUBLANES, D_B), F32)],
        compiler_params=_cparams(("arbitrary", "arbitrary")),
        name="convact_prompt",
    )(proj, proj, proj, caw, cbw, cbb, alog_row, dtb_row)


def _dot_nt(a, b, precision=None):
    return lax.dot_general(a, b, (((1,), (1,)), ((), ())), precision=precision,
                           preferred_element_type=F32)


def _split_bf16(x):
    hi = x.astype(BF16)
    return hi, (x - hi.astype(F32)).astype(BF16)


def _dot3(a, b):
    ah, al = _split_bf16(a)
    bh, bl = _split_bf16(b)
    return (jnp.dot(ah, bh, preferred_element_type=F32)
            + (jnp.dot(al, bh, preferred_element_type=F32) + jnp.dot(ah, bl, preferred_element_type=F32)))


def _delta_prep_body(qkv_ref, bg_ref, u_ref, w_ref, qe_ref, kdt_ref, aqk_ref, el_ref, *, c, ncc):
    row = lax.broadcasted_iota(jnp.int32, (c, c), 0)
    col = lax.broadcasted_iota(jnp.int32, (c, c), 1)
    incl = row >= col
    strict = row > col
    eye = (row == col).astype(F32)
    tril = incl.astype(F32)
    eye_l = (lax.broadcasted_iota(jnp.int32, (LANES, LANES), 0)
             == lax.broadcasted_iota(jnp.int32, (LANES, LANES), 1)).astype(F32)

    pairs = [(cc, h) for cc in range(ncc) for h in range(H_A)]
    rows = [slice(cc * c, (cc + 1) * c) for cc in range(ncc)]
    bgs = [bg_ref[rs, :] for rs in rows]
    gc_alls = [jnp.dot(tril, bg, precision=HIGHEST, preferred_element_type=F32) for bg in bgs]
    gc_ts = [_dot_nt(eye_l, g, precision=HIGHEST) for g in gc_alls]
    lmats, rhss = [], []
    for cc, h in pairs:
        rs, bg, gc_all, gc_t = rows[cc], bgs[cc], gc_alls[cc], gc_ts[cc]
        q = qkv_ref[rs, h * DK:(h + 1) * DK]
        k = qkv_ref[rs, (H_A + h) * DK:(H_A + h + 1) * DK]
        v = qkv_ref[rs, (2 * H_A + h) * DK:(2 * H_A + h + 1) * DK]
        beta = bg[:, h:h + 1]
        gc = gc_all[:, H_A + h:H_A + h + 1]
        g_last = gc_all[c - 1:c, H_A + h:H_A + h + 1]
        diff = gc - gc_t[H_A + h:H_A + h + 1, :]
        decay = jnp.where(incl, jnp.exp(jnp.where(incl, diff, 0.0)), 0.0)
        kb = k * beta
        egc = jnp.exp(gc)
        kbf = k.astype(BF16)
        kk = _dot_nt(kbf, kbf)
        lmats.append(jnp.where(strict, kk * beta * decay, 0.0))
        rhss.append(jnp.concatenate([v * beta, kb * egc], axis=1))
        qe_ref[rs, h * DK:(h + 1) * DK] = (q * egc).astype(BF16)
        aqk_ref[cc, h] = (_dot_nt(q.astype(BF16), kbf) * decay).astype(BF16)
        kdec = k * jnp.exp(g_last - gc)
        kdt_ref[cc, h] = kdec.astype(BF16)
        el_ref[cc, h] = jnp.broadcast_to(jnp.exp(g_last), (SUBLANES, LANES))
    xbs = [(-lm).astype(BF16) for lm in lmats]
    t0s = [eye - lm for lm in lmats]
    p = 2
    while p < c:
        xbs = [jnp.dot(xb, xb, preferred_element_type=F32).astype(BF16) for xb in xbs]
        t0s = [t0 + jnp.dot(t0.astype(BF16), xb, preferred_element_type=F32) for t0, xb in zip(t0s, xbs)]
        p *= 2
    t0bs = [t0.astype(BF16) for t0 in t0s]
    sol0s = [jnp.dot(t0b, rhs.astype(BF16), preferred_element_type=F32) for t0b, rhs in zip(t0bs, rhss)]
    resids = [rhs - sol0 - _dot3(lm, sol0) for rhs, sol0, lm in zip(rhss, sol0s, lmats)]
    for p_, (cc, h) in enumerate(pairs):
        sol = sol0s[p_] + jnp.dot(t0bs[p_], resids[p_].astype(BF16), preferred_element_type=F32)
        u_ref[rows[cc], h * DV:(h + 1) * DV] = sol[:, :DV]
        w_ref[rows[cc], h * DK:(h + 1) * DK] = sol[:, DV:].astype(BF16)


def _delta_prep(qkv_act, bg, bsz, t_len):
    c = DELTA_C
    assert t_len % c == 0
    nc = t_len // c
    mp = bsz * t_len
    n_chunks = bsz * nc
    ncc = DELTA_CPS
    assert n_chunks % ncc == 0
    rowspec = lambda w: pl.BlockSpec((ncc * c, w), lambda i: (i, 0))
    return pl.pallas_call(
        functools.partial(_delta_prep_body, c=c, ncc=ncc),
        grid=(n_chunks // ncc,),
        in_specs=[rowspec(3 * D_A), rowspec(LANES)],
        out_specs=[
            rowspec(D_A), rowspec(D_A), rowspec(D_A),
            pl.BlockSpec((ncc, H_A, c, DK), lambda i: (i, 0, 0, 0)),
            pl.BlockSpec((ncc, H_A, c, c), lambda i: (i, 0, 0, 0)),
            pl.BlockSpec((ncc, H_A, SUBLANES, LANES), lambda i: (i, 0, 0, 0)),
        ],
        out_shape=[
            jax.ShapeDtypeStruct((mp, D_A), F32),
            jax.ShapeDtypeStruct((mp, D_A), BF16),
            jax.ShapeDtypeStruct((mp, D_A), BF16),
            jax.ShapeDtypeStruct((n_chunks, H_A, c, DK), BF16),
            jax.ShapeDtypeStruct((n_chunks, H_A, c, c), BF16),
            jax.ShapeDtypeStruct((n_chunks, H_A, SUBLANES, LANES), F32),
        ],
        compiler_params=_cparams(("arbitrary",)),
        name="delta_prep",
    )(qkv_act, bg)


def _gated_rmsnorm(o, z, nw):
    on = o * lax.rsqrt(jnp.mean(o * o, axis=-1, keepdims=True) + RMS_EPS) * nw
    return on * (z * _sigmoid(z))


def _delta_seq_body(*refs, bsz):
    u_ref, w_ref, qe_ref, kdt_ref, aqk_ref, el_ref = refs[:6]
    z_refs = refs[6:6 + bsz]
    nw_ref, oa_ref, sfin_ref, s_ref = refs[6 + bsz:]
    ci = pl.program_id(0)

    @pl.when(ci == 0)
    def _():
        s_ref[...] = jnp.zeros(s_ref.shape, F32)

    nw = nw_ref[...]
    pairs = [(b, h) for b in range(bsz) for h in range(H_A)]
    sl = lambda h: slice(h * DK, (h + 1) * DK)
    ss = [s_ref[b, h] for b, h in pairs]
    sbs = [s.astype(BF16) for s in ss]
    vbs = [(u_ref[b, :, sl(h)] - jnp.dot(w_ref[b, :, sl(h)], sb, preferred_element_type=F32)).astype(BF16)
           for (b, h), sb in zip(pairs, sbs)]
    for p, (b, h) in enumerate(pairs):
        s_ref[b, h] = ss[p] * el_ref[b, h][0:1, 0:1] + lax.dot_general(
            kdt_ref[b, h], vbs[p], (((0,), (0,)), ((), ())), preferred_element_type=F32)
    for p, (b, h) in enumerate(pairs):
        o = (jnp.dot(qe_ref[b, :, sl(h)], sbs[p], preferred_element_type=F32)
             + jnp.dot(aqk_ref[b, h], vbs[p], preferred_element_type=F32))
        oa_ref[b, :, sl(h)] = _gated_rmsnorm(o, z_refs[b][:, sl(h)], nw).astype(BF16)

    @pl.when(ci == pl.num_programs(0) - 1)
    def _():
        sfin_ref[...] = s_ref[...]


def _delta_seq(u, w, qe, kdt, aqk, el, proj, nw_row, bsz, t_len):
    c = DELTA_C
    nc = t_len // c
    mp = bsz * t_len
    rows4 = lambda a: a.reshape(bsz, nc, c, a.shape[-1])
    chunk5 = lambda a: a.reshape((bsz, nc) + a.shape[1:])
    rowspec = pl.BlockSpec((bsz, None, c, D_A), lambda i: (0, i, 0, 0))
    chunkspec = lambda a, bb: pl.BlockSpec((bsz, None, H_A, a, bb), lambda i: (0, i, 0, 0, 0))
    zspecs = [pl.BlockSpec((c, D_A), functools.partial(lambda i, b: (b * nc + i, COL_Z // D_A), b=b))
              for b in range(bsz)]
    oa, sfin = pl.pallas_call(
        functools.partial(_delta_seq_body, bsz=bsz),
        grid=(nc,),
        in_specs=[rowspec, rowspec, rowspec,
                  chunkspec(c, DK), chunkspec(c, c), chunkspec(SUBLANES, LANES)]
                 + zspecs + [pl.BlockSpec((1, DV), lambda i: (0, 0))],
        out_specs=[
            rowspec,
            pl.BlockSpec((bsz, H_A, DK, DV), lambda i: (0, 0, 0, 0)),
        ],
        out_shape=[
            jax.ShapeDtypeStruct((bsz, nc, c, D_A), BF16),
            jax.ShapeDtypeStruct((bsz, H_A, DK, DV), F32),
        ],
        scratch_shapes=[pltpu.VMEM((bsz, H_A, DK, DV), F32)],
        compiler_params=_cparams(("arbitrary",)),
        name="delta_seq",
    )(rows4(u), rows4(w), rows4(qe), chunk5(kdt), chunk5(aqk), chunk5(el), *([proj] * bsz), nw_row)
    return oa.reshape(mp, D_A), sfin


def _sample_a_body(qkv_ref, z_ref, ba_ref, hist_ref, s0_ref, caw_ref, alog_ref, dtb_ref, nw_ref,
                   oa_ref, s1_ref, hq_ref, *, t_len, nseq):
    hdr = SUBLANES
    nh = CONV_W - 1
    rows = [slice(si * t_len, (si + 1) * t_len) for si in range(nseq)]
    for si in range(nseq):
        hq_ref[si, 0:hdr, :] = jnp.zeros((hdr, 3 * D_A), F32)
        hq_ref[si, hdr - nh:hdr, :] = hist_ref[si]
        hq_ref[si, hdr:hdr + t_len, :] = qkv_ref[rows[si], :]
    bg = _beta_g(ba_ref[...], alog_ref, dtb_ref)
    rloc = lax.rem(lax.broadcasted_iota(jnp.int32, bg.shape, 0), t_len)
    gc_all = bg
    sft = 1
    while sft < t_len:
        gc_all = gc_all + jnp.where(rloc >= sft, pltpu.roll(gc_all, sft, axis=0), 0.0)
        sft *= 2
    nw = nw_ref[...]
    rowc = lax.broadcasted_iota(jnp.int32, (t_len, 1), 0)
    pairs = [(si, h) for si in range(nseq) for h in range(H_A)]
    npair = range(len(pairs))
    act = lambda si, blk: _qkv_activation(_conv_taps(hq_ref.at[si], caw_ref, blk * DK, DK, t_len, hdr), blk)
    qs = [act(si, h) for si, h in pairs]
    ks = [act(si, H_A + h) for si, h in pairs]
    vs = [act(si, 2 * H_A + h) for si, h in pairs]
    betas = [bg[rows[si], h:h + 1] for si, h in pairs]
    gcs = [gc_all[rows[si], H_A + h:H_A + h + 1] for si, h in pairs]
    glasts = [gc[t_len - 1:t_len, :] for gc in gcs]
    kbs = [k * b for k, b in zip(ks, betas)]
    egcs = [jnp.exp(gc) for gc in gcs]
    sols = [jnp.concatenate([v * b, kb * e], axis=1) for v, b, kb, e in zip(vs, betas, kbs, egcs)]
    acols = [[] for _ in npair]
    lcols = [[] for _ in npair]
    for j in range(t_len):
        for p in npair:
            kj = ks[p][j:j + 1, :]
            dcol = jnp.exp(jnp.where(rowc >= j, gcs[p] - gcs[p][j:j + 1, :], 0.0))
            acols[p].append(jnp.where(rowc >= j, jnp.sum(qs[p] * kj, axis=-1, keepdims=True) * dcol, 0.0))
            lcols[p].append(jnp.where(rowc > j, jnp.sum(kbs[p] * kj, axis=-1, keepdims=True) * dcol, 0.0))
    for j in range(t_len - 1):
        sols = [sol - lcols[p][j] * sol[j:j + 1, :] for p, sol in enumerate(sols)]
    ss = [s0_ref[si, h] for si, h in pairs]
    sbs = [s.astype(BF16) for s in ss]
    v_news = [sol[:, :DV] - jnp.dot(sol[:, DV:].astype(BF16), sb, preferred_element_type=F32)
              for sol, sb in zip(sols, sbs)]
    os_ = [jnp.dot((q * e).astype(BF16), sb, preferred_element_type=F32) for q, e, sb in zip(qs, egcs, sbs)]
    for j in range(t_len):
        os_ = [o + acols[p][j] * v_news[p][j:j + 1, :] for p, o in enumerate(os_)]
    upds = [lax.dot_general((k * jnp.exp(gl - gc)).astype(BF16), vn.astype(BF16), (((0,), (0,)), ((), ())),
                            preferred_element_type=F32)
            for k, gl, gc, vn in zip(ks, glasts, gcs, v_news)]
    for p, (si, h) in enumerate(pairs):
        s1_ref[si, h] = ss[p] * jnp.exp(glasts[p]) + upds[p]
        sl = slice(h * DV, (h + 1) * DV)
        oa_ref[rows[si], sl] = _gated_rmsnorm(os_[p], z_ref[rows[si], sl], nw).astype(BF16)


def _sample_a(proj, row0, bsz, t_len, hist, s0, caw, alog_row, dtb_row, nw_row):
    nseq = SAMPLE_NSEQ
    tr = nseq * t_len
    assert t_len == SUBLANES and row0 % tr == 0 and bsz % nseq == 0
    r0 = row0 // tr
    wspec = lambda shape: pl.BlockSpec(shape, lambda b: (0,) * len(shape))
    return pl.pallas_call(
        functools.partial(_sample_a_body, t_len=t_len, nseq=nseq),
        grid=(bsz // nseq,),
        in_specs=[
            pl.BlockSpec((tr, 3 * D_A), lambda b: (r0 + b, COL_QKV // (3 * D_A))),
            pl.BlockSpec((tr, D_A), lambda b: (r0 + b, COL_Z // D_A)),
            pl.BlockSpec((tr, LANES), lambda b: (r0 + b, COL_BA // LANES)),
            pl.BlockSpec((nseq, CONV_W - 1, 3 * D_A), lambda b: (b, 0, 0)),
            pl.BlockSpec((nseq, H_A, DK, DV), lambda b: (b, 0, 0, 0)),
            wspec((CONV_W, 3 * D_A)), wspec((1, LANES)), wspec((1, LANES)), wspec((1, DV)),
        ],
        out_specs=[
            pl.BlockSpec((tr, D_A), lambda b: (b, 0)),
            pl.BlockSpec((nseq, H_A, DK, DV), lambda b: (b, 0, 0, 0)),
        ],
        out_shape=[
            jax.ShapeDtypeStruct((bsz * t_len, D_A), BF16),
            jax.ShapeDtypeStruct((bsz, H_A, DK, DV), F32),
        ],
        scratch_shapes=[pltpu.VMEM((nseq, 2 * SUBLANES, 3 * D_A), F32)],
        compiler_params=_cparams(("arbitrary",)),
        name="sample_a",
    )(proj, proj, proj, hist, s0, caw, alog_row, dtb_row, nw_row)


def _lru_coeffs(xc, wa_ref, wx_ref, lba_ref, lbx_ref, lam_ref):
    parts = []
    for n in range(NB):
        sl = slice(n * BW, (n + 1) * BW)
        xn = xc[:, sl]
        xnb = xn.astype(BF16)
        gr = _sigmoid(jnp.dot(xnb, wa_ref[n].astype(BF16), preferred_element_type=F32) + lba_ref[:, sl])
        gi = _sigmoid(jnp.dot(xnb, wx_ref[n].astype(BF16), preferred_element_type=F32) + lbx_ref[:, sl])
        log_a = -LRU_C * gr * _softplus(-lam_ref[:, sl])
        a = jnp.exp(log_a)
        parts.append((a, jnp.sqrt(1.0 - a * a), gi * xn))
    return parts


def _group_scan(a, b):
    rowi = lax.broadcasted_iota(jnp.int32, a.shape, 0)
    sft = 1
    while sft < SUBLANES:
        keep = rowi >= sft
        b = b + a * jnp.where(keep, pltpu.roll(b, sft, axis=0), 0.0)
        a = a * jnp.where(keep, pltpu.roll(a, sft, axis=0), 1.0)
        sft *= 2
    return a, b


def _gelu_tanh(x):
    return 0.5 * x * (1.0 + jnp.tanh(math.sqrt(2.0 / math.pi) * (x + 0.044715 * x * x * x)))


def _lru_finish(h, yb, nbw):
    hg = h * _gelu_tanh(yb)
    return (hg * lax.rsqrt(jnp.mean(hg * hg, axis=-1, keepdims=True) + RMS_EPS) * nbw).astype(BF16)


def _lru_prompt_body(xc_ref, yb_ref, wa_ref, wx_ref, lba_ref, lbx_ref, lam_ref, nbw_ref,
                     ob_ref, hfin_ref, a_ref, b_ref, h_ref, carry_ref, *, rows):
    t = pl.program_id(1)
    parts = _lru_coeffs(xc_ref[...], wa_ref, wx_ref, lba_ref, lbx_ref, lam_ref)
    rowi = lax.broadcasted_iota(jnp.int32, (rows, BW), 0)
    first = jnp.logical_and(rowi == 0, t == 0)
    for n in range(NB):
        sl = slice(n * BW, (n + 1) * BW)
        a_n, mult_n, gix_n = parts[n]
        a_ref[:, sl] = jnp.where(first, 0.0, a_n)
        b_ref[:, sl] = jnp.where(first, 1.0, mult_n) * gix_n

    @pl.when(t == 0)
    def _():
        carry_ref[...] = jnp.zeros(carry_ref.shape, F32)

    def group(gidx, carry):
        r0 = pl.multiple_of(gidx * SUBLANES, SUBLANES)
        ag, bg = _group_scan(a_ref[pl.ds(r0, SUBLANES), :], b_ref[pl.ds(r0, SUBLANES), :])
        hg = ag * carry + bg
        h_ref[pl.ds(r0, SUBLANES), :] = hg
        return jnp.broadcast_to(hg[SUBLANES - 1:SUBLANES, :], hg.shape)

    carry = lax.fori_loop(0, rows // SUBLANES, group, carry_ref[...])
    carry_ref[...] = carry
    ob_ref[...] = _lru_finish(h_ref[...], yb_ref[...], nbw_ref[...])

    @pl.when(t == pl.num_programs(1) - 1)
    def _():
        hfin_ref[...] = carry[0:1, :]


def _lru_prompt(xbc, proj, bsz, t_len, wa, wx, lba, lbx, lam, nbw):
    rows = min(CONV_L, t_len)
    nt = t_len // rows
    mp = bsz * t_len
    wspec = lambda shape: pl.BlockSpec(shape, lambda b, t: (0,) * len(shape))
    return pl.pallas_call(
        functools.partial(_lru_prompt_body, rows=rows),
        grid=(bsz, nt),
        in_specs=[
            pl.BlockSpec((rows, D_B), lambda b, t: (b * nt + t, 0)),
            pl.BlockSpec((rows, D_B), lambda b, t: (b * nt + t, COL_YB // D_B)),
            wspec((NB, BW, BW)), wspec((NB, BW, BW)),
            wspec((1, D_B)), wspec((1, D_B)), wspec((1, D_B)), wspec((1, D_B)),
        ],
        out_specs=[
            pl.BlockSpec((rows, D_B), lambda b, t: (b * nt + t, 0)),
            pl.BlockSpec((None, 1, D_B), lambda b, t: (b, 0, 0)),
        ],
        out_shape=[
            jax.ShapeDtypeStruct((mp, D_B), BF16),
            jax.ShapeDtypeStruct((bsz, 1, D_B), F32),
        ],
        scratch_shapes=[pltpu.VMEM((rows, D_B), F32), pltpu.VMEM((rows, D_B), F32),
                        pltpu.VMEM((rows, D_B), F32), pltpu.VMEM((SUBLANES, D_B), F32)],
        compiler_params=_cparams(("arbitrary", "arbitrary")),
        name="lru_prompt",
    )(xbc, proj, wa, wx, lba, lbx, lam, nbw)


def _convlru_prompt_body(qkv_ref, xb_ref, ba_ref, yb_ref, caw_ref, cbw_ref, cbb_ref, alog_ref, dtb_ref,
                         wa_ref, wx_ref, lba_ref, lbx_ref, lam_ref, nbw_ref,
                         qkvo_ref, bgo_ref, ob_ref, hfin_ref,
                         hq_ref, hx_ref, xc_ref, a_ref, b_ref, h_ref, carry_ref, *, rows):
    _convact_prompt_body(qkv_ref, xb_ref, ba_ref, caw_ref, cbw_ref, cbb_ref, alog_ref, dtb_ref,
                         qkvo_ref, xc_ref, bgo_ref, hq_ref, hx_ref, rows=rows)
    _lru_prompt_body(xc_ref, yb_ref, wa_ref, wx_ref, lba_ref, lbx_ref, lam_ref, nbw_ref,
                     ob_ref, hfin_ref, a_ref, b_ref, h_ref, carry_ref, rows=rows)


def _convlru_prompt(proj, bsz, t_len, caw, cbw, cbb, alog_row, dtb_row, wa, wx, lba, lbx, lam, nbw):
    rows = min(CONV_L, t_len)
    assert t_len % rows == 0
    nt = t_len // rows
    mp = bsz * t_len
    wspec = lambda shape: pl.BlockSpec(shape, lambda b, t: (0,) * len(shape))
    tile = lambda width, col: pl.BlockSpec((rows, width), lambda b, t: (b * nt + t, col))
    return pl.pallas_call(
        functools.partial(_convlru_prompt_body, rows=rows),
        grid=(bsz, nt),
        in_specs=[
            tile(3 * D_A, COL_QKV // (3 * D_A)), tile(D_B, COL_XB // D_B), tile(LANES, COL_BA // LANES),
            tile(D_B, COL_YB // D_B),
            wspec((CONV_W, 3 * D_A)), wspec((CONV_W, D_B)), wspec((1, D_B)),
            wspec((1, LANES)), wspec((1, LANES)),
            wspec((NB, BW, BW)), wspec((NB, BW, BW)),
            wspec((1, D_B)), wspec((1, D_B)), wspec((1, D_B)), wspec((1, D_B)),
        ],
        out_specs=[
            tile(3 * D_A, 0), tile(LANES, 0), tile(D_B, 0),
            pl.BlockSpec((None, 1, D_B), lambda b, t: (b, 0, 0)),
        ],
        out_shape=[
            jax.ShapeDtypeStruct((mp, 3 * D_A), F32),
            jax.ShapeDtypeStruct((mp, LANES), F32),
            jax.ShapeDtypeStruct((mp, D_B), BF16),
            jax.ShapeDtypeStruct((bsz, 1, D_B), F32),
        ],
        scratch_shapes=[pltpu.VMEM((SUBLANES + rows + SUBLANES, 3 * D_A), F32),
                        pltpu.VMEM((SUBLANES + rows + SUBLANES, D_B), F32),
                        pltpu.VMEM((rows, D_B), F32), pltpu.VMEM((rows, D_B), F32),
                        pltpu.VMEM((rows, D_B), F32), pltpu.VMEM((rows, D_B), F32),
                        pltpu.VMEM((SUBLANES, D_B), F32)],
        compiler_params=_cparams(("arbitrary", "arbitrary")),
        name="convlru_prompt",
    )(proj, proj, proj, proj, caw, cbw, cbb, alog_row, dtb_row, wa, wx, lba, lbx, lam, nbw)


def _lru_sample_body(xb_ref, yb_ref, hist_ref, h0_ref, cbw_ref, cbb_ref, wa_ref, wx_ref, lba_ref,
                     lbx_ref, lam_ref, nbw_ref, ob_ref, h1_ref, hx_ref, a_ref, b_ref, h_ref,
                     *, nseq, t_len):
    hdr = SUBLANES
    nh = CONV_W - 1
    hx_ref[0:hdr, :] = jnp.zeros((hdr, D_B), F32)

    def conv_seq(si, _):
        r0 = pl.multiple_of(si * t_len, t_len)
        hx_ref[hdr - nh:hdr, :] = hist_ref[si]
        hx_ref[hdr:hdr + t_len, :] = xb_ref[pl.ds(r0, t_len), :]
        for n in range(NB):
            c0 = n * BW
            h_ref[pl.ds(r0, t_len), c0:c0 + BW] = (_conv_taps(hx_ref, cbw_ref, c0, BW, t_len, hdr)
                                                  + cbb_ref[:, c0:c0 + BW])
        return 0

    lax.fori_loop(0, nseq, conv_seq, 0)
    parts = _lru_coeffs(h_ref[...], wa_ref, wx_ref, lba_ref, lbx_ref, lam_ref)
    for n in range(NB):
        sl = slice(n * BW, (n + 1) * BW)
        a_n, mult_n, gix_n = parts[n]
        a_ref[:, sl] = a_n
        b_ref[:, sl] = mult_n * gix_n

    def seq(si, _):
        r0 = pl.multiple_of(si * t_len, t_len)
        ag, bg = _group_scan(a_ref[pl.ds(r0, t_len), :], b_ref[pl.ds(r0, t_len), :])
        hg = ag * h0_ref[pl.ds(si, 1), :] + bg
        h_ref[pl.ds(r0, t_len), :] = hg
        h1_ref[pl.ds(si, 1), :] = hg[t_len - 1:t_len, :]
        return 0

    lax.fori_loop(0, nseq, seq, 0)
    ob_ref[...] = _lru_finish(h_ref[...], yb_ref[...], nbw_ref[...])


def _lru_sample(proj, row0, bsz, t_len, hist, h0, cbw, cbb, wa, wx, lba, lbx, lam, nbw):
    assert t_len == SUBLANES
    ms = bsz * t_len
    assert row0 % ms == 0
    rblk = row0 // ms
    wspec = lambda shape: pl.BlockSpec(shape, lambda i: (0,) * len(shape))
    return pl.pallas_call(
        functools.partial(_lru_sample_body, nseq=bsz, t_len=t_len),
        grid=(1,),
        in_specs=[
            pl.BlockSpec((ms, D_B), lambda i: (rblk, COL_XB // D_B)),
            pl.BlockSpec((ms, D_B), lambda i: (rblk, COL_YB // D_B)),
            wspec((bsz, CONV_W - 1, D_B)), wspec((bsz, D_B)),
            wspec((CONV_W, D_B)), wspec((1, D_B)),
            wspec((NB, BW, BW)), wspec((NB, BW, BW)),
            wspec((1, D_B)), wspec((1, D_B)), wspec((1, D_B)), wspec((1, D_B)),
        ],
        out_specs=[wspec((ms, D_B)), wspec((bsz, D_B))],
        out_shape=[
            jax.ShapeDtypeStruct((ms, D_B), BF16),
            jax.ShapeDtypeStruct((bsz, D_B), F32),
        ],
        scratch_shapes=[pltpu.VMEM((2 * SUBLANES, D_B), F32), pltpu.VMEM((ms, D_B), F32),
                        pltpu.VMEM((ms, D_B), F32), pltpu.VMEM((ms, D_B), F32)],
        compiler_params=_cparams(("arbitrary",)),
        name="lru_sample",
    )(proj, proj, hist, h0, cbw, cbb, wa, wx, lba, lbx, lam, nbw)


def _layernorm_rows(v, g, b):
    mu = jnp.mean(v, axis=-1, keepdims=True)
    d = v - mu
    var = jnp.mean(d * d, axis=-1, keepdims=True)
    return d * lax.rsqrt(var + LN_EPS) * g + b


def _store_slabs(slab_ref, base, val):
    n = val.shape[0]
    for s in range(ROW_SLAB):
        slab_ref[pl.ds(base + s, n, stride=ROW_SLAB), :] = val[:, s * LANES:(s + 1) * LANES]


def _load_slab_chunk(slab_ref, base, n, s):
    return slab_ref[pl.ds(base + s, n, stride=ROW_SLAB), :]


def _outproj_body(oap_ref, oas_ref, obp_ref, obs_ref, xp_ref, xs_ref, wo_ref, g_ref, b_ref,
                  wr_ref, br_ref, x1_ref, ti_ref, tg_ref, *, n_ptiles):
    i = pl.program_id(0)

    def run(oa_ref, ob_ref, x_ref):
        mix = (jnp.dot(oa_ref[...], wo_ref[0:D_A, :], preferred_element_type=F32)
               + jnp.dot(ob_ref[...], wo_ref[D_A:D_A + D_B, :], preferred_element_type=F32))
        y = _layernorm_rows(DN_ALPHA * x_ref[...] + mix, g_ref[...], b_ref[...])
        _store_slabs(x1_ref, 0, y)
        logits = _dot3(y, wr_ref[...]) + br_ref[...]
        lane = lax.broadcasted_iota(jnp.int32, logits.shape, 1)
        lane_o = lax.broadcasted_iota(jnp.int32, ti_ref.shape, 1)
        cur = logits
        ti = jnp.zeros(ti_ref.shape, jnp.int32)
        tv = jnp.zeros(tg_ref.shape, F32)
        v0 = None
        den = None
        for kk in range(TOP_K):
            m = jnp.max(cur, axis=-1, keepdims=True)
            idx = jnp.min(jnp.where(cur == m, lane, N_EXPERTS), axis=-1, keepdims=True)
            cur = jnp.where(lane == idx, -jnp.inf, cur)
            if kk == 0:
                v0 = m
            e = jnp.exp(m - v0)
            den = e if den is None else den + e
            ti = jnp.where(lane_o == kk, idx, ti)
            tv = jnp.where(lane_o == kk, e, tv)
        ti_ref[...] = ti
        tg_ref[...] = tv / den

    @pl.when(i < n_ptiles)
    def _():
        run(oap_ref, obp_ref, xp_ref)

    @pl.when(i >= n_ptiles)
    def _():
        run(oas_ref, obs_ref, xs_ref)


def _outproj(oa_p, oa_s, ob_p, ob_s, xp2, xs2, wo, g_row, b_row, wr, br_row):
    mp, ms = xp2.shape[0], xs2.shape[0]
    tm = min(OUT_TM, ms)
    assert mp % tm == 0 and ms % tm == 0
    n_pt, n_st = mp // tm, ms // tm
    pmap = lambda i: (jnp.minimum(i, n_pt - 1), 0)
    smap = lambda i: (jnp.maximum(i - n_pt, 0), 0)
    wspec = lambda shape: pl.BlockSpec(shape, lambda i: (0, 0))
    m = mp + ms
    return pl.pallas_call(
        functools.partial(_outproj_body, n_ptiles=n_pt),
        grid=(n_pt + n_st,),
        in_specs=[
            pl.BlockSpec((tm, D_A), pmap), pl.BlockSpec((tm, D_A), smap),
            pl.BlockSpec((tm, D_B), pmap), pl.BlockSpec((tm, D_B), smap),
            pl.BlockSpec((tm, D_MODEL), pmap), pl.BlockSpec((tm, D_MODEL), smap),
            wspec((D_A + D_B, D_MODEL)), wspec((1, D_MODEL)), wspec((1, D_MODEL)),
            wspec((D_MODEL, N_EXPERTS)), wspec((1, N_EXPERTS)),
        ],
        out_specs=[
            pl.BlockSpec((tm * ROW_SLAB, LANES), lambda i: (i, 0)),
            pl.BlockSpec((tm, LANES), lambda i: (i, 0)),
            pl.BlockSpec((tm, LANES), lambda i: (i, 0)),
        ],
        out_shape=[
            jax.ShapeDtypeStruct((m * ROW_SLAB, LANES), F32),
            jax.ShapeDtypeStruct((m, LANES), jnp.int32),
            jax.ShapeDtypeStruct((m, LANES), F32),
        ],
        compiler_params=_cparams(("arbitrary",)),
        name="outproj_ln_router",
    )(oa_p, oa_s, ob_p, ob_s, xp2, xs2, wo, g_row, b_row, wr, br_row)


def _zero_tail(zbuf_ref, dst_ref, tail_row, n_blocks, sem):
    scale = ROW_SLAB if len(dst_ref.shape) == 2 else 1

    def tail_copy(t):
        z0 = pl.multiple_of((tail_row + t * MOE_PB) * scale, MOE_PB * scale)
        return pltpu.make_async_copy(zbuf_ref, dst_ref.at[pl.ds(z0, MOE_PB * scale)], sem)

    def start(t, _):
        tail_copy(t).start()
        return 0

    def wait(t, _):
        tail_copy(t).wait()
        return 0

    lax.fori_loop(0, n_blocks, start, 0)
    lax.fori_loop(0, n_blocks, wait, 0)


def _dispatch_body(zrow_ref, slot_ref, x_ref, xs_ref, zbuf_ref, zsem, rsem, *, tm):
    i = pl.program_id(0)

    @pl.when(i == 0)
    def _():
        zbuf_ref[...] = jnp.zeros(zbuf_ref.shape, F32)

        def zero_copy(e):
            z0 = pl.multiple_of(zrow_ref[e], MOE_PB)
            return pltpu.make_async_copy(zbuf_ref, xs_ref.at[pl.ds(z0, MOE_PB)], zsem)

        for e in range(N_EXPERTS):
            @pl.when(zrow_ref[e] >= 0)
            def _():
                zero_copy(e).start()
        for e in range(N_EXPERTS):
            @pl.when(zrow_ref[e] >= 0)
            def _():
                zero_copy(e).wait()
        _zero_tail(zbuf_ref, xs_ref, zrow_ref[N_EXPERTS], zrow_ref[N_EXPERTS + 1], zsem)

    def row_copy(r, dst):
        src0 = pl.multiple_of(r * ROW_SLAB, ROW_SLAB)
        return pltpu.make_async_copy(x_ref.at[pl.ds(src0, ROW_SLAB)], xs_ref.at[dst], rsem)

    def issue(r, _):
        for kk in range(TOP_K):
            row_copy(r, slot_ref[0, r * TOP_K + kk]).start(priority=kk % 2)
        return 0

    lax.fori_loop(0, tm, issue, 0, unroll=8)

    for kk in range(TOP_K):
        pltpu.make_async_copy(xs_ref.at[pl.ds(0, tm)], xs_ref.at[pl.ds(0, tm)], rsem).wait()


def _dispatch(x1, slot_tiles, zrow, n_rows):
    m = x1.shape[0] // ROW_SLAB
    tm = TOK_TM
    assert m % tm == 0
    return pl.pallas_call(
        functools.partial(_dispatch_body, tm=tm),
        grid_spec=pltpu.PrefetchScalarGridSpec(
            num_scalar_prefetch=1,
            grid=(m // tm,),
            in_specs=[
                pl.BlockSpec((None, 1, tm * TOP_K), lambda i, z: (i, 0, 0), memory_space=pltpu.SMEM),
                pl.BlockSpec((tm * ROW_SLAB, LANES), lambda i, z: (i, 0)),
            ],
            out_specs=pl.BlockSpec(memory_space=pl.ANY),
            scratch_shapes=[pltpu.VMEM((MOE_PB, ROW_SLAB, LANES), F32),
                            pltpu.SemaphoreType.DMA(()), pltpu.SemaphoreType.DMA(())],
        ),
        out_shape=jax.ShapeDtypeStruct((n_rows, ROW_SLAB, LANES), F32),
        compiler_params=_cparams(("arbitrary",)),
        name="moe_dispatch",
    )(zrow, slot_tiles, x1)


def _for_row_blocks(npb, fn):
    for k in range(1, MOE_SB + 1):
        odd = 2 * k - 1
        hit = npb == 2 * k
        if odd not in MOE_EXACT_ODD:
            hit = jnp.logical_or(hit, npb == odd)

        @pl.when(hit)
        def _():
            fn(0, k * MOE_RB)
    for q in MOE_EXACT_ODD:
        @pl.when(npb == q)
        def _():
            fn(0, q * MOE_PB)


def _moe_body(ie_ref, irow_ref, insub_ref, tail_ref, xs_ref, wg_ref, wu_ref, bup_ref,
              wd_ref, bd_ref, ys_ref, xstage_ref, xb_ref, act_ref, ybuf_ref, zbuf_ref, wgb_ref, wub_ref,
              wdb_ref, isem, osem):
    wi = pl.program_id(0)
    j = pl.program_id(1)
    n_w = pl.num_programs(0)
    to_sub = lambda n: lax.shift_right_logical(n + (MOE_RB // MOE_PB - 1), MOE_RB // MOE_PB - 1)
    npb = insub_ref[wi]
    nsub = to_sub(npb)
    row0 = irow_ref[wi]
    nt = MOE_NT
    n1 = MOE_NT1
    is_last = wi == n_w - 1
    nxt = jnp.minimum(wi + 1, n_w - 1)
    nsub_next = to_sub(jnp.where(is_last, 0, insub_ref[nxt]))
    row_next = irow_ref[nxt]
    prv = jnp.maximum(wi - 1, 0)
    npb_prev = jnp.where(wi == 0, 0, insub_ref[prv])
    row_prev = irow_ref[prv]

    class _XCopy:
        def __init__(self, row_base, s, slot):
            g0 = pl.multiple_of(row_base + s * MOE_RB, MOE_PB)
            self.copies = [
                pltpu.make_async_copy(xs_ref.at[pl.ds(g0, MOE_RB), c],
                                      xstage_ref.at[slot, :, pl.ds(c * LANES, LANES)], isem.at[slot])
                for c in range(ROW_SLAB)]

        def start(self):
            for cp in self.copies:
                cp.start()

        def wait(self):
            for cp in self.copies:
                cp.wait()

    x_copy = _XCopy

    def to_matmul_layout(s, slot):
        r0 = pl.multiple_of(s * MOE_RB, MOE_RB)
        xb_ref[pl.ds(r0, MOE_RB), :] = xstage_ref[slot].astype(BF16)

    class _YCopy:
        def __init__(self, row_base, s):
            r0 = pl.multiple_of(s * MOE_PB, MOE_PB)
            g0 = pl.multiple_of(row_base + r0, MOE_PB)
            cpt = MOE_TF // LANES
            self.copies = [
                pltpu.make_async_copy(
                    ybuf_ref.at[c // cpt, pl.ds(r0, MOE_PB), pl.ds((c % cpt) * LANES, LANES)],
                    ys_ref.at[pl.ds(g0, MOE_PB), c], osem)
                for c in range(ROW_SLAB)]

        def start(self):
            for cp in self.copies:
                cp.start()

        def wait(self):
            for cp in self.copies:
                cp.wait()

    y_copy = _YCopy

    @pl.when(jnp.logical_and(wi == 0, j == 0))
    def _():
        def load(s, _):
            cp = x_copy(row0, s, 0)
            cp.start()
            cp.wait()
            to_matmul_layout(s, 0)
            return 0

        lax.fori_loop(0, nsub, load, 0)

    @pl.when(jnp.logical_and(j == n1 - 1, nsub_next > 0))
    def _():
        x_copy(row_next, 0, 0).start()

    @pl.when(jnp.logical_and(j >= n1, j - n1 < nsub_next))
    def _():
        s = j - n1
        slot = lax.rem(s, 2)
        x_copy(row_next, s, slot).wait()

        @pl.when(s + 1 < nsub_next)
        def _():
            x_copy(row_next, s + 1, 1 - slot).start()

        to_matmul_layout(s, slot)

    @pl.when(jnp.logical_and(j == n1, npb_prev > 0))
    def _():
        def drain(s, _):
            y_copy(row_prev, s).wait()
            return 0

        lax.fori_loop(0, npb_prev, drain, 0)

    @pl.when(npb > 0)
    def _():
        @pl.when(j < n1)
        def _():
            wgb_ref[...] = wg_ref[...].astype(BF16)
            wub_ref[...] = wu_ref[...].astype(BF16)
            b_gate = bup_ref[ie_ref[wi], pl.ds(j, 1), :]
            b_up = bup_ref[ie_ref[wi], pl.ds(n1 + j, 1), :]
            tpw = MOE_TF1 // MOE_TF

            def up_rows(r0, nrows):
                x = xb_ref[pl.ds(r0, nrows), :]
                hg = jnp.dot(x, wgb_ref[...], preferred_element_type=F32) + b_gate
                hu = jnp.dot(x, wub_ref[...], preferred_element_type=F32) + b_up
                gate = jnp.minimum(hg, SWIGLU_LIMIT)
                up = jnp.clip(hu, -SWIGLU_LIMIT, SWIGLU_LIMIT)
                glu = gate * _sigmoid(SWIGLU_ALPHA * gate)
                a = (glu * (up + 1.0)).astype(BF16)
                for tt in range(tpw):
                    act_ref[j * tpw + tt, pl.ds(r0, nrows), :] = a[:, tt * MOE_TF:(tt + 1) * MOE_TF]

            _for_row_blocks(npb, up_rows)

        @pl.when(j >= n1)
        def _():
            wdb_ref[...] = wd_ref[...].astype(BF16)
            n = j - n1
            b_down = bd_ref[ie_ref[wi], pl.ds(n, 1), :]

            def down_rows(r0, nrows):
                acc = jnp.dot(act_ref[0, pl.ds(r0, nrows), :], wdb_ref[0:MOE_TF, :],
                              preferred_element_type=F32)
                for jj in range(1, nt):
                    acc = acc + jnp.dot(act_ref[jj, pl.ds(r0, nrows), :],
                                        wdb_ref[jj * MOE_TF:(jj + 1) * MOE_TF, :],
                                        preferred_element_type=F32)
                ybuf_ref[n, pl.ds(r0, nrows), :] = acc + b_down

            _for_row_blocks(npb, down_rows)

        @pl.when(j == n1 + nt - 1)
        def _():
            def store(s, _):
                y_copy(row0, s).start()
                return 0

            lax.fori_loop(0, npb, store, 0)

    @pl.when(jnp.logical_and(is_last, j == n1 + nt - 1))
    def _():
        def drain(s, _):
            y_copy(row0, s).wait()
            return 0

        lax.fori_loop(0, npb, drain, 0)
        zbuf_ref[...] = jnp.zeros(zbuf_ref.shape, F32)
        _zero_tail(zbuf_ref, ys_ref, tail_ref[0], tail_ref[1], osem)


def _moe(xs, n_items, item_e, item_row, item_nsub, tail, w_up, b_up3, w_down, b_down3):
    n_rows = xs.shape[0]
    nt = MOE_NT
    n1 = MOE_NT1
    assert MOE_SB <= nt and MOE_RB == 2 * MOE_PB
    up_off = D_FF // MOE_TF1
    p1 = lambda j, n, w: jnp.where(n[w] > 0, jnp.minimum(j, n1 - 1), n1 - 1)
    p2 = lambda j, n, w: jnp.where(n[w] > 0, jnp.maximum(j - n1, 0), nt - 1)
    return pl.pallas_call(
        _moe_body,
        grid_spec=pltpu.PrefetchScalarGridSpec(
            num_scalar_prefetch=4,
            grid=(n_items, n1 + nt),
            in_specs=[
                pl.BlockSpec(memory_space=pl.ANY),
                pl.BlockSpec((None, D_MODEL, MOE_TF1), lambda w, j, e, r, n, t: (e[w], 0, p1(j, n, w))),
                pl.BlockSpec((None, D_MODEL, MOE_TF1),
                             lambda w, j, e, r, n, t: (e[w], 0, up_off + p1(j, n, w))),
                pl.BlockSpec((N_EXPERTS, 2 * n1, MOE_TF1), lambda w, j, e, r, n, t: (0, 0, 0)),
                pl.BlockSpec((None, D_FF, MOE_TF), lambda w, j, e, r, n, t: (e[w], 0, p2(j, n, w))),
                pl.BlockSpec((N_EXPERTS, nt, MOE_TF), lambda w, j, e, r, n, t: (0, 0, 0)),
            ],
            out_specs=pl.BlockSpec(memory_space=pl.ANY),
            scratch_shapes=[
                pltpu.VMEM((2, MOE_RB, D_MODEL), F32),
                pltpu.VMEM((MOE_R, D_MODEL), BF16),
                pltpu.VMEM((nt, MOE_R, MOE_TF), BF16),
                pltpu.VMEM((nt, MOE_R, MOE_TF), F32),
                pltpu.VMEM((MOE_PB, ROW_SLAB, LANES), F32),
                pltpu.VMEM((D_MODEL, MOE_TF1), BF16),
                pltpu.VMEM((D_MODEL, MOE_TF1), BF16),
                pltpu.VMEM((D_FF, MOE_TF), BF16),
                pltpu.SemaphoreType.DMA((2,)), pltpu.SemaphoreType.DMA(()),
            ],
        ),
        out_shape=jax.ShapeDtypeStruct((n_rows, ROW_SLAB, LANES), F32),
        compiler_params=_cparams(("arbitrary", "arbitrary")),
        name="moe_experts",
    )(item_e, item_row, item_nsub, tail, xs, w_up, w_up, b_up3, w_down, b_down3)


def _combine_body(slot_ref, slotn_ref, ys_ref, x1_ref, gate_ref, g_ref, b_ref, yp_ref, ysm_ref, buf_ref,
                  v_ref, sem, *, tm, n_ptiles):
    i = pl.program_id(0)
    cur = lax.rem(i, 2)

    def gather(table_ref, bslot):
        def issue(r, _):
            for kk in range(TOP_K):
                dst0 = pl.multiple_of(r * ROW_SLAB, ROW_SLAB)
                pltpu.make_async_copy(ys_ref.at[table_ref[0, r * TOP_K + kk]],
                                      buf_ref.at[bslot, kk, pl.ds(dst0, ROW_SLAB)],
                                      sem.at[bslot]).start(priority=kk % 2)
            return 0

        lax.fori_loop(0, tm, issue, 0, unroll=8)

    @pl.when(i == 0)
    def _():
        gather(slot_ref, 0)

    @pl.when(i + 1 < pl.num_programs(0))
    def _():
        gather(slotn_ref, 1 - cur)

    for kk in range(TOP_K):
        pltpu.make_async_copy(buf_ref.at[cur, kk], buf_ref.at[cur, kk], sem.at[cur]).wait()
    gates = gate_ref[...]
    for c in range(ROW_SLAB):
        acc = DN_ALPHA * _load_slab_chunk(x1_ref, 0, tm, c)
        for kk in range(TOP_K):
            acc = acc + gates[:, kk:kk + 1] * _load_slab_chunk(buf_ref.at[cur, kk], 0, tm, c)
        v_ref[:, c * LANES:(c + 1) * LANES] = acc
    y = _layernorm_rows(v_ref[...], g_ref[...], b_ref[...])

    @pl.when(pl.program_id(0) < n_ptiles)
    def _():
        yp_ref[...] = y

    @pl.when(pl.program_id(0) >= n_ptiles)
    def _():
        ysm_ref[...] = y


def _combine(ys, slot_tiles, x1, gates, g_row, b_row, mp):
    m = x1.shape[0] // ROW_SLAB
    tm = TOK_TM
    assert mp % tm == 0 and (m - mp) % tm == 0
    n_pt = mp // tm
    wspec = lambda shape: pl.BlockSpec(shape, lambda i: (0, 0))
    n_t = m // tm
    return pl.pallas_call(
        functools.partial(_combine_body, tm=tm, n_ptiles=n_pt),
        grid=(n_t,),
        in_specs=[
            pl.BlockSpec((None, 1, tm * TOP_K), lambda i: (i, 0, 0), memory_space=pltpu.SMEM),
            pl.BlockSpec((None, 1, tm * TOP_K), lambda i: (jnp.minimum(i + 1, n_t - 1), 0, 0),
                         memory_space=pltpu.SMEM),
            pl.BlockSpec(memory_space=pl.ANY),
            pl.BlockSpec((tm * ROW_SLAB, LANES), lambda i: (i, 0)),
            pl.BlockSpec((tm, LANES), lambda i: (i, 0)),
            wspec((1, D_MODEL)), wspec((1, D_MODEL)),
        ],
        out_specs=[pl.BlockSpec((tm, D_MODEL), lambda i: (jnp.minimum(i, n_pt - 1), 0)),
                   pl.BlockSpec((tm, D_MODEL), lambda i: (jnp.maximum(i - n_pt, 0), 0))],
        out_shape=[jax.ShapeDtypeStruct((mp, D_MODEL), F32),
                   jax.ShapeDtypeStruct((m - mp, D_MODEL), F32)],
        scratch_shapes=[pltpu.VMEM((2, TOP_K, tm * ROW_SLAB, LANES), F32),
                        pltpu.VMEM((tm, D_MODEL), F32), pltpu.SemaphoreType.DMA((2,))],
        compiler_params=_cparams(("arbitrary",)),
        name="moe_combine_ln",
    )(slot_tiles, slot_tiles, ys, x1, gates, g_row, b_row)


def _routing_tables(top_i, m):
    e_ids = jnp.arange(N_EXPERTS, dtype=jnp.int32)
    onehot = (top_i[:, :, None] == e_ids[None, None, :]).astype(jnp.int32)
    mask = jnp.sum(onehot, axis=1)
    incl = jnp.cumsum(mask, axis=0)
    pos = incl - mask
    counts = incl[-1]
    npb = (counts + MOE_PB - 1) // MOE_PB
    padded = npb * MOE_PB
    gend = jnp.cumsum(padded)
    gstart = gend - padded
    slot = jnp.sum(onehot * (gstart[None, None, :] + pos[:, None, :]), axis=2)
    n_rows = (m * TOP_K // MOE_PB + N_EXPERTS) * MOE_PB + MOE_PB
    tail = jnp.stack([gend[-1], (n_rows - gend[-1]) // MOE_PB]).astype(jnp.int32)
    zrow = jnp.concatenate([jnp.where(counts > 0, gend - MOE_PB, -1).astype(jnp.int32), tail])
    n_items = N_EXPERTS + (m * TOP_K // MOE_PB + N_EXPERTS) // MOE_NPB
    ipe = (npb + MOE_NPB - 1) // MOE_NPB
    iend = jnp.cumsum(ipe)
    istart = iend - ipe
    total = iend[-1]
    wid = jnp.arange(n_items, dtype=jnp.int32)
    wclamp = jnp.minimum(wid, total - 1)
    ie = jnp.minimum(jnp.searchsorted(iend, wclamp, side="right"), N_EXPERTS - 1).astype(jnp.int32)
    jn = wclamp - istart[ie]
    irow = (gstart[ie] + jn * MOE_R).astype(jnp.int32)
    insub = jnp.where(wid < total, jnp.clip(npb[ie] - jn * MOE_NPB, 0, MOE_NPB), 0).astype(jnp.int32)
    return slot.astype(jnp.int32), zrow, tail, total.astype(jnp.int32), ie, irow, insub, n_rows


def kernel(x_prompt, x_sample, state_conv_a, state_delta, state_conv_b, state_lru, w_in, conv_a_w,
           a_log, dt_bias, norm_a_w, conv_b_w, conv_b_b, lru_wa, lru_ba, lru_wx, lru_bx, lru_lambda,
           norm_b_w, w_out, ln1_g, ln1_b, w_router, b_router, w_up, b_up, w_down, b_down, ln2_g, ln2_b):
    bp, tp, _ = x_prompt.shape
    bs, ts, _ = x_sample.shape
    mp, ms = bp * tp, bs * ts
    m = mp + ms
    l = 0
    xp2 = x_prompt.reshape(mp, D_MODEL)
    xs2 = x_sample.reshape(ms, D_MODEL)

    wp = _wprep(jnp.transpose(w_in[l]))
    wo = w_out[l].astype(BF16)
    pad_h = lambda v: jnp.zeros((1, LANES), F32).at[0, H_A:2 * H_A].set(v)
    alog_row = pad_h(a_log[l])
    dtb_row = pad_h(dt_bias[l])
    row = lambda v: v.reshape(1, -1)

    proj = _inproj(xp2, xs2, wp)

    qkv_act, bg, ob_p, h_p = _convlru_prompt(
        proj, bp, tp, conv_a_w[l], conv_b_w[l], row(conv_b_b[l]), alog_row, dtb_row,
        lru_wa[l], lru_wx[l], row(lru_ba[l]), row(lru_bx[l]), row(lru_lambda[l]), row(norm_b_w[l]))
    u, w, qe, kdt, aqk, el = _delta_prep(qkv_act, bg, bp, tp)
    oa_p, sd_p = _delta_seq(u, w, qe, kdt, aqk, el, proj, row(norm_a_w[l]), bp, tp)
    oa_s, sd_s = _sample_a(proj, mp, bs, ts, state_conv_a[l], state_delta[l], conv_a_w[l],
                           alog_row, dtb_row, row(norm_a_w[l]))
    ob_s, h_s = _lru_sample(proj, mp, bs, ts, state_conv_b[l], state_lru[l], conv_b_w[l],
                            row(conv_b_b[l]), lru_wa[l], lru_wx[l], row(lru_ba[l]), row(lru_bx[l]),
                            row(lru_lambda[l]), row(norm_b_w[l]))

    x1, ti, tg = _outproj(oa_p, oa_s, ob_p, ob_s, xp2, xs2, wo, row(ln1_g[l]), row(ln1_b[l]),
                          w_router[l], row(b_router[l]))

    slot, zrow, tail, n_used, ie, irow, insub, n_rows = _routing_tables(ti[:, :TOP_K], m)
    slot_tiles = slot.reshape(m // TOK_TM, 1, TOK_TM * TOP_K)
    xs_sorted = _dispatch(x1, slot_tiles, zrow, n_rows)
    ys = _moe(xs_sorted, n_used, ie, irow, insub, tail, w_up[l],
              b_up[l].reshape(N_EXPERTS, 2 * MOE_NT1, MOE_TF1), w_down[l],
              b_down[l].reshape(N_EXPERTS, MOE_NT, MOE_TF))
    y_p, y_s = _combine(ys, slot_tiles, x1, tg, row(ln2_g[l]), row(ln2_b[l]), mp)

    y_prompt = y_p.reshape(bp, tp, D_MODEL)
    y_sample = y_s.reshape(bs, ts, D_MODEL)
    nh = CONV_W - 1
    assert tp % SUBLANES == 0 and ts == SUBLANES and nh <= SUBLANES
    pg = proj.reshape(m // SUBLANES, SUBLANES, N_PROJ)
    gp = tp // SUBLANES

    def last_rows(g0, g1, gstep, c0, width):
        return lax.slice(pg, (g0, SUBLANES - nh, c0), (g1, SUBLANES, c0 + width), (gstep, 1, 1))

    ca_p = last_rows(gp - 1, mp // SUBLANES, gp, COL_QKV, 3 * D_A)
    cb_p = last_rows(gp - 1, mp // SUBLANES, gp, COL_XB, D_B)
    ca_s = last_rows(mp // SUBLANES, m // SUBLANES, 1, COL_QKV, 3 * D_A)
    cb_s = last_rows(mp // SUBLANES, m // SUBLANES, 1, COL_XB, D_B)
    return (y_prompt, y_sample,
            ca_p[None], sd_p[None], cb_p[None], h_p.reshape(1, bp, D_B),
            ca_s[None], sd_s[None], cb_s[None], h_s[None])
```
